```python
import math
import jax, jax.numpy as jnp
from jax import lax
import numpy as np

D_MODEL = 1024
BATCH = 4
SEQ = 4096
DEPTH = 1
DEC_BATCH = 128
DEC_SEQ = 8
PAST_LEN = 16384
PAGE_SIZE = 128

GLA_HEADS = 4
GLA_KEY_DIM = D_MODEL // 2
GLA_VAL_DIM = D_MODEL
GLA_DK = GLA_KEY_DIM // GLA_HEADS
GLA_DV = GLA_VAL_DIM // GLA_HEADS
GLA_GATE_RANK = 16
GLA_GATE_NORMALIZER = 16.0
GLA_CHUNK = 64
SWA_HEADS = 16
SWA_KV_HEADS = 4
SWA_GROUP = SWA_HEADS // SWA_KV_HEADS
SWA_HEAD_DIM = 64
WINDOW = 128
N_EXPERTS = 256
TOP_K = 8
N_GROUPS = 8
TOPK_GROUPS = 4
EXPERT_FF = 256
SHARED_FF = 256
ROUTED_SCALE = 2.5
EXPERT_BLOCK = 64
DEEPNORM_ALPHA = (2.0 * DEPTH) ** 0.25
DEEPNORM_BETA = (8.0 * DEPTH) ** -0.25
EPS = 1e-5

kernel_name = "hybrid_gla_swa_moe_deepnorm_step"


def _in_sizes():
    return (GLA_KEY_DIM, GLA_KEY_DIM, GLA_VAL_DIM, GLA_VAL_DIM, GLA_GATE_RANK,
            SWA_HEADS * SWA_HEAD_DIM, SWA_KV_HEADS * SWA_HEAD_DIM, SWA_KV_HEADS * SWA_HEAD_DIM,
            D_MODEL, D_MODEL)


def layer_norm(x, g, b):
    xf = x.astype(jnp.float32)
    mu = jnp.mean(xf, axis=-1, keepdims=True)
    var = jnp.mean(jnp.square(xf - mu), axis=-1, keepdims=True)
    return ((xf - mu) * lax.rsqrt(var + EPS) * g.astype(jnp.float32) + b.astype(jnp.float32)).astype(x.dtype)


def gla_scan(q, k, v, log_a, s0):
    B, L = q.shape[0], q.shape[1]
    c = math.gcd(L, GLA_CHUNK)
    n = L // c

    def to_chunks(t):
        return t.astype(jnp.float32).reshape(B, n, c, GLA_HEADS, t.shape[-1]).transpose(1, 0, 3, 2, 4)

    qc, kc, vc, gc = to_chunks(q), to_chunks(k), to_chunks(v), to_chunks(log_a)
    causal = jnp.tril(jnp.ones((c, c), dtype=bool))

    def step(S, inp):
        qi, ki, vi, gi = inp
        b = jnp.cumsum(gi, axis=2)
        b_last = b[:, :, -1:, :]
        q_dec = qi * jnp.exp(b)
        k_dec = ki * jnp.exp(-b)
        att = jnp.where(causal, jnp.einsum('bhtk,bhsk->bhts', q_dec, k_dec), 0.0)
        o = jnp.einsum('bhts,bhsv->bhtv', att, vi) + jnp.einsum('bhtk,bhkv->bhtv', q_dec, S)
        k_rem = ki * jnp.exp(b_last - b)
        S = jnp.exp(b_last[:, :, 0, :])[..., None] * S + jnp.einsum('bhsk,bhsv->bhkv', k_rem, vi)
        return S, o

    s_final, o = lax.scan(step, s0.astype(jnp.float32), (qc, kc, vc, gc))
    o = o.transpose(1, 0, 3, 2, 4).reshape(B, L, GLA_HEADS, GLA_DV)
    return o, s_final


def swa_attention(q, k, v, k_past, v_past, sinks):
    B, L = q.shape[0], q.shape[1]
    n_past = k_past.shape[1]
    bq = math.gcd(L, WINDOW)
    nq = L // bq
    span = WINDOW + bq
    pad = ((0, 0), (WINDOW - n_past, 0), (0, 0), (0, 0))
    k_all = jnp.concatenate([jnp.pad(k_past, pad), k], axis=1)
    v_all = jnp.concatenate([jnp.pad(v_past, pad), v], axis=1)
    idx = jnp.arange(nq)[:, None] * bq + jnp.arange(span)[None, :]
    kb = k_all[:, idx]
    vb = v_all[:, idx]
    qb = q.reshape(B, nq, bq, SWA_KV_HEADS, SWA_GROUP, SWA_HEAD_DIM)
    s = jnp.einsum('bnqkgd,bnskd->bnkgqs', qb, kb, preferred_element_type=jnp.float32) * (SWA_HEAD_DIM ** -0.5)
    dist = (jnp.arange(bq)[:, None] + WINDOW - jnp.arange(span)[None, :]).astype(jnp.float32)
    slopes = 2.0 ** (-8.0 * jnp.arange(1, SWA_HEADS + 1, dtype=jnp.float32) / SWA_HEADS)
    s = s - slopes.reshape(SWA_KV_HEADS, SWA_GROUP, 1, 1) * dist
    mask = (dist >= 0) & (dist < WINDOW)
    mask = mask[None] & (idx >= WINDOW - n_past)[:, None, :]
    s = jnp.where(mask[None, :, None, None], s, -jnp.inf)
    sink = sinks.astype(jnp.float32).reshape(1, 1, SWA_KV_HEADS, SWA_GROUP, 1, 1)
    m = jnp.maximum(jnp.max(s, axis=-1, keepdims=True), sink)
    p = jnp.exp(s - m)
    denom = jnp.sum(p, axis=-1, keepdims=True) + jnp.exp(sink - m)
    o = jnp.einsum('bnkgqs,bnskd->bnqkgd', p / denom, vb.astype(jnp.float32))
    o = o.reshape(B, L, SWA_HEADS * SWA_HEAD_DIM)
    keep = min(WINDOW, n_past + L)
    return o, k_all[:, -keep:], v_all[:, -keep:]


def token_mixer(x, gla_s0, k_past, v_past, w_in, w_gk_up, b_gk, gla_norm_g, attn_sinks,
                w_proj_a, w_proj_b, w_out):
    B, L, _ = x.shape
    offsets = [int(o) for o in np.cumsum(_in_sizes())[:-1]]
    h = x @ w_in
    qa, ka, va, ga, gk_low, qb, kb, vb, gate_a, gate_b = jnp.split(h, offsets, axis=-1)
    log_a = jax.nn.log_sigmoid((gk_low @ w_gk_up + b_gk).astype(jnp.float32)) / GLA_GATE_NORMALIZER
    qa = qa.reshape(B, L, GLA_HEADS, GLA_DK) * (GLA_DK ** -0.5)
    ka = ka.reshape(B, L, GLA_HEADS, GLA_DK)
    va = va.reshape(B, L, GLA_HEADS, GLA_DV)
    log_a = log_a.reshape(B, L, GLA_HEADS, GLA_DK)
    o_a, s_new = gla_scan(qa, ka, va, log_a, gla_s0)
    o_a = o_a * lax.rsqrt(jnp.mean(jnp.square(o_a), axis=-1, keepdims=True) + EPS) * gla_norm_g.astype(jnp.float32)
    o_a = o_a.reshape(B, L, GLA_VAL_DIM) * jax.nn.silu(ga.astype(jnp.float32))
    branch_a = o_a.astype(x.dtype) @ w_proj_a
    qb = qb.reshape(B, L, SWA_HEADS, SWA_HEAD_DIM)
    kb = kb.reshape(B, L, SWA_KV_HEADS, SWA_HEAD_DIM)
    vb = vb.reshape(B, L, SWA_KV_HEADS, SWA_HEAD_DIM)
    o_b, k_new, v_new = swa_attention(qb, kb, vb, k_past, v_past, attn_sinks)
    branch_b = o_b.astype(x.dtype) @ w_proj_b
    merged = jax.nn.sigmoid(gate_a) * branch_a + jax.nn.sigmoid(gate_b) * branch_b
    return merged @ w_out, s_new.astype(gla_s0.dtype), k_new, v_new


def moe_ffn(x2d, w_router, router_bias, w_expert_gate, w_expert_up, w_expert_down,
            w_shared_gate, w_shared_up, w_shared_down):
    T, D = x2d.shape
    scores = jax.nn.sigmoid(jnp.dot(x2d, w_router, preferred_element_type=jnp.float32))
    biased = scores + router_bias.astype(jnp.float32)
    grouped = biased.reshape(T, N_GROUPS, N_EXPERTS // N_GROUPS)
    group_score = jnp.sum(lax.top_k(grouped, 2)[0], axis=-1)
    _, top_groups = lax.top_k(group_score, TOPK_GROUPS)
    group_mask = jnp.sum(jax.nn.one_hot(top_groups, N_GROUPS, dtype=jnp.float32), axis=1) > 0
    masked = jnp.where(group_mask[:, :, None], grouped, -jnp.inf).reshape(T, N_EXPERTS)
    _, top_idx = lax.top_k(masked, TOP_K)
    top_w = jnp.take_along_axis(scores, top_idx, axis=1)
    top_w = top_w / jnp.sum(top_w, axis=-1, keepdims=True) * ROUTED_SCALE
    A = T * TOP_K
    e_flat = top_idx.reshape(A).astype(jnp.int32)
    tok_flat = jnp.repeat(jnp.arange(T, dtype=jnp.int32), TOP_K)
    w_flat = top_w.reshape(A)
    order = jnp.argsort(e_flat)
    e_s, tok_s, w_s = e_flat[order], tok_flat[order], w_flat[order]
    counts = jnp.bincount(e_flat, length=N_EXPERTS).astype(jnp.int32)
    padded = (counts + EXPERT_BLOCK - 1) // EXPERT_BLOCK * EXPERT_BLOCK
    start = jnp.cumsum(counts) - counts
    pstart = jnp.cumsum(padded) - padded
    pend = pstart + padded
    dest = pstart[e_s] + (jnp.arange(A, dtype=jnp.int32) - start[e_s])
    n_blocks = -(-A // EXPERT_BLOCK) + N_EXPERTS
    n_rows = n_blocks * EXPERT_BLOCK
    row_tok = jnp.full((n_rows,), T, dtype=jnp.int32).at[dest].set(tok_s)
    row_w = jnp.zeros((n_rows,), jnp.float32).at[dest].set(w_s)
    block_e = jnp.clip(jnp.searchsorted(pend, jnp.arange(n_blocks, dtype=jnp.int32) * EXPERT_BLOCK, side='right'),
                       0, N_EXPERTS - 1)
    x_pad = jnp.concatenate([x2d, jnp.zeros((1, D), x2d.dtype)], axis=0)

    def block_step(acc, inp):
        tok, w, e = inp
        xb = x_pad[tok]
        hid = jax.nn.silu(xb @ w_expert_gate[e]) * (xb @ w_expert_up[e])
        y = (hid @ w_expert_down[e]).astype(jnp.float32) * w[:, None]
        return acc.at[tok].add(y), None

    acc, _ = lax.scan(block_step, jnp.zeros((T + 1, D), jnp.float32),
                      (row_tok.reshape(n_blocks, EXPERT_BLOCK), row_w.reshape(n_blocks, EXPERT_BLOCK), block_e))
    shared = (jax.nn.silu(x2d @ w_shared_gate) * (x2d @ w_shared_up)) @ w_shared_down
    return (acc[:T] + shared.astype(jnp.float32)).astype(x2d.dtype)


def decoder_layer(x, gla_s0, k_past, v_past, w_in, w_gk_up, b_gk, gla_norm_g, attn_sinks,
                  w_proj_a, w_proj_b, w_out, ln1_g, ln1_b, w_router, router_bias,
                  w_expert_gate, w_expert_up, w_expert_down, w_shared_gate, w_shared_up, w_shared_down,
                  ln2_g, ln2_b):
    mix, s_new, k_new, v_new = token_mixer(x, gla_s0, k_past, v_past, w_in, w_gk_up, b_gk, gla_norm_g,
                                           attn_sinks, w_proj_a, w_proj_b, w_out)
    x = layer_norm(DEEPNORM_ALPHA * x + mix, ln1_g, ln1_b)
    ffn = moe_ffn(x.reshape(-1, x.shape[-1]), w_router, router_bias, w_expert_gate, w_expert_up,
                  w_expert_down, w_shared_gate, w_shared_up, w_shared_down).reshape(x.shape)
    x = layer_norm(DEEPNORM_ALPHA * x + ffn, ln2_g, ln2_b)
    return x, s_new, k_new, v_new


def setup_inputs(seed: int = 0) -> dict:
    key = jax.random.key(seed)
    ks = jax.random.split(key, 26)

    def nrm(k, shape, scale):
        return jax.random.normal(k, shape, jnp.float32) * scale

    d_in = sum(_in_sizes())
    cache_rows = min(WINDOW, PAST_LEN)
    return {
        "x_prompt": nrm(ks[0], (BATCH, SEQ, D_MODEL), 1.0),
        "x_sample": nrm(ks[1], (DEC_BATCH, DEC_SEQ, D_MODEL), 1.0),
        "state_gla": nrm(ks[2], (DEPTH, DEC_BATCH, GLA_HEADS, GLA_DK, GLA_DV), 0.1),
        "cache_swa_k": nrm(ks[3], (DEPTH, DEC_BATCH, cache_rows, SWA_KV_HEADS, SWA_HEAD_DIM), 1.0),
        "cache_swa_v": nrm(ks[4], (DEPTH, DEC_BATCH, cache_rows, SWA_KV_HEADS, SWA_HEAD_DIM), 1.0),
        "w_in": nrm(ks[5], (DEPTH, D_MODEL, d_in), D_MODEL ** -0.5),
        "w_gk_up": nrm(ks[6], (DEPTH, GLA_GATE_RANK, GLA_KEY_DIM), GLA_GATE_RANK ** -0.5),
        "b_gk": nrm(ks[7], (DEPTH, GLA_KEY_DIM), 0.1),
        "gla_norm_g": 1.0 + nrm(ks[8], (DEPTH, GLA_DV), 0.01),
        "attn_sinks": nrm(ks[9], (DEPTH, SWA_HEADS), 0.5),
        "w_proj_a": nrm(ks[10], (DEPTH, GLA_VAL_DIM, D_MODEL), GLA_VAL_DIM ** -0.5),
        "w_proj_b": nrm(ks[11], (DEPTH, SWA_HEADS * SWA_HEAD_DIM, D_MODEL), (SWA_HEADS * SWA_HEAD_DIM) ** -0.5),
        "w_out": nrm(ks[12], (DEPTH, D_MODEL, D_MODEL), D_MODEL ** -0.5 * DEEPNORM_BETA),
        "ln1_g": 1.0 + nrm(ks[13], (DEPTH, D_MODEL), 0.01),
        "ln1_b": nrm(ks[14], (DEPTH, D_MODEL), 0.01),
        "w_router": nrm(ks[15], (DEPTH, D_MODEL, N_EXPERTS), D_MODEL ** -0.5),
        "router_bias": nrm(ks[16], (DEPTH, N_EXPERTS), 0.01),
        "w_expert_gate": nrm(ks[17], (DEPTH, N_EXPERTS, D_MODEL, EXPERT_FF), D_MODEL ** -0.5),
        "w_expert_up": nrm(ks[18], (DEPTH, N_EXPERTS, D_MODEL, EXPERT_FF), D_MODEL ** -0.5),
        "w_expert_down": nrm(ks[19], (DEPTH, N_EXPERTS, EXPERT_FF, D_MODEL), EXPERT_FF ** -0.5 * DEEPNORM_BETA),
        "w_shared_gate": nrm(ks[20], (DEPTH, D_MODEL, SHARED_FF), D_MODEL ** -0.5),
        "w_shared_up": nrm(ks[21], (DEPTH, D_MODEL, SHARED_FF), D_MODEL ** -0.5),
        "w_shared_down": nrm(ks[22], (DEPTH, SHARED_FF, D_MODEL), SHARED_FF ** -0.5 * DEEPNORM_BETA),
        "ln2_g": 1.0 + nrm(ks[23], (DEPTH, D_MODEL), 0.01),
        "ln2_b": nrm(ks[24], (DEPTH, D_MODEL), 0.01),
    }


def reference(x_prompt, x_sample, state_gla, cache_swa_k, cache_swa_v, w_in, w_gk_up, b_gk, gla_norm_g,
              attn_sinks, w_proj_a, w_proj_b, w_out, ln1_g, ln1_b, w_router, router_bias,
              w_expert_gate, w_expert_up, w_expert_down, w_shared_gate, w_shared_up, w_shared_down,
              ln2_g, ln2_b):
    yp, ys = x_prompt, x_sample
    bp = x_prompt.shape[0]
    sp_list, ss_list, kp_list, vp_list, ksm_list, vsm_list = [], [], [], [], [], []
    for l in range(DEPTH):
        w = (w_in[l], w_gk_up[l], b_gk[l], gla_norm_g[l], attn_sinks[l], w_proj_a[l], w_proj_b[l], w_out[l],
             ln1_g[l], ln1_b[l], w_router[l], router_bias[l], w_expert_gate[l], w_expert_up[l],
             w_expert_down[l], w_shared_gate[l], w_shared_up[l], w_shared_down[l], ln2_g[l], ln2_b[l])
        s0 = jnp.zeros((bp, GLA_HEADS, GLA_DK, GLA_DV), state_gla.dtype)
        no_past = jnp.zeros((bp, 0, SWA_KV_HEADS, SWA_HEAD_DIM), cache_swa_k.dtype)
        yp, sp, kp, vp = decoder_layer(yp, s0, no_past, no_past, *w)
        ys, ss, ksm, vsm = decoder_layer(ys, state_gla[l], cache_swa_k[l], cache_swa_v[l], *w)
        sp_list.append(sp); ss_list.append(ss)
        kp_list.append(kp); vp_list.append(vp)
        ksm_list.append(ksm); vsm_list.append(vsm)
    return (yp, ys, jnp.stack(sp_list), jnp.stack(ss_list), jnp.stack(kp_list), jnp.stack(vp_list),
            jnp.stack(ksm_list), jnp.stack(vsm_list))
```

```python
import functools
import math

import jax
import jax.numpy as jnp
from jax import lax
from jax.experimental import pallas as pl
from jax.experimental.pallas import tpu as pltpu

F32 = jnp.float32
BF16 = jnp.bfloat16
I32 = jnp.int32

D_MODEL = 1024
GLA_HEADS = 4
GLA_DK = 128
GLA_DV = 256
GLA_KEY_DIM = GLA_HEADS * GLA_DK
GLA_VAL_DIM = GLA_HEADS * GLA_DV
GLA_GATE_RANK = 16
GLA_GATE_NORMALIZER = 16.0
GLA_CHUNK = 64
SWA_HEADS = 16
SWA_KV_HEADS = 4
SWA_GROUP = SWA_HEADS // SWA_KV_HEADS
SWA_HEAD_DIM = 64
SWA_Q_DIM = SWA_HEADS * SWA_HEAD_DIM
SWA_KV_DIM = SWA_KV_HEADS * SWA_HEAD_DIM
WINDOW = 128
N_EXPERTS = 256
TOP_K = 8
N_GROUPS = 8
GROUP_SIZE = N_EXPERTS // N_GROUPS
TOPK_GROUPS = 4
EXPERT_FF = 256
SHARED_FF = 256
ROUTED_SCALE = 2.5
DEEPNORM_ALPHA = 2.0 ** 0.25
EPS = 1e-5

LANES = 128
SUBLANES = 8
ROW_CHUNKS = D_MODEL // LANES
VMEM_LIMIT = 56 * 1024 * 1024

H_QA, H_KA, H_VA, H_GA, H_QB, H_GATE_A, H_GATE_B, H_KB, H_VB, H_GK = (
    0, 512, 1024, 2048, 3072, 4096, 5120, 6144, 6400, 6656)
H_WIDTH = 6912
H_TN = 2304

EXPERT_ROWS = 256


def _cparams(sem, vmem=VMEM_LIMIT):
    return pltpu.CompilerParams(dimension_semantics=sem, vmem_limit_bytes=vmem)


def _tile(n, pref):
    t = min(n, pref)
    while n % t:
        t -= LANES
    assert t > 0 and t % LANES == 0, (n, pref)
    return t


def _mm_kernel(x_ref, w_ref, o_ref):
    o_ref[...] = jnp.dot(x_ref[...].astype(BF16), w_ref[...],
                         preferred_element_type=F32).astype(o_ref.dtype)


def _matmul(x, w, out_dtype, tm, tn, name):
    m, k = x.shape
    n = w.shape[1]
    return pl.pallas_call(
        _mm_kernel,
        grid=(m // tm, n // tn),
        in_specs=[pl.BlockSpec((tm, k), lambda i, j: (i, 0)),
                  pl.BlockSpec((k, tn), lambda i, j: (0, j))],
        out_specs=pl.BlockSpec((tm, tn), lambda i, j: (i, j)),
        out_shape=jax.ShapeDtypeStruct((m, n), out_dtype),
        compiler_params=_cparams(("parallel", "arbitrary")),
        name=name,
    )(x, w)


def _split_bf16(x):
    hi = x.astype(BF16)
    lo = (x - hi.astype(F32)).astype(BF16)
    return hi, lo


def _gla_kernel(*refs, nb, c, sub, has_s0):
    if has_s0:
        (q_ref, k_ref, v_ref, ga_ref, gk_ref, wup_ref, bgk_ref, gn_ref, s0_ref,
         o_ref, sout_ref, s_scr) = refs
    else:
        (q_ref, k_ref, v_ref, ga_ref, gk_ref, wup_ref, bgk_ref, gn_ref,
         o_ref, sout_ref, s_scr) = refs
        s0_ref = None
    ci = pl.program_id(1)
    rows = nb * c

    @pl.when(ci == 0)
    def _():
        if has_s0:
            s_scr[...] = s0_ref[...]
        else:
            s_scr[...] = jnp.zeros_like(s_scr)

    r_i = lax.broadcasted_iota(I32, (rows, rows), 0)
    c_i = lax.broadcasted_iota(I32, (rows, rows), 1)
    same_seq = (r_i // c) == (c_i // c)
    causal = same_seq & (c_i <= r_i)
    tri = jnp.where(causal, 1.0, 0.0).astype(BF16)
    seg = jnp.where(same_seq, 1.0, 0.0).astype(BF16)
    ones_kv = jnp.ones((rows, GLA_DV), BF16)
    seq_of_row = lax.broadcasted_iota(I32, (rows, 1), 0) // c
    tn = (((0,), (0,)), ((), ()))

    for s in range(sub):
        rs = slice(s * rows, (s + 1) * rows)
        pre = jnp.dot(gk_ref[rs, :], wup_ref[...], preferred_element_type=F32) + bgk_ref[...]
        log_a = jax.nn.log_sigmoid(pre) / GLA_GATE_NORMALIZER
        la_hi, la_lo = _split_bf16(log_a)
        b = (jnp.dot(tri, la_hi, preferred_element_type=F32)
             + jnp.dot(tri, la_lo, preferred_element_type=F32))
        b_last = (jnp.dot(seg, la_hi, preferred_element_type=F32)
                  + jnp.dot(seg, la_lo, preferred_element_type=F32))
        e_pos = jnp.exp(b)
        e_neg = jnp.exp(-b)
        e_rem = jnp.exp(b_last - b)
        q = q_ref[rs, :].astype(F32) * (GLA_DK ** -0.5)
        k = k_ref[rs, :].astype(F32)
        q_dec = (q * e_pos).astype(BF16)
        k_dec = (k * e_neg).astype(BF16)
        k_rem = k * e_rem
        outs = []
        for h in range(GLA_HEADS):
            ks = slice(h * GLA_DK, (h + 1) * GLA_DK)
            vh = v_ref[rs, h * GLA_DV:(h + 1) * GLA_DV]
            att = lax.dot_general(q_dec[:, ks], k_dec[:, ks], (((1,), (1,)), ((), ())),
                                  preferred_element_type=F32)
            att = jnp.where(causal, att, 0.0).astype(BF16)
            o_h = jnp.dot(att, vh, preferred_element_type=F32)
            for j in range(nb):
                mine = (seq_of_row == j) if nb > 1 else None
                pick = (lambda a: jnp.where(mine, a, 0.0)) if nb > 1 else (lambda a: a)
                s_old = s_scr[j, h]
                o_h = o_h + pick(jnp.dot(q_dec[:, ks], s_old.astype(BF16), preferred_element_type=F32))
                dec = (lax.dot_general(pick(la_hi[:, ks].astype(F32)).astype(BF16), ones_kv, tn,
                                       preferred_element_type=F32)
                       + lax.dot_general(pick(la_lo[:, ks].astype(F32)).astype(BF16), ones_kv, tn,
                                         preferred_element_type=F32))
                upd = lax.dot_general(pick(k_rem[:, ks]).astype(BF16), vh, tn, preferred_element_type=F32)
                s_scr[j, h] = jnp.exp(dec) * s_old + upd
            o_h = o_h * lax.rsqrt(jnp.mean(jnp.square(o_h), axis=-1, keepdims=True) + EPS) * gn_ref[...]
            outs.append(o_h)
        o = jnp.concatenate(outs, axis=-1) * jax.nn.silu(ga_ref[rs, :].astype(F32))
        o_ref[rs, :] = o.astype(o_ref.dtype)

    @pl.when(ci == pl.num_programs(1) - 1)
    def _():
        sout_ref[...] = s_scr[...]


def _gla(h, wup, bgk, gn, s0, *, row0, n_seq, seq_len, nb, c, sub, name):
    rows = nb * c * sub
    n_groups = n_seq // nb
    n_steps = seq_len // (c * sub)
    rb0 = row0 // rows

    def rmap(col):
        return lambda g, i: (rb0 + g * n_steps + i, col)

    in_specs = [
        pl.BlockSpec((rows, GLA_KEY_DIM), rmap(H_QA // GLA_KEY_DIM)),
        pl.BlockSpec((rows, GLA_KEY_DIM), rmap(H_KA // GLA_KEY_DIM)),
        pl.BlockSpec((rows, GLA_VAL_DIM), rmap(H_VA // GLA_VAL_DIM)),
        pl.BlockSpec((rows, GLA_VAL_DIM), rmap(H_GA // GLA_VAL_DIM)),
        pl.BlockSpec((rows, LANES), rmap(H_GK // LANES)),
        pl.BlockSpec((LANES, GLA_KEY_DIM), lambda g, i: (0, 0)),
        pl.BlockSpec((1, GLA_KEY_DIM), lambda g, i: (0, 0)),
        pl.BlockSpec((1, GLA_DV), lambda g, i: (0, 0)),
    ]
    args = [h, h, h, h, h, wup, bgk, gn]
    state_spec = pl.BlockSpec((nb, GLA_HEADS, GLA_DK, GLA_DV), lambda g, i: (g, 0, 0, 0))
    if s0 is not None:
        in_specs.append(state_spec)
        args.append(s0)
    kern = functools.partial(_gla_kernel, nb=nb, c=c, sub=sub, has_s0=s0 is not None)
    return pl.pallas_call(
        kern,
        grid=(n_groups, n_steps),
        in_specs=in_specs,
        out_specs=[pl.BlockSpec((rows, GLA_VAL_DIM), lambda g, i: (g * n_steps + i, 0)), state_spec],
        out_shape=[jax.ShapeDtypeStruct((n_seq * seq_len, GLA_VAL_DIM), BF16),
                   jax.ShapeDtypeStruct((n_seq, GLA_HEADS, GLA_DK, GLA_DV), F32)],
        scratch_shapes=[pltpu.VMEM((nb, GLA_HEADS, GLA_DK, GLA_DV), F32)],
        compiler_params=_cparams(("parallel", "arbitrary")),
        name=name,
    )(*args)


def _alibi_slope(head):
    return 2.0 ** (-8.0 * (head + 1) / SWA_HEADS)


def _swa_softmax_pv(parts, sink):
    m = sink
    for s, _ in parts:
        m = jnp.maximum(m, jnp.max(s, axis=-1, keepdims=True))
    denom = jnp.exp(sink - m)
    acc = None
    for s, v in parts:
        p = jnp.exp(s - m)
        denom = denom + jnp.sum(p, axis=-1, keepdims=True)
        pv = jnp.dot(p.astype(BF16), v, preferred_element_type=F32)
        acc = pv if acc is None else acc + pv
    return acc / denom


def _swa_prompt_kernel(sink_ref, q_ref, kp_ref, vp_ref, kc_ref, vc_ref, o_ref):
    i = pl.program_id(1)
    span = 2 * WINDOW
    kcat = jnp.concatenate([kp_ref[...], kc_ref[...]], axis=0)
    vcat = jnp.concatenate([vp_ref[...], vc_ref[...]], axis=0)
    row = lax.broadcasted_iota(I32, (WINDOW, span), 0)
    col = lax.broadcasted_iota(I32, (WINDOW, span), 1)
    dist_i = row + WINDOW - col
    valid = (dist_i >= 0) & (dist_i < WINDOW) & ((col >= WINDOW) | (i > 0))
    dist = dist_i.astype(F32)
    outs = []
    for hh in range(SWA_HEADS):
        g = hh // SWA_GROUP
        gs = slice(g * SWA_HEAD_DIM, (g + 1) * SWA_HEAD_DIM)
        qh = q_ref[:, hh * SWA_HEAD_DIM:(hh + 1) * SWA_HEAD_DIM]
        s = lax.dot_general(qh, kcat[:, gs], (((1,), (1,)), ((), ())), preferred_element_type=F32)
        s = s * (SWA_HEAD_DIM ** -0.5) - _alibi_slope(hh) * dist
        s = jnp.where(valid, s, -jnp.inf)
        outs.append(_swa_softmax_pv([(s, vcat[:, gs])], sink_ref[hh]))
    o_ref[...] = jnp.concatenate(outs, axis=-1).astype(o_ref.dtype)


def _swa_prompt(h, sinks, *, n_seq, seq_len, name):
    nq = seq_len // WINDOW
    qcol = H_QB // SWA_Q_DIM
    kcol = H_KB // SWA_KV_DIM
    vcol = H_VB // SWA_KV_DIM

    def cur(col):
        return lambda b, i, sk: (b * nq + i, col)

    def prev(col):
        return lambda b, i, sk: (b * nq + jnp.maximum(i - 1, 0), col)

    return pl.pallas_call(
        _swa_prompt_kernel,
        grid_spec=pltpu.PrefetchScalarGridSpec(
            num_scalar_prefetch=1,
            grid=(n_seq, nq),
            in_specs=[pl.BlockSpec((WINDOW, SWA_Q_DIM), cur(qcol)),
                      pl.BlockSpec((WINDOW, SWA_KV_DIM), prev(kcol)),
                      pl.BlockSpec((WINDOW, SWA_KV_DIM), prev(vcol)),
                      pl.BlockSpec((WINDOW, SWA_KV_DIM), cur(kcol)),
                      pl.BlockSpec((WINDOW, SWA_KV_DIM), cur(vcol))],
            out_specs=pl.BlockSpec((WINDOW, SWA_Q_DIM), lambda b, i, sk: (b * nq + i, 0)),
        ),
        out_shape=jax.ShapeDtypeStruct((n_seq * seq_len, SWA_Q_DIM), BF16),
        compiler_params=_cparams(("parallel", "arbitrary")),
        name=name,
    )(sinks, h, h, h, h, h)


def _swa_sample_kernel(sink_ref, q_ref, kc_ref, vc_ref, kp_ref, vp_ref, o_ref, *, nb, lq):
    qi = lax.broadcasted_iota(I32, (lq, WINDOW), 0)
    pc = lax.broadcasted_iota(I32, (lq, WINDOW), 1)
    dist_p = qi + WINDOW - pc
    valid_p = dist_p < WINDOW
    dist_p = dist_p.astype(F32)
    qc = lax.broadcasted_iota(I32, (lq, lq), 0)
    cc = lax.broadcasted_iota(I32, (lq, lq), 1)
    dist_c = qc - cc
    valid_c = dist_c >= 0
    dist_c = dist_c.astype(F32)
    q_all = q_ref[...].astype(F32)
    kc_all = kc_ref[...].astype(F32)
    vc_all = vc_ref[...].astype(F32)
    seq_outs = []
    for j in range(nb):
        js = slice(j * lq, (j + 1) * lq)
        kp = kp_ref[j].astype(BF16)
        vp = vp_ref[j].astype(BF16)
        outs = []
        for hh in range(SWA_HEADS):
            g = hh // SWA_GROUP
            gs = slice(g * SWA_HEAD_DIM, (g + 1) * SWA_HEAD_DIM)
            qh = q_all[js, hh * SWA_HEAD_DIM:(hh + 1) * SWA_HEAD_DIM].astype(BF16)
            kcj = kc_all[js, gs].astype(BF16)
            vcj = vc_all[js, gs].astype(BF16)
            scale = SWA_HEAD_DIM ** -0.5
            slope = _alibi_slope(hh)
            s_p = lax.dot_general(qh, kp[:, gs], (((1,), (1,)), ((), ())), preferred_element_type=F32)
            s_p = jnp.where(valid_p, s_p * scale - slope * dist_p, -jnp.inf)
            s_c = lax.dot_general(qh, kcj, (((1,), (1,)), ((), ())), preferred_element_type=F32)
            s_c = jnp.where(valid_c, s_c * scale - slope * dist_c, -jnp.inf)
            outs.append(_swa_softmax_pv([(s_p, vp[:, gs]), (s_c, vcj)], sink_ref[hh]))
        seq_outs.append(jnp.concatenate(outs, axis=-1))
    o_ref[...] = jnp.concatenate(seq_outs, axis=0).astype(o_ref.dtype)


def _swa_sample(h, sinks, k_past, v_past, *, row0, n_seq, lq, nb, name):
    rows = nb * lq
    rb0 = row0 // rows
    qcol = H_QB // SWA_Q_DIM
    kcol = H_KB // SWA_KV_DIM
    vcol = H_VB // SWA_KV_DIM
    kern = functools.partial(_swa_sample_kernel, nb=nb, lq=lq)
    past_spec = pl.BlockSpec((nb, WINDOW, SWA_KV_DIM), lambda g, sk: (g, 0, 0))
    return pl.pallas_call(
        kern,
        grid_spec=pltpu.PrefetchScalarGridSpec(
            num_scalar_prefetch=1,
            grid=(n_seq // nb,),
            in_specs=[pl.BlockSpec((rows, SWA_Q_DIM), lambda g, sk: (rb0 + g, qcol)),
                      pl.BlockSpec((rows, SWA_KV_DIM), lambda g, sk: (rb0 + g, kcol)),
                      pl.BlockSpec((rows, SWA_KV_DIM), lambda g, sk: (rb0 + g, vcol)),
                      past_spec, past_spec],
            out_specs=pl.BlockSpec((rows, SWA_Q_DIM), lambda g, sk: (g, 0)),
        ),
        out_shape=jax.ShapeDtypeStruct((n_seq * lq, SWA_Q_DIM), BF16),
        compiler_params=_cparams(("parallel",)),
        name=name,
    )(sinks, h, h, h, k_past, v_past)


def _layer_norm(x, g, b):
    mu = jnp.mean(x, axis=-1, keepdims=True)
    xc = x - mu
    var = jnp.mean(jnp.square(xc), axis=-1, keepdims=True)
    return xc * lax.rsqrt(var + EPS) * g + b


def _merge_kernel(oa_ref, ob_ref, ga_ref, gb_ref, x_ref, wpa_ref, wpb_ref, wout_ref, g_ref, b_ref, o_ref):
    br_a = jnp.dot(oa_ref[...], wpa_ref[...], preferred_element_type=F32)
    br_b = jnp.dot(ob_ref[...], wpb_ref[...], preferred_element_type=F32)
    merged = (jax.nn.sigmoid(ga_ref[...].astype(F32)) * br_a
              + jax.nn.sigmoid(gb_ref[...].astype(F32)) * br_b)
    mix = jnp.dot(merged.astype(BF16), wout_ref[...], preferred_element_type=F32)
    o_ref[...] = _layer_norm(DEEPNORM_ALPHA * x_ref[...] + mix, g_ref[...], b_ref[...])


def _merge(oa, ob, h, x, wpa, wpb, wout, g, b, *, tm, name):
    t = x.shape[0]
    row = lambda i: (i, 0)
    full = lambda i: (0, 0)
    return pl.pallas_call(
        _merge_kernel,
        grid=(t // tm,),
        in_specs=[pl.BlockSpec((tm, D_MODEL), row),
                  pl.BlockSpec((tm, D_MODEL), row),
                  pl.BlockSpec((tm, D_MODEL), lambda i: (i, H_GATE_A // D_MODEL)),
                  pl.BlockSpec((tm, D_MODEL), lambda i: (i, H_GATE_B // D_MODEL)),
                  pl.BlockSpec((tm, D_MODEL), row),
                  pl.BlockSpec((D_MODEL, D_MODEL), full),
                  pl.BlockSpec((D_MODEL, D_MODEL), full),
                  pl.BlockSpec((D_MODEL, D_MODEL), full),
                  pl.BlockSpec((1, D_MODEL), full),
                  pl.BlockSpec((1, D_MODEL), full)],
        out_specs=pl.BlockSpec((tm, D_MODEL), row),
        out_shape=jax.ShapeDtypeStruct((t, D_MODEL), F32),
        compiler_params=_cparams(("parallel",)),
        name=name,
    )(oa, ob, h, h, x, wpa, wpb, wout, g, b)


def _router_kernel(x_ref, whi_ref, wlo_ref, bias_ref, idx_ref, w_ref, rank_ref, cnt_ref, cnt_scr):
    i = pl.program_id(0)
    tm = x_ref.shape[0]

    @pl.when(i == 0)
    def _():
        cnt_scr[...] = jnp.zeros_like(cnt_scr)

    x_hi, x_lo = _split_bf16(x_ref[...])
    nt = (((1,), (1,)), ((), ()))
    logits = (lax.dot_general(whi_ref[...], x_hi, nt, preferred_element_type=F32)
              + lax.dot_general(whi_ref[...], x_lo, nt, preferred_element_type=F32)
              + lax.dot_general(wlo_ref[...], x_hi, nt, preferred_element_type=F32))
    scores = jax.nn.sigmoid(logits)
    biased = scores + bias_ref[...]

    grouped = biased.reshape(N_GROUPS, GROUP_SIZE, tm)
    m1 = jnp.max(grouped, axis=1)
    n_top = jnp.sum(jnp.where(grouped == m1[:, None, :], 1.0, 0.0), axis=1)
    m2 = jnp.max(jnp.where(grouped < m1[:, None, :], grouped, -jnp.inf), axis=1)
    gscore = m1 + jnp.where(n_top >= 2.0, m1, m2)

    g_iota = lax.broadcasted_iota(I32, (N_GROUPS, tm), 0)
    beaten = jnp.zeros((N_GROUPS, tm), I32)
    for g in range(N_GROUPS):
        other = gscore[g:g + 1, :]
        ahead = (other > gscore) | ((other == gscore) & (g < g_iota))
        beaten = beaten + jnp.where(ahead, 1, 0)
    keep = jnp.where(beaten < TOPK_GROUPS, 1.0, 0.0)
    masked = jnp.where(keep[:, None, :] > 0.5, grouped, -jnp.inf).reshape(N_EXPERTS, tm)

    e_iota = lax.broadcasted_iota(I32, (N_EXPERTS, tm), 0)
    sel_f = jnp.zeros((N_EXPERTS, tm), F32)
    ids = []
    for _ in range(TOP_K):
        best = jnp.max(masked, axis=0, keepdims=True)
        idx = jnp.min(jnp.where(masked == best, e_iota, N_EXPERTS), axis=0, keepdims=True)
        hit = e_iota == idx
        sel_f = sel_f + jnp.where(hit, 1.0, 0.0)
        masked = jnp.where(hit, -jnp.inf, masked)
        ids.append(idx)

    top_sum = jnp.sum(sel_f * scores, axis=0, keepdims=True)

    t_r = lax.broadcasted_iota(I32, (tm, tm), 0)
    t_c = lax.broadcasted_iota(I32, (tm, tm), 1)
    before = jnp.where(t_r < t_c, 1.0, 0.0).astype(BF16)
    rank = jnp.dot(sel_f.astype(BF16), before, preferred_element_type=F32) + cnt_scr[...]
    cnt_scr[...] = cnt_scr[...] + jnp.sum(sel_f, axis=1, keepdims=True)

    idx_rows, w_rows, rank_rows = [], [], []
    for idx in ids:
        hit = e_iota == idx
        w = jnp.sum(jnp.where(hit, scores, 0.0), axis=0, keepdims=True)
        w_rows.append(w / top_sum * ROUTED_SCALE)
        rank_rows.append(jnp.sum(jnp.where(hit, rank, 0.0), axis=0, keepdims=True))
        idx_rows.append(idx)
    idx_ref[...] = jnp.concatenate(idx_rows, axis=0)
    w_ref[...] = jnp.concatenate(w_rows, axis=0)
    rank_ref[...] = jnp.concatenate(rank_rows, axis=0).astype(I32)

    @pl.when(i == pl.num_programs(0) - 1)
    def _():
        cnt_ref[...] = cnt_scr[...].astype(I32)


def _router(x1, w_hi, w_lo, bias, *, tm, name):
    t = x1.shape[0]
    full = lambda i: (0, 0)
    tok = lambda i: (0, i)
    return pl.pallas_call(
        _router_kernel,
        grid=(t // tm,),
        in_specs=[pl.BlockSpec((tm, D_MODEL), lambda i: (i, 0)),
                  pl.BlockSpec((N_EXPERTS, D_MODEL), full),
                  pl.BlockSpec((N_EXPERTS, D_MODEL), full),
                  pl.BlockSpec((N_EXPERTS, 1), full)],
        out_specs=[pl.BlockSpec((TOP_K, tm), tok),
                   pl.BlockSpec((TOP_K, tm), tok),
                   pl.BlockSpec((TOP_K, tm), tok),
                   pl.BlockSpec((N_EXPERTS, 1), full)],
        out_shape=[jax.ShapeDtypeStruct((TOP_K, t), I32),
                   jax.ShapeDtypeStruct((TOP_K, t), F32),
                   jax.ShapeDtypeStruct((TOP_K, t), I32),
                   jax.ShapeDtypeStruct((N_EXPERTS, 1), I32)],
        scratch_shapes=[pltpu.VMEM((N_EXPERTS, 1), F32)],
        compiler_params=_cparams(("arbitrary",)),
        name=name,
    )(x1, w_hi, w_lo, bias)


def _plan_kernel(cnt_ref, idx_ref, rank_ref, dest_ref, blk_ref):
    tm = idx_ref.shape[1]
    n_blocks = blk_ref.shape[1]
    shift = EXPERT_ROWS.bit_length() - 1
    blocks = lax.shift_right_logical(cnt_ref[...] + (EXPERT_ROWS - 1), shift)
    b_hi = lax.shift_right_logical(blocks, 8).astype(F32)
    b_lo = (blocks & 255).astype(F32)
    r_i = lax.broadcasted_iota(I32, (N_EXPERTS, N_EXPERTS), 0)
    c_i = lax.broadcasted_iota(I32, (N_EXPERTS, N_EXPERTS), 1)
    incl = jnp.where(c_i <= r_i, 1.0, 0.0).astype(BF16)
    ones = jnp.ones((N_EXPERTS, LANES), F32)
    end_blk = (256.0 * jnp.dot(incl, (b_hi * ones).astype(BF16), preferred_element_type=F32)
               + jnp.dot(incl, (b_lo * ones).astype(BF16), preferred_element_type=F32))[:, 0:1]
    start_row = (end_blk - blocks.astype(F32)) * float(EXPERT_ROWS)

    e_iota = lax.broadcasted_iota(I32, (N_EXPERTS, tm), 0)
    rows = []
    for k in range(TOP_K):
        hit = e_iota == idx_ref[k:k + 1, :]
        base = jnp.sum(jnp.where(hit, start_row, 0.0), axis=0, keepdims=True)
        rows.append(base.astype(I32) + rank_ref[k:k + 1, :])
    dest_ref[...] = jnp.concatenate(rows, axis=0)

    b_iota = lax.broadcasted_iota(I32, (N_EXPERTS, n_blocks), 1).astype(F32)
    blk_e = jnp.sum(jnp.where(end_blk <= b_iota, 1.0, 0.0), axis=0, keepdims=True)
    used = jnp.max(end_blk, axis=0, keepdims=True)
    blk_ref[...] = jnp.concatenate(
        [jnp.minimum(blk_e, N_EXPERTS - 1.0), jnp.broadcast_to(used, (1, n_blocks))], axis=0).astype(I32)


def _plan(counts, idx, rank, *, tm, n_blocks, name):
    t = idx.shape[1]
    tok = lambda i: (0, i)
    kern = _plan_kernel
    return pl.pallas_call(
        kern,
        grid=(t // tm,),
        in_specs=[pl.BlockSpec((N_EXPERTS, 1), lambda i: (0, 0)),
                  pl.BlockSpec((TOP_K, tm), tok),
                  pl.BlockSpec((TOP_K, tm), tok)],
        out_specs=[pl.BlockSpec((TOP_K, tm), tok),
                   pl.BlockSpec((2, n_blocks), lambda i: (0, 0))],
        out_shape=[jax.ShapeDtypeStruct((TOP_K, t), I32),
                   jax.ShapeDtypeStruct((2, n_blocks), I32)],
        compiler_params=_cparams(("arbitrary",)),
        name=name,
    )(counts, idx, rank)


def _dispatch_kernel(dest_hbm, x_ref, zero_hbm, xs_hbm, dest_smem, sem_idx, sem_rows, *, tm):
    del zero_hbm
    i = pl.program_id(0)
    n = TOP_K * tm
    cp = pltpu.make_async_copy(dest_hbm.at[pl.ds(pl.multiple_of(i * n, n), n)], dest_smem, sem_idx)
    cp.start()
    cp.wait()

    def row_copy(t, k):
        return pltpu.make_async_copy(x_ref.at[t], xs_hbm.at[dest_smem[k * tm + t]], sem_rows)

    def issue(t, carry):
        for k in range(TOP_K):
            row_copy(t, k).start()
        return carry

    lax.fori_loop(0, tm, issue, 0)

    def drain(t, carry):
        for k in range(TOP_K):
            row_copy(t, k).wait()
        return carry

    lax.fori_loop(0, tm, drain, 0)


def _dispatch(dest_flat, x3, n_rows, *, tm, name):
    t = x3.shape[0]
    zeros = jnp.zeros((n_rows, ROW_CHUNKS, LANES), F32)
    kern = functools.partial(_dispatch_kernel, tm=tm)
    return pl.pallas_call(
        kern,
        grid=(t // tm,),
        in_specs=[pl.BlockSpec(memory_space=pl.ANY),
                  pl.BlockSpec((tm, ROW_CHUNKS, LANES), lambda i: (i, 0, 0)),
                  pl.BlockSpec(memory_space=pl.ANY)],
        out_specs=pl.BlockSpec(memory_space=pl.ANY),
        out_shape=jax.ShapeDtypeStruct((n_rows, ROW_CHUNKS, LANES), F32),
        scratch_shapes=[pltpu.SMEM((TOP_K * tm,), I32),
                        pltpu.SemaphoreType.DMA,
                        pltpu.SemaphoreType.DMA],
        input_output_aliases={2: 0},
        compiler_params=_cparams(("arbitrary",)),
        name=name,
    )(dest_flat, x3, zeros)


def _rows_to_2d(ref):
    return jnp.concatenate([ref[:, j, :] for j in range(ROW_CHUNKS)], axis=-1)


def _experts_kernel(blk_ref, xs_ref, wg_ref, wu_ref, wd_ref, ys_ref, *, used_at):
    b = pl.program_id(0)

    @pl.when(b < blk_ref[used_at])
    def _():
        x = _rows_to_2d(xs_ref).astype(BF16)
        gate = jnp.dot(x, wg_ref[0].astype(BF16), preferred_element_type=F32)
        up = jnp.dot(x, wu_ref[0].astype(BF16), preferred_element_type=F32)
        hid = (jax.nn.silu(gate) * up).astype(BF16)
        y = jnp.dot(hid, wd_ref[0].astype(BF16), preferred_element_type=F32)
        for j in range(ROW_CHUNKS):
            ys_ref[:, j, :] = y[:, j * LANES:(j + 1) * LANES]

    @pl.when(b >= blk_ref[used_at])
    def _():
        ys_ref[...] = jnp.zeros_like(ys_ref)


def _experts(blk, xs, wg, wu, wd, *, n_blocks, name):
    def wmap(b, blk_ref):
        return (blk_ref[b], 0, 0)

    def rmap(b, blk_ref):
        return (b, 0, 0)

    return pl.pallas_call(
        functools.partial(_experts_kernel, used_at=blk.shape[0] // 2),
        grid_spec=pltpu.PrefetchScalarGridSpec(
            num_scalar_prefetch=1,
            grid=(n_blocks,),
            in_specs=[pl.BlockSpec((EXPERT_ROWS, ROW_CHUNKS, LANES), rmap),
                      pl.BlockSpec((1, D_MODEL, EXPERT_FF), wmap),
                      pl.BlockSpec((1, D_MODEL, EXPERT_FF), wmap),
                      pl.BlockSpec((1, EXPERT_FF, D_MODEL), wmap)],
            out_specs=pl.BlockSpec((EXPERT_ROWS, ROW_CHUNKS, LANES), rmap),
        ),
        out_shape=jax.ShapeDtypeStruct(xs.shape, F32),
        compiler_params=_cparams(("arbitrary",)),
        name=name,
    )(blk, xs, wg, wu, wd)


def _combine_kernel(dest_hbm, ys_hbm, x_ref, w_ref, wsg_ref, wsu_ref, wsd_ref, g_ref, b_ref, o_ref,
                    dest_smem, buf, sem_idx, sem_rows, *, tm):
    i = pl.program_id(0)
    n = TOP_K * tm
    cp = pltpu.make_async_copy(dest_hbm.at[pl.ds(pl.multiple_of(i * n, n), n)], dest_smem, sem_idx)
    cp.start()
    cp.wait()

    def row_copy(t, k):
        return pltpu.make_async_copy(ys_hbm.at[dest_smem[k * tm + t]], buf.at[k * tm + t], sem_rows)

    def issue(t, carry):
        for k in range(TOP_K):
            row_copy(t, k).start()
        return carry

    lax.fori_loop(0, tm, issue, 0)

    x = x_ref[...]
    xb = x.astype(BF16)
    hid = (jax.nn.silu(jnp.dot(xb, wsg_ref[...], preferred_element_type=F32))
           * jnp.dot(xb, wsu_ref[...], preferred_element_type=F32))
    shared = jnp.dot(hid.astype(BF16), wsd_ref[...], preferred_element_type=F32)

    def drain(t, carry):
        for k in range(TOP_K):
            row_copy(t, k).wait()
        return carry

    lax.fori_loop(0, tm, drain, 0)

    w = w_ref[...]
    chunks = []
    for j in range(ROW_CHUNKS):
        acc = None
        for k in range(TOP_K):
            part = w[:, k:k + 1] * buf[pl.ds(k * tm, tm), j, :]
            acc = part if acc is None else acc + part
        chunks.append(acc)
    ffn = jnp.concatenate(chunks, axis=-1) + shared
    o_ref[...] = _layer_norm(DEEPNORM_ALPHA * x + ffn, g_ref[...], b_ref[...])


def _combine(dest_flat, ys, x1, w_tok, wsg, wsu, wsd, g, b, *, tm, name):
    t = x1.shape[0]
    row = lambda i: (i, 0)
    full = lambda i: (0, 0)
    kern = functools.partial(_combine_kernel, tm=tm)
    return pl.pallas_call(
        kern,
        grid=(t // tm,),
        in_specs=[pl.BlockSpec(memory_space=pl.ANY),
                  pl.BlockSpec(memory_space=pl.ANY),
                  pl.BlockSpec((tm, D_MODEL), row),
                  pl.BlockSpec((tm, TOP_K), row),
                  pl.BlockSpec((D_MODEL, SHARED_FF), full),
                  pl.BlockSpec((D_MODEL, SHARED_FF), full),
                  pl.BlockSpec((SHARED_FF, D_MODEL), full),
                  pl.BlockSpec((1, D_MODEL), full),
                  pl.BlockSpec((1, D_MODEL), full)],
        out_specs=pl.BlockSpec((tm, D_MODEL), row),
        out_shape=jax.ShapeDtypeStruct((t, D_MODEL), F32),
        scratch_shapes=[pltpu.SMEM((TOP_K * tm,), I32),
                        pltpu.VMEM((TOP_K * tm, ROW_CHUNKS, LANES), F32),
                        pltpu.SemaphoreType.DMA,
                        pltpu.SemaphoreType.DMA],
        compiler_params=_cparams(("arbitrary",)),
        name=name,
    )(dest_flat, ys, x1, w_tok, wsg, wsu, wsd, g, b)


def _tile_major(a, tm):
    k, t = a.shape
    return a.reshape(k, t // tm, tm).transpose(1, 0, 2).reshape(-1)


def _pack_w_in(w_in):
    sizes = (GLA_KEY_DIM, GLA_KEY_DIM, GLA_VAL_DIM, GLA_VAL_DIM, GLA_GATE_RANK,
             SWA_Q_DIM, SWA_KV_DIM, SWA_KV_DIM, D_MODEL, D_MODEL)
    offs = [0]
    for s in sizes:
        offs.append(offs[-1] + s)
    qa, ka, va, ga, gk, qb, kb, vb, gate_a, gate_b = (w_in[:, offs[i]:offs[i + 1]] for i in range(10))
    pad = lambda w, n: jnp.pad(w, ((0, 0), (0, n - w.shape[1])))
    packed = jnp.concatenate([qa, ka, va, ga, qb, gate_a, gate_b, kb, vb, pad(gk, 2 * LANES)], axis=1)
    assert packed.shape[1] == H_WIDTH
    return packed.astype(BF16), jnp.concatenate([kb, vb], axis=1).astype(BF16)


def kernel(x_prompt, x_sample, state_gla, cache_swa_k, cache_swa_v, w_in, w_gk_up, b_gk, gla_norm_g,
           attn_sinks, w_proj_a, w_proj_b, w_out, ln1_g, ln1_b, w_router, router_bias,
           w_expert_gate, w_expert_up, w_expert_down, w_shared_gate, w_shared_up, w_shared_down,
           ln2_g, ln2_b):
    assert w_in.shape[0] == 1, "single-layer trunk"
    bp, lp, d = x_prompt.shape
    bs, ls, _ = x_sample.shape
    assert d == D_MODEL and ls == SUBLANES and cache_swa_k.shape[2] == WINDOW
    tp, ts = bp * lp, bs * ls
    t = tp + ts

    x = jnp.concatenate([x_prompt.reshape(tp, d), x_sample.reshape(ts, d)], axis=0)
    w_main, w_kv = _pack_w_in(w_in[0])
    tm_proj = _tile(t, 1024)
    h = _matmul(x, w_main, BF16, tm_proj, H_TN, "proj_in")

    x_tail = jnp.concatenate([x_prompt[:, lp - WINDOW:].reshape(bp * WINDOW, d), x.reshape(t, d)[tp:]], axis=0)
    kv_tail = _matmul(x_tail, w_kv, F32, _tile(x_tail.shape[0], 512), 2 * SWA_KV_DIM, "proj_kv_tail")

    wup = jnp.pad(w_gk_up[0], ((0, LANES - GLA_GATE_RANK), (0, 0))).astype(BF16)
    bgk = b_gk[0].reshape(1, GLA_KEY_DIM)
    gn = gla_norm_g[0].reshape(1, GLA_DV)
    oa_p, s_prompt = _gla(h, wup, bgk, gn, None, row0=0, n_seq=bp, seq_len=lp, nb=1, c=GLA_CHUNK,
                          sub=4, name="gla_prompt")
    oa_s, s_sample = _gla(h, wup, bgk, gn, state_gla[0], row0=tp, n_seq=bs, seq_len=ls, nb=8,
                          c=math.gcd(ls, GLA_CHUNK), sub=1, name="gla_sample")

    sinks = attn_sinks[0]
    k_past = cache_swa_k[0].reshape(bs, WINDOW, SWA_KV_DIM)
    v_past = cache_swa_v[0].reshape(bs, WINDOW, SWA_KV_DIM)
    ob_p = _swa_prompt(h, sinks, n_seq=bp, seq_len=lp, name="swa_prompt")
    ob_s = _swa_sample(h, sinks, k_past, v_past, row0=tp, n_seq=bs, lq=ls, nb=8, name="swa_sample")

    oa = jnp.concatenate([oa_p, oa_s], axis=0)
    ob = jnp.concatenate([ob_p, ob_s], axis=0)
    x1 = _merge(oa, ob, h, x, w_proj_a[0].astype(BF16), w_proj_b[0].astype(BF16), w_out[0].astype(BF16),
                ln1_g[0].reshape(1, d), ln1_b[0].reshape(1, d), tm=_tile(t, 512), name="merge_ln1")

    wr_t = w_router[0].T
    wr_hi = wr_t.astype(BF16)
    wr_lo = (wr_t - wr_hi.astype(F32)).astype(BF16)
    tm_r = _tile(t, 512)
    idx, w_top, rank, counts = _router(x1, wr_hi, wr_lo, router_bias[0].reshape(N_EXPERTS, 1),
                                       tm=tm_r, name="router")
    n_blocks = -(-(t * TOP_K) // EXPERT_ROWS) + N_EXPERTS
    n_pad = -(-n_blocks // LANES) * LANES
    dest, blk = _plan(counts, idx, rank, tm=tm_r, n_blocks=n_pad, name="plan")

    tm_d = _tile(t, 256)
    x3 = x1.reshape(t, ROW_CHUNKS, LANES)
    xs = _dispatch(_tile_major(dest, tm_d), x3, n_blocks * EXPERT_ROWS, tm=tm_d, name="dispatch")
    ys = _experts(blk.reshape(-1), xs, w_expert_gate[0], w_expert_up[0], w_expert_down[0],
                  n_blocks=n_blocks, name="experts")
    tm_c = _tile(t, 128)
    y = _combine(_tile_major(dest, tm_c), ys, x1, w_top.T,
                 w_shared_gate[0].astype(BF16), w_shared_up[0].astype(BF16), w_shared_down[0].astype(BF16),
                 ln2_g[0].reshape(1, d), ln2_b[0].reshape(1, d), tm=tm_c, name="combine_ln2")

    y_prompt = y[:tp].reshape(bp, lp, d)
    y_sample = y[tp:].reshape(bs, ls, d)
    k_tail = kv_tail[:, :SWA_KV_DIM]
    v_tail = kv_tail[:, SWA_KV_DIM:]
    kv_shape = (SWA_KV_HEADS, SWA_HEAD_DIM)
    k_prompt = k_tail[:bp * WINDOW].reshape(1, bp, WINDOW, *kv_shape)
    v_prompt = v_tail[:bp * WINDOW].reshape(1, bp, WINDOW, *kv_shape)
    k_new = k_tail[bp * WINDOW:].reshape(bs, ls, *kv_shape)
    v_new = v_tail[bp * WINDOW:].reshape(bs, ls, *kv_shape)
    k_sample = jnp.concatenate([cache_swa_k[0][:, ls:], k_new], axis=1)[None]
    v_sample = jnp.concatenate([cache_swa_v[0][:, ls:], v_new], axis=1)[None]
    return (y_prompt, y_sample, s_prompt[None], s_sample[None], k_prompt, v_prompt, k_sample, v_sample)
```

```python
import functools
import math

import jax
import jax.numpy as jnp
from jax import lax
from jax.experimental import pallas as pl
from jax.experimental.pallas import tpu as pltpu

F32 = jnp.float32
BF16 = jnp.bfloat16
I32 = jnp.int32

D_MODEL = 1024
GLA_HEADS = 4
GLA_DK = 128
GLA_DV = 256
GLA_KEY_DIM = GLA_HEADS * GLA_DK
GLA_VAL_DIM = GLA_HEADS * GLA_DV
GLA_GATE_RANK = 16
GLA_GATE_NORMALIZER = 16.0
GLA_CHUNK = 64
SWA_HEADS = 16
SWA_KV_HEADS = 4
SWA_GROUP = SWA_HEADS // SWA_KV_HEADS
SWA_HEAD_DIM = 64
SWA_Q_DIM = SWA_HEADS * SWA_HEAD_DIM
SWA_KV_DIM = SWA_KV_HEADS * SWA_HEAD_DIM
WINDOW = 128
N_EXPERTS = 256
TOP_K = 8
N_GROUPS = 8
GROUP_SIZE = N_EXPERTS // N_GROUPS
TOPK_GROUPS = 4
EXPERT_FF = 256
SHARED_FF = 256
ROUTED_SCALE = 2.5
DEEPNORM_ALPHA = 2.0 ** 0.25
EPS = 1e-5

LANES = 128
SUBLANES = 8
ROW_CHUNKS = D_MODEL // LANES
VMEM_LIMIT = 56 * 1024 * 1024

H_QA, H_KA, H_VA, H_GA, H_QB, H_GATE_A, H_GATE_B, H_KB, H_VB, H_GK = (
    0, 512, 1024, 2048, 3072, 4096, 5120, 6144, 6400, 6656)
H_WIDTH = 6912
H_TN = 2304

EXPERT_ROWS = 256
PACKED_ROWS = ROW_CHUNKS // 2
HIGH_HALF = -65536


def _cparams(sem, vmem=VMEM_LIMIT):
    return pltpu.CompilerParams(dimension_semantics=sem, vmem_limit_bytes=vmem)


def _tile(n, pref):
    t = min(n, pref)
    while n % t:
        t -= LANES
    assert t > 0 and t % LANES == 0, (n, pref)
    return t


def _mm_kernel(x_ref, w_ref, o_ref):
    o_ref[...] = jnp.dot(x_ref[...].astype(BF16), w_ref[...],
                         preferred_element_type=F32).astype(o_ref.dtype)


def _matmul(x, w, out_dtype, tm, tn, name):
    m, k = x.shape
    n = w.shape[1]
    return pl.pallas_call(
        _mm_kernel,
        grid=(m // tm, n // tn),
        in_specs=[pl.BlockSpec((tm, k), lambda i, j: (i, 0)),
                  pl.BlockSpec((k, tn), lambda i, j: (0, j))],
        out_specs=pl.BlockSpec((tm, tn), lambda i, j: (i, j)),
        out_shape=jax.ShapeDtypeStruct((m, n), out_dtype),
        compiler_params=_cparams(("parallel", "arbitrary")),
        name=name,
    )(x, w)


def _split_bf16(x):
    hi = x.astype(BF16)
    lo = (x - hi.astype(F32)).astype(BF16)
    return hi, lo


def _gla_kernel(*refs, nb, c, sub, has_s0):
    if has_s0:
        (q_ref, k_ref, v_ref, ga_ref, gk_ref, wup_ref, bgk_ref, gn_ref, s0_ref,
         o_ref, sout_ref, s_scr) = refs
    else:
        (q_ref, k_ref, v_ref, ga_ref, gk_ref, wup_ref, bgk_ref, gn_ref,
         o_ref, sout_ref, s_scr) = refs
        s0_ref = None
    ci = pl.program_id(1)
    rows = nb * c

    @pl.when(ci == 0)
    def _():
        if has_s0:
            s_scr[...] = s0_ref[...]
        else:
            s_scr[...] = jnp.zeros_like(s_scr)

    r_i = lax.broadcasted_iota(I32, (rows, rows), 0)
    c_i = lax.broadcasted_iota(I32, (rows, rows), 1)
    same_seq = (r_i // c) == (c_i // c)
    causal = same_seq & (c_i <= r_i)
    tri = jnp.where(causal, 1.0, 0.0).astype(BF16)
    seg = jnp.where(same_seq, 1.0, 0.0).astype(BF16)
    ones_kv = jnp.ones((rows, GLA_DV), BF16)
    seq_of_row = lax.broadcasted_iota(I32, (rows, 1), 0) // c
    tn = (((0,), (0,)), ((), ()))

    for s in range(sub):
        rs = slice(s * rows, (s + 1) * rows)
        pre = jnp.dot(gk_ref[rs, :], wup_ref[...], preferred_element_type=F32) + bgk_ref[...]
        log_a = jax.nn.log_sigmoid(pre) / GLA_GATE_NORMALIZER
        la_hi, la_lo = _split_bf16(log_a)
        b = (jnp.dot(tri, la_hi, preferred_element_type=F32)
             + jnp.dot(tri, la_lo, preferred_element_type=F32))
        b_last = (jnp.dot(seg, la_hi, preferred_element_type=F32)
                  + jnp.dot(seg, la_lo, preferred_element_type=F32))
        e_pos = jnp.exp(b)
        e_neg = jnp.exp(-b)
        e_rem = jnp.exp(b_last - b)
        q = q_ref[rs, :].astype(F32) * (GLA_DK ** -0.5)
        k = k_ref[rs, :].astype(F32)
        q_dec = (q * e_pos).astype(BF16)
        k_dec = (k * e_neg).astype(BF16)
        k_rem = k * e_rem
        outs = []
        for h in range(GLA_HEADS):
            ks = slice(h * GLA_DK, (h + 1) * GLA_DK)
            vh = v_ref[rs, h * GLA_DV:(h + 1) * GLA_DV]
            att = lax.dot_general(q_dec[:, ks], k_dec[:, ks], (((1,), (1,)), ((), ())),
                                  preferred_element_type=F32)
            att = jnp.where(causal, att, 0.0).astype(BF16)
            o_h = jnp.dot(att, vh, preferred_element_type=F32)
            for j in range(nb):
                mine = (seq_of_row == j) if nb > 1 else None
                pick = (lambda a: jnp.where(mine, a, 0.0)) if nb > 1 else (lambda a: a)
                s_old = s_scr[j, h]
                o_h = o_h + pick(jnp.dot(q_dec[:, ks], s_old.astype(BF16), preferred_element_type=F32))
                dec = (lax.dot_general(pick(la_hi[:, ks].astype(F32)).astype(BF16), ones_kv, tn,
                                       preferred_element_type=F32)
                       + lax.dot_general(pick(la_lo[:, ks].astype(F32)).astype(BF16), ones_kv, tn,
                                         preferred_element_type=F32))
                upd = lax.dot_general(pick(k_rem[:, ks]).astype(BF16), vh, tn, preferred_element_type=F32)
                s_scr[j, h] = jnp.exp(dec) * s_old + upd
            o_h = o_h * lax.rsqrt(jnp.mean(jnp.square(o_h), axis=-1, keepdims=True) + EPS) * gn_ref[...]
            outs.append(o_h)
        o = jnp.concatenate(outs, axis=-1) * jax.nn.silu(ga_ref[rs, :].astype(F32))
        o_ref[rs, :] = o.astype(o_ref.dtype)

    @pl.when(ci == pl.num_programs(1) - 1)
    def _():
        sout_ref[...] = s_scr[...]


def _gla(h, wup, bgk, gn, s0, *, row0, n_seq, seq_len, nb, c, sub, name):
    rows = nb * c * sub
    n_groups = n_seq // nb
    n_steps = seq_len // (c * sub)
    rb0 = row0 // rows

    def rmap(col):
        return lambda g, i: (rb0 + g * n_steps + i, col)

    in_specs = [
        pl.BlockSpec((rows, GLA_KEY_DIM), rmap(H_QA // GLA_KEY_DIM)),
        pl.BlockSpec((rows, GLA_KEY_DIM), rmap(H_KA // GLA_KEY_DIM)),
        pl.BlockSpec((rows, GLA_VAL_DIM), rmap(H_VA // GLA_VAL_DIM)),
        pl.BlockSpec((rows, GLA_VAL_DIM), rmap(H_GA // GLA_VAL_DIM)),
        pl.BlockSpec((rows, LANES), rmap(H_GK // LANES)),
        pl.BlockSpec((LANES, GLA_KEY_DIM), lambda g, i: (0, 0)),
        pl.BlockSpec((1, GLA_KEY_DIM), lambda g, i: (0, 0)),
        pl.BlockSpec((1, GLA_DV), lambda g, i: (0, 0)),
    ]
    args = [h, h, h, h, h, wup, bgk, gn]
    state_spec = pl.BlockSpec((nb, GLA_HEADS, GLA_DK, GLA_DV), lambda g, i: (g, 0, 0, 0))
    if s0 is not None:
        in_specs.append(state_spec)
        args.append(s0)
    kern = functools.partial(_gla_kernel, nb=nb, c=c, sub=sub, has_s0=s0 is not None)
    return pl.pallas_call(
        kern,
        grid=(n_groups, n_steps),
        in_specs=in_specs,
        out_specs=[pl.BlockSpec((rows, GLA_VAL_DIM), lambda g, i: (g * n_steps + i, 0)), state_spec],
        out_shape=[jax.ShapeDtypeStruct((n_seq * seq_len, GLA_VAL_DIM), BF16),
                   jax.ShapeDtypeStruct((n_seq, GLA_HEADS, GLA_DK, GLA_DV), F32)],
        scratch_shapes=[pltpu.VMEM((nb, GLA_HEADS, GLA_DK, GLA_DV), F32)],
        compiler_params=_cparams(("parallel", "arbitrary")),
        name=name,
    )(*args)


def _alibi_slope(head):
    return 2.0 ** (-8.0 * (head + 1) / SWA_HEADS)


def _swa_softmax_pv(parts, sink):
    m = sink
    for s, _ in parts:
        m = jnp.maximum(m, jnp.max(s, axis=-1, keepdims=True))
    denom = jnp.exp(sink - m)
    acc = None
    for s, v in parts:
        p = jnp.exp(s - m)
        denom = denom + jnp.sum(p, axis=-1, keepdims=True)
        pv = jnp.dot(p.astype(BF16), v, preferred_element_type=F32)
        acc = pv if acc is None else acc + pv
    return acc / denom


def _swa_prompt_kernel(sink_ref, q_ref, kp_ref, vp_ref, kc_ref, vc_ref, o_ref):
    i = pl.program_id(1)
    span = 2 * WINDOW
    kcat = jnp.concatenate([kp_ref[...], kc_ref[...]], axis=0)
    vcat = jnp.concatenate([vp_ref[...], vc_ref[...]], axis=0)
    row = lax.broadcasted_iota(I32, (WINDOW, span), 0)
    col = lax.broadcasted_iota(I32, (WINDOW, span), 1)
    dist_i = row + WINDOW - col
    valid = (dist_i >= 0) & (dist_i < WINDOW) & ((col >= WINDOW) | (i > 0))
    dist = dist_i.astype(F32)
    outs = []
    for hh in range(SWA_HEADS):
        g = hh // SWA_GROUP
        gs = slice(g * SWA_HEAD_DIM, (g + 1) * SWA_HEAD_DIM)
        qh = q_ref[:, hh * SWA_HEAD_DIM:(hh + 1) * SWA_HEAD_DIM]
        s = lax.dot_general(qh, kcat[:, gs], (((1,), (1,)), ((), ())), preferred_element_type=F32)
        s = s * (SWA_HEAD_DIM ** -0.5) - _alibi_slope(hh) * dist
        s = jnp.where(valid, s, -jnp.inf)
        outs.append(_swa_softmax_pv([(s, vcat[:, gs])], sink_ref[hh]))
    o_ref[...] = jnp.concatenate(outs, axis=-1).astype(o_ref.dtype)


def _swa_prompt(h, sinks, *, n_seq, seq_len, name):
    nq = seq_len // WINDOW
    qcol = H_QB // SWA_Q_DIM
    kcol = H_KB // SWA_KV_DIM
    vcol = H_VB // SWA_KV_DIM

    def cur(col):
        return lambda b, i, sk: (b * nq + i, col)

    def prev(col):
        return lambda b, i, sk: (b * nq + jnp.maximum(i - 1, 0), col)

    return pl.pallas_call(
        _swa_prompt_kernel,
        grid_spec=pltpu.PrefetchScalarGridSpec(
            num_scalar_prefetch=1,
            grid=(n_seq, nq),
            in_specs=[pl.BlockSpec((WINDOW, SWA_Q_DIM), cur(qcol)),
                      pl.BlockSpec((WINDOW, SWA_KV_DIM), prev(kcol)),
                      pl.BlockSpec((WINDOW, SWA_KV_DIM), prev(vcol)),
                      pl.BlockSpec((WINDOW, SWA_KV_DIM), cur(kcol)),
                      pl.BlockSpec((WINDOW, SWA_KV_DIM), cur(vcol))],
            out_specs=pl.BlockSpec((WINDOW, SWA_Q_DIM), lambda b, i, sk: (b * nq + i, 0)),
        ),
        out_shape=jax.ShapeDtypeStruct((n_seq * seq_len, SWA_Q_DIM), BF16),
        compiler_params=_cparams(("parallel", "arbitrary")),
        name=name,
    )(sinks, h, h, h, h, h)


def _swa_sample_kernel(sink_ref, q_ref, kc_ref, vc_ref, kp_ref, vp_ref, o_ref, *, nb, lq):
    qi = lax.broadcasted_iota(I32, (lq, WINDOW), 0)
    pc = lax.broadcasted_iota(I32, (lq, WINDOW), 1)
    dist_p = qi + WINDOW - pc
    valid_p = dist_p < WINDOW
    dist_p = dist_p.astype(F32)
    qc = lax.broadcasted_iota(I32, (lq, lq), 0)
    cc = lax.broadcasted_iota(I32, (lq, lq), 1)
    dist_c = qc - cc
    valid_c = dist_c >= 0
    dist_c = dist_c.astype(F32)
    q_all = q_ref[...].astype(F32)
    kc_all = kc_ref[...].astype(F32)
    vc_all = vc_ref[...].astype(F32)
    seq_outs = []
    for j in range(nb):
        js = slice(j * lq, (j + 1) * lq)
        kp = kp_ref[j].astype(BF16)
        vp = vp_ref[j].astype(BF16)
        outs = []
        for hh in range(SWA_HEADS):
            g = hh // SWA_GROUP
            gs = slice(g * SWA_HEAD_DIM, (g + 1) * SWA_HEAD_DIM)
            qh = q_all[js, hh * SWA_HEAD_DIM:(hh + 1) * SWA_HEAD_DIM].astype(BF16)
            kcj = kc_all[js, gs].astype(BF16)
            vcj = vc_all[js, gs].astype(BF16)
            scale = SWA_HEAD_DIM ** -0.5
            slope = _alibi_slope(hh)
            s_p = lax.dot_general(qh, kp[:, gs], (((1,), (1,)), ((), ())), preferred_element_type=F32)
            s_p = jnp.where(valid_p, s_p * scale - slope * dist_p, -jnp.inf)
            s_c = lax.dot_general(qh, kcj, (((1,), (1,)), ((), ())), preferred_element_type=F32)
            s_c = jnp.where(valid_c, s_c * scale - slope * dist_c, -jnp.inf)
            outs.append(_swa_softmax_pv([(s_p, vp[:, gs]), (s_c, vcj)], sink_ref[hh]))
        seq_outs.append(jnp.concatenate(outs, axis=-1))
    o_ref[...] = jnp.concatenate(seq_outs, axis=0).astype(o_ref.dtype)


def _swa_sample(h, sinks, k_past, v_past, *, row0, n_seq, lq, nb, name):
    rows = nb * lq
    rb0 = row0 // rows
    qcol = H_QB // SWA_Q_DIM
    kcol = H_KB // SWA_KV_DIM
    vcol = H_VB // SWA_KV_DIM
    kern = functools.partial(_swa_sample_kernel, nb=nb, lq=lq)
    past_spec = pl.BlockSpec((nb, WINDOW, SWA_KV_DIM), lambda g, sk: (g, 0, 0))
    return pl.pallas_call(
        kern,
        grid_spec=pltpu.PrefetchScalarGridSpec(
            num_scalar_prefetch=1,
            grid=(n_seq // nb,),
            in_specs=[pl.BlockSpec((rows, SWA_Q_DIM), lambda g, sk: (rb0 + g, qcol)),
                      pl.BlockSpec((rows, SWA_KV_DIM), lambda g, sk: (rb0 + g, kcol)),
                      pl.BlockSpec((rows, SWA_KV_DIM), lambda g, sk: (rb0 + g, vcol)),
                      past_spec, past_spec],
            out_specs=pl.BlockSpec((rows, SWA_Q_DIM), lambda g, sk: (g, 0)),
        ),
        out_shape=jax.ShapeDtypeStruct((n_seq * lq, SWA_Q_DIM), BF16),
        compiler_params=_cparams(("parallel",)),
        name=name,
    )(sinks, h, h, h, k_past, v_past)


def _layer_norm(x, g, b):
    mu = jnp.mean(x, axis=-1, keepdims=True)
    xc = x - mu
    var = jnp.mean(jnp.square(xc), axis=-1, keepdims=True)
    return xc * lax.rsqrt(var + EPS) * g + b


def _pack_bf16_pairs(x, s):
    lo = lax.bitcast_convert_type(x[:, (2 * s) * LANES:(2 * s + 1) * LANES].astype(BF16).astype(F32), I32)
    hi = lax.bitcast_convert_type(x[:, (2 * s + 1) * LANES:(2 * s + 2) * LANES].astype(BF16).astype(F32), I32)
    return lax.shift_right_logical(lo, 16) | (hi & HIGH_HALF)


def _unpack_bf16_pairs(w):
    lo = lax.bitcast_convert_type(lax.shift_left(w, 16), F32).astype(BF16)
    hi = lax.bitcast_convert_type(w & HIGH_HALF, F32).astype(BF16)
    return lo, hi


def _merge_kernel(oa_ref, ob_ref, ga_ref, gb_ref, x_ref, wpa_ref, wpb_ref, wout_ref, g_ref, b_ref,
                  o_ref, xw_ref):
    tm = o_ref.shape[0]
    br_a = jnp.dot(oa_ref[...], wpa_ref[...], preferred_element_type=F32)
    br_b = jnp.dot(ob_ref[...], wpb_ref[...], preferred_element_type=F32)
    merged = (jax.nn.sigmoid(ga_ref[...].astype(F32)) * br_a
              + jax.nn.sigmoid(gb_ref[...].astype(F32)) * br_b)
    mix = jnp.dot(merged.astype(BF16), wout_ref[...], preferred_element_type=F32)
    x1 = _layer_norm(DEEPNORM_ALPHA * x_ref[...] + mix, g_ref[...], b_ref[...])
    o_ref[...] = x1
    for s in range(PACKED_ROWS):
        xw_ref[pl.ds(s, tm, stride=PACKED_ROWS), :] = _pack_bf16_pairs(x1, s)


def _merge(oa, ob, h, x, wpa, wpb, wout, g, b, *, tm, name):
    t = x.shape[0]
    row = lambda i: (i, 0)
    full = lambda i: (0, 0)
    return pl.pallas_call(
        _merge_kernel,
        grid=(t // tm,),
        in_specs=[pl.BlockSpec((tm, D_MODEL), row),
                  pl.BlockSpec((tm, D_MODEL), row),
                  pl.BlockSpec((tm, D_MODEL), lambda i: (i, H_GATE_A // D_MODEL)),
                  pl.BlockSpec((tm, D_MODEL), lambda i: (i, H_GATE_B // D_MODEL)),
                  pl.BlockSpec((tm, D_MODEL), row),
                  pl.BlockSpec((D_MODEL, D_MODEL), full),
                  pl.BlockSpec((D_MODEL, D_MODEL), full),
                  pl.BlockSpec((D_MODEL, D_MODEL), full),
                  pl.BlockSpec((1, D_MODEL), full),
                  pl.BlockSpec((1, D_MODEL), full)],
        out_specs=[pl.BlockSpec((tm, D_MODEL), row),
                   pl.BlockSpec((tm * PACKED_ROWS, LANES), row)],
        out_shape=[jax.ShapeDtypeStruct((t, D_MODEL), F32),
                   jax.ShapeDtypeStruct((t * PACKED_ROWS, LANES), I32)],
        compiler_params=_cparams(("parallel",)),
        name=name,
    )(oa, ob, h, h, x, wpa, wpb, wout, g, b)


def _router_kernel(x_ref, whi_ref, wlo_ref, bias_ref, idx_ref, w_ref, rank_ref, cntc_ref, cntr_ref,
                   cntc_scr, cntr_scr):
    i = pl.program_id(0)
    tm = x_ref.shape[0]

    @pl.when(i == 0)
    def _():
        cntc_scr[...] = jnp.zeros_like(cntc_scr)
        cntr_scr[...] = jnp.zeros_like(cntr_scr)

    x_hi, x_lo = _split_bf16(x_ref[...])
    nt = (((1,), (1,)), ((), ()))
    logits = (lax.dot_general(whi_ref[...], x_hi, nt, preferred_element_type=F32)
              + lax.dot_general(whi_ref[...], x_lo, nt, preferred_element_type=F32)
              + lax.dot_general(wlo_ref[...], x_hi, nt, preferred_element_type=F32))
    scores = jax.nn.sigmoid(logits)
    biased = scores + bias_ref[...]

    grouped = biased.reshape(N_GROUPS, GROUP_SIZE, tm)
    m1 = jnp.max(grouped, axis=1)
    n_top = jnp.sum(jnp.where(grouped == m1[:, None, :], 1.0, 0.0), axis=1)
    m2 = jnp.max(jnp.where(grouped < m1[:, None, :], grouped, -jnp.inf), axis=1)
    gscore = m1 + jnp.where(n_top >= 2.0, m1, m2)

    g_iota = lax.broadcasted_iota(I32, (N_GROUPS, tm), 0)
    beaten = jnp.zeros((N_GROUPS, tm), I32)
    for g in range(N_GROUPS):
        other = gscore[g:g + 1, :]
        ahead = (other > gscore) | ((other == gscore) & (g < g_iota))
        beaten = beaten + jnp.where(ahead, 1, 0)
    keep = jnp.where(beaten < TOPK_GROUPS, 1.0, 0.0)
    masked = jnp.where(keep[:, None, :] > 0.5, grouped, -jnp.inf).reshape(N_EXPERTS, tm)

    e_iota = lax.broadcasted_iota(I32, (N_EXPERTS, tm), 0)
    sel_f = jnp.zeros((N_EXPERTS, tm), F32)
    ids = []
    for _ in range(TOP_K):
        best = jnp.max(masked, axis=0, keepdims=True)
        idx = jnp.min(jnp.where(masked == best, e_iota, N_EXPERTS), axis=0, keepdims=True)
        hit = e_iota == idx
        sel_f = sel_f + jnp.where(hit, 1.0, 0.0)
        masked = jnp.where(hit, -jnp.inf, masked)
        ids.append(idx)

    top_sum = jnp.sum(sel_f * scores, axis=0, keepdims=True)

    t_r = lax.broadcasted_iota(I32, (tm, tm), 0)
    t_c = lax.broadcasted_iota(I32, (tm, tm), 1)
    before = jnp.where(t_r < t_c, 1.0, 0.0).astype(BF16)
    sel_b = sel_f.astype(BF16)
    rank = jnp.dot(sel_b, before, preferred_element_type=F32) + cntc_scr[...]
    cntc_scr[...] = cntc_scr[...] + jnp.sum(sel_f, axis=1, keepdims=True)
    cntr_scr[...] = cntr_scr[...] + lax.dot_general(jnp.ones((SUBLANES, tm), BF16), sel_b, nt,
                                                    preferred_element_type=F32)

    idx_rows, w_rows, rank_rows = [], [], []
    for idx in ids:
        hit = e_iota == idx
        w = jnp.sum(jnp.where(hit, scores, 0.0), axis=0, keepdims=True)
        w_rows.append(w / top_sum * ROUTED_SCALE)
        rank_rows.append(jnp.sum(jnp.where(hit, rank, 0.0), axis=0, keepdims=True))
        idx_rows.append(idx)
    idx_ref[...] = jnp.concatenate(idx_rows, axis=0)
    w_ref[...] = jnp.concatenate(w_rows, axis=0)
    rank_ref[...] = jnp.concatenate(rank_rows, axis=0).astype(I32)

    @pl.when(i == pl.num_programs(0) - 1)
    def _():
        cntc_ref[...] = cntc_scr[...].astype(I32)
        cntr_ref[...] = cntr_scr[...].astype(I32)


def _router(x1, w_hi, w_lo, bias, *, tm, name):
    t = x1.shape[0]
    full = lambda i: (0, 0)
    tok = lambda i: (0, i)
    return pl.pallas_call(
        _router_kernel,
        grid=(t // tm,),
        in_specs=[pl.BlockSpec((tm, D_MODEL), lambda i: (i, 0)),
                  pl.BlockSpec((N_EXPERTS, D_MODEL), full),
                  pl.BlockSpec((N_EXPERTS, D_MODEL), full),
                  pl.BlockSpec((N_EXPERTS, 1), full)],
        out_specs=[pl.BlockSpec((TOP_K, tm), tok),
                   pl.BlockSpec((TOP_K, tm), tok),
                   pl.BlockSpec((TOP_K, tm), tok),
                   pl.BlockSpec((N_EXPERTS, 1), full),
                   pl.BlockSpec((SUBLANES, N_EXPERTS), full)],
        out_shape=[jax.ShapeDtypeStruct((TOP_K, t), I32),
                   jax.ShapeDtypeStruct((TOP_K, t), F32),
                   jax.ShapeDtypeStruct((TOP_K, t), I32),
                   jax.ShapeDtypeStruct((N_EXPERTS, 1), I32),
                   jax.ShapeDtypeStruct((SUBLANES, N_EXPERTS), I32)],
        scratch_shapes=[pltpu.VMEM((N_EXPERTS, 1), F32),
                        pltpu.VMEM((SUBLANES, N_EXPERTS), F32)],
        compiler_params=_cparams(("arbitrary",)),
        name=name,
    )(x1, w_hi, w_lo, bias)


def _byte_split(v):
    return lax.shift_right_logical(v, 8).astype(F32), (v & 255).astype(F32)


def _plan_kernel(cntc_ref, cntr_ref, idx_ref, rank_ref, meta_ref, items_ref, m_scr, *, n_tok):
    i = pl.program_id(0)
    tm = idx_ref.shape[1]
    nbp = m_scr.shape[0]
    nip = items_ref.shape[1]
    n_rows = n_tok * TOP_K
    n_blk = n_rows // EXPERT_ROWS
    nt = (((1,), (1,)), ((), ()))

    r_i = lax.broadcasted_iota(I32, (N_EXPERTS, N_EXPERTS), 0)
    c_i = lax.broadcasted_iota(I32, (N_EXPERTS, N_EXPERTS), 1)
    below = jnp.where(c_i < r_i, 1.0, 0.0).astype(BF16)
    ones_c = jnp.ones((N_EXPERTS, LANES), F32)
    c_hi, c_lo = _byte_split(cntc_ref[...])
    start_col = (256.0 * jnp.dot(below, (c_hi * ones_c).astype(BF16), preferred_element_type=F32)
                 + jnp.dot(below, (c_lo * ones_c).astype(BF16), preferred_element_type=F32))[:, 0:1]

    @pl.when(i == 0)
    def _():
        m_scr[...] = jnp.zeros_like(m_scr)
        above = jnp.where(r_i < c_i, 1.0, 0.0).astype(BF16)
        r_hi, r_lo = _byte_split(cntr_ref[...])
        start_row = (256.0 * jnp.dot(r_hi.astype(BF16), above, preferred_element_type=F32)
                     + jnp.dot(r_lo.astype(BF16), above, preferred_element_type=F32))[0:1, :]
        b_col = lax.broadcasted_iota(I32, (nbp, 1), 0)
        b_row = lax.broadcasted_iota(I32, (1, nbp), 1)
        blk_col = jnp.where(b_col < n_blk, b_col * EXPERT_ROWS, n_rows).astype(F32)
        blk_row = jnp.where(b_row < n_blk, b_row * EXPERT_ROWS, n_rows).astype(F32)
        v_col = jnp.concatenate([blk_col, start_col], axis=0)
        v_row = jnp.concatenate([blk_row, start_row], axis=1)
        j_col = lax.broadcasted_iota(I32, (nip, 1), 0)
        k_row = lax.broadcasted_iota(I32, (1, nip), 1)
        ahead = (v_row < v_col) | ((v_row == v_col) & (k_row < j_col))
        order_col = jnp.sum(jnp.where(ahead, 1.0, 0.0), axis=1, keepdims=True)
        pos_row = k_row.astype(F32)
        lo_abs = jnp.sum(jnp.where(order_col == pos_row, v_col, 0.0), axis=0, keepdims=True)
        hi_abs = jnp.sum(jnp.where(order_col == pos_row + 1.0, v_col, 0.0), axis=0, keepdims=True)
        hi_abs = jnp.where(k_row == nip - 1, float(n_rows), hi_abs)
        blk = jnp.minimum(jnp.floor(lo_abs * (1.0 / EXPERT_ROWS)), n_blk - 1.0)
        expert = jnp.sum(jnp.where(start_col <= lo_abs, 1.0, 0.0), axis=0, keepdims=True) - 1.0
        base = blk * EXPERT_ROWS
        items_ref[...] = jnp.concatenate([blk, expert, lo_abs - base, hi_abs - base], axis=0).astype(I32)

    e_iota = lax.broadcasted_iota(I32, (N_EXPERTS, tm), 0)
    b_iota = lax.broadcasted_iota(I32, (nbp, tm), 0)
    l_iota = lax.broadcasted_iota(I32, (EXPERT_ROWS, tm), 0)
    tok = i * tm + lax.broadcasted_iota(I32, (1, tm), 1)
    tok_hi, tok_lo = _byte_split(tok)
    acc = jnp.zeros(m_scr.shape, F32)
    for k in range(TOP_K):
        hit = e_iota == idx_ref[k:k + 1, :]
        dest = (jnp.sum(jnp.where(hit, start_col, 0.0), axis=0, keepdims=True).astype(I32)
                + rank_ref[k:k + 1, :])
        oh_blk = jnp.where(b_iota == lax.shift_right_logical(dest, 8), 1.0, 0.0).astype(BF16)
        in_blk = l_iota == (dest & (EXPERT_ROWS - 1))
        vals = jnp.concatenate([jnp.where(in_blk, tok_hi, 0.0), jnp.where(in_blk, tok_lo, 0.0),
                                jnp.where(in_blk, float(k), 0.0)], axis=0).astype(BF16)
        acc = acc + lax.dot_general(oh_blk, vals, nt, preferred_element_type=F32)
    m_scr[...] = m_scr[...] + acc

    @pl.when(i == pl.num_programs(0) - 1)
    def _():
        m = m_scr[...]
        row_tok = m[:, 0:EXPERT_ROWS] * 256.0 + m[:, EXPERT_ROWS:2 * EXPERT_ROWS]
        row_slot = m[:, 2 * EXPERT_ROWS:3 * EXPERT_ROWS] * float(n_tok) + row_tok
        meta_ref[...] = jnp.concatenate([row_tok, row_slot], axis=1).astype(I32)


def _plan(cnt_col, cnt_row, idx, rank, *, tm, name):
    t = idx.shape[1]
    assert EXPERT_ROWS == 256 and (t * TOP_K) % (2 * EXPERT_ROWS) == 0
    n_blk = t * TOP_K // EXPERT_ROWS
    nbp = -(-n_blk // LANES) * LANES
    nip = nbp + N_EXPERTS
    tok = lambda i: (0, i)
    full = lambda i: (0, 0)
    return pl.pallas_call(
        functools.partial(_plan_kernel, n_tok=t),
        grid=(t // tm,),
        in_specs=[pl.BlockSpec((N_EXPERTS, 1), full),
                  pl.BlockSpec((SUBLANES, N_EXPERTS), full),
                  pl.BlockSpec((TOP_K, tm), tok),
                  pl.BlockSpec((TOP_K, tm), tok)],
        out_specs=[pl.BlockSpec((nbp, 2 * EXPERT_ROWS), full),
                   pl.BlockSpec((4, nip), full)],
        out_shape=[jax.ShapeDtypeStruct((nbp, 2 * EXPERT_ROWS), I32),
                   jax.ShapeDtypeStruct((4, nip), I32)],
        scratch_shapes=[pltpu.VMEM((nbp, 3 * EXPERT_ROWS), F32)],
        compiler_params=_cparams(("arbitrary",)),
        name=name,
    )(cnt_col, cnt_row, idx, rank)


META_CHUNK = 4 * EXPERT_ROWS
GATHER_STRIDE = EXPERT_ROWS + 8
UNROLL = 16


def _experts_kernel(items_ref, xw_hbm, meta_hbm, wg_ref, wu_ref, wd_ref, ys_hbm,
                    xw, meta, tile, lhs, wg_b, wu_b, wd_b, yacc, ybuf, sem_x, sem_m, sem_y, *, nip, n_blk):
    i = pl.program_id(0)
    blk = items_ref[i]
    expert = items_ref[nip + i]
    lo = items_ref[2 * nip + i]
    hi = items_ref[3 * nip + i]
    prev_expert = items_ref[nip + jnp.maximum(i - 1, 0)]
    nonempty = hi > lo
    par = blk & 1
    chunk = lax.shift_right_logical(blk, 1)
    cpar = chunk & 1
    mbase = cpar * META_CHUNK + par * (2 * EXPERT_ROWS)

    def meta_copy(c, slot):
        return pltpu.make_async_copy(meta_hbm.at[pl.ds(pl.multiple_of(c * META_CHUNK, META_CHUNK), META_CHUNK)],
                                     meta.at[pl.ds(pl.multiple_of(slot * META_CHUNK, META_CHUNK), META_CHUNK)],
                                     sem_m.at[slot])

    def ybuf_drain(slot):
        pltpu.make_async_copy(ys_hbm.at[pl.ds(0, EXPERT_ROWS * ROW_CHUNKS), :], ybuf.at[slot],
                              sem_y.at[slot]).wait()

    def ybuf_fill_and_send(slot):
        for j in range(ROW_CHUNKS):
            ybuf[slot, pl.ds(j, EXPERT_ROWS, stride=ROW_CHUNKS), :] = yacc[:, j * LANES:(j + 1) * LANES]

        def scatter(c, carry):
            for u in range(UNROLL):
                r = c * UNROLL + u
                src = ybuf.at[slot, pl.ds(pl.multiple_of(r * ROW_CHUNKS, ROW_CHUNKS), ROW_CHUNKS), :]
                dst_row = pl.multiple_of(meta[mbase + EXPERT_ROWS + r] * ROW_CHUNKS, ROW_CHUNKS)
                pltpu.make_async_copy(src, ys_hbm.at[pl.ds(dst_row, ROW_CHUNKS), :], sem_y.at[slot]).start()
            return carry

        lax.fori_loop(0, EXPERT_ROWS // UNROLL, scatter, 0)

    @pl.when(i == 0)
    def _():
        resident = pltpu.make_async_copy(xw_hbm, xw, sem_x)
        resident.start()
        meta_copy(0, 0).start()
        resident.wait()

    @pl.when((i == 0) | (expert != prev_expert))
    def _():
        wg_b[...] = wg_ref[0].astype(BF16)
        wu_b[...] = wu_ref[0].astype(BF16)
        wd_b[...] = wd_ref[0].astype(BF16)

    @pl.when(nonempty & (lo == 0))
    def _():
        @pl.when(par == 0)
        def _():
            meta_copy(chunk, cpar).wait()

            @pl.when(2 * (chunk + 1) < n_blk)
            def _():
                meta_copy(chunk + 1, 1 - cpar).start()

        def gather(c, carry):
            for u in range(UNROLL):
                r = c * UNROLL + u
                t4 = pl.multiple_of(meta[mbase + r] * PACKED_ROWS, PACKED_ROWS)
                tile[pl.ds(r, PACKED_ROWS, stride=GATHER_STRIDE), :] = xw[pl.ds(t4, PACKED_ROWS), :]
            return carry

        lax.fori_loop(0, EXPERT_ROWS // UNROLL, gather, 0)
        cols = []
        for s in range(PACKED_ROWS):
            cols.extend(_unpack_bf16_pairs(tile[pl.ds(s * GATHER_STRIDE, EXPERT_ROWS), :]))
        lhs[...] = jnp.concatenate(cols, axis=-1)

    half_rows = EXPERT_ROWS // 2
    for half in range(2):
        r0 = half * half_rows

        @pl.when(nonempty & (lo < r0 + half_rows) & (hi > r0))
        def _():
            x = lhs[r0:r0 + half_rows, :]
            gate = jnp.dot(x, wg_b[...], preferred_element_type=F32)
            up = jnp.dot(x, wu_b[...], preferred_element_type=F32)
            hid = (jax.nn.silu(gate) * up).astype(BF16)
            y = jnp.dot(hid, wd_b[...], preferred_element_type=F32)
            row = r0 + lax.broadcasted_iota(I32, (half_rows, 1), 0)
            y = jnp.where((row >= lo) & (row < hi), y, 0.0)

            @pl.when(lo <= r0)
            def _():
                yacc[r0:r0 + half_rows, :] = y

            @pl.when(lo > r0)
            def _():
                yacc[r0:r0 + half_rows, :] = yacc[r0:r0 + half_rows, :] + y

    @pl.when(nonempty & (hi == EXPERT_ROWS))
    def _():
        @pl.when(blk >= 2)
        def _():
            ybuf_drain(par)

        for slot in range(2):
            @pl.when(par == slot)
            def _():
                ybuf_fill_and_send(slot)

    @pl.when(i == pl.num_programs(0) - 1)
    def _():
        ybuf_drain(0)
        ybuf_drain(1)


def _experts(items, xw, meta, wg, wu, wd, *, n_tok, name):
    nip = items.shape[0] // 4
    n_rows = n_tok * TOP_K
    n_blk = n_rows // EXPERT_ROWS
    n_items = n_blk + N_EXPERTS

    def wmap(i, items_ref):
        return (items_ref[nip + i], 0, 0)

    kern = functools.partial(_experts_kernel, nip=nip, n_blk=n_blk)
    any_spec = pl.BlockSpec(memory_space=pl.ANY)
    return pl.pallas_call(
        kern,
        grid_spec=pltpu.PrefetchScalarGridSpec(
            num_scalar_prefetch=1,
            grid=(n_items,),
            in_specs=[any_spec, any_spec,
                      pl.BlockSpec((1, D_MODEL, EXPERT_FF), wmap),
                      pl.BlockSpec((1, D_MODEL, EXPERT_FF), wmap),
                      pl.BlockSpec((1, EXPERT_FF, D_MODEL), wmap)],
            out_specs=any_spec,
            scratch_shapes=[pltpu.VMEM(xw.shape, I32),
                            pltpu.SMEM((2 * META_CHUNK,), I32),
                            pltpu.VMEM((PACKED_ROWS * GATHER_STRIDE, LANES), I32),
                            pltpu.VMEM((EXPERT_ROWS, D_MODEL), BF16),
                            pltpu.VMEM((D_MODEL, EXPERT_FF), BF16),
                            pltpu.VMEM((D_MODEL, EXPERT_FF), BF16),
                            pltpu.VMEM((EXPERT_FF, D_MODEL), BF16),
                            pltpu.VMEM((EXPERT_ROWS, D_MODEL), F32),
                            pltpu.VMEM((2, EXPERT_ROWS * ROW_CHUNKS, LANES), F32),
                            pltpu.SemaphoreType.DMA,
                            pltpu.SemaphoreType.DMA((2,)),
                            pltpu.SemaphoreType.DMA((2,))],
        ),
        out_shape=jax.ShapeDtypeStruct((n_rows * ROW_CHUNKS, LANES), F32),
        compiler_params=_cparams(("arbitrary",)),
        name=name,
    )(items, xw, meta, wg, wu, wd)


def _combine_kernel(ys_ref, x_ref, w_ref, wsg_ref, wsu_ref, wsd_ref, g_ref, b_ref, o_ref):
    x = x_ref[...]
    xb = x.astype(BF16)
    hid = (jax.nn.silu(jnp.dot(xb, wsg_ref[...], preferred_element_type=F32))
           * jnp.dot(xb, wsu_ref[...], preferred_element_type=F32))
    shared = jnp.dot(hid.astype(BF16), wsd_ref[...], preferred_element_type=F32)
    w = w_ref[...]
    chunks = []
    for j in range(ROW_CHUNKS):
        acc = None
        for k in range(TOP_K):
            part = w[:, k:k + 1] * ys_ref[k, :, j, :]
            acc = part if acc is None else acc + part
        chunks.append(acc)
    ffn = jnp.concatenate(chunks, axis=-1) + shared
    o_ref[...] = _layer_norm(DEEPNORM_ALPHA * x + ffn, g_ref[...], b_ref[...])


def _combine(ys, x1, w_tok, wsg, wsu, wsd, g, b, *, tm, name):
    t = x1.shape[0]
    row = lambda i: (i, 0)
    full = lambda i: (0, 0)
    return pl.pallas_call(
        _combine_kernel,
        grid=(t // tm,),
        in_specs=[pl.BlockSpec((TOP_K, tm, ROW_CHUNKS, LANES), lambda i: (0, i, 0, 0)),
                  pl.BlockSpec((tm, D_MODEL), row),
                  pl.BlockSpec((tm, TOP_K), row),
                  pl.BlockSpec((D_MODEL, SHARED_FF), full),
                  pl.BlockSpec((D_MODEL, SHARED_FF), full),
                  pl.BlockSpec((SHARED_FF, D_MODEL), full),
                  pl.BlockSpec((1, D_MODEL), full),
                  pl.BlockSpec((1, D_MODEL), full)],
        out_specs=pl.BlockSpec((tm, D_MODEL), row),
        out_shape=jax.ShapeDtypeStruct((t, D_MODEL), F32),
        compiler_params=_cparams(("parallel",)),
        name=name,
    )(ys, x1, w_tok, wsg, wsu, wsd, g, b)


def _pack_w_in(w_in):
    sizes = (GLA_KEY_DIM, GLA_KEY_DIM, GLA_VAL_DIM, GLA_VAL_DIM, GLA_GATE_RANK,
             SWA_Q_DIM, SWA_KV_DIM, SWA_KV_DIM, D_MODEL, D_MODEL)
    offs = [0]
    for s in sizes:
        offs.append(offs[-1] + s)
    qa, ka, va, ga, gk, qb, kb, vb, gate_a, gate_b = (w_in[:, offs[i]:offs[i + 1]] for i in range(10))
    pad = lambda w, n: jnp.pad(w, ((0, 0), (0, n - w.shape[1])))
    packed = jnp.concatenate([qa, ka, va, ga, qb, gate_a, gate_b, kb, vb, pad(gk, 2 * LANES)], axis=1)
    assert packed.shape[1] == H_WIDTH
    return packed.astype(BF16), jnp.concatenate([kb, vb], axis=1).astype(BF16)


def kernel(x_prompt, x_sample, state_gla, cache_swa_k, cache_swa_v, w_in, w_gk_up, b_gk, gla_norm_g,
           attn_sinks, w_proj_a, w_proj_b, w_out, ln1_g, ln1_b, w_router, router_bias,
           w_expert_gate, w_expert_up, w_expert_down, w_shared_gate, w_shared_up, w_shared_down,
           ln2_g, ln2_b):
    assert w_in.shape[0] == 1, "single-layer trunk"
    bp, lp, d = x_prompt.shape
    bs, ls, _ = x_sample.shape
    assert d == D_MODEL and ls == SUBLANES and cache_swa_k.shape[2] == WINDOW
    tp, ts = bp * lp, bs * ls
    t = tp + ts

    x = jnp.concatenate([x_prompt.reshape(tp, d), x_sample.reshape(ts, d)], axis=0)
    w_main, w_kv = _pack_w_in(w_in[0])
    tm_proj = _tile(t, 1024)
    h = _matmul(x, w_main, BF16, tm_proj, H_TN, "proj_in")

    x_tail = jnp.concatenate([x_prompt[:, lp - WINDOW:].reshape(bp * WINDOW, d), x.reshape(t, d)[tp:]], axis=0)
    kv_tail = _matmul(x_tail, w_kv, F32, _tile(x_tail.shape[0], 512), 2 * SWA_KV_DIM, "proj_kv_tail")

    wup = jnp.pad(w_gk_up[0], ((0, LANES - GLA_GATE_RANK), (0, 0))).astype(BF16)
    bgk = b_gk[0].reshape(1, GLA_KEY_DIM)
    gn = gla_norm_g[0].reshape(1, GLA_DV)
    oa_p, s_prompt = _gla(h, wup, bgk, gn, None, row0=0, n_seq=bp, seq_len=lp, nb=1, c=GLA_CHUNK,
                          sub=4, name="gla_prompt")
    oa_s, s_sample = _gla(h, wup, bgk, gn, state_gla[0], row0=tp, n_seq=bs, seq_len=ls, nb=8,
                          c=math.gcd(ls, GLA_CHUNK), sub=1, name="gla_sample")

    sinks = attn_sinks[0]
    k_past = cache_swa_k[0].reshape(bs, WINDOW, SWA_KV_DIM)
    v_past = cache_swa_v[0].reshape(bs, WINDOW, SWA_KV_DIM)
    ob_p = _swa_prompt(h, sinks, n_seq=bp, seq_len=lp, name="swa_prompt")
    ob_s = _swa_sample(h, sinks, k_past, v_past, row0=tp, n_seq=bs, lq=ls, nb=8, name="swa_sample")

    oa = jnp.concatenate([oa_p, oa_s], axis=0)
    ob = jnp.concatenate([ob_p, ob_s], axis=0)
    x1, xw = _merge(oa, ob, h, x, w_proj_a[0].astype(BF16), w_proj_b[0].astype(BF16),
                    w_out[0].astype(BF16), ln1_g[0].reshape(1, d), ln1_b[0].reshape(1, d),
                    tm=_tile(t, 512), name="merge_ln1")

    wr_t = w_router[0].T
    wr_hi = wr_t.astype(BF16)
    wr_lo = (wr_t - wr_hi.astype(F32)).astype(BF16)
    tm_r = _tile(t, 512)
    idx, w_top, rank, cnt_col, cnt_row = _router(x1, wr_hi, wr_lo, router_bias[0].reshape(N_EXPERTS, 1),
                                                 tm=tm_r, name="router")
    meta, items = _plan(cnt_col, cnt_row, idx, rank, tm=tm_r, name="plan")
    ys = _experts(items.reshape(-1), xw, meta.reshape(-1), w_expert_gate[0], w_expert_up[0],
                  w_expert_down[0], n_tok=t, name="experts")
    y = _combine(ys.reshape(TOP_K, t, ROW_CHUNKS, LANES), x1, w_top.T,
                 w_shared_gate[0].astype(BF16), w_shared_up[0].astype(BF16), w_shared_down[0].astype(BF16),
                 ln2_g[0].reshape(1, d), ln2_b[0].reshape(1, d), tm=_tile(t, 128), name="combine_ln2")

    y_prompt = y[:tp].reshape(bp, lp, d)
    y_sample = y[tp:].reshape(bs, ls, d)
    k_tail = kv_tail[:, :SWA_KV_DIM]
    v_tail = kv_tail[:, SWA_KV_DIM:]
    kv_shape = (SWA_KV_HEADS, SWA_HEAD_DIM)
    k_prompt = k_tail[:bp * WINDOW].reshape(1, bp, WINDOW, *kv_shape)
    v_prompt = v_tail[:bp * WINDOW].reshape(1, bp, WINDOW, *kv_shape)
    k_new = k_tail[bp * WINDOW:].reshape(bs, ls, *kv_shape)
    v_new = v_tail[bp * WINDOW:].reshape(bs, ls, *kv_shape)
    k_sample = jnp.concatenate([cache_swa_k[0][:, ls:], k_new], axis=1)[None]
    v_sample = jnp.concatenate([cache_swa_v[0][:, ls:], v_new], axis=1)[None]
    return (y_prompt, y_sample, s_prompt[None], s_sample[None], k_prompt, v_prompt, k_sample, v_sample)
```

```python
import functools
import math

import jax
import jax.numpy as jnp
from jax import lax
from jax.experimental import pallas as pl
from jax.experimental.pallas import tpu as pltpu

F32 = jnp.float32
BF16 = jnp.bfloat16
I32 = jnp.int32

D_MODEL = 1024
GLA_HEADS = 4
GLA_DK = 128
GLA_DV = 256
GLA_KEY_DIM = GLA_HEADS * GLA_DK
GLA_VAL_DIM = GLA_HEADS * GLA_DV
GLA_GATE_RANK = 16
GLA_GATE_NORMALIZER = 16.0
GLA_CHUNK = 64
SWA_HEADS = 16
SWA_KV_HEADS = 4
SWA_GROUP = SWA_HEADS // SWA_KV_HEADS
SWA_HEAD_DIM = 64
SWA_Q_DIM = SWA_HEADS * SWA_HEAD_DIM
SWA_KV_DIM = SWA_KV_HEADS * SWA_HEAD_DIM
WINDOW = 128
N_EXPERTS = 256
TOP_K = 8
N_GROUPS = 8
GROUP_SIZE = N_EXPERTS // N_GROUPS
TOPK_GROUPS = 4
EXPERT_FF = 256
SHARED_FF = 256
ROUTED_SCALE = 2.5
DEEPNORM_ALPHA = 2.0 ** 0.25
EPS = 1e-5

LANES = 128
SUBLANES = 8
ROW_CHUNKS = D_MODEL // LANES
VMEM_LIMIT = 56 * 1024 * 1024

H_QA, H_KA, H_VA, H_GA, H_QB, H_GATE_A, H_GATE_B, H_KB, H_VB, H_GK = (
    0, 512, 1024, 2048, 3072, 4096, 5120, 6144, 6400, 6656)
H_WIDTH = 6912
H_TN = 2304

EXPERT_ROWS = 256
PACKED_ROWS = ROW_CHUNKS // 2
HIGH_HALF = -65536


def _cparams(sem, vmem=VMEM_LIMIT):
    return pltpu.CompilerParams(dimension_semantics=sem, vmem_limit_bytes=vmem)


def _tile(n, pref):
    t = min(n, pref)
    while n % t:
        t -= LANES
    assert t > 0 and t % LANES == 0, (n, pref)
    return t


def _mm_kernel(x_ref, w_ref, o_ref):
    o_ref[...] = jnp.dot(x_ref[...].astype(BF16), w_ref[...],
                         preferred_element_type=F32).astype(o_ref.dtype)


def _matmul(x, w, out_dtype, tm, tn, name):
    m, k = x.shape
    n = w.shape[1]
    return pl.pallas_call(
        _mm_kernel,
        grid=(m // tm, n // tn),
        in_specs=[pl.BlockSpec((tm, k), lambda i, j: (i, 0)),
                  pl.BlockSpec((k, tn), lambda i, j: (0, j))],
        out_specs=pl.BlockSpec((tm, tn), lambda i, j: (i, j)),
        out_shape=jax.ShapeDtypeStruct((m, n), out_dtype),
        compiler_params=_cparams(("parallel", "arbitrary")),
        name=name,
    )(x, w)


def _split_bf16(x):
    hi = x.astype(BF16)
    lo = (x - hi.astype(F32)).astype(BF16)
    return hi, lo


def _gla_kernel(*refs, nb, c, sub, has_s0):
    if has_s0:
        (q_ref, k_ref, v_ref, ga_ref, gk_ref, wup_ref, bgk_ref, gn_ref, s0_ref,
         o_ref, sout_ref, s_scr) = refs
    else:
        (q_ref, k_ref, v_ref, ga_ref, gk_ref, wup_ref, bgk_ref, gn_ref,
         o_ref, sout_ref, s_scr) = refs
        s0_ref = None
    ci = pl.program_id(1)
    rows = nb * c

    @pl.when(ci == 0)
    def _():
        if has_s0:
            s_scr[...] = s0_ref[...]
        else:
            s_scr[...] = jnp.zeros_like(s_scr)

    r_i = lax.broadcasted_iota(I32, (rows, rows), 0)
    c_i = lax.broadcasted_iota(I32, (rows, rows), 1)
    same_seq = (r_i // c) == (c_i // c)
    causal = same_seq & (c_i <= r_i)
    tri = jnp.where(causal, 1.0, 0.0).astype(BF16)
    seg = jnp.where(same_seq, 1.0, 0.0).astype(BF16)
    ones_kv = jnp.ones((rows, GLA_DV), BF16)
    seq_of_row = lax.broadcasted_iota(I32, (rows, 1), 0) // c
    tn = (((0,), (0,)), ((), ()))

    for s in range(sub):
        rs = slice(s * rows, (s + 1) * rows)
        pre = jnp.dot(gk_ref[rs, :], wup_ref[...], preferred_element_type=F32) + bgk_ref[...]
        log_a = jax.nn.log_sigmoid(pre) / GLA_GATE_NORMALIZER
        la_hi, la_lo = _split_bf16(log_a)
        b = (jnp.dot(tri, la_hi, preferred_element_type=F32)
             + jnp.dot(tri, la_lo, preferred_element_type=F32))
        b_last = (jnp.dot(seg, la_hi, preferred_element_type=F32)
                  + jnp.dot(seg, la_lo, preferred_element_type=F32))
        e_pos = jnp.exp(b)
        e_neg = jnp.exp(-b)
        e_rem = jnp.exp(b_last - b)
        q = q_ref[rs, :].astype(F32) * (GLA_DK ** -0.5)
        k = k_ref[rs, :].astype(F32)
        q_dec = (q * e_pos).astype(BF16)
        k_dec = (k * e_neg).astype(BF16)
        k_rem = k * e_rem
        outs = []
        for h in range(GLA_HEADS):
            ks = slice(h * GLA_DK, (h + 1) * GLA_DK)
            vh = v_ref[rs, h * GLA_DV:(h + 1) * GLA_DV]
            att = lax.dot_general(q_dec[:, ks], k_dec[:, ks], (((1,), (1,)), ((), ())),
                                  preferred_element_type=F32)
            att = jnp.where(causal, att, 0.0).astype(BF16)
            o_h = jnp.dot(att, vh, preferred_element_type=F32)
            for j in range(nb):
                mine = (seq_of_row == j) if nb > 1 else None
                pick = (lambda a: jnp.where(mine, a, 0.0)) if nb > 1 else (lambda a: a)
                s_old = s_scr[j, h]
                o_h = o_h + pick(jnp.dot(q_dec[:, ks], s_old.astype(BF16), preferred_element_type=F32))
                dec = (lax.dot_general(pick(la_hi[:, ks].astype(F32)).astype(BF16), ones_kv, tn,
                                       preferred_element_type=F32)
                       + lax.dot_general(pick(la_lo[:, ks].astype(F32)).astype(BF16), ones_kv, tn,
                                         preferred_element_type=F32))
                upd = lax.dot_general(pick(k_rem[:, ks]).astype(BF16), vh, tn, preferred_element_type=F32)
                s_scr[j, h] = jnp.exp(dec) * s_old + upd
            o_h = o_h * lax.rsqrt(jnp.mean(jnp.square(o_h), axis=-1, keepdims=True) + EPS) * gn_ref[...]
            outs.append(o_h)
        o = jnp.concatenate(outs, axis=-1) * jax.nn.silu(ga_ref[rs, :].astype(F32))
        o_ref[rs, :] = o.astype(o_ref.dtype)

    @pl.when(ci == pl.num_programs(1) - 1)
    def _():
        sout_ref[...] = s_scr[...]


def _gla(h, wup, bgk, gn, s0, *, row0, n_seq, seq_len, nb, c, sub, name):
    rows = nb * c * sub
    n_groups = n_seq // nb
    n_steps = seq_len // (c * sub)
    rb0 = row0 // rows

    def rmap(col):
        return lambda g, i: (rb0 + g * n_steps + i, col)

    in_specs = [
        pl.BlockSpec((rows, GLA_KEY_DIM), rmap(H_QA // GLA_KEY_DIM)),
        pl.BlockSpec((rows, GLA_KEY_DIM), rmap(H_KA // GLA_KEY_DIM)),
        pl.BlockSpec((rows, GLA_VAL_DIM), rmap(H_VA // GLA_VAL_DIM)),
        pl.BlockSpec((rows, GLA_VAL_DIM), rmap(H_GA // GLA_VAL_DIM)),
        pl.BlockSpec((rows, LANES), rmap(H_GK // LANES)),
        pl.BlockSpec((LANES, GLA_KEY_DIM), lambda g, i: (0, 0)),
        pl.BlockSpec((1, GLA_KEY_DIM), lambda g, i: (0, 0)),
        pl.BlockSpec((1, GLA_DV), lambda g, i: (0, 0)),
    ]
    args = [h, h, h, h, h, wup, bgk, gn]
    state_spec = pl.BlockSpec((nb, GLA_HEADS, GLA_DK, GLA_DV), lambda g, i: (g, 0, 0, 0))
    if s0 is not None:
        in_specs.append(state_spec)
        args.append(s0)
    kern = functools.partial(_gla_kernel, nb=nb, c=c, sub=sub, has_s0=s0 is not None)
    return pl.pallas_call(
        kern,
        grid=(n_groups, n_steps),
        in_specs=in_specs,
        out_specs=[pl.BlockSpec((rows, GLA_VAL_DIM), lambda g, i: (g * n_steps + i, 0)), state_spec],
        out_shape=[jax.ShapeDtypeStruct((n_seq * seq_len, GLA_VAL_DIM), BF16),
                   jax.ShapeDtypeStruct((n_seq, GLA_HEADS, GLA_DK, GLA_DV), F32)],
        scratch_shapes=[pltpu.VMEM((nb, GLA_HEADS, GLA_DK, GLA_DV), F32)],
        compiler_params=_cparams(("parallel", "arbitrary")),
        name=name,
    )(*args)


def _alibi_slope(head):
    return 2.0 ** (-8.0 * (head + 1) / SWA_HEADS)


def _swa_softmax_pv(parts, sink):
    m = sink
    for s, _ in parts:
        m = jnp.maximum(m, jnp.max(s, axis=-1, keepdims=True))
    denom = jnp.exp(sink - m)
    acc = None
    for s, v in parts:
        p = jnp.exp(s - m)
        denom = denom + jnp.sum(p, axis=-1, keepdims=True)
        pv = jnp.dot(p.astype(BF16), v, preferred_element_type=F32)
        acc = pv if acc is None else acc + pv
    return acc / denom


def _swa_prompt_kernel(sink_ref, q_ref, kp_ref, vp_ref, kc_ref, vc_ref, o_ref):
    i = pl.program_id(1)
    span = 2 * WINDOW
    kcat = jnp.concatenate([kp_ref[...], kc_ref[...]], axis=0)
    vcat = jnp.concatenate([vp_ref[...], vc_ref[...]], axis=0)
    row = lax.broadcasted_iota(I32, (WINDOW, span), 0)
    col = lax.broadcasted_iota(I32, (WINDOW, span), 1)
    dist_i = row + WINDOW - col
    valid = (dist_i >= 0) & (dist_i < WINDOW) & ((col >= WINDOW) | (i > 0))
    dist = dist_i.astype(F32)
    outs = []
    for hh in range(SWA_HEADS):
        g = hh // SWA_GROUP
        gs = slice(g * SWA_HEAD_DIM, (g + 1) * SWA_HEAD_DIM)
        qh = q_ref[:, hh * SWA_HEAD_DIM:(hh + 1) * SWA_HEAD_DIM]
        s = lax.dot_general(qh, kcat[:, gs], (((1,), (1,)), ((), ())), preferred_element_type=F32)
        s = s * (SWA_HEAD_DIM ** -0.5) - _alibi_slope(hh) * dist
        s = jnp.where(valid, s, -jnp.inf)
        outs.append(_swa_softmax_pv([(s, vcat[:, gs])], sink_ref[hh]))
    o_ref[...] = jnp.concatenate(outs, axis=-1).astype(o_ref.dtype)


def _swa_prompt(h, sinks, *, n_seq, seq_len, name):
    nq = seq_len // WINDOW
    qcol = H_QB // SWA_Q_DIM
    kcol = H_KB // SWA_KV_DIM
    vcol = H_VB // SWA_KV_DIM

    def cur(col):
        return lambda b, i, sk: (b * nq + i, col)

    def prev(col):
        return lambda b, i, sk: (b * nq + jnp.maximum(i - 1, 0), col)

    return pl.pallas_call(
        _swa_prompt_kernel,
        grid_spec=pltpu.PrefetchScalarGridSpec(
            num_scalar_prefetch=1,
            grid=(n_seq, nq),
            in_specs=[pl.BlockSpec((WINDOW, SWA_Q_DIM), cur(qcol)),
                      pl.BlockSpec((WINDOW, SWA_KV_DIM), prev(kcol)),
                      pl.BlockSpec((WINDOW, SWA_KV_DIM), prev(vcol)),
                      pl.BlockSpec((WINDOW, SWA_KV_DIM), cur(kcol)),
                      pl.BlockSpec((WINDOW, SWA_KV_DIM), cur(vcol))],
            out_specs=pl.BlockSpec((WINDOW, SWA_Q_DIM), lambda b, i, sk: (b * nq + i, 0)),
        ),
        out_shape=jax.ShapeDtypeStruct((n_seq * seq_len, SWA_Q_DIM), BF16),
        compiler_params=_cparams(("parallel", "arbitrary")),
        name=name,
    )(sinks, h, h, h, h, h)


def _swa_sample_kernel(sink_ref, q_ref, kc_ref, vc_ref, kp_ref, vp_ref, o_ref, *, nb, lq):
    rows = SWA_HEADS * lq
    grp_rows = SWA_GROUP * lq
    hd = SWA_HEAD_DIM
    nt = (((1,), (1,)), ((), ()))
    head_of_row = lax.broadcasted_iota(I32, (rows, 1), 0) // lq
    slope = jnp.zeros((rows, 1), F32)
    sink = jnp.zeros((rows, 1), F32)
    for hh in range(SWA_HEADS):
        slope = jnp.where(head_of_row == hh, _alibi_slope(hh), slope)
        sink = jnp.where(head_of_row == hh, sink_ref[hh], sink)
    qi = lax.broadcasted_iota(I32, (rows, WINDOW), 0) % lq
    dist_p = qi + WINDOW - lax.broadcasted_iota(I32, (rows, WINDOW), 1)
    valid_p = dist_p < WINDOW
    bias_p = slope * dist_p.astype(F32)
    dist_c = lax.broadcasted_iota(I32, (rows, lq), 0) % lq - lax.broadcasted_iota(I32, (rows, lq), 1)
    valid_c = dist_c >= 0
    bias_c = slope * dist_c.astype(F32)
    scale = hd ** -0.5

    q_all = q_ref[...].astype(F32)
    kc_all = kc_ref[...].astype(F32)
    vc_all = vc_ref[...].astype(F32)
    seq_outs = []
    for j in range(nb):
        js = slice(j * lq, (j + 1) * lq)
        pieces = []
        for hh in range(SWA_HEADS):
            g = hh // SWA_GROUP
            parts = []
            if g:
                parts.append(jnp.zeros((lq, g * hd), F32))
            parts.append(q_all[js, hh * hd:(hh + 1) * hd])
            if g < SWA_KV_HEADS - 1:
                parts.append(jnp.zeros((lq, (SWA_KV_HEADS - 1 - g) * hd), F32))
            pieces.append(jnp.concatenate(parts, axis=-1))
        q_big = jnp.concatenate(pieces, axis=0).astype(BF16)
        kp = kp_ref[j].astype(BF16)
        vp = vp_ref[j].astype(BF16)
        kcj = kc_all[js, :].astype(BF16)
        vcj = vc_all[js, :].astype(BF16)
        s_p = lax.dot_general(q_big, kp, nt, preferred_element_type=F32)
        s_p = jnp.where(valid_p, s_p * scale - bias_p, -jnp.inf)
        s_c = lax.dot_general(q_big, kcj, nt, preferred_element_type=F32)
        s_c = jnp.where(valid_c, s_c * scale - bias_c, -jnp.inf)
        o_big = _swa_softmax_pv([(s_p, vp), (s_c, vcj)], sink)
        o_grp = [o_big[g * grp_rows:(g + 1) * grp_rows, g * hd:(g + 1) * hd] for g in range(SWA_KV_HEADS)]
        o_heads = jnp.concatenate(o_grp, axis=0)
        seq_outs.append(jnp.concatenate([o_heads[hh * lq:(hh + 1) * lq, :] for hh in range(SWA_HEADS)],
                                        axis=-1))
    o_ref[...] = jnp.concatenate(seq_outs, axis=0).astype(o_ref.dtype)


def _swa_sample(h, sinks, k_past, v_past, *, row0, n_seq, lq, nb, name):
    rows = nb * lq
    rb0 = row0 // rows
    qcol = H_QB // SWA_Q_DIM
    kcol = H_KB // SWA_KV_DIM
    vcol = H_VB // SWA_KV_DIM
    kern = functools.partial(_swa_sample_kernel, nb=nb, lq=lq)
    past_spec = pl.BlockSpec((nb, WINDOW, SWA_KV_DIM), lambda g, sk: (g, 0, 0))
    return pl.pallas_call(
        kern,
        grid_spec=pltpu.PrefetchScalarGridSpec(
            num_scalar_prefetch=1,
            grid=(n_seq // nb,),
            in_specs=[pl.BlockSpec((rows, SWA_Q_DIM), lambda g, sk: (rb0 + g, qcol)),
                      pl.BlockSpec((rows, SWA_KV_DIM), lambda g, sk: (rb0 + g, kcol)),
                      pl.BlockSpec((rows, SWA_KV_DIM), lambda g, sk: (rb0 + g, vcol)),
                      past_spec, past_spec],
            out_specs=pl.BlockSpec((rows, SWA_Q_DIM), lambda g, sk: (g, 0)),
        ),
        out_shape=jax.ShapeDtypeStruct((n_seq * lq, SWA_Q_DIM), BF16),
        compiler_params=_cparams(("parallel",)),
        name=name,
    )(sinks, h, h, h, k_past, v_past)


def _layer_norm(x, g, b):
    mu = jnp.mean(x, axis=-1, keepdims=True)
    xc = x - mu
    var = jnp.mean(jnp.square(xc), axis=-1, keepdims=True)
    return xc * lax.rsqrt(var + EPS) * g + b


def _pack_bf16_pairs(x, s):
    lo = lax.bitcast_convert_type(x[:, (2 * s) * LANES:(2 * s + 1) * LANES].astype(BF16).astype(F32), I32)
    hi = lax.bitcast_convert_type(x[:, (2 * s + 1) * LANES:(2 * s + 2) * LANES].astype(BF16).astype(F32), I32)
    return lax.shift_right_logical(lo, 16) | (hi & HIGH_HALF)


def _unpack_bf16_pairs(w):
    lo = lax.bitcast_convert_type(lax.shift_left(w, 16), F32).astype(BF16)
    hi = lax.bitcast_convert_type(w & HIGH_HALF, F32).astype(BF16)
    return lo, hi


def _merge_kernel(oa_ref, ob_ref, ga_ref, gb_ref, x_ref, wpa_ref, wpb_ref, wout_ref, g_ref, b_ref,
                  o_ref, xw_ref):
    tm = o_ref.shape[0]
    br_a = jnp.dot(oa_ref[...], wpa_ref[...], preferred_element_type=F32)
    br_b = jnp.dot(ob_ref[...], wpb_ref[...], preferred_element_type=F32)
    merged = (jax.nn.sigmoid(ga_ref[...].astype(F32)) * br_a
              + jax.nn.sigmoid(gb_ref[...].astype(F32)) * br_b)
    mix = jnp.dot(merged.astype(BF16), wout_ref[...], preferred_element_type=F32)
    x1 = _layer_norm(DEEPNORM_ALPHA * x_ref[...] + mix, g_ref[...], b_ref[...])
    o_ref[...] = x1
    for s in range(PACKED_ROWS):
        xw_ref[pl.ds(s, tm, stride=PACKED_ROWS), :] = _pack_bf16_pairs(x1, s)


def _merge(oa, ob, h, x, wpa, wpb, wout, g, b, *, tm, name):
    t = x.shape[0]
    row = lambda i: (i, 0)
    full = lambda i: (0, 0)
    return pl.pallas_call(
        _merge_kernel,
        grid=(t // tm,),
        in_specs=[pl.BlockSpec((tm, D_MODEL), row),
                  pl.BlockSpec((tm, D_MODEL), row),
                  pl.BlockSpec((tm, D_MODEL), lambda i: (i, H_GATE_A // D_MODEL)),
                  pl.BlockSpec((tm, D_MODEL), lambda i: (i, H_GATE_B // D_MODEL)),
                  pl.BlockSpec((tm, D_MODEL), row),
                  pl.BlockSpec((D_MODEL, D_MODEL), full),
                  pl.BlockSpec((D_MODEL, D_MODEL), full),
                  pl.BlockSpec((D_MODEL, D_MODEL), full),
                  pl.BlockSpec((1, D_MODEL), full),
                  pl.BlockSpec((1, D_MODEL), full)],
        out_specs=[pl.BlockSpec((tm, D_MODEL), row),
                   pl.BlockSpec((tm * PACKED_ROWS, LANES), row)],
        out_shape=[jax.ShapeDtypeStruct((t, D_MODEL), F32),
                   jax.ShapeDtypeStruct((t * PACKED_ROWS, LANES), I32)],
        compiler_params=_cparams(("parallel",)),
        name=name,
    )(oa, ob, h, h, x, wpa, wpb, wout, g, b)


def _router_kernel(x_ref, whi_ref, wlo_ref, bias_ref, idx_ref, w_ref, rank_ref, cntc_ref, cntr_ref,
                   cntc_scr, cntr_scr):
    i = pl.program_id(0)
    tm = x_ref.shape[0]

    @pl.when(i == 0)
    def _():
        cntc_scr[...] = jnp.zeros_like(cntc_scr)
        cntr_scr[...] = jnp.zeros_like(cntr_scr)

    x_hi, x_lo = _split_bf16(x_ref[...])
    nt = (((1,), (1,)), ((), ()))
    logits = (lax.dot_general(whi_ref[...], x_hi, nt, preferred_element_type=F32)
              + lax.dot_general(whi_ref[...], x_lo, nt, preferred_element_type=F32)
              + lax.dot_general(wlo_ref[...], x_hi, nt, preferred_element_type=F32))
    scores = jax.nn.sigmoid(logits)
    biased = scores + bias_ref[...]

    grouped = biased.reshape(N_GROUPS, GROUP_SIZE, tm)
    m1 = jnp.max(grouped, axis=1)
    n_top = jnp.sum(jnp.where(grouped == m1[:, None, :], 1.0, 0.0), axis=1)
    m2 = jnp.max(jnp.where(grouped < m1[:, None, :], grouped, -jnp.inf), axis=1)
    gscore = m1 + jnp.where(n_top >= 2.0, m1, m2)

    g_iota = lax.broadcasted_iota(I32, (N_GROUPS, tm), 0)
    beaten = jnp.zeros((N_GROUPS, tm), I32)
    for g in range(N_GROUPS):
        other = gscore[g:g + 1, :]
        ahead = (other > gscore) | ((other == gscore) & (g < g_iota))
        beaten = beaten + jnp.where(ahead, 1, 0)
    keep = jnp.where(beaten < TOPK_GROUPS, 1.0, 0.0)
    masked = jnp.where(keep[:, None, :] > 0.5, grouped, -jnp.inf).reshape(N_EXPERTS, tm)

    e_iota = lax.broadcasted_iota(I32, (N_EXPERTS, tm), 0)
    sel_f = jnp.zeros((N_EXPERTS, tm), F32)
    ids = []
    for _ in range(TOP_K):
        best = jnp.max(masked, axis=0, keepdims=True)
        idx = jnp.min(jnp.where(masked == best, e_iota, N_EXPERTS), axis=0, keepdims=True)
        hit = e_iota == idx
        sel_f = sel_f + jnp.where(hit, 1.0, 0.0)
        masked = jnp.where(hit, -jnp.inf, masked)
        ids.append(idx)

    top_sum = jnp.sum(sel_f * scores, axis=0, keepdims=True)

    t_r = lax.broadcasted_iota(I32, (tm, tm), 0)
    t_c = lax.broadcasted_iota(I32, (tm, tm), 1)
    before = jnp.where(t_r < t_c, 1.0, 0.0).astype(BF16)
    sel_b = sel_f.astype(BF16)
    rank = jnp.dot(sel_b, before, preferred_element_type=F32) + cntc_scr[...]
    cntc_scr[...] = cntc_scr[...] + jnp.sum(sel_f, axis=1, keepdims=True)
    cntr_scr[...] = cntr_scr[...] + lax.dot_general(jnp.ones((SUBLANES, tm), BF16), sel_b, nt,
                                                    preferred_element_type=F32)

    idx_rows, w_rows, rank_rows = [], [], []
    for idx in ids:
        hit = e_iota == idx
        w = jnp.sum(jnp.where(hit, scores, 0.0), axis=0, keepdims=True)
        w_rows.append(w / top_sum * ROUTED_SCALE)
        rank_rows.append(jnp.sum(jnp.where(hit, rank, 0.0), axis=0, keepdims=True))
        idx_rows.append(idx)
    idx_ref[...] = jnp.concatenate(idx_rows, axis=0)
    w_ref[...] = jnp.concatenate(w_rows, axis=0)
    rank_ref[...] = jnp.concatenate(rank_rows, axis=0).astype(I32)

    @pl.when(i == pl.num_programs(0) - 1)
    def _():
        cntc_ref[...] = cntc_scr[...].astype(I32)
        cntr_ref[...] = cntr_scr[...].astype(I32)


def _router(x1, w_hi, w_lo, bias, *, tm, name):
    t = x1.shape[0]
    full = lambda i: (0, 0)
    tok = lambda i: (0, i)
    return pl.pallas_call(
        _router_kernel,
        grid=(t // tm,),
        in_specs=[pl.BlockSpec((tm, D_MODEL), lambda i: (i, 0)),
                  pl.BlockSpec((N_EXPERTS, D_MODEL), full),
                  pl.BlockSpec((N_EXPERTS, D_MODEL), full),
                  pl.BlockSpec((N_EXPERTS, 1), full)],
        out_specs=[pl.BlockSpec((TOP_K, tm), tok),
                   pl.BlockSpec((TOP_K, tm), tok),
                   pl.BlockSpec((TOP_K, tm), tok),
                   pl.BlockSpec((N_EXPERTS, 1), full),
                   pl.BlockSpec((SUBLANES, N_EXPERTS), full)],
        out_shape=[jax.ShapeDtypeStruct((TOP_K, t), I32),
                   jax.ShapeDtypeStruct((TOP_K, t), F32),
                   jax.ShapeDtypeStruct((TOP_K, t), I32),
                   jax.ShapeDtypeStruct((N_EXPERTS, 1), I32),
                   jax.ShapeDtypeStruct((SUBLANES, N_EXPERTS), I32)],
        scratch_shapes=[pltpu.VMEM((N_EXPERTS, 1), F32),
                        pltpu.VMEM((SUBLANES, N_EXPERTS), F32)],
        compiler_params=_cparams(("arbitrary",)),
        name=name,
    )(x1, w_hi, w_lo, bias)


def _byte_split(v):
    return lax.shift_right_logical(v, 8).astype(F32), (v & 255).astype(F32)


def _plan_kernel(cntc_ref, cntr_ref, idx_ref, rank_ref, meta_ref, items_ref, m_scr, *, n_tok):
    i = pl.program_id(0)
    tm = idx_ref.shape[1]
    nbp = m_scr.shape[0]
    nip = items_ref.shape[1]
    n_rows = n_tok * TOP_K
    n_blk = n_rows // EXPERT_ROWS
    nt = (((1,), (1,)), ((), ()))

    r_i = lax.broadcasted_iota(I32, (N_EXPERTS, N_EXPERTS), 0)
    c_i = lax.broadcasted_iota(I32, (N_EXPERTS, N_EXPERTS), 1)
    below = jnp.where(c_i < r_i, 1.0, 0.0).astype(BF16)
    ones_c = jnp.ones((N_EXPERTS, LANES), F32)
    c_hi, c_lo = _byte_split(cntc_ref[...])
    start_col = (256.0 * jnp.dot(below, (c_hi * ones_c).astype(BF16), preferred_element_type=F32)
                 + jnp.dot(below, (c_lo * ones_c).astype(BF16), preferred_element_type=F32))[:, 0:1]

    @pl.when(i == 0)
    def _():
        m_scr[...] = jnp.zeros_like(m_scr)
        above = jnp.where(r_i < c_i, 1.0, 0.0).astype(BF16)
        r_hi, r_lo = _byte_split(cntr_ref[...])
        start_row = (256.0 * jnp.dot(r_hi.astype(BF16), above, preferred_element_type=F32)
                     + jnp.dot(r_lo.astype(BF16), above, preferred_element_type=F32))[0:1, :]
        b_col = lax.broadcasted_iota(I32, (nbp, 1), 0)
        b_row = lax.broadcasted_iota(I32, (1, nbp), 1)
        blk_col = jnp.where(b_col < n_blk, b_col * EXPERT_ROWS, n_rows).astype(F32)
        blk_row = jnp.where(b_row < n_blk, b_row * EXPERT_ROWS, n_rows).astype(F32)
        v_col = jnp.concatenate([blk_col, start_col], axis=0)
        v_row = jnp.concatenate([blk_row, start_row], axis=1)
        j_col = lax.broadcasted_iota(I32, (nip, 1), 0)
        k_row = lax.broadcasted_iota(I32, (1, nip), 1)
        ahead = (v_row < v_col) | ((v_row == v_col) & (k_row < j_col))
        order_col = jnp.sum(jnp.where(ahead, 1.0, 0.0), axis=1, keepdims=True)
        pos_row = k_row.astype(F32)
        lo_abs = jnp.sum(jnp.where(order_col == pos_row, v_col, 0.0), axis=0, keepdims=True)
        hi_abs = jnp.sum(jnp.where(order_col == pos_row + 1.0, v_col, 0.0), axis=0, keepdims=True)
        hi_abs = jnp.where(k_row == nip - 1, float(n_rows), hi_abs)
        blk = jnp.minimum(jnp.floor(lo_abs * (1.0 / EXPERT_ROWS)), n_blk - 1.0)
        expert = jnp.sum(jnp.where(start_col <= lo_abs, 1.0, 0.0), axis=0, keepdims=True) - 1.0
        base = blk * EXPERT_ROWS
        items_ref[...] = jnp.concatenate([blk, expert, lo_abs - base, hi_abs - base], axis=0).astype(I32)

    e_iota = lax.broadcasted_iota(I32, (N_EXPERTS, tm), 0)
    b_iota = lax.broadcasted_iota(I32, (nbp, tm), 0)
    l_iota = lax.broadcasted_iota(I32, (EXPERT_ROWS, tm), 0)
    tok = i * tm + lax.broadcasted_iota(I32, (1, tm), 1)
    tok_hi, tok_lo = _byte_split(tok)
    acc = jnp.zeros(m_scr.shape, F32)
    for k in range(TOP_K):
        hit = e_iota == idx_ref[k:k + 1, :]
        dest = (jnp.sum(jnp.where(hit, start_col, 0.0), axis=0, keepdims=True).astype(I32)
                + rank_ref[k:k + 1, :])
        oh_blk = jnp.where(b_iota == lax.shift_right_logical(dest, 8), 1.0, 0.0).astype(BF16)
        in_blk = l_iota == (dest & (EXPERT_ROWS - 1))
        vals = jnp.concatenate([jnp.where(in_blk, tok_hi, 0.0), jnp.where(in_blk, tok_lo, 0.0),
                                jnp.where(in_blk, float(k), 0.0)], axis=0).astype(BF16)
        acc = acc + lax.dot_general(oh_blk, vals, nt, preferred_element_type=F32)
    m_scr[...] = m_scr[...] + acc

    @pl.when(i == pl.num_programs(0) - 1)
    def _():
        m = m_scr[...]
        row_tok = m[:, 0:EXPERT_ROWS] * 256.0 + m[:, EXPERT_ROWS:2 * EXPERT_ROWS]
        row_slot = m[:, 2 * EXPERT_ROWS:3 * EXPERT_ROWS] * float(n_tok) + row_tok
        meta_ref[...] = jnp.concatenate([row_tok, row_slot], axis=1).astype(I32)


def _plan(cnt_col, cnt_row, idx, rank, *, tm, name):
    t = idx.shape[1]
    assert EXPERT_ROWS == 256 and (t * TOP_K) % (2 * EXPERT_ROWS) == 0
    n_blk = t * TOP_K // EXPERT_ROWS
    nbp = -(-n_blk // LANES) * LANES
    nip = nbp + N_EXPERTS
    tok = lambda i: (0, i)
    full = lambda i: (0, 0)
    return pl.pallas_call(
        functools.partial(_plan_kernel, n_tok=t),
        grid=(t // tm,),
        in_specs=[pl.BlockSpec((N_EXPERTS, 1), full),
                  pl.BlockSpec((SUBLANES, N_EXPERTS), full),
                  pl.BlockSpec((TOP_K, tm), tok),
                  pl.BlockSpec((TOP_K, tm), tok)],
        out_specs=[pl.BlockSpec((nbp, 2 * EXPERT_ROWS), full),
                   pl.BlockSpec((4, nip), full)],
        out_shape=[jax.ShapeDtypeStruct((nbp, 2 * EXPERT_ROWS), I32),
                   jax.ShapeDtypeStruct((4, nip), I32)],
        scratch_shapes=[pltpu.VMEM((nbp, 3 * EXPERT_ROWS), F32)],
        compiler_params=_cparams(("arbitrary",)),
        name=name,
    )(cnt_col, cnt_row, idx, rank)


META_CHUNK = 4 * EXPERT_ROWS
GATHER_STRIDE = EXPERT_ROWS + 8
UNROLL = 256


def _experts_kernel(items_ref, xw_hbm, meta_hbm, wg_ref, wu_ref, wd_ref, ys_hbm,
                    xw, meta, tile, lhs, wg_b, wu_b, wd_b, yacc, ybuf, sem_x, sem_m, sem_y, *, nip, n_blk):
    i = pl.program_id(0)
    blk = items_ref[i]
    expert = items_ref[nip + i]
    lo = items_ref[2 * nip + i]
    hi = items_ref[3 * nip + i]
    prev_expert = items_ref[nip + jnp.maximum(i - 1, 0)]
    nonempty = hi > lo
    par = blk & 1
    chunk = lax.shift_right_logical(blk, 1)
    cpar = chunk & 1
    mbase = cpar * META_CHUNK + par * (2 * EXPERT_ROWS)

    def meta_copy(c, slot):
        return pltpu.make_async_copy(meta_hbm.at[pl.ds(pl.multiple_of(c * META_CHUNK, META_CHUNK), META_CHUNK)],
                                     meta.at[pl.ds(pl.multiple_of(slot * META_CHUNK, META_CHUNK), META_CHUNK)],
                                     sem_m.at[slot])

    def ybuf_drain(slot):
        pltpu.make_async_copy(ys_hbm.at[pl.ds(0, EXPERT_ROWS * ROW_CHUNKS), :], ybuf.at[slot],
                              sem_y.at[slot]).wait()

    def ybuf_fill_and_send(slot):
        for j in range(ROW_CHUNKS):
            ybuf[slot, pl.ds(j, EXPERT_ROWS, stride=ROW_CHUNKS), :] = yacc[:, j * LANES:(j + 1) * LANES]

        def scatter(c, carry):
            for u in range(UNROLL):
                r = c * UNROLL + u
                src = ybuf.at[slot, pl.ds(pl.multiple_of(r * ROW_CHUNKS, ROW_CHUNKS), ROW_CHUNKS), :]
                dst_row = pl.multiple_of(meta[mbase + EXPERT_ROWS + r] * ROW_CHUNKS, ROW_CHUNKS)
                pltpu.make_async_copy(src, ys_hbm.at[pl.ds(dst_row, ROW_CHUNKS), :], sem_y.at[slot]).start()
            return carry

        lax.fori_loop(0, EXPERT_ROWS // UNROLL, scatter, 0)

    @pl.when(i == 0)
    def _():
        resident = pltpu.make_async_copy(xw_hbm, xw, sem_x)
        resident.start()
        meta_copy(0, 0).start()
        resident.wait()

    @pl.when((i == 0) | (expert != prev_expert))
    def _():
        wg_b[...] = wg_ref[0].astype(BF16)
        wu_b[...] = wu_ref[0].astype(BF16)
        wd_b[...] = wd_ref[0].astype(BF16)

    @pl.when(nonempty & (lo == 0))
    def _():
        @pl.when(par == 0)
        def _():
            meta_copy(chunk, cpar).wait()

            @pl.when(2 * (chunk + 1) < n_blk)
            def _():
                meta_copy(chunk + 1, 1 - cpar).start()

        def gather(c, carry):
            for u in range(UNROLL):
                r = c * UNROLL + u
                t4 = pl.multiple_of(meta[mbase + r] * PACKED_ROWS, PACKED_ROWS)
                tile[pl.ds(r, PACKED_ROWS, stride=GATHER_STRIDE), :] = xw[pl.ds(t4, PACKED_ROWS), :]
            return carry

        lax.fori_loop(0, EXPERT_ROWS // UNROLL, gather, 0)
        cols = []
        for s in range(PACKED_ROWS):
            cols.extend(_unpack_bf16_pairs(tile[pl.ds(s * GATHER_STRIDE, EXPERT_ROWS), :]))
        lhs[...] = jnp.concatenate(cols, axis=-1)

    def ffn(x):
        gate = jnp.dot(x, wg_b[...], preferred_element_type=F32)
        up = jnp.dot(x, wu_b[...], preferred_element_type=F32)
        hid = (jax.nn.silu(gate) * up).astype(BF16)
        return jnp.dot(hid, wd_b[...], preferred_element_type=F32)

    whole = (lo == 0) & (hi == EXPERT_ROWS)

    @pl.when(whole)
    def _():
        yacc[...] = ffn(lhs[...])

    half_rows = EXPERT_ROWS // 2
    for half in range(2):
        r0 = half * half_rows

        @pl.when(nonempty & jnp.logical_not(whole) & (lo < r0 + half_rows) & (hi > r0))
        def _():
            y = ffn(lhs[r0:r0 + half_rows, :])
            row = r0 + lax.broadcasted_iota(I32, (half_rows, 1), 0)
            y = jnp.where((row >= lo) & (row < hi), y, 0.0)

            @pl.when(lo <= r0)
            def _():
                yacc[r0:r0 + half_rows, :] = y

            @pl.when(lo > r0)
            def _():
                yacc[r0:r0 + half_rows, :] = yacc[r0:r0 + half_rows, :] + y

    @pl.when(nonempty & (hi == EXPERT_ROWS))
    def _():
        @pl.when(blk >= 2)
        def _():
            ybuf_drain(par)

        for slot in range(2):
            @pl.when(par == slot)
            def _():
                ybuf_fill_and_send(slot)

    @pl.when(i == pl.num_programs(0) - 1)
    def _():
        ybuf_drain(0)
        ybuf_drain(1)


def _experts(items, xw, meta, wg, wu, wd, *, n_tok, name):
    nip = items.shape[0] // 4
    n_rows = n_tok * TOP_K
    n_blk = n_rows // EXPERT_ROWS
    n_items = n_blk + N_EXPERTS

    def wmap(i, items_ref):
        return (items_ref[nip + i], 0, 0)

    kern = functools.partial(_experts_kernel, nip=nip, n_blk=n_blk)
    any_spec = pl.BlockSpec(memory_space=pl.ANY)
    return pl.pallas_call(
        kern,
        grid_spec=pltpu.PrefetchScalarGridSpec(
            num_scalar_prefetch=1,
            grid=(n_items,),
            in_specs=[any_spec, any_spec,
                      pl.BlockSpec((1, D_MODEL, EXPERT_FF), wmap),
                      pl.BlockSpec((1, D_MODEL, EXPERT_FF), wmap),
                      pl.BlockSpec((1, EXPERT_FF, D_MODEL), wmap)],
            out_specs=any_spec,
            scratch_shapes=[pltpu.VMEM(xw.shape, I32),
                            pltpu.SMEM((2 * META_CHUNK,), I32),
                            pltpu.VMEM((PACKED_ROWS * GATHER_STRIDE, LANES), I32),
                            pltpu.VMEM((EXPERT_ROWS, D_MODEL), BF16),
                            pltpu.VMEM((D_MODEL, EXPERT_FF), BF16),
                            pltpu.VMEM((D_MODEL, EXPERT_FF), BF16),
                            pltpu.VMEM((EXPERT_FF, D_MODEL), BF16),
                            pltpu.VMEM((EXPERT_ROWS, D_MODEL), F32),
                            pltpu.VMEM((2, EXPERT_ROWS * ROW_CHUNKS, LANES), F32),
                            pltpu.SemaphoreType.DMA,
                            pltpu.SemaphoreType.DMA((2,)),
                            pltpu.SemaphoreType.DMA((2,))],
        ),
        out_shape=jax.ShapeDtypeStruct((n_rows * ROW_CHUNKS, LANES), F32),
        compiler_params=_cparams(("arbitrary",)),
        name=name,
    )(items, xw, meta, wg, wu, wd)


def _combine_kernel(ys_ref, x_ref, w_ref, wsg_ref, wsu_ref, wsd_ref, g_ref, b_ref, o_ref):
    x = x_ref[...]
    xb = x.astype(BF16)
    hid = (jax.nn.silu(jnp.dot(xb, wsg_ref[...], preferred_element_type=F32))
           * jnp.dot(xb, wsu_ref[...], preferred_element_type=F32))
    shared = jnp.dot(hid.astype(BF16), wsd_ref[...], preferred_element_type=F32)
    w = w_ref[...]
    chunks = []
    for j in range(ROW_CHUNKS):
        acc = None
        for k in range(TOP_K):
            part = w[:, k:k + 1] * ys_ref[k, :, j, :]
            acc = part if acc is None else acc + part
        chunks.append(acc)
    ffn = jnp.concatenate(chunks, axis=-1) + shared
    o_ref[...] = _layer_norm(DEEPNORM_ALPHA * x + ffn, g_ref[...], b_ref[...])


def _combine(ys, x1, w_tok, wsg, wsu, wsd, g, b, *, tm, name):
    t = x1.shape[0]
    row = lambda i: (i, 0)
    full = lambda i: (0, 0)
    return pl.pallas_call(
        _combine_kernel,
        grid=(t // tm,),
        in_specs=[pl.BlockSpec((TOP_K, tm, ROW_CHUNKS, LANES), lambda i: (0, i, 0, 0)),
                  pl.BlockSpec((tm, D_MODEL), row),
                  pl.BlockSpec((tm, TOP_K), row),
                  pl.BlockSpec((D_MODEL, SHARED_FF), full),
                  pl.BlockSpec((D_MODEL, SHARED_FF), full),
                  pl.BlockSpec((SHARED_FF, D_MODEL), full),
                  pl.BlockSpec((1, D_MODEL), full),
                  pl.BlockSpec((1, D_MODEL), full)],
        out_specs=pl.BlockSpec((tm, D_MODEL), row),
        out_shape=jax.ShapeDtypeStruct((t, D_MODEL), F32),
        compiler_params=_cparams(("parallel",)),
        name=name,
    )(ys, x1, w_tok, wsg, wsu, wsd, g, b)


def _pack_w_in(w_in):
    sizes = (GLA_KEY_DIM, GLA_KEY_DIM, GLA_VAL_DIM, GLA_VAL_DIM, GLA_GATE_RANK,
             SWA_Q_DIM, SWA_KV_DIM, SWA_KV_DIM, D_MODEL, D_MODEL)
    offs = [0]
    for s in sizes:
        offs.append(offs[-1] + s)
    qa, ka, va, ga, gk, qb, kb, vb, gate_a, gate_b = (w_in[:, offs[i]:offs[i + 1]] for i in range(10))
    pad = lambda w, n: jnp.pad(w, ((0, 0), (0, n - w.shape[1])))
    packed = jnp.concatenate([qa, ka, va, ga, qb, gate_a, gate_b, kb, vb, pad(gk, 2 * LANES)], axis=1)
    assert packed.shape[1] == H_WIDTH
    return packed.astype(BF16), jnp.concatenate([kb, vb], axis=1).astype(BF16)


def kernel(x_prompt, x_sample, state_gla, cache_swa_k, cache_swa_v, w_in, w_gk_up, b_gk, gla_norm_g,
           attn_sinks, w_proj_a, w_proj_b, w_out, ln1_g, ln1_b, w_router, router_bias,
           w_expert_gate, w_expert_up, w_expert_down, w_shared_gate, w_shared_up, w_shared_down,
           ln2_g, ln2_b):
    assert w_in.shape[0] == 1, "single-layer trunk"
    bp, lp, d = x_prompt.shape
    bs, ls, _ = x_sample.shape
    assert d == D_MODEL and ls == SUBLANES and cache_swa_k.shape[2] == WINDOW
    tp, ts = bp * lp, bs * ls
    t = tp + ts

    x = jnp.concatenate([x_prompt.reshape(tp, d), x_sample.reshape(ts, d)], axis=0)
    w_main, w_kv = _pack_w_in(w_in[0])
    tm_proj = _tile(t, 1024)
    h = _matmul(x, w_main, BF16, tm_proj, H_TN, "proj_in")

    x_tail = jnp.concatenate([x_prompt[:, lp - WINDOW:].reshape(bp * WINDOW, d), x.reshape(t, d)[tp:]], axis=0)
    kv_tail = _matmul(x_tail, w_kv, F32, _tile(x_tail.shape[0], 512), 2 * SWA_KV_DIM, "proj_kv_tail")

    wup = jnp.pad(w_gk_up[0], ((0, LANES - GLA_GATE_RANK), (0, 0))).astype(BF16)
    bgk = b_gk[0].reshape(1, GLA_KEY_DIM)
    gn = gla_norm_g[0].reshape(1, GLA_DV)
    oa_p, s_prompt = _gla(h, wup, bgk, gn, None, row0=0, n_seq=bp, seq_len=lp, nb=1, c=GLA_CHUNK,
                          sub=4, name="gla_prompt")
    oa_s, s_sample = _gla(h, wup, bgk, gn, state_gla[0], row0=tp, n_seq=bs, seq_len=ls, nb=8,
                          c=math.gcd(ls, GLA_CHUNK), sub=1, name="gla_sample")

    sinks = attn_sinks[0]
    k_past = cache_swa_k[0].reshape(bs, WINDOW, SWA_KV_DIM)
    v_past = cache_swa_v[0].reshape(bs, WINDOW, SWA_KV_DIM)
    ob_p = _swa_prompt(h, sinks, n_seq=bp, seq_len=lp, name="swa_prompt")
    ob_s = _swa_sample(h, sinks, k_past, v_past, row0=tp, n_seq=bs, lq=ls, nb=8, name="swa_sample")

    oa = jnp.concatenate([oa_p, oa_s], axis=0)
    ob = jnp.concatenate([ob_p, ob_s], axis=0)
    x1, xw = _merge(oa, ob, h, x, w_proj_a[0].astype(BF16), w_proj_b[0].astype(BF16),
                    w_out[0].astype(BF16), ln1_g[0].reshape(1, d), ln1_b[0].reshape(1, d),
                    tm=_tile(t, 512), name="merge_ln1")

    wr_t = w_router[0].T
    wr_hi = wr_t.astype(BF16)
    wr_lo = (wr_t - wr_hi.astype(F32)).astype(BF16)
    tm_r = _tile(t, 512)
    idx, w_top, rank, cnt_col, cnt_row = _router(x1, wr_hi, wr_lo, router_bias[0].reshape(N_EXPERTS, 1),
                                                 tm=tm_r, name="router")
    meta, items = _plan(cnt_col, cnt_row, idx, rank, tm=tm_r, name="plan")
    ys = _experts(items.reshape(-1), xw, meta.reshape(-1), w_expert_gate[0], w_expert_up[0],
                  w_expert_down[0], n_tok=t, name="experts")
    y = _combine(ys.reshape(TOP_K, t, ROW_CHUNKS, LANES), x1, w_top.T,
                 w_shared_gate[0].astype(BF16), w_shared_up[0].astype(BF16), w_shared_down[0].astype(BF16),
                 ln2_g[0].reshape(1, d), ln2_b[0].reshape(1, d), tm=_tile(t, 128), name="combine_ln2")

    y_prompt = y[:tp].reshape(bp, lp, d)
    y_sample = y[tp:].reshape(bs, ls, d)
    k_tail = kv_tail[:, :SWA_KV_DIM]
    v_tail = kv_tail[:, SWA_KV_DIM:]
    kv_shape = (SWA_KV_HEADS, SWA_HEAD_DIM)
    k_prompt = k_tail[:bp * WINDOW].reshape(1, bp, WINDOW, *kv_shape)
    v_prompt = v_tail[:bp * WINDOW].reshape(1, bp, WINDOW, *kv_shape)
    k_new = k_tail[bp * WINDOW:].reshape(bs, ls, *kv_shape)
    v_new = v_tail[bp * WINDOW:].reshape(bs, ls, *kv_shape)
    k_sample = jnp.concatenate([cache_swa_k[0][:, ls:], k_new], axis=1)[None]
    v_sample = jnp.concatenate([cache_swa_v[0][:, ls:], v_new], axis=1)[None]
    return (y_prompt, y_sample, s_prompt[None], s_sample[None], k_prompt, v_prompt, k_sample, v_sample)
```

```python
import functools
import math

import jax
import jax.numpy as jnp
from jax import lax
from jax.experimental import pallas as pl
from jax.experimental.pallas import tpu as pltpu

F32 = jnp.float32
BF16 = jnp.bfloat16
I32 = jnp.int32

D_MODEL = 1024
GLA_HEADS = 4
GLA_DK = 128
GLA_DV = 256
GLA_KEY_DIM = GLA_HEADS * GLA_DK
GLA_VAL_DIM = GLA_HEADS * GLA_DV
GLA_GATE_RANK = 16
GLA_GATE_NORMALIZER = 16.0
GLA_CHUNK = 64
SWA_HEADS = 16
SWA_KV_HEADS = 4
SWA_GROUP = SWA_HEADS // SWA_KV_HEADS
SWA_HEAD_DIM = 64
SWA_Q_DIM = SWA_HEADS * SWA_HEAD_DIM
SWA_KV_DIM = SWA_KV_HEADS * SWA_HEAD_DIM
WINDOW = 128
N_EXPERTS = 256
TOP_K = 8
N_GROUPS = 8
GROUP_SIZE = N_EXPERTS // N_GROUPS
TOPK_GROUPS = 4
EXPERT_FF = 256
SHARED_FF = 256
ROUTED_SCALE = 2.5
DEEPNORM_ALPHA = 2.0 ** 0.25
EPS = 1e-5

LANES = 128
SUBLANES = 8
ROW_CHUNKS = D_MODEL // LANES
VMEM_LIMIT = 56 * 1024 * 1024

H_QA, H_KA, H_VA, H_GA, H_QB, H_GATE_A, H_GATE_B, H_KB, H_VB, H_GK = (
    0, 512, 1024, 2048, 3072, 4096, 5120, 6144, 6400, 6656)
H_WIDTH = 6912
H_TN = 2304

EXPERT_ROWS = 256
PACKED_ROWS = ROW_CHUNKS // 2
HIGH_HALF = -65536


def _cparams(sem, vmem=VMEM_LIMIT):
    return pltpu.CompilerParams(dimension_semantics=sem, vmem_limit_bytes=vmem)


def _tile(n, pref):
    t = min(n, pref)
    while n % t:
        t -= LANES
    assert t > 0 and t % LANES == 0, (n, pref)
    return t


def _pair_tile(n_p, n_s, pref):
    return _tile(math.gcd(n_p, n_s), pref)


def _pair_maps(np_tiles, col=0, extra=0):
    def p_map(i, *_):
        return (jnp.minimum(i, np_tiles - 1), col)

    def s_map(i, *_):
        return (jnp.maximum(i - np_tiles, 0), col)

    return p_map, s_map


def _pair_value(i, np_tiles, p_ref, s_ref):
    return jnp.where(i < np_tiles, p_ref[...], s_ref[...])


def _mm_kernel(xp_ref, xs_ref, w_ref, o_ref, *, np_tiles):
    x = _pair_value(pl.program_id(0), np_tiles, xp_ref, xs_ref)
    o_ref[...] = jnp.dot(x.astype(BF16), w_ref[...], preferred_element_type=F32).astype(o_ref.dtype)


def _matmul(xp, xs, w, out_dtype, tm, tn, name):
    k = xp.shape[1]
    m = xp.shape[0] + xs.shape[0]
    n = w.shape[1]
    np_tiles = xp.shape[0] // tm
    p_map, s_map = _pair_maps(np_tiles)
    return pl.pallas_call(
        functools.partial(_mm_kernel, np_tiles=np_tiles),
        grid=(m // tm, n // tn),
        in_specs=[pl.BlockSpec((tm, k), p_map),
                  pl.BlockSpec((tm, k), s_map),
                  pl.BlockSpec((k, tn), lambda i, j: (0, j))],
        out_specs=pl.BlockSpec((tm, tn), lambda i, j: (i, j)),
        out_shape=jax.ShapeDtypeStruct((m, n), out_dtype),
        compiler_params=_cparams(("parallel", "arbitrary")),
        name=name,
    )(xp, xs, w)


def _split_bf16(x):
    hi = x.astype(BF16)
    lo = (x - hi.astype(F32)).astype(BF16)
    return hi, lo


def _gla_kernel(*refs, nb, c, sub, has_s0):
    if has_s0:
        (q_ref, k_ref, v_ref, ga_ref, gk_ref, wup_ref, bgk_ref, gn_ref, s0_ref,
         o_ref, sout_ref, s_scr) = refs
    else:
        (q_ref, k_ref, v_ref, ga_ref, gk_ref, wup_ref, bgk_ref, gn_ref,
         o_ref, sout_ref, s_scr) = refs
        s0_ref = None
    ci = pl.program_id(1)
    rows = nb * c

    @pl.when(ci == 0)
    def _():
        if has_s0:
            s_scr[...] = s0_ref[...]
        else:
            s_scr[...] = jnp.zeros_like(s_scr)

    r_i = lax.broadcasted_iota(I32, (rows, rows), 0)
    c_i = lax.broadcasted_iota(I32, (rows, rows), 1)
    same_seq = (r_i // c) == (c_i // c)
    causal = same_seq & (c_i <= r_i)
    tri = jnp.where(causal, 1.0, 0.0).astype(BF16)
    seg = jnp.where(same_seq, 1.0, 0.0).astype(BF16)
    ones_kv = jnp.ones((rows, GLA_DV), BF16)
    seq_of_row = lax.broadcasted_iota(I32, (rows, 1), 0) // c
    tn = (((0,), (0,)), ((), ()))

    for s in range(sub):
        rs = slice(s * rows, (s + 1) * rows)
        pre = jnp.dot(gk_ref[rs, :], wup_ref[...], preferred_element_type=F32) + bgk_ref[...]
        log_a = jax.nn.log_sigmoid(pre) / GLA_GATE_NORMALIZER
        la_hi, la_lo = _split_bf16(log_a)
        b = (jnp.dot(tri, la_hi, preferred_element_type=F32)
             + jnp.dot(tri, la_lo, preferred_element_type=F32))
        b_last = (jnp.dot(seg, la_hi, preferred_element_type=F32)
                  + jnp.dot(seg, la_lo, preferred_element_type=F32))
        e_pos = jnp.exp(b)
        e_neg = jnp.exp(-b)
        e_rem = jnp.exp(b_last - b)
        q = q_ref[rs, :].astype(F32) * (GLA_DK ** -0.5)
        k = k_ref[rs, :].astype(F32)
        q_dec = (q * e_pos).astype(BF16)
        k_dec = (k * e_neg).astype(BF16)
        k_rem = k * e_rem
        outs = []
        for h in range(GLA_HEADS):
            ks = slice(h * GLA_DK, (h + 1) * GLA_DK)
            vh = v_ref[rs, h * GLA_DV:(h + 1) * GLA_DV]
            att = lax.dot_general(q_dec[:, ks], k_dec[:, ks], (((1,), (1,)), ((), ())),
                                  preferred_element_type=F32)
            att = jnp.where(causal, att, 0.0).astype(BF16)
            o_h = jnp.dot(att, vh, preferred_element_type=F32)
            for j in range(nb):
                mine = (seq_of_row == j) if nb > 1 else None
                pick = (lambda a: jnp.where(mine, a, 0.0)) if nb > 1 else (lambda a: a)
                s_old = s_scr[j, h]
                o_h = o_h + pick(jnp.dot(q_dec[:, ks], s_old.astype(BF16), preferred_element_type=F32))
                dec = (lax.dot_general(pick(la_hi[:, ks].astype(F32)).astype(BF16), ones_kv, tn,
                                       preferred_element_type=F32)
                       + lax.dot_general(pick(la_lo[:, ks].astype(F32)).astype(BF16), ones_kv, tn,
                                         preferred_element_type=F32))
                upd = lax.dot_general(pick(k_rem[:, ks]).astype(BF16), vh, tn, preferred_element_type=F32)
                s_scr[j, h] = jnp.exp(dec) * s_old + upd
            o_h = o_h * lax.rsqrt(jnp.mean(jnp.square(o_h), axis=-1, keepdims=True) + EPS) * gn_ref[...]
            outs.append(o_h)
        o = jnp.concatenate(outs, axis=-1) * jax.nn.silu(ga_ref[rs, :].astype(F32))
        o_ref[rs, :] = o.astype(o_ref.dtype)

    @pl.when(ci == pl.num_programs(1) - 1)
    def _():
        sout_ref[...] = s_scr[...]


def _gla(h, wup, bgk, gn, s0, *, row0, n_seq, seq_len, nb, c, sub, name):
    rows = nb * c * sub
    n_groups = n_seq // nb
    n_steps = seq_len // (c * sub)
    rb0 = row0 // rows

    def rmap(col):
        return lambda g, i: (rb0 + g * n_steps + i, col)

    in_specs = [
        pl.BlockSpec((rows, GLA_KEY_DIM), rmap(H_QA // GLA_KEY_DIM)),
        pl.BlockSpec((rows, GLA_KEY_DIM), rmap(H_KA // GLA_KEY_DIM)),
        pl.BlockSpec((rows, GLA_VAL_DIM), rmap(H_VA // GLA_VAL_DIM)),
        pl.BlockSpec((rows, GLA_VAL_DIM), rmap(H_GA // GLA_VAL_DIM)),
        pl.BlockSpec((rows, LANES), rmap(H_GK // LANES)),
        pl.BlockSpec((LANES, GLA_KEY_DIM), lambda g, i: (0, 0)),
        pl.BlockSpec((1, GLA_KEY_DIM), lambda g, i: (0, 0)),
        pl.BlockSpec((1, GLA_DV), lambda g, i: (0, 0)),
    ]
    args = [h, h, h, h, h, wup, bgk, gn]
    state_spec = pl.BlockSpec((nb, GLA_HEADS, GLA_DK, GLA_DV), lambda g, i: (g, 0, 0, 0))
    if s0 is not None:
        in_specs.append(state_spec)
        args.append(s0)
    kern = functools.partial(_gla_kernel, nb=nb, c=c, sub=sub, has_s0=s0 is not None)
    return pl.pallas_call(
        kern,
        grid=(n_groups, n_steps),
        in_specs=in_specs,
        out_specs=[pl.BlockSpec((rows, GLA_VAL_DIM), lambda g, i: (g * n_steps + i, 0)), state_spec],
        out_shape=[jax.ShapeDtypeStruct((n_seq * seq_len, GLA_VAL_DIM), BF16),
                   jax.ShapeDtypeStruct((n_seq, GLA_HEADS, GLA_DK, GLA_DV), F32)],
        scratch_shapes=[pltpu.VMEM((nb, GLA_HEADS, GLA_DK, GLA_DV), F32)],
        compiler_params=_cparams(("parallel", "arbitrary")),
        name=name,
    )(*args)


def _alibi_slope(head):
    return 2.0 ** (-8.0 * (head + 1) / SWA_HEADS)


def _swa_softmax_pv(parts, sink):
    m = sink
    for s, _ in parts:
        m = jnp.maximum(m, jnp.max(s, axis=-1, keepdims=True))
    denom = jnp.exp(sink - m)
    acc = None
    for s, v in parts:
        p = jnp.exp(s - m)
        denom = denom + jnp.sum(p, axis=-1, keepdims=True)
        pv = jnp.dot(p.astype(BF16), v, preferred_element_type=F32)
        acc = pv if acc is None else acc + pv
    return acc / denom


def _swa_prompt_kernel(sink_ref, q_ref, kp_ref, vp_ref, kc_ref, vc_ref, o_ref):
    i = pl.program_id(1)
    span = 2 * WINDOW
    kcat = jnp.concatenate([kp_ref[...], kc_ref[...]], axis=0)
    vcat = jnp.concatenate([vp_ref[...], vc_ref[...]], axis=0)
    row = lax.broadcasted_iota(I32, (WINDOW, span), 0)
    col = lax.broadcasted_iota(I32, (WINDOW, span), 1)
    dist_i = row + WINDOW - col
    valid = (dist_i >= 0) & (dist_i < WINDOW) & ((col >= WINDOW) | (i > 0))
    dist = dist_i.astype(F32)
    outs = []
    for hh in range(SWA_HEADS):
        g = hh // SWA_GROUP
        gs = slice(g * SWA_HEAD_DIM, (g + 1) * SWA_HEAD_DIM)
        qh = q_ref[:, hh * SWA_HEAD_DIM:(hh + 1) * SWA_HEAD_DIM]
        s = lax.dot_general(qh, kcat[:, gs], (((1,), (1,)), ((), ())), preferred_element_type=F32)
        s = s * (SWA_HEAD_DIM ** -0.5) - _alibi_slope(hh) * dist
        s = jnp.where(valid, s, -jnp.inf)
        outs.append(_swa_softmax_pv([(s, vcat[:, gs])], sink_ref[hh]))
    o_ref[...] = jnp.concatenate(outs, axis=-1).astype(o_ref.dtype)


def _swa_prompt(h, sinks, *, n_seq, seq_len, name):
    nq = seq_len // WINDOW
    qcol = H_QB // SWA_Q_DIM
    kcol = H_KB // SWA_KV_DIM
    vcol = H_VB // SWA_KV_DIM

    def cur(col):
        return lambda b, i, sk: (b * nq + i, col)

    def prev(col):
        return lambda b, i, sk: (b * nq + jnp.maximum(i - 1, 0), col)

    return pl.pallas_call(
        _swa_prompt_kernel,
        grid_spec=pltpu.PrefetchScalarGridSpec(
            num_scalar_prefetch=1,
            grid=(n_seq, nq),
            in_specs=[pl.BlockSpec((WINDOW, SWA_Q_DIM), cur(qcol)),
                      pl.BlockSpec((WINDOW, SWA_KV_DIM), prev(kcol)),
                      pl.BlockSpec((WINDOW, SWA_KV_DIM), prev(vcol)),
                      pl.BlockSpec((WINDOW, SWA_KV_DIM), cur(kcol)),
                      pl.BlockSpec((WINDOW, SWA_KV_DIM), cur(vcol))],
            out_specs=pl.BlockSpec((WINDOW, SWA_Q_DIM), lambda b, i, sk: (b * nq + i, 0)),
        ),
        out_shape=jax.ShapeDtypeStruct((n_seq * seq_len, SWA_Q_DIM), BF16),
        compiler_params=_cparams(("parallel", "arbitrary")),
        name=name,
    )(sinks, h, h, h, h, h)


def _swa_sample_kernel(sink_ref, q_ref, kc_ref, vc_ref, kp_ref, vp_ref, o_ref, *, nb, lq):
    rows = SWA_HEADS * lq
    grp_rows = SWA_GROUP * lq
    hd = SWA_HEAD_DIM
    nt = (((1,), (1,)), ((), ()))
    head_of_row = lax.broadcasted_iota(I32, (rows, 1), 0) // lq
    slope = jnp.zeros((rows, 1), F32)
    sink = jnp.zeros((rows, 1), F32)
    for hh in range(SWA_HEADS):
        slope = jnp.where(head_of_row == hh, _alibi_slope(hh), slope)
        sink = jnp.where(head_of_row == hh, sink_ref[hh], sink)
    qi = lax.broadcasted_iota(I32, (rows, WINDOW), 0) % lq
    dist_p = qi + WINDOW - lax.broadcasted_iota(I32, (rows, WINDOW), 1)
    valid_p = dist_p < WINDOW
    bias_p = slope * dist_p.astype(F32)
    dist_c = lax.broadcasted_iota(I32, (rows, lq), 0) % lq - lax.broadcasted_iota(I32, (rows, lq), 1)
    valid_c = dist_c >= 0
    bias_c = slope * dist_c.astype(F32)
    scale = hd ** -0.5

    q_all = q_ref[...].astype(F32)
    kc_all = kc_ref[...].astype(F32)
    vc_all = vc_ref[...].astype(F32)
    seq_outs = []
    for j in range(nb):
        js = slice(j * lq, (j + 1) * lq)
        pieces = []
        for hh in range(SWA_HEADS):
            g = hh // SWA_GROUP
            parts = []
            if g:
                parts.append(jnp.zeros((lq, g * hd), F32))
            parts.append(q_all[js, hh * hd:(hh + 1) * hd])
            if g < SWA_KV_HEADS - 1:
                parts.append(jnp.zeros((lq, (SWA_KV_HEADS - 1 - g) * hd), F32))
            pieces.append(jnp.concatenate(parts, axis=-1))
        q_big = jnp.concatenate(pieces, axis=0).astype(BF16)
        kp = kp_ref[j].astype(BF16)
        vp = vp_ref[j].astype(BF16)
        kcj = kc_all[js, :].astype(BF16)
        vcj = vc_all[js, :].astype(BF16)
        s_p = lax.dot_general(q_big, kp, nt, preferred_element_type=F32)
        s_p = jnp.where(valid_p, s_p * scale - bias_p, -jnp.inf)
        s_c = lax.dot_general(q_big, kcj, nt, preferred_element_type=F32)
        s_c = jnp.where(valid_c, s_c * scale - bias_c, -jnp.inf)
        o_big = _swa_softmax_pv([(s_p, vp), (s_c, vcj)], sink)
        o_grp = [o_big[g * grp_rows:(g + 1) * grp_rows, g * hd:(g + 1) * hd] for g in range(SWA_KV_HEADS)]
        o_heads = jnp.concatenate(o_grp, axis=0)
        seq_outs.append(jnp.concatenate([o_heads[hh * lq:(hh + 1) * lq, :] for hh in range(SWA_HEADS)],
                                        axis=-1))
    o_ref[...] = jnp.concatenate(seq_outs, axis=0).astype(o_ref.dtype)


def _swa_sample(h, sinks, k_past, v_past, *, row0, n_seq, lq, nb, name):
    rows = nb * lq
    rb0 = row0 // rows
    qcol = H_QB // SWA_Q_DIM
    kcol = H_KB // SWA_KV_DIM
    vcol = H_VB // SWA_KV_DIM
    kern = functools.partial(_swa_sample_kernel, nb=nb, lq=lq)
    past_spec = pl.BlockSpec((nb, WINDOW, SWA_KV_DIM), lambda g, sk: (g, 0, 0))
    return pl.pallas_call(
        kern,
        grid_spec=pltpu.PrefetchScalarGridSpec(
            num_scalar_prefetch=1,
            grid=(n_seq // nb,),
            in_specs=[pl.BlockSpec((rows, SWA_Q_DIM), lambda g, sk: (rb0 + g, qcol)),
                      pl.BlockSpec((rows, SWA_KV_DIM), lambda g, sk: (rb0 + g, kcol)),
                      pl.BlockSpec((rows, SWA_KV_DIM), lambda g, sk: (rb0 + g, vcol)),
                      past_spec, past_spec],
            out_specs=pl.BlockSpec((rows, SWA_Q_DIM), lambda g, sk: (g, 0)),
        ),
        out_shape=jax.ShapeDtypeStruct((n_seq * lq, SWA_Q_DIM), BF16),
        compiler_params=_cparams(("parallel",)),
        name=name,
    )(sinks, h, h, h, k_past, v_past)


def _layer_norm(x, g, b):
    mu = jnp.mean(x, axis=-1, keepdims=True)
    xc = x - mu
    var = jnp.mean(jnp.square(xc), axis=-1, keepdims=True)
    return xc * lax.rsqrt(var + EPS) * g + b


def _pack_bf16_pairs(x, s):
    lo = lax.bitcast_convert_type(x[:, (2 * s) * LANES:(2 * s + 1) * LANES].astype(BF16).astype(F32), I32)
    hi = lax.bitcast_convert_type(x[:, (2 * s + 1) * LANES:(2 * s + 2) * LANES].astype(BF16).astype(F32), I32)
    return lax.shift_right_logical(lo, 16) | (hi & HIGH_HALF)


def _unpack_bf16_pairs(w):
    lo = lax.bitcast_convert_type(lax.shift_left(w, 16), F32).astype(BF16)
    hi = lax.bitcast_convert_type(w & HIGH_HALF, F32).astype(BF16)
    return lo, hi


def _merge_kernel(oap_ref, oas_ref, obp_ref, obs_ref, ga_ref, gb_ref, xp_ref, xs_ref,
                  wpa_ref, wpb_ref, wout_ref, g_ref, b_ref, o_ref, xw_ref, *, np_tiles):
    i = pl.program_id(0)
    tm = o_ref.shape[0]
    br_a = jnp.dot(_pair_value(i, np_tiles, oap_ref, oas_ref), wpa_ref[...], preferred_element_type=F32)
    br_b = jnp.dot(_pair_value(i, np_tiles, obp_ref, obs_ref), wpb_ref[...], preferred_element_type=F32)
    merged = (jax.nn.sigmoid(ga_ref[...].astype(F32)) * br_a
              + jax.nn.sigmoid(gb_ref[...].astype(F32)) * br_b)
    mix = jnp.dot(merged.astype(BF16), wout_ref[...], preferred_element_type=F32)
    x = _pair_value(i, np_tiles, xp_ref, xs_ref)
    x1 = _layer_norm(DEEPNORM_ALPHA * x + mix, g_ref[...], b_ref[...])
    o_ref[...] = x1
    for s in range(PACKED_ROWS):
        xw_ref[pl.ds(s, tm, stride=PACKED_ROWS), :] = _pack_bf16_pairs(x1, s)


def _merge(oa, ob, h, x, wpa, wpb, wout, g, b, *, tm, name):
    t = h.shape[0]
    np_tiles = x[0].shape[0] // tm
    p_map, s_map = _pair_maps(np_tiles)
    row = lambda i: (i, 0)
    full = lambda i: (0, 0)
    pair = [pl.BlockSpec((tm, D_MODEL), p_map), pl.BlockSpec((tm, D_MODEL), s_map)]
    return pl.pallas_call(
        functools.partial(_merge_kernel, np_tiles=np_tiles),
        grid=(t // tm,),
        in_specs=pair + pair + [
                  pl.BlockSpec((tm, D_MODEL), lambda i: (i, H_GATE_A // D_MODEL)),
                  pl.BlockSpec((tm, D_MODEL), lambda i: (i, H_GATE_B // D_MODEL))] + pair + [
                  pl.BlockSpec((D_MODEL, D_MODEL), full),
                  pl.BlockSpec((D_MODEL, D_MODEL), full),
                  pl.BlockSpec((D_MODEL, D_MODEL), full),
                  pl.BlockSpec((1, D_MODEL), full),
                  pl.BlockSpec((1, D_MODEL), full)],
        out_specs=[pl.BlockSpec((tm, D_MODEL), row),
                   pl.BlockSpec((tm * PACKED_ROWS, LANES), row)],
        out_shape=[jax.ShapeDtypeStruct((t, D_MODEL), F32),
                   jax.ShapeDtypeStruct((t * PACKED_ROWS, LANES), I32)],
        compiler_params=_cparams(("parallel",)),
        name=name,
    )(*oa, *ob, h, h, *x, wpa, wpb, wout, g, b)


def _router_kernel(x_ref, whi_ref, wlo_ref, bias_ref, idx_ref, w_ref, rank_ref, cntc_ref, cntr_ref,
                   cntc_scr, cntr_scr):
    i = pl.program_id(0)
    tm = x_ref.shape[0]

    @pl.when(i == 0)
    def _():
        cntc_scr[...] = jnp.zeros_like(cntc_scr)
        cntr_scr[...] = jnp.zeros_like(cntr_scr)

    x_hi, x_lo = _split_bf16(x_ref[...])
    nt = (((1,), (1,)), ((), ()))
    logits = (lax.dot_general(whi_ref[...], x_hi, nt, preferred_element_type=F32)
              + lax.dot_general(whi_ref[...], x_lo, nt, preferred_element_type=F32)
              + lax.dot_general(wlo_ref[...], x_hi, nt, preferred_element_type=F32))
    scores = jax.nn.sigmoid(logits)
    biased = scores + bias_ref[...]

    grouped = biased.reshape(N_GROUPS, GROUP_SIZE, tm)
    m1 = jnp.max(grouped, axis=1)
    n_top = jnp.sum(jnp.where(grouped == m1[:, None, :], 1.0, 0.0), axis=1)
    m2 = jnp.max(jnp.where(grouped < m1[:, None, :], grouped, -jnp.inf), axis=1)
    gscore = m1 + jnp.where(n_top >= 2.0, m1, m2)

    g_iota = lax.broadcasted_iota(I32, (N_GROUPS, tm), 0)
    beaten = jnp.zeros((N_GROUPS, tm), I32)
    for g in range(N_GROUPS):
        other = gscore[g:g + 1, :]
        ahead = (other > gscore) | ((other == gscore) & (g < g_iota))
        beaten = beaten + jnp.where(ahead, 1, 0)
    keep = jnp.where(beaten < TOPK_GROUPS, 1.0, 0.0)
    masked = jnp.where(keep[:, None, :] > 0.5, grouped, -jnp.inf).reshape(N_EXPERTS, tm)

    e_iota = lax.broadcasted_iota(I32, (N_EXPERTS, tm), 0)
    sel_f = jnp.zeros((N_EXPERTS, tm), F32)
    ids = []
    for _ in range(TOP_K):
        best = jnp.max(masked, axis=0, keepdims=True)
        idx = jnp.min(jnp.where(masked == best, e_iota, N_EXPERTS), axis=0, keepdims=True)
        hit = e_iota == idx
        sel_f = sel_f + jnp.where(hit, 1.0, 0.0)
        masked = jnp.where(hit, -jnp.inf, masked)
        ids.append(idx)

    top_sum = jnp.sum(sel_f * scores, axis=0, keepdims=True)

    t_r = lax.broadcasted_iota(I32, (tm, tm), 0)
    t_c = lax.broadcasted_iota(I32, (tm, tm), 1)
    before = jnp.where(t_r < t_c, 1.0, 0.0).astype(BF16)
    sel_b = sel_f.astype(BF16)
    rank = jnp.dot(sel_b, before, preferred_element_type=F32) + cntc_scr[...]
    cntc_scr[...] = cntc_scr[...] + jnp.sum(sel_f, axis=1, keepdims=True)
    cntr_scr[...] = cntr_scr[...] + lax.dot_general(jnp.ones((SUBLANES, tm), BF16), sel_b, nt,
                                                    preferred_element_type=F32)

    idx_rows, w_rows, rank_rows = [], [], []
    for idx in ids:
        hit = e_iota == idx
        w = jnp.sum(jnp.where(hit, scores, 0.0), axis=0, keepdims=True)
        w_rows.append(w / top_sum * ROUTED_SCALE)
        rank_rows.append(jnp.sum(jnp.where(hit, rank, 0.0), axis=0, keepdims=True))
        idx_rows.append(idx)
    idx_ref[...] = jnp.concatenate(idx_rows, axis=0)
    w_ref[...] = jnp.concatenate(w_rows, axis=0)
    rank_ref[...] = jnp.concatenate(rank_rows, axis=0).astype(I32)

    @pl.when(i == pl.num_programs(0) - 1)
    def _():
        cntc_ref[...] = cntc_scr[...].astype(I32)
        cntr_ref[...] = cntr_scr[...].astype(I32)


def _router(x1, w_hi, w_lo, bias, *, tm, name):
    t = x1.shape[0]
    full = lambda i: (0, 0)
    tok = lambda i: (0, i)
    return pl.pallas_call(
        _router_kernel,
        grid=(t // tm,),
        in_specs=[pl.BlockSpec((tm, D_MODEL), lambda i: (i, 0)),
                  pl.BlockSpec((N_EXPERTS, D_MODEL), full),
                  pl.BlockSpec((N_EXPERTS, D_MODEL), full),
                  pl.BlockSpec((N_EXPERTS, 1), full)],
        out_specs=[pl.BlockSpec((TOP_K, tm), tok),
                   pl.BlockSpec((TOP_K, tm), tok),
                   pl.BlockSpec((TOP_K, tm), tok),
                   pl.BlockSpec((N_EXPERTS, 1), full),
                   pl.BlockSpec((SUBLANES, N_EXPERTS), full)],
        out_shape=[jax.ShapeDtypeStruct((TOP_K, t), I32),
                   jax.ShapeDtypeStruct((TOP_K, t), F32),
                   jax.ShapeDtypeStruct((TOP_K, t), I32),
                   jax.ShapeDtypeStruct((N_EXPERTS, 1), I32),
                   jax.ShapeDtypeStruct((SUBLANES, N_EXPERTS), I32)],
        scratch_shapes=[pltpu.VMEM((N_EXPERTS, 1), F32),
                        pltpu.VMEM((SUBLANES, N_EXPERTS), F32)],
        compiler_params=_cparams(("arbitrary",)),
        name=name,
    )(x1, w_hi, w_lo, bias)


def _byte_split(v):
    return lax.shift_right_logical(v, 8).astype(F32), (v & 255).astype(F32)


def _plan_kernel(cntc_ref, cntr_ref, idx_ref, rank_ref, meta_ref, items_ref, m_scr, *, n_tok):
    i = pl.program_id(0)
    tm = idx_ref.shape[1]
    nbp = m_scr.shape[0]
    nip = items_ref.shape[1]
    n_rows = n_tok * TOP_K
    n_blk = n_rows // EXPERT_ROWS
    nt = (((1,), (1,)), ((), ()))

    r_i = lax.broadcasted_iota(I32, (N_EXPERTS, N_EXPERTS), 0)
    c_i = lax.broadcasted_iota(I32, (N_EXPERTS, N_EXPERTS), 1)
    below = jnp.where(c_i < r_i, 1.0, 0.0).astype(BF16)
    ones_c = jnp.ones((N_EXPERTS, LANES), F32)
    c_hi, c_lo = _byte_split(cntc_ref[...])
    start_col = (256.0 * jnp.dot(below, (c_hi * ones_c).astype(BF16), preferred_element_type=F32)
                 + jnp.dot(below, (c_lo * ones_c).astype(BF16), preferred_element_type=F32))[:, 0:1]

    @pl.when(i == 0)
    def _():
        m_scr[...] = jnp.zeros_like(m_scr)
        above = jnp.where(r_i < c_i, 1.0, 0.0).astype(BF16)
        r_hi, r_lo = _byte_split(cntr_ref[...])
        start_row = (256.0 * jnp.dot(r_hi.astype(BF16), above, preferred_element_type=F32)
                     + jnp.dot(r_lo.astype(BF16), above, preferred_element_type=F32))[0:1, :]
        b_col = lax.broadcasted_iota(I32, (nbp, 1), 0)
        b_row = lax.broadcasted_iota(I32, (1, nbp), 1)
        blk_col = jnp.where(b_col < n_blk, b_col * EXPERT_ROWS, n_rows).astype(F32)
        blk_row = jnp.where(b_row < n_blk, b_row * EXPERT_ROWS, n_rows).astype(F32)
        v_col = jnp.concatenate([blk_col, start_col], axis=0)
        v_row = jnp.concatenate([blk_row, start_row], axis=1)
        j_col = lax.broadcasted_iota(I32, (nip, 1), 0)
        k_row = lax.broadcasted_iota(I32, (1, nip), 1)
        ahead = (v_row < v_col) | ((v_row == v_col) & (k_row < j_col))
        order_col = jnp.sum(jnp.where(ahead, 1.0, 0.0), axis=1, keepdims=True)
        pos_row = k_row.astype(F32)
        lo_abs = jnp.sum(jnp.where(order_col == pos_row, v_col, 0.0), axis=0, keepdims=True)
        hi_abs = jnp.sum(jnp.where(order_col == pos_row + 1.0, v_col, 0.0), axis=0, keepdims=True)
        hi_abs = jnp.where(k_row == nip - 1, float(n_rows), hi_abs)
        blk = jnp.minimum(jnp.floor(lo_abs * (1.0 / EXPERT_ROWS)), n_blk - 1.0)
        expert = jnp.sum(jnp.where(start_col <= lo_abs, 1.0, 0.0), axis=0, keepdims=True) - 1.0
        base = blk * EXPERT_ROWS
        items_ref[...] = jnp.concatenate([blk, expert, lo_abs - base, hi_abs - base], axis=0).astype(I32)

    e_iota = lax.broadcasted_iota(I32, (N_EXPERTS, tm), 0)
    b_iota = lax.broadcasted_iota(I32, (nbp, tm), 0)
    l_iota = lax.broadcasted_iota(I32, (EXPERT_ROWS, tm), 0)
    tok = i * tm + lax.broadcasted_iota(I32, (1, tm), 1)
    tok_hi, tok_lo = _byte_split(tok)
    acc = jnp.zeros(m_scr.shape, F32)
    for k in range(TOP_K):
        hit = e_iota == idx_ref[k:k + 1, :]
        dest = (jnp.sum(jnp.where(hit, start_col, 0.0), axis=0, keepdims=True).astype(I32)
                + rank_ref[k:k + 1, :])
        oh_blk = jnp.where(b_iota == lax.shift_right_logical(dest, 8), 1.0, 0.0).astype(BF16)
        in_blk = l_iota == (dest & (EXPERT_ROWS - 1))
        vals = jnp.concatenate([jnp.where(in_blk, tok_hi, 0.0), jnp.where(in_blk, tok_lo, 0.0),
                                jnp.where(in_blk, float(k), 0.0)], axis=0).astype(BF16)
        acc = acc + lax.dot_general(oh_blk, vals, nt, preferred_element_type=F32)
    m_scr[...] = m_scr[...] + acc

    @pl.when(i == pl.num_programs(0) - 1)
    def _():
        m = m_scr[...]
        row_tok = m[:, 0:EXPERT_ROWS] * 256.0 + m[:, EXPERT_ROWS:2 * EXPERT_ROWS]
        row_slot = m[:, 2 * EXPERT_ROWS:3 * EXPERT_ROWS] * float(n_tok) + row_tok
        meta_ref[...] = jnp.concatenate([row_tok, row_slot], axis=1).astype(I32)


def _plan(cnt_col, cnt_row, idx, rank, *, tm, name):
    t = idx.shape[1]
    assert EXPERT_ROWS == 256 and (t * TOP_K) % (2 * EXPERT_ROWS) == 0
    n_blk = t * TOP_K // EXPERT_ROWS
    nbp = -(-n_blk // LANES) * LANES
    nip = nbp + N_EXPERTS
    tok = lambda i: (0, i)
    full = lambda i: (0, 0)
    return pl.pallas_call(
        functools.partial(_plan_kernel, n_tok=t),
        grid=(t // tm,),
        in_specs=[pl.BlockSpec((N_EXPERTS, 1), full),
                  pl.BlockSpec((SUBLANES, N_EXPERTS), full),
                  pl.BlockSpec((TOP_K, tm), tok),
                  pl.BlockSpec((TOP_K, tm), tok)],
        out_specs=[pl.BlockSpec((nbp, 2 * EXPERT_ROWS), full),
                   pl.BlockSpec((4, nip), full)],
        out_shape=[jax.ShapeDtypeStruct((nbp, 2 * EXPERT_ROWS), I32),
                   jax.ShapeDtypeStruct((4, nip), I32)],
        scratch_shapes=[pltpu.VMEM((nbp, 3 * EXPERT_ROWS), F32)],
        compiler_params=_cparams(("arbitrary",)),
        name=name,
    )(cnt_col, cnt_row, idx, rank)


META_CHUNK = 4 * EXPERT_ROWS
GATHER_STRIDE = EXPERT_ROWS + 8


def _experts_kernel(items_ref, xw_hbm, meta_hbm, wg_ref, wu_ref, wd_ref, ys_hbm,
                    xw, meta, tile, lhs, wg_b, wu_b, wd_b, yacc, ybuf, sem_x, sem_m, sem_y, *, nip, n_blk):
    i = pl.program_id(0)
    blk = items_ref[i]
    expert = items_ref[nip + i]
    lo = items_ref[2 * nip + i]
    hi = items_ref[3 * nip + i]
    prev_expert = items_ref[nip + jnp.maximum(i - 1, 0)]
    nonempty = hi > lo
    par = blk & 1
    chunk = lax.shift_right_logical(blk, 1)
    cpar = chunk & 1
    mbase = cpar * META_CHUNK + par * (2 * EXPERT_ROWS)

    def meta_copy(c, slot):
        return pltpu.make_async_copy(meta_hbm.at[pl.ds(pl.multiple_of(c * META_CHUNK, META_CHUNK), META_CHUNK)],
                                     meta.at[pl.ds(pl.multiple_of(slot * META_CHUNK, META_CHUNK), META_CHUNK)],
                                     sem_m.at[slot])

    def ybuf_drain(slot):
        pltpu.make_async_copy(ys_hbm.at[pl.ds(0, EXPERT_ROWS * ROW_CHUNKS), :], ybuf.at[slot],
                              sem_y.at[slot]).wait()

    def ybuf_fill_and_send(slot):
        for j in range(ROW_CHUNKS):
            ybuf[slot, pl.ds(j, EXPERT_ROWS, stride=ROW_CHUNKS), :] = yacc[:, j * LANES:(j + 1) * LANES]

        for r in range(EXPERT_ROWS):
            src = ybuf.at[slot, pl.ds(r * ROW_CHUNKS, ROW_CHUNKS), :]
            dst_row = pl.multiple_of(meta[mbase + EXPERT_ROWS + r] * ROW_CHUNKS, ROW_CHUNKS)
            pltpu.make_async_copy(src, ys_hbm.at[pl.ds(dst_row, ROW_CHUNKS), :],
                                  sem_y.at[slot]).start(priority=r % 2)

    @pl.when(i == 0)
    def _():
        resident = pltpu.make_async_copy(xw_hbm, xw, sem_x)
        resident.start()
        meta_copy(0, 0).start()
        resident.wait()

    @pl.when((i == 0) | (expert != prev_expert))
    def _():
        wg_b[...] = wg_ref[0].astype(BF16)
        wu_b[...] = wu_ref[0].astype(BF16)
        wd_b[...] = wd_ref[0].astype(BF16)

    @pl.when(nonempty & (lo == 0))
    def _():
        @pl.when(par == 0)
        def _():
            meta_copy(chunk, cpar).wait()

            @pl.when(2 * (chunk + 1) < n_blk)
            def _():
                meta_copy(chunk + 1, 1 - cpar).start()

        for r in range(EXPERT_ROWS):
            t4 = pl.multiple_of(meta[mbase + r] * PACKED_ROWS, PACKED_ROWS)
            tile[pl.ds(r, PACKED_ROWS, stride=GATHER_STRIDE), :] = xw[pl.ds(t4, PACKED_ROWS), :]
        cols = []
        for s in range(PACKED_ROWS):
            cols.extend(_unpack_bf16_pairs(tile[pl.ds(s * GATHER_STRIDE, EXPERT_ROWS), :]))
        lhs[...] = jnp.concatenate(cols, axis=-1)

    def ffn(x):
        gate = jnp.dot(x, wg_b[...], preferred_element_type=F32)
        up = jnp.dot(x, wu_b[...], preferred_element_type=F32)
        hid = (jax.nn.silu(gate) * up).astype(BF16)
        return jnp.dot(hid, wd_b[...], preferred_element_type=F32)

    whole = (lo == 0) & (hi == EXPERT_ROWS)

    @pl.when(whole)
    def _():
        yacc[...] = ffn(lhs[...])

    half_rows = EXPERT_ROWS // 2
    for half in range(2):
        r0 = half * half_rows

        @pl.when(nonempty & jnp.logical_not(whole) & (lo < r0 + half_rows) & (hi > r0))
        def _():
            y = ffn(lhs[r0:r0 + half_rows, :])
            row = r0 + lax.broadcasted_iota(I32, (half_rows, 1), 0)
            y = jnp.where((row >= lo) & (row < hi), y, 0.0)

            @pl.when(lo <= r0)
            def _():
                yacc[r0:r0 + half_rows, :] = y

            @pl.when(lo > r0)
            def _():
                yacc[r0:r0 + half_rows, :] = yacc[r0:r0 + half_rows, :] + y

    @pl.when(nonempty & (hi == EXPERT_ROWS))
    def _():
        @pl.when(blk >= 2)
        def _():
            ybuf_drain(par)

        for slot in range(2):
            @pl.when(par == slot)
            def _():
                ybuf_fill_and_send(slot)

    @pl.when(i == pl.num_programs(0) - 1)
    def _():
        ybuf_drain(0)
        ybuf_drain(1)


def _experts(items, xw, meta, wg, wu, wd, *, n_tok, name):
    nip = items.shape[0] // 4
    n_rows = n_tok * TOP_K
    n_blk = n_rows // EXPERT_ROWS
    n_items = n_blk + N_EXPERTS

    def wmap(i, items_ref):
        return (items_ref[nip + i], 0, 0)

    kern = functools.partial(_experts_kernel, nip=nip, n_blk=n_blk)
    any_spec = pl.BlockSpec(memory_space=pl.ANY)
    return pl.pallas_call(
        kern,
        grid_spec=pltpu.PrefetchScalarGridSpec(
            num_scalar_prefetch=1,
            grid=(n_items,),
            in_specs=[any_spec, any_spec,
                      pl.BlockSpec((1, D_MODEL, EXPERT_FF), wmap),
                      pl.BlockSpec((1, D_MODEL, EXPERT_FF), wmap),
                      pl.BlockSpec((1, EXPERT_FF, D_MODEL), wmap)],
            out_specs=any_spec,
            scratch_shapes=[pltpu.VMEM(xw.shape, I32),
                            pltpu.SMEM((2 * META_CHUNK,), I32),
                            pltpu.VMEM((PACKED_ROWS * GATHER_STRIDE, LANES), I32),
                            pltpu.VMEM((EXPERT_ROWS, D_MODEL), BF16),
                            pltpu.VMEM((D_MODEL, EXPERT_FF), BF16),
                            pltpu.VMEM((D_MODEL, EXPERT_FF), BF16),
                            pltpu.VMEM((EXPERT_FF, D_MODEL), BF16),
                            pltpu.VMEM((EXPERT_ROWS, D_MODEL), F32),
                            pltpu.VMEM((2, EXPERT_ROWS * ROW_CHUNKS, LANES), F32),
                            pltpu.SemaphoreType.DMA,
                            pltpu.SemaphoreType.DMA((2,)),
                            pltpu.SemaphoreType.DMA((2,))],
        ),
        out_shape=jax.ShapeDtypeStruct((n_rows * ROW_CHUNKS, LANES), F32),
        compiler_params=_cparams(("arbitrary",)),
        name=name,
    )(items, xw, meta, wg, wu, wd)


def _combine_kernel(ys_ref, x_ref, w_ref, wsg_ref, wsu_ref, wsd_ref, g_ref, b_ref, op_ref, os_ref,
                    *, np_tiles):
    x = x_ref[...]
    xb = x.astype(BF16)
    hid = (jax.nn.silu(jnp.dot(xb, wsg_ref[...], preferred_element_type=F32))
           * jnp.dot(xb, wsu_ref[...], preferred_element_type=F32))
    shared = jnp.dot(hid.astype(BF16), wsd_ref[...], preferred_element_type=F32)
    w = w_ref[...]
    chunks = []
    for j in range(ROW_CHUNKS):
        acc = None
        for k in range(TOP_K):
            part = w[:, k:k + 1] * ys_ref[k, :, j, :]
            acc = part if acc is None else acc + part
        chunks.append(acc)
    ffn = jnp.concatenate(chunks, axis=-1) + shared
    out = _layer_norm(DEEPNORM_ALPHA * x + ffn, g_ref[...], b_ref[...])
    i = pl.program_id(0)

    @pl.when(i < np_tiles)
    def _():
        op_ref[...] = out

    @pl.when(i >= np_tiles)
    def _():
        os_ref[...] = out


def _combine(ys, x1, w_tok, wsg, wsu, wsd, g, b, *, n_prompt, tm, name):
    t = x1.shape[0]
    np_tiles = n_prompt // tm
    p_map, s_map = _pair_maps(np_tiles)
    row = lambda i: (i, 0)
    full = lambda i: (0, 0)
    return pl.pallas_call(
        functools.partial(_combine_kernel, np_tiles=np_tiles),
        grid=(t // tm,),
        in_specs=[pl.BlockSpec((TOP_K, tm, ROW_CHUNKS, LANES), lambda i: (0, i, 0, 0)),
                  pl.BlockSpec((tm, D_MODEL), row),
                  pl.BlockSpec((tm, TOP_K), row),
                  pl.BlockSpec((D_MODEL, SHARED_FF), full),
                  pl.BlockSpec((D_MODEL, SHARED_FF), full),
                  pl.BlockSpec((SHARED_FF, D_MODEL), full),
                  pl.BlockSpec((1, D_MODEL), full),
                  pl.BlockSpec((1, D_MODEL), full)],
        out_specs=[pl.BlockSpec((tm, D_MODEL), p_map), pl.BlockSpec((tm, D_MODEL), s_map)],
        out_shape=[jax.ShapeDtypeStruct((n_prompt, D_MODEL), F32),
                   jax.ShapeDtypeStruct((t - n_prompt, D_MODEL), F32)],
        compiler_params=_cparams(("arbitrary",)),
        name=name,
    )(ys, x1, w_tok, wsg, wsu, wsd, g, b)


def _pack_w_in(w_in):
    sizes = (GLA_KEY_DIM, GLA_KEY_DIM, GLA_VAL_DIM, GLA_VAL_DIM, GLA_GATE_RANK,
             SWA_Q_DIM, SWA_KV_DIM, SWA_KV_DIM, D_MODEL, D_MODEL)
    offs = [0]
    for s in sizes:
        offs.append(offs[-1] + s)
    qa, ka, va, ga, gk, qb, kb, vb, gate_a, gate_b = (w_in[:, offs[i]:offs[i + 1]] for i in range(10))
    pad = lambda w, n: jnp.pad(w, ((0, 0), (0, n - w.shape[1])))
    packed = jnp.concatenate([qa, ka, va, ga, qb, gate_a, gate_b, kb, vb, pad(gk, 2 * LANES)], axis=1)
    assert packed.shape[1] == H_WIDTH
    return packed.astype(BF16), jnp.concatenate([kb, vb], axis=1).astype(BF16)


def kernel(x_prompt, x_sample, state_gla, cache_swa_k, cache_swa_v, w_in, w_gk_up, b_gk, gla_norm_g,
           attn_sinks, w_proj_a, w_proj_b, w_out, ln1_g, ln1_b, w_router, router_bias,
           w_expert_gate, w_expert_up, w_expert_down, w_shared_gate, w_shared_up, w_shared_down,
           ln2_g, ln2_b):
    assert w_in.shape[0] == 1, "single-layer trunk"
    bp, lp, d = x_prompt.shape
    bs, ls, _ = x_sample.shape
    assert d == D_MODEL and ls == SUBLANES and cache_swa_k.shape[2] == WINDOW
    tp, ts = bp * lp, bs * ls
    t = tp + ts

    xp = x_prompt.reshape(tp, d)
    xs = x_sample.reshape(ts, d)
    w_main, w_kv = _pack_w_in(w_in[0])
    h = _matmul(xp, xs, w_main, BF16, _pair_tile(tp, ts, 1024), H_TN, "proj_in")

    xp_tail = x_prompt[:, lp - WINDOW:].reshape(bp * WINDOW, d)
    kv_tail = _matmul(xp_tail, xs, w_kv, F32, _pair_tile(bp * WINDOW, ts, 512), 2 * SWA_KV_DIM,
                      "proj_kv_tail")

    wup = jnp.pad(w_gk_up[0], ((0, LANES - GLA_GATE_RANK), (0, 0))).astype(BF16)
    bgk = b_gk[0].reshape(1, GLA_KEY_DIM)
    gn = gla_norm_g[0].reshape(1, GLA_DV)
    oa_p, s_prompt = _gla(h, wup, bgk, gn, None, row0=0, n_seq=bp, seq_len=lp, nb=1, c=GLA_CHUNK,
                          sub=4, name="gla_prompt")
    oa_s, s_sample = _gla(h, wup, bgk, gn, state_gla[0], row0=tp, n_seq=bs, seq_len=ls, nb=8,
                          c=math.gcd(ls, GLA_CHUNK), sub=1, name="gla_sample")

    sinks = attn_sinks[0]
    k_past = cache_swa_k[0].reshape(bs, WINDOW, SWA_KV_DIM)
    v_past = cache_swa_v[0].reshape(bs, WINDOW, SWA_KV_DIM)
    ob_p = _swa_prompt(h, sinks, n_seq=bp, seq_len=lp, name="swa_prompt")
    ob_s = _swa_sample(h, sinks, k_past, v_past, row0=tp, n_seq=bs, lq=ls, nb=8, name="swa_sample")

    x1, xw = _merge((oa_p, oa_s), (ob_p, ob_s), h, (xp, xs), w_proj_a[0].astype(BF16),
                    w_proj_b[0].astype(BF16), w_out[0].astype(BF16), ln1_g[0].reshape(1, d),
                    ln1_b[0].reshape(1, d), tm=_pair_tile(tp, ts, 512), name="merge_ln1")

    wr_t = w_router[0].T
    wr_hi = wr_t.astype(BF16)
    wr_lo = (wr_t - wr_hi.astype(F32)).astype(BF16)
    tm_r = _tile(t, 512)
    idx, w_top, rank, cnt_col, cnt_row = _router(x1, wr_hi, wr_lo, router_bias[0].reshape(N_EXPERTS, 1),
                                                 tm=tm_r, name="router")
    meta, items = _plan(cnt_col, cnt_row, idx, rank, tm=tm_r, name="plan")
    ys = _experts(items.reshape(-1), xw, meta.reshape(-1), w_expert_gate[0], w_expert_up[0],
                  w_expert_down[0], n_tok=t, name="experts")
    y_p, y_s = _combine(ys.reshape(TOP_K, t, ROW_CHUNKS, LANES), x1, w_top.T,
                        w_shared_gate[0].astype(BF16), w_shared_up[0].astype(BF16),
                        w_shared_down[0].astype(BF16), ln2_g[0].reshape(1, d), ln2_b[0].reshape(1, d),
                        n_prompt=tp, tm=_pair_tile(tp, ts, 128), name="combine_ln2")

    y_prompt = y_p.reshape(bp, lp, d)
    y_sample = y_s.reshape(bs, ls, d)
    k_tail = kv_tail[:, :SWA_KV_DIM]
    v_tail = kv_tail[:, SWA_KV_DIM:]
    kv_shape = (SWA_KV_HEADS, SWA_HEAD_DIM)
    k_prompt = k_tail[:bp * WINDOW].reshape(1, bp, WINDOW, *kv_shape)
    v_prompt = v_tail[:bp * WINDOW].reshape(1, bp, WINDOW, *kv_shape)
    k_new = k_tail[bp * WINDOW:].reshape(bs, ls, *kv_shape)
    v_new = v_tail[bp * WINDOW:].reshape(bs, ls, *kv_shape)
    k_sample = jnp.concatenate([cache_swa_k[0][:, ls:], k_new], axis=1)[None]
    v_sample = jnp.concatenate([cache_swa_v[0][:, ls:], v_new], axis=1)[None]
    return (y_prompt, y_sample, s_prompt[None], s_sample[None], k_prompt, v_prompt, k_sample, v_sample)
```

```python
import functools
import math

import jax
import jax.numpy as jnp
from jax import lax
from jax.experimental import pallas as pl
from jax.experimental.pallas import tpu as pltpu

F32 = jnp.float32
BF16 = jnp.bfloat16
I32 = jnp.int32

D_MODEL = 1024
GLA_HEADS = 4
GLA_DK = 128
GLA_DV = 256
GLA_KEY_DIM = GLA_HEADS * GLA_DK
GLA_VAL_DIM = GLA_HEADS * GLA_DV
GLA_GATE_RANK = 16
GLA_GATE_NORMALIZER = 16.0
GLA_CHUNK = 64
SWA_HEADS = 16
SWA_KV_HEADS = 4
SWA_GROUP = SWA_HEADS // SWA_KV_HEADS
SWA_HEAD_DIM = 64
SWA_Q_DIM = SWA_HEADS * SWA_HEAD_DIM
SWA_KV_DIM = SWA_KV_HEADS * SWA_HEAD_DIM
WINDOW = 128
N_EXPERTS = 256
TOP_K = 8
N_GROUPS = 8
GROUP_SIZE = N_EXPERTS // N_GROUPS
TOPK_GROUPS = 4
EXPERT_FF = 256
SHARED_FF = 256
ROUTED_SCALE = 2.5
DEEPNORM_ALPHA = 2.0 ** 0.25
EPS = 1e-5

LANES = 128
SUBLANES = 8
ROW_CHUNKS = D_MODEL // LANES
VMEM_LIMIT = 56 * 1024 * 1024

H_QA, H_KA, H_VA, H_GA, H_QB, H_GATE_A, H_GATE_B, H_KB, H_VB, H_GK = (
    0, 512, 1024, 2048, 3072, 4096, 5120, 6144, 6400, 6656)
H_WIDTH = 6912
H_TN = 2304

EXPERT_ROWS = 256
PACKED_ROWS = ROW_CHUNKS // 2
HIGH_HALF = -65536


def _cparams(sem, vmem=VMEM_LIMIT):
    return pltpu.CompilerParams(dimension_semantics=sem, vmem_limit_bytes=vmem)


def _tile(n, pref):
    t = min(n, pref)
    while n % t:
        t -= LANES
    assert t > 0 and t % LANES == 0, (n, pref)
    return t


def _pair_tile(n_p, n_s, pref):
    return _tile(math.gcd(n_p, n_s), pref)


def _pair_maps(np_tiles, col=0, extra=0):
    def p_map(i, *_):
        return (jnp.minimum(i, np_tiles - 1), col)

    def s_map(i, *_):
        return (jnp.maximum(i - np_tiles, 0), col)

    return p_map, s_map


def _pair_value(i, np_tiles, p_ref, s_ref):
    return jnp.where(i < np_tiles, p_ref[...], s_ref[...])


def _mm_kernel(xp_ref, xs_ref, w_ref, o_ref, *, np_tiles):
    x = _pair_value(pl.program_id(0), np_tiles, xp_ref, xs_ref)
    o_ref[...] = jnp.dot(x.astype(BF16), w_ref[...], preferred_element_type=F32).astype(o_ref.dtype)


def _matmul(xp, xs, w, out_dtype, tm, tn, name):
    k = xp.shape[1]
    m = xp.shape[0] + xs.shape[0]
    n = w.shape[1]
    np_tiles = xp.shape[0] // tm
    p_map, s_map = _pair_maps(np_tiles)
    return pl.pallas_call(
        functools.partial(_mm_kernel, np_tiles=np_tiles),
        grid=(m // tm, n // tn),
        in_specs=[pl.BlockSpec((tm, k), p_map),
                  pl.BlockSpec((tm, k), s_map),
                  pl.BlockSpec((k, tn), lambda i, j: (0, j))],
        out_specs=pl.BlockSpec((tm, tn), lambda i, j: (i, j)),
        out_shape=jax.ShapeDtypeStruct((m, n), out_dtype),
        compiler_params=_cparams(("parallel", "arbitrary")),
        name=name,
    )(xp, xs, w)


def _split_bf16(x):
    hi = x.astype(BF16)
    lo = (x - hi.astype(F32)).astype(BF16)
    return hi, lo


def _gla_kernel(*refs, nb, c, sub, has_s0):
    if has_s0:
        (q_ref, k_ref, v_ref, ga_ref, gk_ref, wup_ref, bgk_ref, gn_ref, s0_ref,
         o_ref, sout_ref, s_scr) = refs
    else:
        (q_ref, k_ref, v_ref, ga_ref, gk_ref, wup_ref, bgk_ref, gn_ref,
         o_ref, sout_ref, s_scr) = refs
        s0_ref = None
    ci = pl.program_id(1)
    rows = nb * c

    @pl.when(ci == 0)
    def _():
        if has_s0:
            s_scr[...] = s0_ref[...]
        else:
            s_scr[...] = jnp.zeros_like(s_scr)

    r_i = lax.broadcasted_iota(I32, (rows, rows), 0)
    c_i = lax.broadcasted_iota(I32, (rows, rows), 1)
    same_seq = (r_i // c) == (c_i // c)
    causal = same_seq & (c_i <= r_i)
    tri = jnp.where(causal, 1.0, 0.0).astype(BF16)
    seg = jnp.where(same_seq, 1.0, 0.0).astype(BF16)
    ones_kv = jnp.ones((rows, GLA_DV), BF16)
    seq_of_row = lax.broadcasted_iota(I32, (rows, 1), 0) // c
    tn = (((0,), (0,)), ((), ()))

    for s in range(sub):
        rs = slice(s * rows, (s + 1) * rows)
        pre = jnp.dot(gk_ref[rs, :], wup_ref[...], preferred_element_type=F32) + bgk_ref[...]
        log_a = jax.nn.log_sigmoid(pre) / GLA_GATE_NORMALIZER
        la_hi, la_lo = _split_bf16(log_a)
        b = (jnp.dot(tri, la_hi, preferred_element_type=F32)
             + jnp.dot(tri, la_lo, preferred_element_type=F32))
        b_last = (jnp.dot(seg, la_hi, preferred_element_type=F32)
                  + jnp.dot(seg, la_lo, preferred_element_type=F32))
        e_pos = jnp.exp(b)
        e_neg = jnp.exp(-b)
        e_rem = jnp.exp(b_last - b)
        q = q_ref[rs, :].astype(F32) * (GLA_DK ** -0.5)
        k = k_ref[rs, :].astype(F32)
        q_dec = (q * e_pos).astype(BF16)
        k_dec = (k * e_neg).astype(BF16)
        k_rem = k * e_rem
        outs = []
        for h in range(GLA_HEADS):
            ks = slice(h * GLA_DK, (h + 1) * GLA_DK)
            vh = v_ref[rs, h * GLA_DV:(h + 1) * GLA_DV]
            att = lax.dot_general(q_dec[:, ks], k_dec[:, ks], (((1,), (1,)), ((), ())),
                                  preferred_element_type=F32)
            att = jnp.where(causal, att, 0.0).astype(BF16)
            o_h = jnp.dot(att, vh, preferred_element_type=F32)
            for j in range(nb):
                mine = (seq_of_row == j) if nb > 1 else None
                pick = (lambda a: jnp.where(mine, a, 0.0)) if nb > 1 else (lambda a: a)
                s_old = s_scr[j, h]
                o_h = o_h + pick(jnp.dot(q_dec[:, ks], s_old.astype(BF16), preferred_element_type=F32))
                dec = (lax.dot_general(pick(la_hi[:, ks].astype(F32)).astype(BF16), ones_kv, tn,
                                       preferred_element_type=F32)
                       + lax.dot_general(pick(la_lo[:, ks].astype(F32)).astype(BF16), ones_kv, tn,
                                         preferred_element_type=F32))
                upd = lax.dot_general(pick(k_rem[:, ks]).astype(BF16), vh, tn, preferred_element_type=F32)
                s_scr[j, h] = jnp.exp(dec) * s_old + upd
            o_h = o_h * lax.rsqrt(jnp.mean(jnp.square(o_h), axis=-1, keepdims=True) + EPS) * gn_ref[...]
            outs.append(o_h)
        o = jnp.concatenate(outs, axis=-1) * jax.nn.silu(ga_ref[rs, :].astype(F32))
        o_ref[rs, :] = o.astype(o_ref.dtype)

    @pl.when(ci == pl.num_programs(1) - 1)
    def _():
        sout_ref[...] = s_scr[...]


def _gla(h, wup, bgk, gn, s0, *, row0, n_seq, seq_len, nb, c, sub, name):
    rows = nb * c * sub
    n_groups = n_seq // nb
    n_steps = seq_len // (c * sub)
    rb0 = row0 // rows

    def rmap(col):
        return lambda g, i: (rb0 + g * n_steps + i, col)

    in_specs = [
        pl.BlockSpec((rows, GLA_KEY_DIM), rmap(H_QA // GLA_KEY_DIM)),
        pl.BlockSpec((rows, GLA_KEY_DIM), rmap(H_KA // GLA_KEY_DIM)),
        pl.BlockSpec((rows, GLA_VAL_DIM), rmap(H_VA // GLA_VAL_DIM)),
        pl.BlockSpec((rows, GLA_VAL_DIM), rmap(H_GA // GLA_VAL_DIM)),
        pl.BlockSpec((rows, LANES), rmap(H_GK // LANES)),
        pl.BlockSpec((LANES, GLA_KEY_DIM), lambda g, i: (0, 0)),
        pl.BlockSpec((1, GLA_KEY_DIM), lambda g, i: (0, 0)),
        pl.BlockSpec((1, GLA_DV), lambda g, i: (0, 0)),
    ]
    args = [h, h, h, h, h, wup, bgk, gn]
    state_spec = pl.BlockSpec((nb, GLA_HEADS, GLA_DK, GLA_DV), lambda g, i: (g, 0, 0, 0))
    if s0 is not None:
        in_specs.append(state_spec)
        args.append(s0)
    kern = functools.partial(_gla_kernel, nb=nb, c=c, sub=sub, has_s0=s0 is not None)
    return pl.pallas_call(
        kern,
        grid=(n_groups, n_steps),
        in_specs=in_specs,
        out_specs=[pl.BlockSpec((rows, GLA_VAL_DIM), lambda g, i: (g * n_steps + i, 0)), state_spec],
        out_shape=[jax.ShapeDtypeStruct((n_seq * seq_len, GLA_VAL_DIM), BF16),
                   jax.ShapeDtypeStruct((n_seq, GLA_HEADS, GLA_DK, GLA_DV), F32)],
        scratch_shapes=[pltpu.VMEM((nb, GLA_HEADS, GLA_DK, GLA_DV), F32)],
        compiler_params=_cparams(("parallel", "arbitrary")),
        name=name,
    )(*args)


def _alibi_slope(head):
    return 2.0 ** (-8.0 * (head + 1) / SWA_HEADS)


def _swa_softmax_pv(parts, sink):
    m = sink
    for s, _ in parts:
        m = jnp.maximum(m, jnp.max(s, axis=-1, keepdims=True))
    denom = jnp.exp(sink - m)
    acc = None
    for s, v in parts:
        p = jnp.exp(s - m)
        denom = denom + jnp.sum(p, axis=-1, keepdims=True)
        pv = jnp.dot(p.astype(BF16), v, preferred_element_type=F32)
        acc = pv if acc is None else acc + pv
    return acc / denom


def _swa_prompt_kernel(sink_ref, q_ref, kp_ref, vp_ref, kc_ref, vc_ref, o_ref):
    i = pl.program_id(1)
    span = 2 * WINDOW
    kcat = jnp.concatenate([kp_ref[...], kc_ref[...]], axis=0)
    vcat = jnp.concatenate([vp_ref[...], vc_ref[...]], axis=0)
    row = lax.broadcasted_iota(I32, (WINDOW, span), 0)
    col = lax.broadcasted_iota(I32, (WINDOW, span), 1)
    dist_i = row + WINDOW - col
    valid = (dist_i >= 0) & (dist_i < WINDOW) & ((col >= WINDOW) | (i > 0))
    dist = dist_i.astype(F32)
    outs = []
    for hh in range(SWA_HEADS):
        g = hh // SWA_GROUP
        gs = slice(g * SWA_HEAD_DIM, (g + 1) * SWA_HEAD_DIM)
        qh = q_ref[:, hh * SWA_HEAD_DIM:(hh + 1) * SWA_HEAD_DIM]
        s = lax.dot_general(qh, kcat[:, gs], (((1,), (1,)), ((), ())), preferred_element_type=F32)
        s = s * (SWA_HEAD_DIM ** -0.5) - _alibi_slope(hh) * dist
        s = jnp.where(valid, s, -jnp.inf)
        outs.append(_swa_softmax_pv([(s, vcat[:, gs])], sink_ref[hh]))
    o_ref[...] = jnp.concatenate(outs, axis=-1).astype(o_ref.dtype)


def _swa_prompt(h, sinks, *, n_seq, seq_len, name):
    nq = seq_len // WINDOW
    qcol = H_QB // SWA_Q_DIM
    kcol = H_KB // SWA_KV_DIM
    vcol = H_VB // SWA_KV_DIM

    def cur(col):
        return lambda b, i, sk: (b * nq + i, col)

    def prev(col):
        return lambda b, i, sk: (b * nq + jnp.maximum(i - 1, 0), col)

    return pl.pallas_call(
        _swa_prompt_kernel,
        grid_spec=pltpu.PrefetchScalarGridSpec(
            num_scalar_prefetch=1,
            grid=(n_seq, nq),
            in_specs=[pl.BlockSpec((WINDOW, SWA_Q_DIM), cur(qcol)),
                      pl.BlockSpec((WINDOW, SWA_KV_DIM), prev(kcol)),
                      pl.BlockSpec((WINDOW, SWA_KV_DIM), prev(vcol)),
                      pl.BlockSpec((WINDOW, SWA_KV_DIM), cur(kcol)),
                      pl.BlockSpec((WINDOW, SWA_KV_DIM), cur(vcol))],
            out_specs=pl.BlockSpec((WINDOW, SWA_Q_DIM), lambda b, i, sk: (b * nq + i, 0)),
        ),
        out_shape=jax.ShapeDtypeStruct((n_seq * seq_len, SWA_Q_DIM), BF16),
        compiler_params=_cparams(("parallel", "arbitrary")),
        name=name,
    )(sinks, h, h, h, h, h)


def _swa_sample_kernel(sink_ref, q_ref, kc_ref, vc_ref, kp_ref, vp_ref, o_ref, *, nb, lq):
    rows = SWA_HEADS * lq
    grp_rows = SWA_GROUP * lq
    hd = SWA_HEAD_DIM
    nt = (((1,), (1,)), ((), ()))
    head_of_row = lax.broadcasted_iota(I32, (rows, 1), 0) // lq
    slope = jnp.zeros((rows, 1), F32)
    sink = jnp.zeros((rows, 1), F32)
    for hh in range(SWA_HEADS):
        slope = jnp.where(head_of_row == hh, _alibi_slope(hh), slope)
        sink = jnp.where(head_of_row == hh, sink_ref[hh], sink)
    qi = lax.broadcasted_iota(I32, (rows, WINDOW), 0) % lq
    dist_p = qi + WINDOW - lax.broadcasted_iota(I32, (rows, WINDOW), 1)
    valid_p = dist_p < WINDOW
    bias_p = slope * dist_p.astype(F32)
    dist_c = lax.broadcasted_iota(I32, (rows, lq), 0) % lq - lax.broadcasted_iota(I32, (rows, lq), 1)
    valid_c = dist_c >= 0
    bias_c = slope * dist_c.astype(F32)
    scale = hd ** -0.5

    q_all = q_ref[...].astype(F32)
    kc_all = kc_ref[...].astype(F32)
    vc_all = vc_ref[...].astype(F32)
    seq_outs = []
    for j in range(nb):
        js = slice(j * lq, (j + 1) * lq)
        pieces = []
        for hh in range(SWA_HEADS):
            g = hh // SWA_GROUP
            parts = []
            if g:
                parts.append(jnp.zeros((lq, g * hd), F32))
            parts.append(q_all[js, hh * hd:(hh + 1) * hd])
            if g < SWA_KV_HEADS - 1:
                parts.append(jnp.zeros((lq, (SWA_KV_HEADS - 1 - g) * hd), F32))
            pieces.append(jnp.concatenate(parts, axis=-1))
        q_big = jnp.concatenate(pieces, axis=0).astype(BF16)
        kp = kp_ref[j].astype(BF16)
        vp = vp_ref[j].astype(BF16)
        kcj = kc_all[js, :].astype(BF16)
        vcj = vc_all[js, :].astype(BF16)
        s_p = lax.dot_general(q_big, kp, nt, preferred_element_type=F32)
        s_p = jnp.where(valid_p, s_p * scale - bias_p, -jnp.inf)
        s_c = lax.dot_general(q_big, kcj, nt, preferred_element_type=F32)
        s_c = jnp.where(valid_c, s_c * scale - bias_c, -jnp.inf)
        o_big = _swa_softmax_pv([(s_p, vp), (s_c, vcj)], sink)
        o_grp = [o_big[g * grp_rows:(g + 1) * grp_rows, g * hd:(g + 1) * hd] for g in range(SWA_KV_HEADS)]
        o_heads = jnp.concatenate(o_grp, axis=0)
        seq_outs.append(jnp.concatenate([o_heads[hh * lq:(hh + 1) * lq, :] for hh in range(SWA_HEADS)],
                                        axis=-1))
    o_ref[...] = jnp.concatenate(seq_outs, axis=0).astype(o_ref.dtype)


def _swa_sample(h, sinks, k_past, v_past, *, row0, n_seq, lq, nb, name):
    rows = nb * lq
    rb0 = row0 // rows
    qcol = H_QB // SWA_Q_DIM
    kcol = H_KB // SWA_KV_DIM
    vcol = H_VB // SWA_KV_DIM
    kern = functools.partial(_swa_sample_kernel, nb=nb, lq=lq)
    past_spec = pl.BlockSpec((nb, WINDOW, SWA_KV_DIM), lambda g, sk: (g, 0, 0))
    return pl.pallas_call(
        kern,
        grid_spec=pltpu.PrefetchScalarGridSpec(
            num_scalar_prefetch=1,
            grid=(n_seq // nb,),
            in_specs=[pl.BlockSpec((rows, SWA_Q_DIM), lambda g, sk: (rb0 + g, qcol)),
                      pl.BlockSpec((rows, SWA_KV_DIM), lambda g, sk: (rb0 + g, kcol)),
                      pl.BlockSpec((rows, SWA_KV_DIM), lambda g, sk: (rb0 + g, vcol)),
                      past_spec, past_spec],
            out_specs=pl.BlockSpec((rows, SWA_Q_DIM), lambda g, sk: (g, 0)),
        ),
        out_shape=jax.ShapeDtypeStruct((n_seq * lq, SWA_Q_DIM), BF16),
        compiler_params=_cparams(("parallel",)),
        name=name,
    )(sinks, h, h, h, k_past, v_past)


def _layer_norm(x, g, b):
    mu = jnp.mean(x, axis=-1, keepdims=True)
    xc = x - mu
    var = jnp.mean(jnp.square(xc), axis=-1, keepdims=True)
    return xc * lax.rsqrt(var + EPS) * g + b


def _pack_bf16_pairs(x, s):
    lo = lax.bitcast_convert_type(x[:, (2 * s) * LANES:(2 * s + 1) * LANES].astype(BF16).astype(F32), I32)
    hi = lax.bitcast_convert_type(x[:, (2 * s + 1) * LANES:(2 * s + 2) * LANES].astype(BF16).astype(F32), I32)
    return lax.shift_right_logical(lo, 16) | (hi & HIGH_HALF)


def _unpack_bf16_pairs(w):
    lo = lax.bitcast_convert_type(lax.shift_left(w, 16), F32).astype(BF16)
    hi = lax.bitcast_convert_type(w & HIGH_HALF, F32).astype(BF16)
    return lo, hi


def _merge_kernel(oap_ref, oas_ref, obp_ref, obs_ref, ga_ref, gb_ref, xp_ref, xs_ref,
                  wpa_ref, wpb_ref, wout_ref, g_ref, b_ref, o_ref, xw_ref, *, np_tiles):
    i = pl.program_id(0)
    tm = o_ref.shape[0]
    br_a = jnp.dot(_pair_value(i, np_tiles, oap_ref, oas_ref), wpa_ref[...], preferred_element_type=F32)
    br_b = jnp.dot(_pair_value(i, np_tiles, obp_ref, obs_ref), wpb_ref[...], preferred_element_type=F32)
    merged = (jax.nn.sigmoid(ga_ref[...].astype(F32)) * br_a
              + jax.nn.sigmoid(gb_ref[...].astype(F32)) * br_b)
    mix = jnp.dot(merged.astype(BF16), wout_ref[...], preferred_element_type=F32)
    x = _pair_value(i, np_tiles, xp_ref, xs_ref)
    x1 = _layer_norm(DEEPNORM_ALPHA * x + mix, g_ref[...], b_ref[...])
    o_ref[...] = x1
    for s in range(PACKED_ROWS):
        xw_ref[pl.ds(s, tm, stride=PACKED_ROWS), :] = _pack_bf16_pairs(x1, s)


def _merge(oa, ob, h, x, wpa, wpb, wout, g, b, *, tm, name):
    t = h.shape[0]
    np_tiles = x[0].shape[0] // tm
    p_map, s_map = _pair_maps(np_tiles)
    row = lambda i: (i, 0)
    full = lambda i: (0, 0)
    pair = [pl.BlockSpec((tm, D_MODEL), p_map), pl.BlockSpec((tm, D_MODEL), s_map)]
    return pl.pallas_call(
        functools.partial(_merge_kernel, np_tiles=np_tiles),
        grid=(t // tm,),
        in_specs=pair + pair + [
                  pl.BlockSpec((tm, D_MODEL), lambda i: (i, H_GATE_A // D_MODEL)),
                  pl.BlockSpec((tm, D_MODEL), lambda i: (i, H_GATE_B // D_MODEL))] + pair + [
                  pl.BlockSpec((D_MODEL, D_MODEL), full),
                  pl.BlockSpec((D_MODEL, D_MODEL), full),
                  pl.BlockSpec((D_MODEL, D_MODEL), full),
                  pl.BlockSpec((1, D_MODEL), full),
                  pl.BlockSpec((1, D_MODEL), full)],
        out_specs=[pl.BlockSpec((tm, D_MODEL), row),
                   pl.BlockSpec((tm * PACKED_ROWS, LANES), row)],
        out_shape=[jax.ShapeDtypeStruct((t, D_MODEL), F32),
                   jax.ShapeDtypeStruct((t * PACKED_ROWS, LANES), I32)],
        compiler_params=_cparams(("parallel",)),
        name=name,
    )(*oa, *ob, h, h, *x, wpa, wpb, wout, g, b)


def _router_kernel(x_ref, whi_ref, wlo_ref, bias_ref, idx_ref, wl_ref, rank_ref, cntc_ref, cntr_ref,
                   cntc_scr, cntr_scr):
    i = pl.program_id(0)
    tm = x_ref.shape[0]

    @pl.when(i == 0)
    def _():
        cntc_scr[...] = jnp.zeros_like(cntc_scr)
        cntr_scr[...] = jnp.zeros_like(cntr_scr)

    x_hi, x_lo = _split_bf16(x_ref[...])
    nt = (((1,), (1,)), ((), ()))
    logits = (lax.dot_general(whi_ref[...], x_hi, nt, preferred_element_type=F32)
              + lax.dot_general(whi_ref[...], x_lo, nt, preferred_element_type=F32)
              + lax.dot_general(wlo_ref[...], x_hi, nt, preferred_element_type=F32))
    scores = jax.nn.sigmoid(logits)
    biased = scores + bias_ref[...]

    grouped = biased.reshape(N_GROUPS, GROUP_SIZE, tm)
    m1 = jnp.max(grouped, axis=1)
    n_top = jnp.sum(jnp.where(grouped == m1[:, None, :], 1.0, 0.0), axis=1)
    m2 = jnp.max(jnp.where(grouped < m1[:, None, :], grouped, -jnp.inf), axis=1)
    gscore = m1 + jnp.where(n_top >= 2.0, m1, m2)

    g_iota = lax.broadcasted_iota(I32, (N_GROUPS, tm), 0)
    beaten = jnp.zeros((N_GROUPS, tm), I32)
    for g in range(N_GROUPS):
        other = gscore[g:g + 1, :]
        ahead = (other > gscore) | ((other == gscore) & (g < g_iota))
        beaten = beaten + jnp.where(ahead, 1, 0)
    keep = jnp.where(beaten < TOPK_GROUPS, 1.0, 0.0)
    masked = jnp.where(keep[:, None, :] > 0.5, grouped, -jnp.inf).reshape(N_EXPERTS, tm)

    e_iota = lax.broadcasted_iota(I32, (N_EXPERTS, tm), 0)
    sel_f = jnp.zeros((N_EXPERTS, tm), F32)
    ids = []
    for _ in range(TOP_K):
        best = jnp.max(masked, axis=0, keepdims=True)
        idx = jnp.min(jnp.where(masked == best, e_iota, N_EXPERTS), axis=0, keepdims=True)
        hit = e_iota == idx
        sel_f = sel_f + jnp.where(hit, 1.0, 0.0)
        masked = jnp.where(hit, -jnp.inf, masked)
        ids.append(idx)

    top_sum = jnp.sum(sel_f * scores, axis=0, keepdims=True)

    t_r = lax.broadcasted_iota(I32, (tm, tm), 0)
    t_c = lax.broadcasted_iota(I32, (tm, tm), 1)
    before = jnp.where(t_r < t_c, 1.0, 0.0).astype(BF16)
    sel_b = sel_f.astype(BF16)
    rank = jnp.dot(sel_b, before, preferred_element_type=F32) + cntc_scr[...]
    cntc_scr[...] = cntc_scr[...] + jnp.sum(sel_f, axis=1, keepdims=True)
    cntr_scr[...] = cntr_scr[...] + lax.dot_general(jnp.ones((SUBLANES, tm), BF16), sel_b, nt,
                                                    preferred_element_type=F32)

    idx_rows, w_rows, rank_rows = [], [], []
    for idx in ids:
        hit = e_iota == idx
        w = jnp.sum(jnp.where(hit, scores, 0.0), axis=0, keepdims=True)
        w_rows.append(w / top_sum * ROUTED_SCALE)
        rank_rows.append(jnp.sum(jnp.where(hit, rank, 0.0), axis=0, keepdims=True))
        idx_rows.append(idx)
    idx_ref[...] = jnp.concatenate(idx_rows, axis=0)
    rank_ref[...] = jnp.concatenate(rank_rows, axis=0).astype(I32)

    w_all = jnp.concatenate(w_rows, axis=0)
    w_1 = w_all.astype(BF16)
    r_1 = w_all - w_1.astype(F32)
    w_2 = r_1.astype(BF16)
    w_3 = (r_1 - w_2.astype(F32)).astype(BF16)
    terms = jnp.concatenate([w_1, w_2, w_3, jnp.zeros_like(w_1)], axis=0)
    t_row = lax.broadcasted_iota(I32, (4 * TOP_K, TOP_K * LANES), 0)
    t_col = lax.broadcasted_iota(I32, (4 * TOP_K, TOP_K * LANES), 1)
    spread = jnp.where((t_row % TOP_K) == (t_col // LANES), 1.0, 0.0).astype(BF16)
    w_lanes = lax.dot_general(terms, spread, (((0,), (0,)), ((), ())), preferred_element_type=F32)
    for k in range(TOP_K):
        wl_ref[pl.ds(k, tm, stride=TOP_K), :] = w_lanes[:, k * LANES:(k + 1) * LANES]

    @pl.when(i == pl.num_programs(0) - 1)
    def _():
        cntc_ref[...] = cntc_scr[...].astype(I32)
        cntr_ref[...] = cntr_scr[...].astype(I32)


def _router(x1, w_hi, w_lo, bias, *, tm, name):
    t = x1.shape[0]
    full = lambda i: (0, 0)
    tok = lambda i: (0, i)
    return pl.pallas_call(
        _router_kernel,
        grid=(t // tm,),
        in_specs=[pl.BlockSpec((tm, D_MODEL), lambda i: (i, 0)),
                  pl.BlockSpec((N_EXPERTS, D_MODEL), full),
                  pl.BlockSpec((N_EXPERTS, D_MODEL), full),
                  pl.BlockSpec((N_EXPERTS, 1), full)],
        out_specs=[pl.BlockSpec((TOP_K, tm), tok),
                   pl.BlockSpec((TOP_K * tm, LANES), lambda i: (i, 0)),
                   pl.BlockSpec((TOP_K, tm), tok),
                   pl.BlockSpec((N_EXPERTS, 1), full),
                   pl.BlockSpec((SUBLANES, N_EXPERTS), full)],
        out_shape=[jax.ShapeDtypeStruct((TOP_K, t), I32),
                   jax.ShapeDtypeStruct((TOP_K * t, LANES), F32),
                   jax.ShapeDtypeStruct((TOP_K, t), I32),
                   jax.ShapeDtypeStruct((N_EXPERTS, 1), I32),
                   jax.ShapeDtypeStruct((SUBLANES, N_EXPERTS), I32)],
        scratch_shapes=[pltpu.VMEM((N_EXPERTS, 1), F32),
                        pltpu.VMEM((SUBLANES, N_EXPERTS), F32)],
        compiler_params=_cparams(("arbitrary",)),
        name=name,
    )(x1, w_hi, w_lo, bias)


def _byte_split(v):
    return lax.shift_right_logical(v, 8).astype(F32), (v & 255).astype(F32)


def _plan_kernel(cntc_ref, cntr_ref, idx_ref, rank_ref, meta_ref, items_ref, m_scr, *, n_tok):
    i = pl.program_id(0)
    tm = idx_ref.shape[1]
    nbp = m_scr.shape[0]
    nip = items_ref.shape[1]
    n_rows = n_tok * TOP_K
    n_blk = n_rows // EXPERT_ROWS
    nt = (((1,), (1,)), ((), ()))

    r_i = lax.broadcasted_iota(I32, (N_EXPERTS, N_EXPERTS), 0)
    c_i = lax.broadcasted_iota(I32, (N_EXPERTS, N_EXPERTS), 1)
    below = jnp.where(c_i < r_i, 1.0, 0.0).astype(BF16)
    ones_c = jnp.ones((N_EXPERTS, LANES), F32)
    c_hi, c_lo = _byte_split(cntc_ref[...])
    start_col = (256.0 * jnp.dot(below, (c_hi * ones_c).astype(BF16), preferred_element_type=F32)
                 + jnp.dot(below, (c_lo * ones_c).astype(BF16), preferred_element_type=F32))[:, 0:1]

    @pl.when(i == 0)
    def _():
        m_scr[...] = jnp.zeros_like(m_scr)
        above = jnp.where(r_i < c_i, 1.0, 0.0).astype(BF16)
        r_hi, r_lo = _byte_split(cntr_ref[...])
        start_row = (256.0 * jnp.dot(r_hi.astype(BF16), above, preferred_element_type=F32)
                     + jnp.dot(r_lo.astype(BF16), above, preferred_element_type=F32))[0:1, :]
        b_col = lax.broadcasted_iota(I32, (nbp, 1), 0)
        b_row = lax.broadcasted_iota(I32, (1, nbp), 1)
        blk_col = jnp.where(b_col < n_blk, b_col * EXPERT_ROWS, n_rows).astype(F32)
        blk_row = jnp.where(b_row < n_blk, b_row * EXPERT_ROWS, n_rows).astype(F32)
        v_col = jnp.concatenate([blk_col, start_col], axis=0)
        v_row = jnp.concatenate([blk_row, start_row], axis=1)
        j_col = lax.broadcasted_iota(I32, (nip, 1), 0)
        k_row = lax.broadcasted_iota(I32, (1, nip), 1)
        ahead = (v_row < v_col) | ((v_row == v_col) & (k_row < j_col))
        order_col = jnp.sum(jnp.where(ahead, 1.0, 0.0), axis=1, keepdims=True)
        pos_row = k_row.astype(F32)
        lo_abs = jnp.sum(jnp.where(order_col == pos_row, v_col, 0.0), axis=0, keepdims=True)
        hi_abs = jnp.sum(jnp.where(order_col == pos_row + 1.0, v_col, 0.0), axis=0, keepdims=True)
        hi_abs = jnp.where(k_row == nip - 1, float(n_rows), hi_abs)
        blk = jnp.minimum(jnp.floor(lo_abs * (1.0 / EXPERT_ROWS)), n_blk - 1.0)
        expert = jnp.sum(jnp.where(start_col <= lo_abs, 1.0, 0.0), axis=0, keepdims=True) - 1.0
        base = blk * EXPERT_ROWS
        items_ref[...] = jnp.concatenate([blk, expert, lo_abs - base, hi_abs - base], axis=0).astype(I32)

    e_iota = lax.broadcasted_iota(I32, (N_EXPERTS, tm), 0)
    b_iota = lax.broadcasted_iota(I32, (nbp, tm), 0)
    l_iota = lax.broadcasted_iota(I32, (EXPERT_ROWS, tm), 0)
    tok = i * tm + lax.broadcasted_iota(I32, (1, tm), 1)
    tok_hi, tok_lo = _byte_split(tok)
    acc = jnp.zeros(m_scr.shape, F32)
    for k in range(TOP_K):
        hit = e_iota == idx_ref[k:k + 1, :]
        dest = (jnp.sum(jnp.where(hit, start_col, 0.0), axis=0, keepdims=True).astype(I32)
                + rank_ref[k:k + 1, :])
        oh_blk = jnp.where(b_iota == lax.shift_right_logical(dest, 8), 1.0, 0.0).astype(BF16)
        in_blk = l_iota == (dest & (EXPERT_ROWS - 1))
        vals = jnp.concatenate([jnp.where(in_blk, tok_hi, 0.0), jnp.where(in_blk, tok_lo, 0.0),
                                jnp.where(in_blk, float(k), 0.0)], axis=0).astype(BF16)
        acc = acc + lax.dot_general(oh_blk, vals, nt, preferred_element_type=F32)
    m_scr[...] = m_scr[...] + acc

    @pl.when(i == pl.num_programs(0) - 1)
    def _():
        m = m_scr[...]
        row_tok = m[:, 0:EXPERT_ROWS] * 256.0 + m[:, EXPERT_ROWS:2 * EXPERT_ROWS]
        row_slot = m[:, 2 * EXPERT_ROWS:3 * EXPERT_ROWS] * float(n_tok) + row_tok
        meta_ref[...] = jnp.concatenate([row_tok, row_slot], axis=1).astype(I32)


def _plan(cnt_col, cnt_row, idx, rank, *, tm, name):
    t = idx.shape[1]
    assert EXPERT_ROWS == 256 and (t * TOP_K) % (2 * EXPERT_ROWS) == 0
    n_blk = t * TOP_K // EXPERT_ROWS
    nbp = -(-n_blk // LANES) * LANES
    nip = nbp + N_EXPERTS
    tok = lambda i: (0, i)
    full = lambda i: (0, 0)
    return pl.pallas_call(
        functools.partial(_plan_kernel, n_tok=t),
        grid=(t // tm,),
        in_specs=[pl.BlockSpec((N_EXPERTS, 1), full),
                  pl.BlockSpec((SUBLANES, N_EXPERTS), full),
                  pl.BlockSpec((TOP_K, tm), tok),
                  pl.BlockSpec((TOP_K, tm), tok)],
        out_specs=[pl.BlockSpec((nbp, 2 * EXPERT_ROWS), full),
                   pl.BlockSpec((4, nip), full)],
        out_shape=[jax.ShapeDtypeStruct((nbp, 2 * EXPERT_ROWS), I32),
                   jax.ShapeDtypeStruct((4, nip), I32)],
        scratch_shapes=[pltpu.VMEM((nbp, 3 * EXPERT_ROWS), F32)],
        compiler_params=_cparams(("arbitrary",)),
        name=name,
    )(cnt_col, cnt_row, idx, rank)


META_CHUNK = 4 * EXPERT_ROWS
GATHER_STRIDE = EXPERT_ROWS + 8


def _experts_kernel(items_ref, xw_hbm, meta_hbm, wg_ref, wu_ref, wd_ref, ys_hbm,
                    xw, meta, tile, lhs, wg_b, wu_b, wd_b, yacc, ybuf, sem_x, sem_m, sem_y, *, nip, n_blk):
    i = pl.program_id(0)
    blk = items_ref[i]
    expert = items_ref[nip + i]
    lo = items_ref[2 * nip + i]
    hi = items_ref[3 * nip + i]
    prev_expert = items_ref[nip + jnp.maximum(i - 1, 0)]
    nonempty = hi > lo
    par = blk & 1
    chunk = lax.shift_right_logical(blk, 1)
    cpar = chunk & 1
    mbase = cpar * META_CHUNK + par * (2 * EXPERT_ROWS)

    def meta_copy(c, slot):
        return pltpu.make_async_copy(meta_hbm.at[pl.ds(pl.multiple_of(c * META_CHUNK, META_CHUNK), META_CHUNK)],
                                     meta.at[pl.ds(pl.multiple_of(slot * META_CHUNK, META_CHUNK), META_CHUNK)],
                                     sem_m.at[slot])

    def ybuf_drain(slot):
        pltpu.make_async_copy(ys_hbm.at[pl.ds(0, EXPERT_ROWS * ROW_CHUNKS), :], ybuf.at[slot],
                              sem_y.at[slot]).wait()

    def ybuf_fill_and_send(slot):
        for j in range(ROW_CHUNKS):
            ybuf[slot, pl.ds(j, EXPERT_ROWS, stride=ROW_CHUNKS), :] = yacc[:, j * LANES:(j + 1) * LANES]

        for r in range(EXPERT_ROWS):
            src = ybuf.at[slot, pl.ds(r * ROW_CHUNKS, ROW_CHUNKS), :]
            dst_row = pl.multiple_of(meta[mbase + EXPERT_ROWS + r] * ROW_CHUNKS, ROW_CHUNKS)
            pltpu.make_async_copy(src, ys_hbm.at[pl.ds(dst_row, ROW_CHUNKS), :],
                                  sem_y.at[slot]).start(priority=r % 2)

    @pl.when(i == 0)
    def _():
        resident = pltpu.make_async_copy(xw_hbm, xw, sem_x)
        resident.start()
        meta_copy(0, 0).start()
        resident.wait()

    @pl.when((i == 0) | (expert != prev_expert))
    def _():
        wg_b[...] = wg_ref[0].astype(BF16)
        wu_b[...] = wu_ref[0].astype(BF16)
        wd_b[...] = wd_ref[0].astype(BF16)

    @pl.when(nonempty & (lo == 0))
    def _():
        @pl.when(par == 0)
        def _():
            meta_copy(chunk, cpar).wait()

            @pl.when(2 * (chunk + 1) < n_blk)
            def _():
                meta_copy(chunk + 1, 1 - cpar).start()

        for r in range(EXPERT_ROWS):
            t4 = pl.multiple_of(meta[mbase + r] * PACKED_ROWS, PACKED_ROWS)
            tile[pl.ds(r, PACKED_ROWS, stride=GATHER_STRIDE), :] = xw[pl.ds(t4, PACKED_ROWS), :]
        cols = []
        for s in range(PACKED_ROWS):
            cols.extend(_unpack_bf16_pairs(tile[pl.ds(s * GATHER_STRIDE, EXPERT_ROWS), :]))
        lhs[...] = jnp.concatenate(cols, axis=-1)

    def ffn(x):
        gate = jnp.dot(x, wg_b[...], preferred_element_type=F32)
        up = jnp.dot(x, wu_b[...], preferred_element_type=F32)
        hid = (jax.nn.silu(gate) * up).astype(BF16)
        return jnp.dot(hid, wd_b[...], preferred_element_type=F32)

    whole = (lo == 0) & (hi == EXPERT_ROWS)

    @pl.when(whole)
    def _():
        yacc[...] = ffn(lhs[...])

    @pl.when(nonempty & jnp.logical_not(whole))
    def _():
        row = lax.broadcasted_iota(I32, (EXPERT_ROWS, 1), 0)
        y = jnp.where((row >= lo) & (row < hi), ffn(lhs[...]), 0.0)

        @pl.when(lo == 0)
        def _():
            yacc[...] = y

        @pl.when(lo > 0)
        def _():
            yacc[...] = yacc[...] + y

    @pl.when(nonempty & (hi == EXPERT_ROWS))
    def _():
        @pl.when(blk >= 2)
        def _():
            ybuf_drain(par)

        for slot in range(2):
            @pl.when(par == slot)
            def _():
                ybuf_fill_and_send(slot)

    @pl.when(i == pl.num_programs(0) - 1)
    def _():
        ybuf_drain(0)
        ybuf_drain(1)


def _experts(items, xw, meta, wg, wu, wd, *, n_tok, name):
    nip = items.shape[0] // 4
    n_rows = n_tok * TOP_K
    n_blk = n_rows // EXPERT_ROWS
    n_items = n_blk + N_EXPERTS

    def wmap(i, items_ref):
        return (items_ref[nip + i], 0, 0)

    kern = functools.partial(_experts_kernel, nip=nip, n_blk=n_blk)
    any_spec = pl.BlockSpec(memory_space=pl.ANY)
    return pl.pallas_call(
        kern,
        grid_spec=pltpu.PrefetchScalarGridSpec(
            num_scalar_prefetch=1,
            grid=(n_items,),
            in_specs=[any_spec, any_spec,
                      pl.BlockSpec((1, D_MODEL, EXPERT_FF), wmap),
                      pl.BlockSpec((1, D_MODEL, EXPERT_FF), wmap),
                      pl.BlockSpec((1, EXPERT_FF, D_MODEL), wmap)],
            out_specs=any_spec,
            scratch_shapes=[pltpu.VMEM(xw.shape, I32),
                            pltpu.SMEM((2 * META_CHUNK,), I32),
                            pltpu.VMEM((PACKED_ROWS * GATHER_STRIDE, LANES), I32),
                            pltpu.VMEM((EXPERT_ROWS, D_MODEL), BF16),
                            pltpu.VMEM((D_MODEL, EXPERT_FF), BF16),
                            pltpu.VMEM((D_MODEL, EXPERT_FF), BF16),
                            pltpu.VMEM((EXPERT_FF, D_MODEL), BF16),
                            pltpu.VMEM((EXPERT_ROWS, D_MODEL), F32),
                            pltpu.VMEM((2, EXPERT_ROWS * ROW_CHUNKS, LANES), F32),
                            pltpu.SemaphoreType.DMA,
                            pltpu.SemaphoreType.DMA((2,)),
                            pltpu.SemaphoreType.DMA((2,))],
        ),
        out_shape=jax.ShapeDtypeStruct((n_rows * ROW_CHUNKS, LANES), F32),
        compiler_params=_cparams(("arbitrary",)),
        name=name,
    )(items, xw, meta, wg, wu, wd)


def _combine_kernel(ys_ref, x_ref, wl_ref, wsg_ref, wsu_ref, wsd_ref, g_ref, b_ref, op_ref, os_ref,
                    routed, *, np_tiles):
    x = x_ref[...]
    xb = x.astype(BF16)
    hid = (jax.nn.silu(jnp.dot(xb, wsg_ref[...], preferred_element_type=F32))
           * jnp.dot(xb, wsu_ref[...], preferred_element_type=F32))
    shared = jnp.dot(hid.astype(BF16), wsd_ref[...], preferred_element_type=F32)
    acc = None
    for k in range(TOP_K):
        part = wl_ref[:, k:k + 1, :] * ys_ref[k]
        acc = part if acc is None else acc + part
    routed[...] = acc
    ffn = jnp.concatenate([routed[:, j, :] for j in range(ROW_CHUNKS)], axis=-1) + shared
    out = _layer_norm(DEEPNORM_ALPHA * x + ffn, g_ref[...], b_ref[...])
    i = pl.program_id(0)

    @pl.when(i < np_tiles)
    def _():
        op_ref[...] = out

    @pl.when(i >= np_tiles)
    def _():
        os_ref[...] = out


def _combine(ys, x1, w_tok, wsg, wsu, wsd, g, b, *, n_prompt, tm, name):
    t = x1.shape[0]
    np_tiles = n_prompt // tm
    p_map, s_map = _pair_maps(np_tiles)
    row = lambda i: (i, 0)
    full = lambda i: (0, 0)
    return pl.pallas_call(
        functools.partial(_combine_kernel, np_tiles=np_tiles),
        grid=(t // tm,),
        in_specs=[pl.BlockSpec((TOP_K, tm, ROW_CHUNKS, LANES), lambda i: (0, i, 0, 0)),
                  pl.BlockSpec((tm, D_MODEL), row),
                  pl.BlockSpec((tm, TOP_K, LANES), lambda i: (i, 0, 0)),
                  pl.BlockSpec((D_MODEL, SHARED_FF), full),
                  pl.BlockSpec((D_MODEL, SHARED_FF), full),
                  pl.BlockSpec((SHARED_FF, D_MODEL), full),
                  pl.BlockSpec((1, D_MODEL), full),
                  pl.BlockSpec((1, D_MODEL), full)],
        out_specs=[pl.BlockSpec((tm, D_MODEL), p_map), pl.BlockSpec((tm, D_MODEL), s_map)],
        out_shape=[jax.ShapeDtypeStruct((n_prompt, D_MODEL), F32),
                   jax.ShapeDtypeStruct((t - n_prompt, D_MODEL), F32)],
        scratch_shapes=[pltpu.VMEM((tm, ROW_CHUNKS, LANES), F32)],
        compiler_params=_cparams(("arbitrary",)),
        name=name,
    )(ys, x1, w_tok, wsg, wsu, wsd, g, b)


def _pack_w_in(w_in):
    sizes = (GLA_KEY_DIM, GLA_KEY_DIM, GLA_VAL_DIM, GLA_VAL_DIM, GLA_GATE_RANK,
             SWA_Q_DIM, SWA_KV_DIM, SWA_KV_DIM, D_MODEL, D_MODEL)
    offs = [0]
    for s in sizes:
        offs.append(offs[-1] + s)
    qa, ka, va, ga, gk, qb, kb, vb, gate_a, gate_b = (w_in[:, offs[i]:offs[i + 1]] for i in range(10))
    pad = lambda w, n: jnp.pad(w, ((0, 0), (0, n - w.shape[1])))
    packed = jnp.concatenate([qa, ka, va, ga, qb, gate_a, gate_b, kb, vb, pad(gk, 2 * LANES)], axis=1)
    assert packed.shape[1] == H_WIDTH
    return packed.astype(BF16), jnp.concatenate([kb, vb], axis=1).astype(BF16)


def kernel(x_prompt, x_sample, state_gla, cache_swa_k, cache_swa_v, w_in, w_gk_up, b_gk, gla_norm_g,
           attn_sinks, w_proj_a, w_proj_b, w_out, ln1_g, ln1_b, w_router, router_bias,
           w_expert_gate, w_expert_up, w_expert_down, w_shared_gate, w_shared_up, w_shared_down,
           ln2_g, ln2_b):
    assert w_in.shape[0] == 1, "single-layer trunk"
    bp, lp, d = x_prompt.shape
    bs, ls, _ = x_sample.shape
    assert d == D_MODEL and ls == SUBLANES and cache_swa_k.shape[2] == WINDOW
    tp, ts = bp * lp, bs * ls
    t = tp + ts

    xp = x_prompt.reshape(tp, d)
    xs = x_sample.reshape(ts, d)
    w_main, w_kv = _pack_w_in(w_in[0])
    h = _matmul(xp, xs, w_main, BF16, _pair_tile(tp, ts, 1024), H_TN, "proj_in")

    xp_tail = x_prompt[:, lp - WINDOW:].reshape(bp * WINDOW, d)
    kv_tail = _matmul(xp_tail, xs, w_kv, F32, _pair_tile(bp * WINDOW, ts, 512), 2 * SWA_KV_DIM,
                      "proj_kv_tail")

    wup = jnp.pad(w_gk_up[0], ((0, LANES - GLA_GATE_RANK), (0, 0))).astype(BF16)
    bgk = b_gk[0].reshape(1, GLA_KEY_DIM)
    gn = gla_norm_g[0].reshape(1, GLA_DV)
    oa_p, s_prompt = _gla(h, wup, bgk, gn, None, row0=0, n_seq=bp, seq_len=lp, nb=1, c=GLA_CHUNK,
                          sub=4, name="gla_prompt")
    oa_s, s_sample = _gla(h, wup, bgk, gn, state_gla[0], row0=tp, n_seq=bs, seq_len=ls, nb=8,
                          c=math.gcd(ls, GLA_CHUNK), sub=1, name="gla_sample")

    sinks = attn_sinks[0]
    k_past = cache_swa_k[0].reshape(bs, WINDOW, SWA_KV_DIM)
    v_past = cache_swa_v[0].reshape(bs, WINDOW, SWA_KV_DIM)
    ob_p = _swa_prompt(h, sinks, n_seq=bp, seq_len=lp, name="swa_prompt")
    ob_s = _swa_sample(h, sinks, k_past, v_past, row0=tp, n_seq=bs, lq=ls, nb=8, name="swa_sample")

    x1, xw = _merge((oa_p, oa_s), (ob_p, ob_s), h, (xp, xs), w_proj_a[0].astype(BF16),
                    w_proj_b[0].astype(BF16), w_out[0].astype(BF16), ln1_g[0].reshape(1, d),
                    ln1_b[0].reshape(1, d), tm=_pair_tile(tp, ts, 512), name="merge_ln1")

    wr_t = w_router[0].T
    wr_hi = wr_t.astype(BF16)
    wr_lo = (wr_t - wr_hi.astype(F32)).astype(BF16)
    tm_r = _tile(t, 512)
    idx, w_lanes, rank, cnt_col, cnt_row = _router(x1, wr_hi, wr_lo, router_bias[0].reshape(N_EXPERTS, 1),
                                                   tm=tm_r, name="router")
    meta, items = _plan(cnt_col, cnt_row, idx, rank, tm=tm_r, name="plan")
    ys = _experts(items.reshape(-1), xw, meta.reshape(-1), w_expert_gate[0], w_expert_up[0],
                  w_expert_down[0], n_tok=t, name="experts")
    y_p, y_s = _combine(ys.reshape(TOP_K, t, ROW_CHUNKS, LANES), x1, w_lanes.reshape(t, TOP_K, LANES),
                        w_shared_gate[0].astype(BF16), w_shared_up[0].astype(BF16),
                        w_shared_down[0].astype(BF16), ln2_g[0].reshape(1, d), ln2_b[0].reshape(1, d),
                        n_prompt=tp, tm=_pair_tile(tp, ts, 128), name="combine_ln2")

    y_prompt = y_p.reshape(bp, lp, d)
    y_sample = y_s.reshape(bs, ls, d)
    k_tail = kv_tail[:, :SWA_KV_DIM]
    v_tail = kv_tail[:, SWA_KV_DIM:]
    kv_shape = (SWA_KV_HEADS, SWA_HEAD_DIM)
    k_prompt = k_tail[:bp * WINDOW].reshape(1, bp, WINDOW, *kv_shape)
    v_prompt = v_tail[:bp * WINDOW].reshape(1, bp, WINDOW, *kv_shape)
    k_new = k_tail[bp * WINDOW:].reshape(bs, ls, *kv_shape)
    v_new = v_tail[bp * WINDOW:].reshape(bs, ls, *kv_shape)
    k_sample = jnp.concatenate([cache_swa_k[0][:, ls:], k_new], axis=1)[None]
    v_sample = jnp.concatenate([cache_swa_v[0][:, ls:], v_new], axis=1)[None]
    return (y_prompt, y_sample, s_prompt[None], s_sample[None], k_prompt, v_prompt, k_sample, v_sample)
```

```python
import functools
import math

import jax
import jax.numpy as jnp
from jax import lax
from jax.experimental import pallas as pl
from jax.experimental.pallas import tpu as pltpu

F32 = jnp.float32
BF16 = jnp.bfloat16
I32 = jnp.int32

D_MODEL = 1024
GLA_HEADS = 4
GLA_DK = 128
GLA_DV = 256
GLA_KEY_DIM = GLA_HEADS * GLA_DK
GLA_VAL_DIM = GLA_HEADS * GLA_DV
GLA_GATE_RANK = 16
GLA_GATE_NORMALIZER = 16.0
GLA_CHUNK = 64
SWA_HEADS = 16
SWA_KV_HEADS = 4
SWA_GROUP = SWA_HEADS // SWA_KV_HEADS
SWA_HEAD_DIM = 64
SWA_Q_DIM = SWA_HEADS * SWA_HEAD_DIM
SWA_KV_DIM = SWA_KV_HEADS * SWA_HEAD_DIM
WINDOW = 128
N_EXPERTS = 256
TOP_K = 8
N_GROUPS = 8
GROUP_SIZE = N_EXPERTS // N_GROUPS
TOPK_GROUPS = 4
EXPERT_FF = 256
SHARED_FF = 256
ROUTED_SCALE = 2.5
DEEPNORM_ALPHA = 2.0 ** 0.25
EPS = 1e-5

LANES = 128
SUBLANES = 8
ROW_CHUNKS = D_MODEL // LANES
VMEM_LIMIT = 56 * 1024 * 1024

H_QA, H_KA, H_VA, H_GA, H_QB, H_GATE_A, H_GATE_B, H_KB, H_VB, H_GK = (
    0, 512, 1024, 2048, 3072, 4096, 5120, 6144, 6400, 6656)
H_WIDTH = 6912
H_TN = 2304

EXPERT_ROWS = 256
PACKED_ROWS = ROW_CHUNKS // 2
HIGH_HALF = -65536


def _cparams(sem, vmem=VMEM_LIMIT):
    return pltpu.CompilerParams(dimension_semantics=sem, vmem_limit_bytes=vmem)


def _tile(n, pref):
    t = min(n, pref)
    while n % t:
        t -= LANES
    assert t > 0 and t % LANES == 0, (n, pref)
    return t


def _pair_tile(n_p, n_s, pref):
    return _tile(math.gcd(n_p, n_s), pref)


def _pair_maps(np_tiles, col=0, extra=0):
    def p_map(i, *_):
        return (jnp.minimum(i, np_tiles - 1), col)

    def s_map(i, *_):
        return (jnp.maximum(i - np_tiles, 0), col)

    return p_map, s_map


def _pair_value(i, np_tiles, p_ref, s_ref):
    return jnp.where(i < np_tiles, p_ref[...], s_ref[...])


def _mm_kernel(xp_ref, xs_ref, w_ref, o_ref, *, np_tiles):
    x = _pair_value(pl.program_id(0), np_tiles, xp_ref, xs_ref)
    o_ref[...] = jnp.dot(x.astype(BF16), w_ref[...], preferred_element_type=F32).astype(o_ref.dtype)


def _matmul(xp, xs, w, out_dtype, tm, tn, name):
    k = xp.shape[1]
    m = xp.shape[0] + xs.shape[0]
    n = w.shape[1]
    np_tiles = xp.shape[0] // tm
    p_map, s_map = _pair_maps(np_tiles)
    return pl.pallas_call(
        functools.partial(_mm_kernel, np_tiles=np_tiles),
        grid=(m // tm, n // tn),
        in_specs=[pl.BlockSpec((tm, k), p_map),
                  pl.BlockSpec((tm, k), s_map),
                  pl.BlockSpec((k, tn), lambda i, j: (0, j))],
        out_specs=pl.BlockSpec((tm, tn), lambda i, j: (i, j)),
        out_shape=jax.ShapeDtypeStruct((m, n), out_dtype),
        compiler_params=_cparams(("parallel", "arbitrary")),
        name=name,
    )(xp, xs, w)


def _split_bf16(x):
    hi = x.astype(BF16)
    lo = (x - hi.astype(F32)).astype(BF16)
    return hi, lo


def _gla_kernel(*refs, nb, c, sub, has_s0):
    if has_s0:
        (q_ref, k_ref, v_ref, ga_ref, gk_ref, wup_ref, bgk_ref, gn_ref, s0_ref,
         o_ref, sout_ref, s_scr) = refs
    else:
        (q_ref, k_ref, v_ref, ga_ref, gk_ref, wup_ref, bgk_ref, gn_ref,
         o_ref, sout_ref, s_scr) = refs
        s0_ref = None
    ci = pl.program_id(1)
    rows = nb * c

    @pl.when(ci == 0)
    def _():
        if has_s0:
            s_scr[...] = s0_ref[...]
        else:
            s_scr[...] = jnp.zeros_like(s_scr)

    r_i = lax.broadcasted_iota(I32, (rows, rows), 0)
    c_i = lax.broadcasted_iota(I32, (rows, rows), 1)
    same_seq = (r_i // c) == (c_i // c)
    causal = same_seq & (c_i <= r_i)
    tri = jnp.where(causal, 1.0, 0.0).astype(BF16)
    seg = jnp.where(same_seq, 1.0, 0.0).astype(BF16)
    ones_kv = jnp.ones((rows, GLA_DV), BF16)
    seq_of_row = lax.broadcasted_iota(I32, (rows, 1), 0) // c
    tn = (((0,), (0,)), ((), ()))

    for s in range(sub):
        rs = slice(s * rows, (s + 1) * rows)
        pre = jnp.dot(gk_ref[rs, :], wup_ref[...], preferred_element_type=F32) + bgk_ref[...]
        log_a = jax.nn.log_sigmoid(pre) / GLA_GATE_NORMALIZER
        la_hi, la_lo = _split_bf16(log_a)
        b = (jnp.dot(tri, la_hi, preferred_element_type=F32)
             + jnp.dot(tri, la_lo, preferred_element_type=F32))
        b_last = (jnp.dot(seg, la_hi, preferred_element_type=F32)
                  + jnp.dot(seg, la_lo, preferred_element_type=F32))
        e_pos = jnp.exp(b)
        e_neg = jnp.exp(-b)
        e_rem = jnp.exp(b_last - b)
        q = q_ref[rs, :].astype(F32) * (GLA_DK ** -0.5)
        k = k_ref[rs, :].astype(F32)
        q_dec = (q * e_pos).astype(BF16)
        k_dec = (k * e_neg).astype(BF16)
        k_rem = k * e_rem
        outs = []
        for h in range(GLA_HEADS):
            ks = slice(h * GLA_DK, (h + 1) * GLA_DK)
            vh = v_ref[rs, h * GLA_DV:(h + 1) * GLA_DV]
            att = lax.dot_general(q_dec[:, ks], k_dec[:, ks], (((1,), (1,)), ((), ())),
                                  preferred_element_type=F32)
            att = jnp.where(causal, att, 0.0).astype(BF16)
            o_h = jnp.dot(att, vh, preferred_element_type=F32)
            for j in range(nb):
                mine = (seq_of_row == j) if nb > 1 else None
                pick = (lambda a: jnp.where(mine, a, 0.0)) if nb > 1 else (lambda a: a)
                s_old = s_scr[j, h]
                o_h = o_h + pick(jnp.dot(q_dec[:, ks], s_old.astype(BF16), preferred_element_type=F32))
                dec = (lax.dot_general(pick(la_hi[:, ks].astype(F32)).astype(BF16), ones_kv, tn,
                                       preferred_element_type=F32)
                       + lax.dot_general(pick(la_lo[:, ks].astype(F32)).astype(BF16), ones_kv, tn,
                                         preferred_element_type=F32))
                upd = lax.dot_general(pick(k_rem[:, ks]).astype(BF16), vh, tn, preferred_element_type=F32)
                s_scr[j, h] = jnp.exp(dec) * s_old + upd
            o_h = o_h * lax.rsqrt(jnp.mean(jnp.square(o_h), axis=-1, keepdims=True) + EPS) * gn_ref[...]
            outs.append(o_h)
        o = jnp.concatenate(outs, axis=-1) * jax.nn.silu(ga_ref[rs, :].astype(F32))
        o_ref[rs, :] = o.astype(o_ref.dtype)

    @pl.when(ci == pl.num_programs(1) - 1)
    def _():
        sout_ref[...] = s_scr[...]


def _gla(h, wup, bgk, gn, s0, *, row0, n_seq, seq_len, nb, c, sub, name):
    rows = nb * c * sub
    n_groups = n_seq // nb
    n_steps = seq_len // (c * sub)
    rb0 = row0 // rows

    def rmap(col):
        return lambda g, i: (rb0 + g * n_steps + i, col)

    in_specs = [
        pl.BlockSpec((rows, GLA_KEY_DIM), rmap(H_QA // GLA_KEY_DIM)),
        pl.BlockSpec((rows, GLA_KEY_DIM), rmap(H_KA // GLA_KEY_DIM)),
        pl.BlockSpec((rows, GLA_VAL_DIM), rmap(H_VA // GLA_VAL_DIM)),
        pl.BlockSpec((rows, GLA_VAL_DIM), rmap(H_GA // GLA_VAL_DIM)),
        pl.BlockSpec((rows, LANES), rmap(H_GK // LANES)),
        pl.BlockSpec((LANES, GLA_KEY_DIM), lambda g, i: (0, 0)),
        pl.BlockSpec((1, GLA_KEY_DIM), lambda g, i: (0, 0)),
        pl.BlockSpec((1, GLA_DV), lambda g, i: (0, 0)),
    ]
    args = [h, h, h, h, h, wup, bgk, gn]
    state_spec = pl.BlockSpec((nb, GLA_HEADS, GLA_DK, GLA_DV), lambda g, i: (g, 0, 0, 0))
    if s0 is not None:
        in_specs.append(state_spec)
        args.append(s0)
    kern = functools.partial(_gla_kernel, nb=nb, c=c, sub=sub, has_s0=s0 is not None)
    return pl.pallas_call(
        kern,
        grid=(n_groups, n_steps),
        in_specs=in_specs,
        out_specs=[pl.BlockSpec((rows, GLA_VAL_DIM), lambda g, i: (g * n_steps + i, 0)), state_spec],
        out_shape=[jax.ShapeDtypeStruct((n_seq * seq_len, GLA_VAL_DIM), BF16),
                   jax.ShapeDtypeStruct((n_seq, GLA_HEADS, GLA_DK, GLA_DV), F32)],
        scratch_shapes=[pltpu.VMEM((nb, GLA_HEADS, GLA_DK, GLA_DV), F32)],
        compiler_params=_cparams(("parallel", "arbitrary")),
        name=name,
    )(*args)


def _alibi_slope(head):
    return 2.0 ** (-8.0 * (head + 1) / SWA_HEADS)


def _swa_softmax_pv(parts, sink):
    m = sink
    for s, _ in parts:
        m = jnp.maximum(m, jnp.max(s, axis=-1, keepdims=True))
    denom = jnp.exp(sink - m)
    acc = None
    for s, v in parts:
        p = jnp.exp(s - m)
        denom = denom + jnp.sum(p, axis=-1, keepdims=True)
        pv = jnp.dot(p.astype(BF16), v, preferred_element_type=F32)
        acc = pv if acc is None else acc + pv
    return acc / denom


def _swa_prompt_kernel(sink_ref, q_ref, kp_ref, vp_ref, kc_ref, vc_ref, o_ref):
    i = pl.program_id(1)
    span = 2 * WINDOW
    kcat = jnp.concatenate([kp_ref[...], kc_ref[...]], axis=0)
    vcat = jnp.concatenate([vp_ref[...], vc_ref[...]], axis=0)
    row = lax.broadcasted_iota(I32, (WINDOW, span), 0)
    col = lax.broadcasted_iota(I32, (WINDOW, span), 1)
    dist_i = row + WINDOW - col
    valid = (dist_i >= 0) & (dist_i < WINDOW) & ((col >= WINDOW) | (i > 0))
    dist = dist_i.astype(F32)
    outs = []
    for hh in range(SWA_HEADS):
        g = hh // SWA_GROUP
        gs = slice(g * SWA_HEAD_DIM, (g + 1) * SWA_HEAD_DIM)
        qh = q_ref[:, hh * SWA_HEAD_DIM:(hh + 1) * SWA_HEAD_DIM]
        s = lax.dot_general(qh, kcat[:, gs], (((1,), (1,)), ((), ())), preferred_element_type=F32)
        s = s * (SWA_HEAD_DIM ** -0.5) - _alibi_slope(hh) * dist
        s = jnp.where(valid, s, -jnp.inf)
        outs.append(_swa_softmax_pv([(s, vcat[:, gs])], sink_ref[hh]))
    o_ref[...] = jnp.concatenate(outs, axis=-1).astype(o_ref.dtype)


def _swa_prompt(h, sinks, *, n_seq, seq_len, name):
    nq = seq_len // WINDOW
    qcol = H_QB // SWA_Q_DIM
    kcol = H_KB // SWA_KV_DIM
    vcol = H_VB // SWA_KV_DIM

    def cur(col):
        return lambda b, i, sk: (b * nq + i, col)

    def prev(col):
        return lambda b, i, sk: (b * nq + jnp.maximum(i - 1, 0), col)

    return pl.pallas_call(
        _swa_prompt_kernel,
        grid_spec=pltpu.PrefetchScalarGridSpec(
            num_scalar_prefetch=1,
            grid=(n_seq, nq),
            in_specs=[pl.BlockSpec((WINDOW, SWA_Q_DIM), cur(qcol)),
                      pl.BlockSpec((WINDOW, SWA_KV_DIM), prev(kcol)),
                      pl.BlockSpec((WINDOW, SWA_KV_DIM), prev(vcol)),
                      pl.BlockSpec((WINDOW, SWA_KV_DIM), cur(kcol)),
                      pl.BlockSpec((WINDOW, SWA_KV_DIM), cur(vcol))],
            out_specs=pl.BlockSpec((WINDOW, SWA_Q_DIM), lambda b, i, sk: (b * nq + i, 0)),
        ),
        out_shape=jax.ShapeDtypeStruct((n_seq * seq_len, SWA_Q_DIM), BF16),
        compiler_params=_cparams(("parallel", "arbitrary")),
        name=name,
    )(sinks, h, h, h, h, h)


def _swa_sample_kernel(sink_ref, q_ref, kc_ref, vc_ref, kp_ref, vp_ref, o_ref, *, nb, lq):
    rows = SWA_HEADS * lq
    grp_rows = SWA_GROUP * lq
    hd = SWA_HEAD_DIM
    nt = (((1,), (1,)), ((), ()))
    head_of_row = lax.broadcasted_iota(I32, (rows, 1), 0) // lq
    slope = jnp.zeros((rows, 1), F32)
    sink = jnp.zeros((rows, 1), F32)
    for hh in range(SWA_HEADS):
        slope = jnp.where(head_of_row == hh, _alibi_slope(hh), slope)
        sink = jnp.where(head_of_row == hh, sink_ref[hh], sink)
    qi = lax.broadcasted_iota(I32, (rows, WINDOW), 0) % lq
    dist_p = qi + WINDOW - lax.broadcasted_iota(I32, (rows, WINDOW), 1)
    valid_p = dist_p < WINDOW
    bias_p = slope * dist_p.astype(F32)
    dist_c = lax.broadcasted_iota(I32, (rows, lq), 0) % lq - lax.broadcasted_iota(I32, (rows, lq), 1)
    valid_c = dist_c >= 0
    bias_c = slope * dist_c.astype(F32)
    scale = hd ** -0.5

    q_all = q_ref[...].astype(F32)
    kc_all = kc_ref[...].astype(F32)
    vc_all = vc_ref[...].astype(F32)
    seq_outs = []
    for j in range(nb):
        js = slice(j * lq, (j + 1) * lq)
        pieces = []
        for hh in range(SWA_HEADS):
            g = hh // SWA_GROUP
            parts = []
            if g:
                parts.append(jnp.zeros((lq, g * hd), F32))
            parts.append(q_all[js, hh * hd:(hh + 1) * hd])
            if g < SWA_KV_HEADS - 1:
                parts.append(jnp.zeros((lq, (SWA_KV_HEADS - 1 - g) * hd), F32))
            pieces.append(jnp.concatenate(parts, axis=-1))
        q_big = jnp.concatenate(pieces, axis=0).astype(BF16)
        kp = kp_ref[j].astype(BF16)
        vp = vp_ref[j].astype(BF16)
        kcj = kc_all[js, :].astype(BF16)
        vcj = vc_all[js, :].astype(BF16)
        s_p = lax.dot_general(q_big, kp, nt, preferred_element_type=F32)
        s_p = jnp.where(valid_p, s_p * scale - bias_p, -jnp.inf)
        s_c = lax.dot_general(q_big, kcj, nt, preferred_element_type=F32)
        s_c = jnp.where(valid_c, s_c * scale - bias_c, -jnp.inf)
        o_big = _swa_softmax_pv([(s_p, vp), (s_c, vcj)], sink)
        o_grp = [o_big[g * grp_rows:(g + 1) * grp_rows, g * hd:(g + 1) * hd] for g in range(SWA_KV_HEADS)]
        o_heads = jnp.concatenate(o_grp, axis=0)
        seq_outs.append(jnp.concatenate([o_heads[hh * lq:(hh + 1) * lq, :] for hh in range(SWA_HEADS)],
                                        axis=-1))
    o_ref[...] = jnp.concatenate(seq_outs, axis=0).astype(o_ref.dtype)


def _swa_sample(h, sinks, k_past, v_past, *, row0, n_seq, lq, nb, name):
    rows = nb * lq
    rb0 = row0 // rows
    qcol = H_QB // SWA_Q_DIM
    kcol = H_KB // SWA_KV_DIM
    vcol = H_VB // SWA_KV_DIM
    kern = functools.partial(_swa_sample_kernel, nb=nb, lq=lq)
    past_spec = pl.BlockSpec((nb, WINDOW, SWA_KV_DIM), lambda g, sk: (g, 0, 0))
    return pl.pallas_call(
        kern,
        grid_spec=pltpu.PrefetchScalarGridSpec(
            num_scalar_prefetch=1,
            grid=(n_seq // nb,),
            in_specs=[pl.BlockSpec((rows, SWA_Q_DIM), lambda g, sk: (rb0 + g, qcol)),
                      pl.BlockSpec((rows, SWA_KV_DIM), lambda g, sk: (rb0 + g, kcol)),
                      pl.BlockSpec((rows, SWA_KV_DIM), lambda g, sk: (rb0 + g, vcol)),
                      past_spec, past_spec],
            out_specs=pl.BlockSpec((rows, SWA_Q_DIM), lambda g, sk: (g, 0)),
        ),
        out_shape=jax.ShapeDtypeStruct((n_seq * lq, SWA_Q_DIM), BF16),
        compiler_params=_cparams(("parallel",)),
        name=name,
    )(sinks, h, h, h, k_past, v_past)


def _layer_norm(x, g, b):
    mu = jnp.mean(x, axis=-1, keepdims=True)
    xc = x - mu
    var = jnp.mean(jnp.square(xc), axis=-1, keepdims=True)
    return xc * lax.rsqrt(var + EPS) * g + b


def _pack_bf16_pairs(x, s):
    lo = lax.bitcast_convert_type(x[:, (2 * s) * LANES:(2 * s + 1) * LANES].astype(BF16).astype(F32), I32)
    hi = lax.bitcast_convert_type(x[:, (2 * s + 1) * LANES:(2 * s + 2) * LANES].astype(BF16).astype(F32), I32)
    return lax.shift_right_logical(lo, 16) | (hi & HIGH_HALF)


def _unpack_bf16_pairs(w):
    lo = lax.bitcast_convert_type(lax.shift_left(w, 16), F32).astype(BF16)
    hi = lax.bitcast_convert_type(w & HIGH_HALF, F32).astype(BF16)
    return lo, hi


def _merge_kernel(oap_ref, oas_ref, obp_ref, obs_ref, ga_ref, gb_ref, xp_ref, xs_ref,
                  wpa_ref, wpb_ref, wout_ref, g_ref, b_ref, o_ref, xw_ref, *, np_tiles):
    i = pl.program_id(0)
    tm = o_ref.shape[0]
    br_a = jnp.dot(_pair_value(i, np_tiles, oap_ref, oas_ref), wpa_ref[...], preferred_element_type=F32)
    br_b = jnp.dot(_pair_value(i, np_tiles, obp_ref, obs_ref), wpb_ref[...], preferred_element_type=F32)
    merged = (jax.nn.sigmoid(ga_ref[...].astype(F32)) * br_a
              + jax.nn.sigmoid(gb_ref[...].astype(F32)) * br_b)
    mix = jnp.dot(merged.astype(BF16), wout_ref[...], preferred_element_type=F32)
    x = _pair_value(i, np_tiles, xp_ref, xs_ref)
    x1 = _layer_norm(DEEPNORM_ALPHA * x + mix, g_ref[...], b_ref[...])
    o_ref[...] = x1
    for s in range(PACKED_ROWS):
        xw_ref[pl.ds(s, tm, stride=PACKED_ROWS), :] = _pack_bf16_pairs(x1, s)


def _merge(oa, ob, h, x, wpa, wpb, wout, g, b, *, tm, name):
    t = h.shape[0]
    np_tiles = x[0].shape[0] // tm
    p_map, s_map = _pair_maps(np_tiles)
    row = lambda i: (i, 0)
    full = lambda i: (0, 0)
    pair = [pl.BlockSpec((tm, D_MODEL), p_map), pl.BlockSpec((tm, D_MODEL), s_map)]
    return pl.pallas_call(
        functools.partial(_merge_kernel, np_tiles=np_tiles),
        grid=(t // tm,),
        in_specs=pair + pair + [
                  pl.BlockSpec((tm, D_MODEL), lambda i: (i, H_GATE_A // D_MODEL)),
                  pl.BlockSpec((tm, D_MODEL), lambda i: (i, H_GATE_B // D_MODEL))] + pair + [
                  pl.BlockSpec((D_MODEL, D_MODEL), full),
                  pl.BlockSpec((D_MODEL, D_MODEL), full),
                  pl.BlockSpec((D_MODEL, D_MODEL), full),
                  pl.BlockSpec((1, D_MODEL), full),
                  pl.BlockSpec((1, D_MODEL), full)],
        out_specs=[pl.BlockSpec((tm, D_MODEL), row),
                   pl.BlockSpec((tm * PACKED_ROWS, LANES), row)],
        out_shape=[jax.ShapeDtypeStruct((t, D_MODEL), F32),
                   jax.ShapeDtypeStruct((t * PACKED_ROWS, LANES), I32)],
        compiler_params=_cparams(("parallel",)),
        name=name,
    )(*oa, *ob, h, h, *x, wpa, wpb, wout, g, b)


def _router_kernel(x_ref, whi_ref, wlo_ref, bias_ref, idx_ref, wl_ref, rank_ref, cntc_ref, cntr_ref,
                   cntc_scr, cntr_scr):
    i = pl.program_id(0)
    tm = x_ref.shape[0]

    @pl.when(i == 0)
    def _():
        cntc_scr[...] = jnp.zeros_like(cntc_scr)
        cntr_scr[...] = jnp.zeros_like(cntr_scr)

    x_hi, x_lo = _split_bf16(x_ref[...])
    nt = (((1,), (1,)), ((), ()))
    logits = (lax.dot_general(whi_ref[...], x_hi, nt, preferred_element_type=F32)
              + lax.dot_general(whi_ref[...], x_lo, nt, preferred_element_type=F32)
              + lax.dot_general(wlo_ref[...], x_hi, nt, preferred_element_type=F32))
    scores = jax.nn.sigmoid(logits)
    biased = scores + bias_ref[...]

    grouped = biased.reshape(N_GROUPS, GROUP_SIZE, tm)
    m1 = jnp.max(grouped, axis=1)
    n_top = jnp.sum(jnp.where(grouped == m1[:, None, :], 1.0, 0.0), axis=1)
    m2 = jnp.max(jnp.where(grouped < m1[:, None, :], grouped, -jnp.inf), axis=1)
    gscore = m1 + jnp.where(n_top >= 2.0, m1, m2)

    g_iota = lax.broadcasted_iota(I32, (N_GROUPS, tm), 0)
    beaten = jnp.zeros((N_GROUPS, tm), I32)
    for g in range(N_GROUPS):
        other = gscore[g:g + 1, :]
        ahead = (other > gscore) | ((other == gscore) & (g < g_iota))
        beaten = beaten + jnp.where(ahead, 1, 0)
    keep = jnp.where(beaten < TOPK_GROUPS, 1.0, 0.0)
    masked = jnp.where(keep[:, None, :] > 0.5, grouped, -jnp.inf).reshape(N_EXPERTS, tm)

    e_iota = lax.broadcasted_iota(I32, (N_EXPERTS, tm), 0)
    sel_f = jnp.zeros((N_EXPERTS, tm), F32)
    ids = []
    for _ in range(TOP_K):
        best = jnp.max(masked, axis=0, keepdims=True)
        idx = jnp.min(jnp.where(masked == best, e_iota, N_EXPERTS), axis=0, keepdims=True)
        hit = e_iota == idx
        sel_f = sel_f + jnp.where(hit, 1.0, 0.0)
        masked = jnp.where(hit, -jnp.inf, masked)
        ids.append(idx)

    top_sum = jnp.sum(sel_f * scores, axis=0, keepdims=True)

    t_r = lax.broadcasted_iota(I32, (tm, tm), 0)
    t_c = lax.broadcasted_iota(I32, (tm, tm), 1)
    before = jnp.where(t_r < t_c, 1.0, 0.0).astype(BF16)
    sel_b = sel_f.astype(BF16)
    rank = jnp.dot(sel_b, before, preferred_element_type=F32) + cntc_scr[...]
    cntc_scr[...] = cntc_scr[...] + jnp.sum(sel_f, axis=1, keepdims=True)
    cntr_scr[...] = cntr_scr[...] + lax.dot_general(jnp.ones((SUBLANES, tm), BF16), sel_b, nt,
                                                    preferred_element_type=F32)

    idx_rows, w_rows, rank_rows = [], [], []
    for idx in ids:
        hit = e_iota == idx
        w = jnp.sum(jnp.where(hit, scores, 0.0), axis=0, keepdims=True)
        w_rows.append(w / top_sum * ROUTED_SCALE)
        rank_rows.append(jnp.sum(jnp.where(hit, rank, 0.0), axis=0, keepdims=True))
        idx_rows.append(idx)
    idx_ref[...] = jnp.concatenate(idx_rows, axis=0)
    rank_ref[...] = jnp.concatenate(rank_rows, axis=0).astype(I32)

    w_all = jnp.concatenate(w_rows, axis=0)
    w_1 = w_all.astype(BF16)
    r_1 = w_all - w_1.astype(F32)
    w_2 = r_1.astype(BF16)
    w_3 = (r_1 - w_2.astype(F32)).astype(BF16)
    terms = jnp.concatenate([w_1, w_2, w_3, jnp.zeros_like(w_1)], axis=0)
    t_row = lax.broadcasted_iota(I32, (4 * TOP_K, TOP_K * LANES), 0)
    t_col = lax.broadcasted_iota(I32, (4 * TOP_K, TOP_K * LANES), 1)
    spread = jnp.where((t_row % TOP_K) == (t_col // LANES), 1.0, 0.0).astype(BF16)
    w_lanes = lax.dot_general(terms, spread, (((0,), (0,)), ((), ())), preferred_element_type=F32)
    for k in range(TOP_K):
        wl_ref[pl.ds(k, tm, stride=TOP_K), :] = w_lanes[:, k * LANES:(k + 1) * LANES]

    @pl.when(i == pl.num_programs(0) - 1)
    def _():
        cntc_ref[...] = cntc_scr[...].astype(I32)
        cntr_ref[...] = cntr_scr[...].astype(I32)


def _router(x1, w_hi, w_lo, bias, *, tm, name):
    t = x1.shape[0]
    full = lambda i: (0, 0)
    tok = lambda i: (0, i)
    return pl.pallas_call(
        _router_kernel,
        grid=(t // tm,),
        in_specs=[pl.BlockSpec((tm, D_MODEL), lambda i: (i, 0)),
                  pl.BlockSpec((N_EXPERTS, D_MODEL), full),
                  pl.BlockSpec((N_EXPERTS, D_MODEL), full),
                  pl.BlockSpec((N_EXPERTS, 1), full)],
        out_specs=[pl.BlockSpec((TOP_K, tm), tok),
                   pl.BlockSpec((TOP_K * tm, LANES), lambda i: (i, 0)),
                   pl.BlockSpec((TOP_K, tm), tok),
                   pl.BlockSpec((N_EXPERTS, 1), full),
                   pl.BlockSpec((SUBLANES, N_EXPERTS), full)],
        out_shape=[jax.ShapeDtypeStruct((TOP_K, t), I32),
                   jax.ShapeDtypeStruct((TOP_K * t, LANES), F32),
                   jax.ShapeDtypeStruct((TOP_K, t), I32),
                   jax.ShapeDtypeStruct((N_EXPERTS, 1), I32),
                   jax.ShapeDtypeStruct((SUBLANES, N_EXPERTS), I32)],
        scratch_shapes=[pltpu.VMEM((N_EXPERTS, 1), F32),
                        pltpu.VMEM((SUBLANES, N_EXPERTS), F32)],
        compiler_params=_cparams(("arbitrary",)),
        name=name,
    )(x1, w_hi, w_lo, bias)


ITEM_FIELDS = 8


def _byte_split(v):
    return lax.shift_right_logical(v, 8).astype(F32), (v & 255).astype(F32)


def _plan_kernel(cntc_ref, cntr_ref, idx_ref, rank_ref, meta_ref, items_ref, m_scr, *, n_tok):
    i = pl.program_id(0)
    tm = idx_ref.shape[1]
    nbp = m_scr.shape[0]
    nip = items_ref.shape[1]
    n_rows = n_tok * TOP_K
    n_blk = n_rows // EXPERT_ROWS
    nt = (((1,), (1,)), ((), ()))

    r_i = lax.broadcasted_iota(I32, (N_EXPERTS, N_EXPERTS), 0)
    c_i = lax.broadcasted_iota(I32, (N_EXPERTS, N_EXPERTS), 1)
    below = jnp.where(c_i < r_i, 1.0, 0.0).astype(BF16)
    ones_c = jnp.ones((N_EXPERTS, LANES), F32)
    c_hi, c_lo = _byte_split(cntc_ref[...])
    start_col = (256.0 * jnp.dot(below, (c_hi * ones_c).astype(BF16), preferred_element_type=F32)
                 + jnp.dot(below, (c_lo * ones_c).astype(BF16), preferred_element_type=F32))[:, 0:1]

    @pl.when(i == 0)
    def _():
        m_scr[...] = jnp.zeros_like(m_scr)
        above = jnp.where(r_i < c_i, 1.0, 0.0).astype(BF16)
        r_hi, r_lo = _byte_split(cntr_ref[...])
        start_row = (256.0 * jnp.dot(r_hi.astype(BF16), above, preferred_element_type=F32)
                     + jnp.dot(r_lo.astype(BF16), above, preferred_element_type=F32))[0:1, :]
        b_col = lax.broadcasted_iota(I32, (nbp, 1), 0)
        b_row = lax.broadcasted_iota(I32, (1, nbp), 1)
        blk_col = jnp.where(b_col < n_blk, b_col * EXPERT_ROWS, n_rows).astype(F32)
        blk_row = jnp.where(b_row < n_blk, b_row * EXPERT_ROWS, n_rows).astype(F32)
        v_col = jnp.concatenate([blk_col, start_col], axis=0)
        v_row = jnp.concatenate([blk_row, start_row], axis=1)
        j_col = lax.broadcasted_iota(I32, (nip, 1), 0)
        k_row = lax.broadcasted_iota(I32, (1, nip), 1)
        ahead = (v_row < v_col) | ((v_row == v_col) & (k_row < j_col))
        order_col = jnp.sum(jnp.where(ahead, 1.0, 0.0), axis=1, keepdims=True)
        pos_row = k_row.astype(F32)
        lo_abs = jnp.sum(jnp.where(order_col == pos_row, v_col, 0.0), axis=0, keepdims=True)
        hi_abs = jnp.sum(jnp.where(order_col == pos_row + 1.0, v_col, 0.0), axis=0, keepdims=True)
        hi_abs = jnp.where(k_row == nip - 1, float(n_rows), hi_abs)
        blk = jnp.minimum(jnp.floor(lo_abs * (1.0 / EXPERT_ROWS)), n_blk - 1.0)
        expert = jnp.sum(jnp.where(start_col <= lo_abs, 1.0, 0.0), axis=0, keepdims=True) - 1.0
        base = blk * EXPERT_ROWS
        e_col = lax.broadcasted_iota(I32, (N_EXPERTS, 1), 0).astype(F32)
        end_col = start_col + cntc_ref[...].astype(F32)
        seg_end = jnp.sum(jnp.where(e_col == expert, end_col, 0.0), axis=0, keepdims=True)
        follower = jnp.sum(jnp.where(start_col <= seg_end, 1.0, 0.0), axis=0, keepdims=True) - 1.0
        follower = jnp.where(seg_end < float(n_rows), follower, -1.0)
        fields = [blk, expert, lo_abs - base, hi_abs - base, follower]
        fields.append(jnp.zeros((ITEM_FIELDS - len(fields), nip), F32))
        items_ref[...] = jnp.concatenate(fields, axis=0).astype(I32)

    e_iota = lax.broadcasted_iota(I32, (N_EXPERTS, tm), 0)
    b_iota = lax.broadcasted_iota(I32, (nbp, tm), 0)
    l_iota = lax.broadcasted_iota(I32, (EXPERT_ROWS, tm), 0)
    tok = i * tm + lax.broadcasted_iota(I32, (1, tm), 1)
    tok_hi, tok_lo = _byte_split(tok)
    acc = jnp.zeros(m_scr.shape, F32)
    for k in range(TOP_K):
        hit = e_iota == idx_ref[k:k + 1, :]
        dest = (jnp.sum(jnp.where(hit, start_col, 0.0), axis=0, keepdims=True).astype(I32)
                + rank_ref[k:k + 1, :])
        oh_blk = jnp.where(b_iota == lax.shift_right_logical(dest, 8), 1.0, 0.0).astype(BF16)
        in_blk = l_iota == (dest & (EXPERT_ROWS - 1))
        vals = jnp.concatenate([jnp.where(in_blk, tok_hi, 0.0), jnp.where(in_blk, tok_lo, 0.0),
                                jnp.where(in_blk, float(k), 0.0)], axis=0).astype(BF16)
        acc = acc + lax.dot_general(oh_blk, vals, nt, preferred_element_type=F32)
    m_scr[...] = m_scr[...] + acc

    @pl.when(i == pl.num_programs(0) - 1)
    def _():
        m = m_scr[...]
        row_tok = m[:, 0:EXPERT_ROWS] * 256.0 + m[:, EXPERT_ROWS:2 * EXPERT_ROWS]
        row_slot = m[:, 2 * EXPERT_ROWS:3 * EXPERT_ROWS] * float(n_tok) + row_tok
        meta_ref[...] = jnp.concatenate([row_tok, row_slot], axis=1).astype(I32)


def _plan(cnt_col, cnt_row, idx, rank, *, tm, name):
    t = idx.shape[1]
    assert EXPERT_ROWS == 256 and (t * TOP_K) % (2 * EXPERT_ROWS) == 0
    n_blk = t * TOP_K // EXPERT_ROWS
    nbp = -(-n_blk // LANES) * LANES
    nip = nbp + N_EXPERTS
    tok = lambda i: (0, i)
    full = lambda i: (0, 0)
    return pl.pallas_call(
        functools.partial(_plan_kernel, n_tok=t),
        grid=(t // tm,),
        in_specs=[pl.BlockSpec((N_EXPERTS, 1), full),
                  pl.BlockSpec((SUBLANES, N_EXPERTS), full),
                  pl.BlockSpec((TOP_K, tm), tok),
                  pl.BlockSpec((TOP_K, tm), tok)],
        out_specs=[pl.BlockSpec((nbp, 2 * EXPERT_ROWS), full),
                   pl.BlockSpec((ITEM_FIELDS, nip), full)],
        out_shape=[jax.ShapeDtypeStruct((nbp, 2 * EXPERT_ROWS), I32),
                   jax.ShapeDtypeStruct((ITEM_FIELDS, nip), I32)],
        scratch_shapes=[pltpu.VMEM((nbp, 3 * EXPERT_ROWS), F32)],
        compiler_params=_cparams(("arbitrary",)),
        name=name,
    )(cnt_col, cnt_row, idx, rank)


META_CHUNK = 4 * EXPERT_ROWS
GATHER_STRIDE = EXPERT_ROWS + 8


def _experts_kernel(items_ref, xw_hbm, meta_hbm, wg_hbm, wu_hbm, wd_hbm, ys_hbm,
                    xw, meta, tile, lhs, wg_f, wu_f, wd_f, wg_b, wu_b, wd_b, yacc, ybuf, state,
                    sem_x, sem_m, sem_y, sem_w, *, nip, n_blk):
    i = pl.program_id(0)
    blk = items_ref[i]
    expert = items_ref[nip + i]
    lo = items_ref[2 * nip + i]
    hi = items_ref[3 * nip + i]
    follower = items_ref[4 * nip + i]
    nonempty = hi > lo
    par = blk & 1
    chunk = lax.shift_right_logical(blk, 1)
    cpar = chunk & 1
    mbase = cpar * META_CHUNK + par * (2 * EXPERT_ROWS)

    def meta_copy(c, slot):
        return pltpu.make_async_copy(meta_hbm.at[pl.ds(pl.multiple_of(c * META_CHUNK, META_CHUNK), META_CHUNK)],
                                     meta.at[pl.ds(pl.multiple_of(slot * META_CHUNK, META_CHUNK), META_CHUNK)],
                                     sem_m.at[slot])

    def ybuf_drain(slot):
        pltpu.make_async_copy(ys_hbm.at[pl.ds(0, EXPERT_ROWS * ROW_CHUNKS), :], ybuf.at[slot],
                              sem_y.at[slot]).wait()

    def ybuf_fill_and_send(slot):
        for j in range(ROW_CHUNKS):
            ybuf[slot, pl.ds(j, EXPERT_ROWS, stride=ROW_CHUNKS), :] = yacc[:, j * LANES:(j + 1) * LANES]

        for r in range(EXPERT_ROWS):
            src = ybuf.at[slot, pl.ds(r * ROW_CHUNKS, ROW_CHUNKS), :]
            dst_row = pl.multiple_of(meta[mbase + EXPERT_ROWS + r] * ROW_CHUNKS, ROW_CHUNKS)
            pltpu.make_async_copy(src, ys_hbm.at[pl.ds(dst_row, ROW_CHUNKS), :],
                                  sem_y.at[slot]).start(priority=r % 2)

    def weight_copies(e, slot):
        return [pltpu.make_async_copy(src.at[e], dst.at[slot], sem_w.at[slot])
                for src, dst in ((wg_hbm, wg_f), (wu_hbm, wu_f), (wd_hbm, wd_f))]

    @pl.when(i == 0)
    def _():
        resident = pltpu.make_async_copy(xw_hbm, xw, sem_x)
        resident.start()
        meta_copy(0, 0).start()
        for cp in weight_copies(expert, 0):
            cp.start()
        state[0] = -1
        state[1] = 1
        resident.wait()

    @pl.when(nonempty & (expert != state[0]))
    def _():
        slot = 1 - state[1]
        for cp in weight_copies(expert, slot):
            cp.wait()
        state[0] = expert
        state[1] = slot

        @pl.when(follower >= 0)
        def _():
            for cp in weight_copies(follower, 1 - slot):
                cp.start()

        wg_b[...] = wg_f[slot].astype(BF16)
        wu_b[...] = wu_f[slot].astype(BF16)
        wd_b[...] = wd_f[slot].astype(BF16)

    @pl.when(nonempty & (lo == 0))
    def _():
        @pl.when(par == 0)
        def _():
            meta_copy(chunk, cpar).wait()

            @pl.when(2 * (chunk + 1) < n_blk)
            def _():
                meta_copy(chunk + 1, 1 - cpar).start()

        for r in range(EXPERT_ROWS):
            t4 = pl.multiple_of(meta[mbase + r] * PACKED_ROWS, PACKED_ROWS)
            tile[pl.ds(r, PACKED_ROWS, stride=GATHER_STRIDE), :] = xw[pl.ds(t4, PACKED_ROWS), :]
        cols = []
        for s in range(PACKED_ROWS):
            cols.extend(_unpack_bf16_pairs(tile[pl.ds(s * GATHER_STRIDE, EXPERT_ROWS), :]))
        lhs[...] = jnp.concatenate(cols, axis=-1)

    def ffn(x):
        gate = jnp.dot(x, wg_b[...], preferred_element_type=F32)
        up = jnp.dot(x, wu_b[...], preferred_element_type=F32)
        hid = (jax.nn.silu(gate) * up).astype(BF16)
        return jnp.dot(hid, wd_b[...], preferred_element_type=F32)

    whole = (lo == 0) & (hi == EXPERT_ROWS)

    @pl.when(whole)
    def _():
        yacc[...] = ffn(lhs[...])

    @pl.when(nonempty & jnp.logical_not(whole))
    def _():
        row = lax.broadcasted_iota(I32, (EXPERT_ROWS, 1), 0)
        y = jnp.where((row >= lo) & (row < hi), ffn(lhs[...]), 0.0)

        @pl.when(lo == 0)
        def _():
            yacc[...] = y

        @pl.when(lo > 0)
        def _():
            yacc[...] = yacc[...] + y

    @pl.when(nonempty & (hi == EXPERT_ROWS))
    def _():
        @pl.when(blk >= 2)
        def _():
            ybuf_drain(par)

        for slot in range(2):
            @pl.when(par == slot)
            def _():
                ybuf_fill_and_send(slot)

    @pl.when(i == pl.num_programs(0) - 1)
    def _():
        ybuf_drain(0)
        ybuf_drain(1)


def _experts(items, xw, meta, wg, wu, wd, *, n_tok, name):
    nip = items.shape[0] // ITEM_FIELDS
    n_rows = n_tok * TOP_K
    n_blk = n_rows // EXPERT_ROWS
    n_items = n_blk + N_EXPERTS
    kern = functools.partial(_experts_kernel, nip=nip, n_blk=n_blk)
    any_spec = pl.BlockSpec(memory_space=pl.ANY)
    return pl.pallas_call(
        kern,
        grid_spec=pltpu.PrefetchScalarGridSpec(
            num_scalar_prefetch=1,
            grid=(n_items,),
            in_specs=[any_spec] * 5,
            out_specs=any_spec,
            scratch_shapes=[pltpu.VMEM(xw.shape, I32),
                            pltpu.SMEM((2 * META_CHUNK,), I32),
                            pltpu.VMEM((PACKED_ROWS * GATHER_STRIDE, LANES), I32),
                            pltpu.VMEM((EXPERT_ROWS, D_MODEL), BF16),
                            pltpu.VMEM((2, D_MODEL, EXPERT_FF), F32),
                            pltpu.VMEM((2, D_MODEL, EXPERT_FF), F32),
                            pltpu.VMEM((2, EXPERT_FF, D_MODEL), F32),
                            pltpu.VMEM((D_MODEL, EXPERT_FF), BF16),
                            pltpu.VMEM((D_MODEL, EXPERT_FF), BF16),
                            pltpu.VMEM((EXPERT_FF, D_MODEL), BF16),
                            pltpu.VMEM((EXPERT_ROWS, D_MODEL), F32),
                            pltpu.VMEM((2, EXPERT_ROWS * ROW_CHUNKS, LANES), F32),
                            pltpu.SMEM((2,), I32),
                            pltpu.SemaphoreType.DMA,
                            pltpu.SemaphoreType.DMA((2,)),
                            pltpu.SemaphoreType.DMA((2,)),
                            pltpu.SemaphoreType.DMA((2,))],
        ),
        out_shape=jax.ShapeDtypeStruct((n_rows * ROW_CHUNKS, LANES), F32),
        compiler_params=_cparams(("arbitrary",)),
        name=name,
    )(items, xw, meta, wg, wu, wd)


def _combine_kernel(ys_ref, x_ref, wl_ref, wsg_ref, wsu_ref, wsd_ref, g_ref, b_ref, op_ref, os_ref,
                    routed, *, np_tiles):
    x = x_ref[...]
    xb = x.astype(BF16)
    hid = (jax.nn.silu(jnp.dot(xb, wsg_ref[...], preferred_element_type=F32))
           * jnp.dot(xb, wsu_ref[...], preferred_element_type=F32))
    shared = jnp.dot(hid.astype(BF16), wsd_ref[...], preferred_element_type=F32)
    acc = None
    for k in range(TOP_K):
        part = wl_ref[:, k:k + 1, :] * ys_ref[k]
        acc = part if acc is None else acc + part
    routed[...] = acc
    ffn = jnp.concatenate([routed[:, j, :] for j in range(ROW_CHUNKS)], axis=-1) + shared
    out = _layer_norm(DEEPNORM_ALPHA * x + ffn, g_ref[...], b_ref[...])
    i = pl.program_id(0)

    @pl.when(i < np_tiles)
    def _():
        op_ref[...] = out

    @pl.when(i >= np_tiles)
    def _():
        os_ref[...] = out


def _combine(ys, x1, w_tok, wsg, wsu, wsd, g, b, *, n_prompt, tm, name):
    t = x1.shape[0]
    np_tiles = n_prompt // tm
    p_map, s_map = _pair_maps(np_tiles)
    row = lambda i: (i, 0)
    full = lambda i: (0, 0)
    return pl.pallas_call(
        functools.partial(_combine_kernel, np_tiles=np_tiles),
        grid=(t // tm,),
        in_specs=[pl.BlockSpec((TOP_K, tm, ROW_CHUNKS, LANES), lambda i: (0, i, 0, 0)),
                  pl.BlockSpec((tm, D_MODEL), row),
                  pl.BlockSpec((tm, TOP_K, LANES), lambda i: (i, 0, 0)),
                  pl.BlockSpec((D_MODEL, SHARED_FF), full),
                  pl.BlockSpec((D_MODEL, SHARED_FF), full),
                  pl.BlockSpec((SHARED_FF, D_MODEL), full),
                  pl.BlockSpec((1, D_MODEL), full),
                  pl.BlockSpec((1, D_MODEL), full)],
        out_specs=[pl.BlockSpec((tm, D_MODEL), p_map), pl.BlockSpec((tm, D_MODEL), s_map)],
        out_shape=[jax.ShapeDtypeStruct((n_prompt, D_MODEL), F32),
                   jax.ShapeDtypeStruct((t - n_prompt, D_MODEL), F32)],
        scratch_shapes=[pltpu.VMEM((tm, ROW_CHUNKS, LANES), F32)],
        compiler_params=_cparams(("arbitrary",)),
        name=name,
    )(ys, x1, w_tok, wsg, wsu, wsd, g, b)


def _pack_w_in(w_in):
    sizes = (GLA_KEY_DIM, GLA_KEY_DIM, GLA_VAL_DIM, GLA_VAL_DIM, GLA_GATE_RANK,
             SWA_Q_DIM, SWA_KV_DIM, SWA_KV_DIM, D_MODEL, D_MODEL)
    offs = [0]
    for s in sizes:
        offs.append(offs[-1] + s)
    qa, ka, va, ga, gk, qb, kb, vb, gate_a, gate_b = (w_in[:, offs[i]:offs[i + 1]] for i in range(10))
    pad = lambda w, n: jnp.pad(w, ((0, 0), (0, n - w.shape[1])))
    packed = jnp.concatenate([qa, ka, va, ga, qb, gate_a, gate_b, kb, vb, pad(gk, 2 * LANES)], axis=1)
    assert packed.shape[1] == H_WIDTH
    return packed.astype(BF16), jnp.concatenate([kb, vb], axis=1).astype(BF16)


def kernel(x_prompt, x_sample, state_gla, cache_swa_k, cache_swa_v, w_in, w_gk_up, b_gk, gla_norm_g,
           attn_sinks, w_proj_a, w_proj_b, w_out, ln1_g, ln1_b, w_router, router_bias,
           w_expert_gate, w_expert_up, w_expert_down, w_shared_gate, w_shared_up, w_shared_down,
           ln2_g, ln2_b):
    assert w_in.shape[0] == 1, "single-layer trunk"
    bp, lp, d = x_prompt.shape
    bs, ls, _ = x_sample.shape
    assert d == D_MODEL and ls == SUBLANES and cache_swa_k.shape[2] == WINDOW
    tp, ts = bp * lp, bs * ls
    t = tp + ts

    xp = x_prompt.reshape(tp, d)
    xs = x_sample.reshape(ts, d)
    w_main, w_kv = _pack_w_in(w_in[0])
    h = _matmul(xp, xs, w_main, BF16, _pair_tile(tp, ts, 1024), H_TN, "proj_in")

    xp_tail = x_prompt[:, lp - WINDOW:].reshape(bp * WINDOW, d)
    kv_tail = _matmul(xp_tail, xs, w_kv, F32, _pair_tile(bp * WINDOW, ts, 512), 2 * SWA_KV_DIM,
                      "proj_kv_tail")

    wup = jnp.pad(w_gk_up[0], ((0, LANES - GLA_GATE_RANK), (0, 0))).astype(BF16)
    bgk = b_gk[0].reshape(1, GLA_KEY_DIM)
    gn = gla_norm_g[0].reshape(1, GLA_DV)
    oa_p, s_prompt = _gla(h, wup, bgk, gn, None, row0=0, n_seq=bp, seq_len=lp, nb=1, c=GLA_CHUNK,
                          sub=4, name="gla_prompt")
    oa_s, s_sample = _gla(h, wup, bgk, gn, state_gla[0], row0=tp, n_seq=bs, seq_len=ls, nb=8,
                          c=math.gcd(ls, GLA_CHUNK), sub=1, name="gla_sample")

    sinks = attn_sinks[0]
    k_past = cache_swa_k[0].reshape(bs, WINDOW, SWA_KV_DIM)
    v_past = cache_swa_v[0].reshape(bs, WINDOW, SWA_KV_DIM)
    ob_p = _swa_prompt(h, sinks, n_seq=bp, seq_len=lp, name="swa_prompt")
    ob_s = _swa_sample(h, sinks, k_past, v_past, row0=tp, n_seq=bs, lq=ls, nb=8, name="swa_sample")

    x1, xw = _merge((oa_p, oa_s), (ob_p, ob_s), h, (xp, xs), w_proj_a[0].astype(BF16),
                    w_proj_b[0].astype(BF16), w_out[0].astype(BF16), ln1_g[0].reshape(1, d),
                    ln1_b[0].reshape(1, d), tm=_pair_tile(tp, ts, 512), name="merge_ln1")

    wr_t = w_router[0].T
    wr_hi = wr_t.astype(BF16)
    wr_lo = (wr_t - wr_hi.astype(F32)).astype(BF16)
    tm_r = _tile(t, 512)
    idx, w_lanes, rank, cnt_col, cnt_row = _router(x1, wr_hi, wr_lo, router_bias[0].reshape(N_EXPERTS, 1),
                                                   tm=tm_r, name="router")
    meta, items = _plan(cnt_col, cnt_row, idx, rank, tm=tm_r, name="plan")
    ys = _experts(items.reshape(-1), xw, meta.reshape(-1), w_expert_gate[0], w_expert_up[0],
                  w_expert_down[0], n_tok=t, name="experts")
    y_p, y_s = _combine(ys.reshape(TOP_K, t, ROW_CHUNKS, LANES), x1, w_lanes.reshape(t, TOP_K, LANES),
                        w_shared_gate[0].astype(BF16), w_shared_up[0].astype(BF16),
                        w_shared_down[0].astype(BF16), ln2_g[0].reshape(1, d), ln2_b[0].reshape(1, d),
                        n_prompt=tp, tm=_pair_tile(tp, ts, 128), name="combine_ln2")

    y_prompt = y_p.reshape(bp, lp, d)
    y_sample = y_s.reshape(bs, ls, d)
    k_tail = kv_tail[:, :SWA_KV_DIM]
    v_tail = kv_tail[:, SWA_KV_DIM:]
    kv_shape = (SWA_KV_HEADS, SWA_HEAD_DIM)
    k_prompt = k_tail[:bp * WINDOW].reshape(1, bp, WINDOW, *kv_shape)
    v_prompt = v_tail[:bp * WINDOW].reshape(1, bp, WINDOW, *kv_shape)
    k_new = k_tail[bp * WINDOW:].reshape(bs, ls, *kv_shape)
    v_new = v_tail[bp * WINDOW:].reshape(bs, ls, *kv_shape)
    k_sample = jnp.concatenate([cache_swa_k[0][:, ls:], k_new], axis=1)[None]
    v_sample = jnp.concatenate([cache_swa_v[0][:, ls:], v_new], axis=1)[None]
    return (y_prompt, y_sample, s_prompt[None], s_sample[None], k_prompt, v_prompt, k_sample, v_sample)
```

```python
import functools
import math

import jax
import jax.numpy as jnp
from jax import lax
from jax.experimental import pallas as pl
from jax.experimental.pallas import tpu as pltpu

F32 = jnp.float32
BF16 = jnp.bfloat16
I32 = jnp.int32

D_MODEL = 1024
GLA_HEADS = 4
GLA_DK = 128
GLA_DV = 256
GLA_KEY_DIM = GLA_HEADS * GLA_DK
GLA_VAL_DIM = GLA_HEADS * GLA_DV
GLA_GATE_RANK = 16
GLA_GATE_NORMALIZER = 16.0
GLA_CHUNK = 64
SWA_HEADS = 16
SWA_KV_HEADS = 4
SWA_GROUP = SWA_HEADS // SWA_KV_HEADS
SWA_HEAD_DIM = 64
SWA_Q_DIM = SWA_HEADS * SWA_HEAD_DIM
SWA_KV_DIM = SWA_KV_HEADS * SWA_HEAD_DIM
WINDOW = 128
N_EXPERTS = 256
TOP_K = 8
N_GROUPS = 8
GROUP_SIZE = N_EXPERTS // N_GROUPS
TOPK_GROUPS = 4
EXPERT_FF = 256
SHARED_FF = 256
ROUTED_SCALE = 2.5
DEEPNORM_ALPHA = 2.0 ** 0.25
EPS = 1e-5

LANES = 128
SUBLANES = 8
ROW_CHUNKS = D_MODEL // LANES
VMEM_LIMIT = 56 * 1024 * 1024

H_QA, H_KA, H_VA, H_GA, H_QB, H_GATE_A, H_GATE_B, H_KB, H_VB, H_GK = (
    0, 512, 1024, 2048, 3072, 4096, 5120, 6144, 6400, 6656)
H_WIDTH = 6912
H_TN = 2304

EXPERT_ROWS = 256
PACKED_ROWS = ROW_CHUNKS // 2
HIGH_HALF = -65536


def _cparams(sem, vmem=VMEM_LIMIT):
    return pltpu.CompilerParams(dimension_semantics=sem, vmem_limit_bytes=vmem)


def _tile(n, pref):
    t = min(n, pref)
    while n % t:
        t -= LANES
    assert t > 0 and t % LANES == 0, (n, pref)
    return t


def _pair_tile(n_p, n_s, pref):
    return _tile(math.gcd(n_p, n_s), pref)


def _pair_maps(np_tiles, col=0, extra=0):
    def p_map(i, *_):
        return (jnp.minimum(i, np_tiles - 1), col)

    def s_map(i, *_):
        return (jnp.maximum(i - np_tiles, 0), col)

    return p_map, s_map


def _pair_value(i, np_tiles, p_ref, s_ref):
    return jnp.where(i < np_tiles, p_ref[...], s_ref[...])


def _mm_kernel(xp_ref, xs_ref, w_ref, o_ref, *, np_tiles):
    x = _pair_value(pl.program_id(0), np_tiles, xp_ref, xs_ref)
    o_ref[...] = jnp.dot(x.astype(BF16), w_ref[...], preferred_element_type=F32).astype(o_ref.dtype)


def _matmul(xp, xs, w, out_dtype, tm, tn, name):
    k = xp.shape[1]
    m = xp.shape[0] + xs.shape[0]
    n = w.shape[1]
    np_tiles = xp.shape[0] // tm
    p_map, s_map = _pair_maps(np_tiles)
    return pl.pallas_call(
        functools.partial(_mm_kernel, np_tiles=np_tiles),
        grid=(m // tm, n // tn),
        in_specs=[pl.BlockSpec((tm, k), p_map),
                  pl.BlockSpec((tm, k), s_map),
                  pl.BlockSpec((k, tn), lambda i, j: (0, j))],
        out_specs=pl.BlockSpec((tm, tn), lambda i, j: (i, j)),
        out_shape=jax.ShapeDtypeStruct((m, n), out_dtype),
        compiler_params=_cparams(("parallel", "arbitrary")),
        name=name,
    )(xp, xs, w)


def _split_bf16(x):
    hi = x.astype(BF16)
    lo = (x - hi.astype(F32)).astype(BF16)
    return hi, lo


def _gla_kernel(*refs, par, nb, c, sub, has_s0):
    q_refs, k_refs, v_refs, ga_refs, gk_refs = (refs[n * par:(n + 1) * par] for n in range(5))
    rest = refs[5 * par:]
    n_state = par * GLA_HEADS
    s_scr = [rest[len(rest) - n_state + p * GLA_HEADS:len(rest) - n_state + (p + 1) * GLA_HEADS]
             for p in range(par)]
    rest = rest[:len(rest) - n_state]
    if has_s0:
        wup_ref, bgk_ref, gn_ref, s0_ref, o_ref, sout_ref = rest
    else:
        wup_ref, bgk_ref, gn_ref, o_ref, sout_ref = rest
        s0_ref = None
    ci = pl.program_id(1)
    rows = nb * c

    @pl.when(ci == 0)
    def _():
        for p in range(par):
            for h in range(GLA_HEADS):
                if has_s0:
                    s_scr[p][h][...] = s0_ref[p * nb:(p + 1) * nb, h]
                else:
                    s_scr[p][h][...] = jnp.zeros_like(s_scr[p][h])

    r_i = lax.broadcasted_iota(I32, (rows, rows), 0)
    c_i = lax.broadcasted_iota(I32, (rows, rows), 1)
    same_seq = (r_i // c) == (c_i // c)
    causal = same_seq & (c_i <= r_i)
    tri = jnp.where(causal, 1.0, 0.0).astype(BF16)
    seg = jnp.where(same_seq, 1.0, 0.0).astype(BF16)
    ones_kv = jnp.ones((rows, GLA_DV), BF16)
    seq_of_row = lax.broadcasted_iota(I32, (rows, 1), 0) // c
    tn = (((0,), (0,)), ((), ()))

    for s, p in [(s, p) for s in range(sub) for p in range(par)]:
        q_ref, k_ref, v_ref, ga_ref, gk_ref = q_refs[p], k_refs[p], v_refs[p], ga_refs[p], gk_refs[p]
        rs = slice(s * rows, (s + 1) * rows)
        pre = jnp.dot(gk_ref[rs, :], wup_ref[...], preferred_element_type=F32) + bgk_ref[...]
        log_a = jax.nn.log_sigmoid(pre) / GLA_GATE_NORMALIZER
        la_hi, la_lo = _split_bf16(log_a)
        b = (jnp.dot(tri, la_hi, preferred_element_type=F32)
             + jnp.dot(tri, la_lo, preferred_element_type=F32))
        b_last = (jnp.dot(seg, la_hi, preferred_element_type=F32)
                  + jnp.dot(seg, la_lo, preferred_element_type=F32))
        e_pos = jnp.exp(b)
        e_neg = jnp.exp(-b)
        e_rem = jnp.exp(b_last - b)
        q = q_ref[rs, :].astype(F32) * (GLA_DK ** -0.5)
        k = k_ref[rs, :].astype(F32)
        q_dec = (q * e_pos).astype(BF16)
        k_dec = (k * e_neg).astype(BF16)
        k_rem = k * e_rem
        outs = []
        for h in range(GLA_HEADS):
            ks = slice(h * GLA_DK, (h + 1) * GLA_DK)
            vh = v_ref[rs, h * GLA_DV:(h + 1) * GLA_DV]
            att = lax.dot_general(q_dec[:, ks], k_dec[:, ks], (((1,), (1,)), ((), ())),
                                  preferred_element_type=F32)
            att = jnp.where(causal, att, 0.0).astype(BF16)
            o_h = jnp.dot(att, vh, preferred_element_type=F32)
            for j in range(nb):
                mine = (seq_of_row == j) if nb > 1 else None
                pick = (lambda a: jnp.where(mine, a, 0.0)) if nb > 1 else (lambda a: a)
                s_old = s_scr[p][h][j]
                o_h = o_h + pick(jnp.dot(q_dec[:, ks], s_old.astype(BF16), preferred_element_type=F32))
                dec = (lax.dot_general(pick(la_hi[:, ks].astype(F32)).astype(BF16), ones_kv, tn,
                                       preferred_element_type=F32)
                       + lax.dot_general(pick(la_lo[:, ks].astype(F32)).astype(BF16), ones_kv, tn,
                                         preferred_element_type=F32))
                upd = lax.dot_general(pick(k_rem[:, ks]).astype(BF16), vh, tn, preferred_element_type=F32)
                s_scr[p][h][j] = jnp.exp(dec) * s_old + upd
            o_h = o_h * lax.rsqrt(jnp.mean(jnp.square(o_h), axis=-1, keepdims=True) + EPS) * gn_ref[...]
            outs.append(o_h)
        o = jnp.concatenate(outs, axis=-1) * jax.nn.silu(ga_ref[rs, :].astype(F32))
        o_ref[p, rs, :] = o.astype(o_ref.dtype)

    @pl.when(ci == pl.num_programs(1) - 1)
    def _():
        for p in range(par):
            for h in range(GLA_HEADS):
                sout_ref[p * nb:(p + 1) * nb, h] = s_scr[p][h][...]


def _gla(h, wup, bgk, gn, s0, *, row0, n_seq, seq_len, par, nb, c, sub, name):
    rows = nb * c * sub
    n_blocks = n_seq // nb
    n_steps = seq_len // (c * sub)
    rb0 = row0 // rows

    def rmap(p, col):
        return lambda g, i: (rb0 + (g * par + p) * n_steps + i, col)

    fields = ((GLA_KEY_DIM, H_QA), (GLA_KEY_DIM, H_KA), (GLA_VAL_DIM, H_VA), (GLA_VAL_DIM, H_GA), (LANES, H_GK))
    in_specs = [pl.BlockSpec((rows, width), rmap(p, off // width)) for width, off in fields for p in range(par)]
    in_specs += [pl.BlockSpec((LANES, GLA_KEY_DIM), lambda g, i: (0, 0)),
                 pl.BlockSpec((1, GLA_KEY_DIM), lambda g, i: (0, 0)),
                 pl.BlockSpec((1, GLA_DV), lambda g, i: (0, 0))]
    args = [h] * (5 * par) + [wup, bgk, gn]
    state_spec = pl.BlockSpec((par * nb, GLA_HEADS, GLA_DK, GLA_DV), lambda g, i: (g, 0, 0, 0))
    if s0 is not None:
        in_specs.append(state_spec)
        args.append(s0)
    kern = functools.partial(_gla_kernel, par=par, nb=nb, c=c, sub=sub, has_s0=s0 is not None)
    return pl.pallas_call(
        kern,
        grid=(n_blocks // par, n_steps),
        in_specs=in_specs,
        out_specs=[pl.BlockSpec((par, rows, GLA_VAL_DIM), lambda g, i: (g, i, 0)), state_spec],
        out_shape=[jax.ShapeDtypeStruct((n_blocks, nb * seq_len, GLA_VAL_DIM), BF16),
                   jax.ShapeDtypeStruct((n_seq, GLA_HEADS, GLA_DK, GLA_DV), F32)],
        scratch_shapes=[pltpu.VMEM((nb, GLA_DK, GLA_DV), F32)] * (par * GLA_HEADS),
        compiler_params=_cparams(("parallel", "arbitrary")),
        name=name,
    )(*args)


def _alibi_slope(head):
    return 2.0 ** (-8.0 * (head + 1) / SWA_HEADS)


def _swa_softmax_pv(parts, sink):
    m = sink
    for s, _ in parts:
        m = jnp.maximum(m, jnp.max(s, axis=-1, keepdims=True))
    denom = jnp.exp(sink - m)
    acc = None
    for s, v in parts:
        p = jnp.exp(s - m)
        denom = denom + jnp.sum(p, axis=-1, keepdims=True)
        pv = jnp.dot(p.astype(BF16), v, preferred_element_type=F32)
        acc = pv if acc is None else acc + pv
    return acc / denom


def _swa_prompt_kernel(sink_ref, q_ref, kp_ref, vp_ref, kc_ref, vc_ref, o_ref):
    i = pl.program_id(1)
    span = 2 * WINDOW
    kcat = jnp.concatenate([kp_ref[...], kc_ref[...]], axis=0)
    vcat = jnp.concatenate([vp_ref[...], vc_ref[...]], axis=0)
    row = lax.broadcasted_iota(I32, (WINDOW, span), 0)
    col = lax.broadcasted_iota(I32, (WINDOW, span), 1)
    dist_i = row + WINDOW - col
    valid = (dist_i >= 0) & (dist_i < WINDOW) & ((col >= WINDOW) | (i > 0))
    dist = dist_i.astype(F32)
    outs = []
    for hh in range(SWA_HEADS):
        g = hh // SWA_GROUP
        gs = slice(g * SWA_HEAD_DIM, (g + 1) * SWA_HEAD_DIM)
        qh = q_ref[:, hh * SWA_HEAD_DIM:(hh + 1) * SWA_HEAD_DIM]
        s = lax.dot_general(qh, kcat[:, gs], (((1,), (1,)), ((), ())), preferred_element_type=F32)
        s = s * (SWA_HEAD_DIM ** -0.5) - _alibi_slope(hh) * dist
        s = jnp.where(valid, s, -jnp.inf)
        outs.append(_swa_softmax_pv([(s, vcat[:, gs])], sink_ref[hh]))
    o_ref[...] = jnp.concatenate(outs, axis=-1).astype(o_ref.dtype)


def _swa_prompt(h, sinks, *, n_seq, seq_len, name):
    nq = seq_len // WINDOW
    qcol = H_QB // SWA_Q_DIM
    kcol = H_KB // SWA_KV_DIM
    vcol = H_VB // SWA_KV_DIM

    def cur(col):
        return lambda b, i, sk: (b * nq + i, col)

    def prev(col):
        return lambda b, i, sk: (b * nq + jnp.maximum(i - 1, 0), col)

    return pl.pallas_call(
        _swa_prompt_kernel,
        grid_spec=pltpu.PrefetchScalarGridSpec(
            num_scalar_prefetch=1,
            grid=(n_seq, nq),
            in_specs=[pl.BlockSpec((WINDOW, SWA_Q_DIM), cur(qcol)),
                      pl.BlockSpec((WINDOW, SWA_KV_DIM), prev(kcol)),
                      pl.BlockSpec((WINDOW, SWA_KV_DIM), prev(vcol)),
                      pl.BlockSpec((WINDOW, SWA_KV_DIM), cur(kcol)),
                      pl.BlockSpec((WINDOW, SWA_KV_DIM), cur(vcol))],
            out_specs=pl.BlockSpec((WINDOW, SWA_Q_DIM), lambda b, i, sk: (b * nq + i, 0)),
        ),
        out_shape=jax.ShapeDtypeStruct((n_seq * seq_len, SWA_Q_DIM), BF16),
        compiler_params=_cparams(("parallel", "arbitrary")),
        name=name,
    )(sinks, h, h, h, h, h)


def _swa_sample_kernel(sink_ref, q_ref, kc_ref, vc_ref, kp_ref, vp_ref, o_ref, *, nb, lq):
    rows = SWA_HEADS * lq
    grp_rows = SWA_GROUP * lq
    hd = SWA_HEAD_DIM
    nt = (((1,), (1,)), ((), ()))
    head_of_row = lax.broadcasted_iota(I32, (rows, 1), 0) // lq
    slope = jnp.zeros((rows, 1), F32)
    sink = jnp.zeros((rows, 1), F32)
    for hh in range(SWA_HEADS):
        slope = jnp.where(head_of_row == hh, _alibi_slope(hh), slope)
        sink = jnp.where(head_of_row == hh, sink_ref[hh], sink)
    qi = lax.broadcasted_iota(I32, (rows, WINDOW), 0) % lq
    dist_p = qi + WINDOW - lax.broadcasted_iota(I32, (rows, WINDOW), 1)
    valid_p = dist_p < WINDOW
    bias_p = slope * dist_p.astype(F32)
    dist_c = lax.broadcasted_iota(I32, (rows, lq), 0) % lq - lax.broadcasted_iota(I32, (rows, lq), 1)
    valid_c = dist_c >= 0
    bias_c = slope * dist_c.astype(F32)
    scale = hd ** -0.5

    q_all = q_ref[...].astype(F32)
    kc_all = kc_ref[...].astype(F32)
    vc_all = vc_ref[...].astype(F32)
    seq_outs = []
    for j in range(nb):
        js = slice(j * lq, (j + 1) * lq)
        pieces = []
        for hh in range(SWA_HEADS):
            g = hh // SWA_GROUP
            parts = []
            if g:
                parts.append(jnp.zeros((lq, g * hd), F32))
            parts.append(q_all[js, hh * hd:(hh + 1) * hd])
            if g < SWA_KV_HEADS - 1:
                parts.append(jnp.zeros((lq, (SWA_KV_HEADS - 1 - g) * hd), F32))
            pieces.append(jnp.concatenate(parts, axis=-1))
        q_big = jnp.concatenate(pieces, axis=0).astype(BF16)
        kp = kp_ref[j].astype(BF16)
        vp = vp_ref[j].astype(BF16)
        kcj = kc_all[js, :].astype(BF16)
        vcj = vc_all[js, :].astype(BF16)
        s_p = lax.dot_general(q_big, kp, nt, preferred_element_type=F32)
        s_p = jnp.where(valid_p, s_p * scale - bias_p, -jnp.inf)
        s_c = lax.dot_general(q_big, kcj, nt, preferred_element_type=F32)
        s_c = jnp.where(valid_c, s_c * scale - bias_c, -jnp.inf)
        o_big = _swa_softmax_pv([(s_p, vp), (s_c, vcj)], sink)
        o_grp = [o_big[g * grp_rows:(g + 1) * grp_rows, g * hd:(g + 1) * hd] for g in range(SWA_KV_HEADS)]
        o_heads = jnp.concatenate(o_grp, axis=0)
        seq_outs.append(jnp.concatenate([o_heads[hh * lq:(hh + 1) * lq, :] for hh in range(SWA_HEADS)],
                                        axis=-1))
    o_ref[...] = jnp.concatenate(seq_outs, axis=0).astype(o_ref.dtype)


def _swa_sample(h, sinks, k_past, v_past, *, row0, n_seq, lq, nb, name):
    rows = nb * lq
    rb0 = row0 // rows
    qcol = H_QB // SWA_Q_DIM
    kcol = H_KB // SWA_KV_DIM
    vcol = H_VB // SWA_KV_DIM
    kern = functools.partial(_swa_sample_kernel, nb=nb, lq=lq)
    past_spec = pl.BlockSpec((nb, WINDOW, SWA_KV_DIM), lambda g, sk: (g, 0, 0))
    return pl.pallas_call(
        kern,
        grid_spec=pltpu.PrefetchScalarGridSpec(
            num_scalar_prefetch=1,
            grid=(n_seq // nb,),
            in_specs=[pl.BlockSpec((rows, SWA_Q_DIM), lambda g, sk: (rb0 + g, qcol)),
                      pl.BlockSpec((rows, SWA_KV_DIM), lambda g, sk: (rb0 + g, kcol)),
                      pl.BlockSpec((rows, SWA_KV_DIM), lambda g, sk: (rb0 + g, vcol)),
                      past_spec, past_spec],
            out_specs=pl.BlockSpec((rows, SWA_Q_DIM), lambda g, sk: (g, 0)),
        ),
        out_shape=jax.ShapeDtypeStruct((n_seq * lq, SWA_Q_DIM), BF16),
        compiler_params=_cparams(("parallel",)),
        name=name,
    )(sinks, h, h, h, k_past, v_past)


def _layer_norm(x, g, b):
    mu = jnp.mean(x, axis=-1, keepdims=True)
    xc = x - mu
    var = jnp.mean(jnp.square(xc), axis=-1, keepdims=True)
    return xc * lax.rsqrt(var + EPS) * g + b


def _pack_bf16_pairs(x, s):
    lo = lax.bitcast_convert_type(x[:, (2 * s) * LANES:(2 * s + 1) * LANES].astype(BF16).astype(F32), I32)
    hi = lax.bitcast_convert_type(x[:, (2 * s + 1) * LANES:(2 * s + 2) * LANES].astype(BF16).astype(F32), I32)
    return lax.shift_right_logical(lo, 16) | (hi & HIGH_HALF)


def _unpack_bf16_pairs(w):
    lo = lax.bitcast_convert_type(lax.shift_left(w, 16), F32).astype(BF16)
    hi = lax.bitcast_convert_type(w & HIGH_HALF, F32).astype(BF16)
    return lo, hi


def _merge_kernel(oap_ref, oas_ref, obp_ref, obs_ref, ga_ref, gb_ref, xp_ref, xs_ref,
                  wpa_ref, wpb_ref, wout_ref, g_ref, b_ref, o_ref, xw_ref, *, np_tiles):
    i = pl.program_id(0)
    tm = o_ref.shape[0]
    br_a = jnp.dot(_pair_value(i, np_tiles, oap_ref, oas_ref), wpa_ref[...], preferred_element_type=F32)
    br_b = jnp.dot(_pair_value(i, np_tiles, obp_ref, obs_ref), wpb_ref[...], preferred_element_type=F32)
    merged = (jax.nn.sigmoid(ga_ref[...].astype(F32)) * br_a
              + jax.nn.sigmoid(gb_ref[...].astype(F32)) * br_b)
    mix = jnp.dot(merged.astype(BF16), wout_ref[...], preferred_element_type=F32)
    x = _pair_value(i, np_tiles, xp_ref, xs_ref)
    x1 = _layer_norm(DEEPNORM_ALPHA * x + mix, g_ref[...], b_ref[...])
    o_ref[...] = x1
    for s in range(PACKED_ROWS):
        xw_ref[pl.ds(s, tm, stride=PACKED_ROWS), :] = _pack_bf16_pairs(x1, s)


def _merge(oa, ob, h, x, wpa, wpb, wout, g, b, *, tm, name):
    t = h.shape[0]
    np_tiles = x[0].shape[0] // tm
    p_map, s_map = _pair_maps(np_tiles)
    row = lambda i: (i, 0)
    full = lambda i: (0, 0)
    pair = [pl.BlockSpec((tm, D_MODEL), p_map), pl.BlockSpec((tm, D_MODEL), s_map)]
    return pl.pallas_call(
        functools.partial(_merge_kernel, np_tiles=np_tiles),
        grid=(t // tm,),
        in_specs=pair + pair + [
                  pl.BlockSpec((tm, D_MODEL), lambda i: (i, H_GATE_A // D_MODEL)),
                  pl.BlockSpec((tm, D_MODEL), lambda i: (i, H_GATE_B // D_MODEL))] + pair + [
                  pl.BlockSpec((D_MODEL, D_MODEL), full),
                  pl.BlockSpec((D_MODEL, D_MODEL), full),
                  pl.BlockSpec((D_MODEL, D_MODEL), full),
                  pl.BlockSpec((1, D_MODEL), full),
                  pl.BlockSpec((1, D_MODEL), full)],
        out_specs=[pl.BlockSpec((tm, D_MODEL), row),
                   pl.BlockSpec((tm * PACKED_ROWS, LANES), row)],
        out_shape=[jax.ShapeDtypeStruct((t, D_MODEL), F32),
                   jax.ShapeDtypeStruct((t * PACKED_ROWS, LANES), I32)],
        compiler_params=_cparams(("parallel",)),
        name=name,
    )(*oa, *ob, h, h, *x, wpa, wpb, wout, g, b)


def _router_kernel(x_ref, whi_ref, wlo_ref, bias_ref, idx_ref, wl_ref, rank_ref, cntc_ref, cntr_ref,
                   cntc_scr, cntr_scr):
    i = pl.program_id(0)
    tm = x_ref.shape[0]

    @pl.when(i == 0)
    def _():
        cntc_scr[...] = jnp.zeros_like(cntc_scr)
        cntr_scr[...] = jnp.zeros_like(cntr_scr)

    x_hi, x_lo = _split_bf16(x_ref[...])
    nt = (((1,), (1,)), ((), ()))
    logits = (lax.dot_general(whi_ref[...], x_hi, nt, preferred_element_type=F32)
              + lax.dot_general(whi_ref[...], x_lo, nt, preferred_element_type=F32)
              + lax.dot_general(wlo_ref[...], x_hi, nt, preferred_element_type=F32))
    scores = jax.nn.sigmoid(logits)
    biased = scores + bias_ref[...]

    grouped = biased.reshape(N_GROUPS, GROUP_SIZE, tm)
    m1 = jnp.max(grouped, axis=1)
    n_top = jnp.sum(jnp.where(grouped == m1[:, None, :], 1.0, 0.0), axis=1)
    m2 = jnp.max(jnp.where(grouped < m1[:, None, :], grouped, -jnp.inf), axis=1)
    gscore = m1 + jnp.where(n_top >= 2.0, m1, m2)

    g_iota = lax.broadcasted_iota(I32, (N_GROUPS, tm), 0)
    beaten = jnp.zeros((N_GROUPS, tm), I32)
    for g in range(N_GROUPS):
        other = gscore[g:g + 1, :]
        ahead = (other > gscore) | ((other == gscore) & (g < g_iota))
        beaten = beaten + jnp.where(ahead, 1, 0)
    keep = jnp.where(beaten < TOPK_GROUPS, 1.0, 0.0)
    masked = jnp.where(keep[:, None, :] > 0.5, grouped, -jnp.inf).reshape(N_EXPERTS, tm)

    e_iota = lax.broadcasted_iota(I32, (N_EXPERTS, tm), 0)
    sel_f = jnp.zeros((N_EXPERTS, tm), F32)
    ids = []
    for _ in range(TOP_K):
        best = jnp.max(masked, axis=0, keepdims=True)
        idx = jnp.min(jnp.where(masked == best, e_iota, N_EXPERTS), axis=0, keepdims=True)
        hit = e_iota == idx
        sel_f = sel_f + jnp.where(hit, 1.0, 0.0)
        masked = jnp.where(hit, -jnp.inf, masked)
        ids.append(idx)

    top_sum = jnp.sum(sel_f * scores, axis=0, keepdims=True)

    t_r = lax.broadcasted_iota(I32, (tm, tm), 0)
    t_c = lax.broadcasted_iota(I32, (tm, tm), 1)
    before = jnp.where(t_r < t_c, 1.0, 0.0).astype(BF16)
    sel_b = sel_f.astype(BF16)
    rank = jnp.dot(sel_b, before, preferred_element_type=F32) + cntc_scr[...]
    cntc_scr[...] = cntc_scr[...] + jnp.sum(sel_f, axis=1, keepdims=True)
    cntr_scr[...] = cntr_scr[...] + lax.dot_general(jnp.ones((SUBLANES, tm), BF16), sel_b, nt,
                                                    preferred_element_type=F32)

    idx_rows, w_rows, rank_rows = [], [], []
    for idx in ids:
        hit = e_iota == idx
        w = jnp.sum(jnp.where(hit, scores, 0.0), axis=0, keepdims=True)
        w_rows.append(w / top_sum * ROUTED_SCALE)
        rank_rows.append(jnp.sum(jnp.where(hit, rank, 0.0), axis=0, keepdims=True))
        idx_rows.append(idx)
    idx_ref[...] = jnp.concatenate(idx_rows, axis=0)
    rank_ref[...] = jnp.concatenate(rank_rows, axis=0).astype(I32)

    w_all = jnp.concatenate(w_rows, axis=0)
    w_1 = w_all.astype(BF16)
    r_1 = w_all - w_1.astype(F32)
    w_2 = r_1.astype(BF16)
    w_3 = (r_1 - w_2.astype(F32)).astype(BF16)
    terms = jnp.concatenate([w_1, w_2, w_3, jnp.zeros_like(w_1)], axis=0)
    t_row = lax.broadcasted_iota(I32, (4 * TOP_K, TOP_K * LANES), 0)
    t_col = lax.broadcasted_iota(I32, (4 * TOP_K, TOP_K * LANES), 1)
    spread = jnp.where((t_row % TOP_K) == (t_col // LANES), 1.0, 0.0).astype(BF16)
    w_lanes = lax.dot_general(terms, spread, (((0,), (0,)), ((), ())), preferred_element_type=F32)
    for k in range(TOP_K):
        wl_ref[pl.ds(k, tm, stride=TOP_K), :] = w_lanes[:, k * LANES:(k + 1) * LANES]

    @pl.when(i == pl.num_programs(0) - 1)
    def _():
        cntc_ref[...] = cntc_scr[...].astype(I32)
        cntr_ref[...] = cntr_scr[...].astype(I32)


def _router(x1, w_hi, w_lo, bias, *, tm, name):
    t = x1.shape[0]
    full = lambda i: (0, 0)
    tok = lambda i: (0, i)
    return pl.pallas_call(
        _router_kernel,
        grid=(t // tm,),
        in_specs=[pl.BlockSpec((tm, D_MODEL), lambda i: (i, 0)),
                  pl.BlockSpec((N_EXPERTS, D_MODEL), full),
                  pl.BlockSpec((N_EXPERTS, D_MODEL), full),
                  pl.BlockSpec((N_EXPERTS, 1), full)],
        out_specs=[pl.BlockSpec((TOP_K, tm), tok),
                   pl.BlockSpec((TOP_K * tm, LANES), lambda i: (i, 0)),
                   pl.BlockSpec((TOP_K, tm), tok),
                   pl.BlockSpec((N_EXPERTS, 1), full),
                   pl.BlockSpec((SUBLANES, N_EXPERTS), full)],
        out_shape=[jax.ShapeDtypeStruct((TOP_K, t), I32),
                   jax.ShapeDtypeStruct((TOP_K * t, LANES), F32),
                   jax.ShapeDtypeStruct((TOP_K, t), I32),
                   jax.ShapeDtypeStruct((N_EXPERTS, 1), I32),
                   jax.ShapeDtypeStruct((SUBLANES, N_EXPERTS), I32)],
        scratch_shapes=[pltpu.VMEM((N_EXPERTS, 1), F32),
                        pltpu.VMEM((SUBLANES, N_EXPERTS), F32)],
        compiler_params=_cparams(("arbitrary",)),
        name=name,
    )(x1, w_hi, w_lo, bias)


ITEM_FIELDS = 8


def _byte_split(v):
    return lax.shift_right_logical(v, 8).astype(F32), (v & 255).astype(F32)


def _plan_kernel(cntc_ref, cntr_ref, idx_ref, rank_ref, meta_ref, items_ref, m_scr, *, n_tok):
    i = pl.program_id(0)
    tm = idx_ref.shape[1]
    nbp = m_scr.shape[0]
    nip = items_ref.shape[1]
    n_rows = n_tok * TOP_K
    n_blk = n_rows // EXPERT_ROWS
    nt = (((1,), (1,)), ((), ()))

    r_i = lax.broadcasted_iota(I32, (N_EXPERTS, N_EXPERTS), 0)
    c_i = lax.broadcasted_iota(I32, (N_EXPERTS, N_EXPERTS), 1)
    below = jnp.where(c_i < r_i, 1.0, 0.0).astype(BF16)
    ones_c = jnp.ones((N_EXPERTS, LANES), F32)
    c_hi, c_lo = _byte_split(cntc_ref[...])
    start_col = (256.0 * jnp.dot(below, (c_hi * ones_c).astype(BF16), preferred_element_type=F32)
                 + jnp.dot(below, (c_lo * ones_c).astype(BF16), preferred_element_type=F32))[:, 0:1]

    @pl.when(i == 0)
    def _():
        m_scr[...] = jnp.zeros_like(m_scr)
        above = jnp.where(r_i < c_i, 1.0, 0.0).astype(BF16)
        r_hi, r_lo = _byte_split(cntr_ref[...])
        start_row = (256.0 * jnp.dot(r_hi.astype(BF16), above, preferred_element_type=F32)
                     + jnp.dot(r_lo.astype(BF16), above, preferred_element_type=F32))[0:1, :]
        b_col = lax.broadcasted_iota(I32, (nbp, 1), 0)
        b_row = lax.broadcasted_iota(I32, (1, nbp), 1)
        blk_col = jnp.where(b_col < n_blk, b_col * EXPERT_ROWS, n_rows).astype(F32)
        blk_row = jnp.where(b_row < n_blk, b_row * EXPERT_ROWS, n_rows).astype(F32)
        v_col = jnp.concatenate([blk_col, start_col], axis=0)
        v_row = jnp.concatenate([blk_row, start_row], axis=1)
        j_col = lax.broadcasted_iota(I32, (nip, 1), 0)
        k_row = lax.broadcasted_iota(I32, (1, nip), 1)
        ahead = (v_row < v_col) | ((v_row == v_col) & (k_row < j_col))
        order_col = jnp.sum(jnp.where(ahead, 1.0, 0.0), axis=1, keepdims=True)
        pos_row = k_row.astype(F32)
        lo_abs = jnp.sum(jnp.where(order_col == pos_row, v_col, 0.0), axis=0, keepdims=True)
        hi_abs = jnp.sum(jnp.where(order_col == pos_row + 1.0, v_col, 0.0), axis=0, keepdims=True)
        hi_abs = jnp.where(k_row == nip - 1, float(n_rows), hi_abs)
        blk = jnp.minimum(jnp.floor(lo_abs * (1.0 / EXPERT_ROWS)), n_blk - 1.0)
        expert = jnp.sum(jnp.where(start_col <= lo_abs, 1.0, 0.0), axis=0, keepdims=True) - 1.0
        base = blk * EXPERT_ROWS
        e_col = lax.broadcasted_iota(I32, (N_EXPERTS, 1), 0).astype(F32)
        end_col = start_col + cntc_ref[...].astype(F32)
        seg_end = jnp.sum(jnp.where(e_col == expert, end_col, 0.0), axis=0, keepdims=True)
        follower = jnp.sum(jnp.where(start_col <= seg_end, 1.0, 0.0), axis=0, keepdims=True) - 1.0
        follower = jnp.where(seg_end < float(n_rows), follower, -1.0)
        fields = [blk, expert, lo_abs - base, hi_abs - base, follower]
        fields.append(jnp.zeros((ITEM_FIELDS - len(fields), nip), F32))
        items_ref[...] = jnp.concatenate(fields, axis=0).astype(I32)

    e_iota = lax.broadcasted_iota(I32, (N_EXPERTS, tm), 0)
    b_iota = lax.broadcasted_iota(I32, (nbp, tm), 0)
    l_iota = lax.broadcasted_iota(I32, (EXPERT_ROWS, tm), 0)
    tok = i * tm + lax.broadcasted_iota(I32, (1, tm), 1)
    tok_hi, tok_lo = _byte_split(tok)
    acc = jnp.zeros(m_scr.shape, F32)
    for k in range(TOP_K):
        hit = e_iota == idx_ref[k:k + 1, :]
        dest = (jnp.sum(jnp.where(hit, start_col, 0.0), axis=0, keepdims=True).astype(I32)
                + rank_ref[k:k + 1, :])
        oh_blk = jnp.where(b_iota == lax.shift_right_logical(dest, 8), 1.0, 0.0).astype(BF16)
        in_blk = l_iota == (dest & (EXPERT_ROWS - 1))
        vals = jnp.concatenate([jnp.where(in_blk, tok_hi, 0.0), jnp.where(in_blk, tok_lo, 0.0),
                                jnp.where(in_blk, float(k), 0.0)], axis=0).astype(BF16)
        acc = acc + lax.dot_general(oh_blk, vals, nt, preferred_element_type=F32)
    m_scr[...] = m_scr[...] + acc

    @pl.when(i == pl.num_programs(0) - 1)
    def _():
        m = m_scr[...]
        row_tok = m[:, 0:EXPERT_ROWS] * 256.0 + m[:, EXPERT_ROWS:2 * EXPERT_ROWS]
        row_slot = m[:, 2 * EXPERT_ROWS:3 * EXPERT_ROWS] * float(n_tok) + row_tok
        meta_ref[...] = jnp.concatenate([row_tok * float(PACKED_ROWS), row_slot * float(ROW_CHUNKS)],
                                        axis=1).astype(I32)


def _plan(cnt_col, cnt_row, idx, rank, *, tm, name):
    t = idx.shape[1]
    assert EXPERT_ROWS == 256 and (t * TOP_K) % (2 * EXPERT_ROWS) == 0
    n_blk = t * TOP_K // EXPERT_ROWS
    nbp = -(-n_blk // LANES) * LANES
    nip = nbp + N_EXPERTS
    tok = lambda i: (0, i)
    full = lambda i: (0, 0)
    return pl.pallas_call(
        functools.partial(_plan_kernel, n_tok=t),
        grid=(t // tm,),
        in_specs=[pl.BlockSpec((N_EXPERTS, 1), full),
                  pl.BlockSpec((SUBLANES, N_EXPERTS), full),
                  pl.BlockSpec((TOP_K, tm), tok),
                  pl.BlockSpec((TOP_K, tm), tok)],
        out_specs=[pl.BlockSpec((nbp, 2 * EXPERT_ROWS), full),
                   pl.BlockSpec((ITEM_FIELDS, nip), full)],
        out_shape=[jax.ShapeDtypeStruct((nbp, 2 * EXPERT_ROWS), I32),
                   jax.ShapeDtypeStruct((ITEM_FIELDS, nip), I32)],
        scratch_shapes=[pltpu.VMEM((nbp, 3 * EXPERT_ROWS), F32)],
        compiler_params=_cparams(("arbitrary",)),
        name=name,
    )(cnt_col, cnt_row, idx, rank)


META_CHUNK = 4 * EXPERT_ROWS
GATHER_STRIDE = EXPERT_ROWS + 8


def _experts_kernel(items_ref, xw_hbm, meta_hbm, wg_hbm, wu_hbm, wd_hbm, ys_hbm,
                    xw, meta, tile, lhs, wg_f, wu_f, wd_f, wg_b, wu_b, wd_b, yacc, ybuf, state,
                    sem_x, sem_m, sem_y, sem_w, *, nip, n_blk):
    i = pl.program_id(0)
    blk = items_ref[i]
    expert = items_ref[nip + i]
    lo = items_ref[2 * nip + i]
    hi = items_ref[3 * nip + i]
    follower = items_ref[4 * nip + i]
    nonempty = hi > lo
    par = blk & 1
    chunk = lax.shift_right_logical(blk, 1)
    cpar = chunk & 1
    mbase = cpar * META_CHUNK + par * (2 * EXPERT_ROWS)

    def meta_copy(c, slot):
        return pltpu.make_async_copy(meta_hbm.at[pl.ds(pl.multiple_of(c * META_CHUNK, META_CHUNK), META_CHUNK)],
                                     meta.at[pl.ds(pl.multiple_of(slot * META_CHUNK, META_CHUNK), META_CHUNK)],
                                     sem_m.at[slot])

    def ybuf_drain(slot):
        pltpu.make_async_copy(ys_hbm.at[pl.ds(0, EXPERT_ROWS * ROW_CHUNKS), :], ybuf.at[slot],
                              sem_y.at[slot]).wait()

    def ybuf_fill_and_send(slot):
        for j in range(ROW_CHUNKS):
            ybuf[slot, pl.ds(j, EXPERT_ROWS, stride=ROW_CHUNKS), :] = yacc[:, j * LANES:(j + 1) * LANES]

        for r in range(EXPERT_ROWS):
            src = ybuf.at[slot, pl.ds(r * ROW_CHUNKS, ROW_CHUNKS), :]
            dst_row = pl.multiple_of(meta[mbase + EXPERT_ROWS + r], ROW_CHUNKS)
            pltpu.make_async_copy(src, ys_hbm.at[pl.ds(dst_row, ROW_CHUNKS), :],
                                  sem_y.at[slot]).start(priority=r % 2)

    def weight_copies(e, slot):
        return [pltpu.make_async_copy(src.at[e], dst.at[slot], sem_w.at[slot])
                for src, dst in ((wg_hbm, wg_f), (wu_hbm, wu_f), (wd_hbm, wd_f))]

    @pl.when(i == 0)
    def _():
        resident = pltpu.make_async_copy(xw_hbm, xw, sem_x)
        resident.start()
        meta_copy(0, 0).start()
        for cp in weight_copies(expert, 0):
            cp.start()
        state[0] = -1
        state[1] = 1
        resident.wait()

    @pl.when(nonempty & (expert != state[0]))
    def _():
        slot = 1 - state[1]
        for cp in weight_copies(expert, slot):
            cp.wait()
        state[0] = expert
        state[1] = slot

        @pl.when(follower >= 0)
        def _():
            for cp in weight_copies(follower, 1 - slot):
                cp.start()

        wg_b[...] = wg_f[slot].astype(BF16)
        wu_b[...] = wu_f[slot].astype(BF16)
        wd_b[...] = wd_f[slot].astype(BF16)

    @pl.when(nonempty & (lo == 0))
    def _():
        @pl.when(par == 0)
        def _():
            meta_copy(chunk, cpar).wait()

            @pl.when(2 * (chunk + 1) < n_blk)
            def _():
                meta_copy(chunk + 1, 1 - cpar).start()

        for r in range(EXPERT_ROWS):
            t4 = pl.multiple_of(meta[mbase + r], PACKED_ROWS)
            tile[pl.ds(r, PACKED_ROWS, stride=GATHER_STRIDE), :] = xw[pl.ds(t4, PACKED_ROWS), :]
        cols = []
        for s in range(PACKED_ROWS):
            cols.extend(_unpack_bf16_pairs(tile[pl.ds(s * GATHER_STRIDE, EXPERT_ROWS), :]))
        lhs[...] = jnp.concatenate(cols, axis=-1)

    def ffn(x):
        gate = jnp.dot(x, wg_b[...], preferred_element_type=F32)
        up = jnp.dot(x, wu_b[...], preferred_element_type=F32)
        hid = (jax.nn.silu(gate) * up).astype(BF16)
        return jnp.dot(hid, wd_b[...], preferred_element_type=F32)

    whole = (lo == 0) & (hi == EXPERT_ROWS)

    @pl.when(whole)
    def _():
        yacc[...] = ffn(lhs[...])

    @pl.when(nonempty & jnp.logical_not(whole))
    def _():
        row = lax.broadcasted_iota(I32, (EXPERT_ROWS, 1), 0)
        y = jnp.where((row >= lo) & (row < hi), ffn(lhs[...]), 0.0)

        @pl.when(lo == 0)
        def _():
            yacc[...] = y

        @pl.when(lo > 0)
        def _():
            yacc[...] = yacc[...] + y

    @pl.when(nonempty & (hi == EXPERT_ROWS))
    def _():
        @pl.when(blk >= 2)
        def _():
            ybuf_drain(par)

        for slot in range(2):
            @pl.when(par == slot)
            def _():
                ybuf_fill_and_send(slot)

    @pl.when(i == pl.num_programs(0) - 1)
    def _():
        ybuf_drain(0)
        ybuf_drain(1)


def _experts(items, xw, meta, wg, wu, wd, *, n_tok, name):
    nip = items.shape[0] // ITEM_FIELDS
    n_rows = n_tok * TOP_K
    n_blk = n_rows // EXPERT_ROWS
    n_items = n_blk + N_EXPERTS
    kern = functools.partial(_experts_kernel, nip=nip, n_blk=n_blk)
    any_spec = pl.BlockSpec(memory_space=pl.ANY)
    return pl.pallas_call(
        kern,
        grid_spec=pltpu.PrefetchScalarGridSpec(
            num_scalar_prefetch=1,
            grid=(n_items,),
            in_specs=[any_spec] * 5,
            out_specs=any_spec,
            scratch_shapes=[pltpu.VMEM(xw.shape, I32),
                            pltpu.SMEM((2 * META_CHUNK,), I32),
                            pltpu.VMEM((PACKED_ROWS * GATHER_STRIDE, LANES), I32),
                            pltpu.VMEM((EXPERT_ROWS, D_MODEL), BF16),
                            pltpu.VMEM((2, D_MODEL, EXPERT_FF), F32),
                            pltpu.VMEM((2, D_MODEL, EXPERT_FF), F32),
                            pltpu.VMEM((2, EXPERT_FF, D_MODEL), F32),
                            pltpu.VMEM((D_MODEL, EXPERT_FF), BF16),
                            pltpu.VMEM((D_MODEL, EXPERT_FF), BF16),
                            pltpu.VMEM((EXPERT_FF, D_MODEL), BF16),
                            pltpu.VMEM((EXPERT_ROWS, D_MODEL), F32),
                            pltpu.VMEM((2, EXPERT_ROWS * ROW_CHUNKS, LANES), F32),
                            pltpu.SMEM((2,), I32),
                            pltpu.SemaphoreType.DMA,
                            pltpu.SemaphoreType.DMA((2,)),
                            pltpu.SemaphoreType.DMA((2,)),
                            pltpu.SemaphoreType.DMA((2,))],
        ),
        out_shape=jax.ShapeDtypeStruct((n_rows * ROW_CHUNKS, LANES), F32),
        compiler_params=_cparams(("arbitrary",)),
        name=name,
    )(items, xw, meta, wg, wu, wd)


def _combine_kernel(ys_ref, x_ref, wl_ref, wsg_ref, wsu_ref, wsd_ref, g_ref, b_ref, op_ref, os_ref,
                    routed, *, np_tiles):
    x = x_ref[...]
    xb = x.astype(BF16)
    hid = (jax.nn.silu(jnp.dot(xb, wsg_ref[...], preferred_element_type=F32))
           * jnp.dot(xb, wsu_ref[...], preferred_element_type=F32))
    shared = jnp.dot(hid.astype(BF16), wsd_ref[...], preferred_element_type=F32)
    acc = None
    for k in range(TOP_K):
        part = wl_ref[:, k:k + 1, :] * ys_ref[k]
        acc = part if acc is None else acc + part
    routed[...] = acc
    ffn = jnp.concatenate([routed[:, j, :] for j in range(ROW_CHUNKS)], axis=-1) + shared
    out = _layer_norm(DEEPNORM_ALPHA * x + ffn, g_ref[...], b_ref[...])
    i = pl.program_id(0)

    @pl.when(i < np_tiles)
    def _():
        op_ref[...] = out

    @pl.when(i >= np_tiles)
    def _():
        os_ref[...] = out


def _combine(ys, x1, w_tok, wsg, wsu, wsd, g, b, *, n_prompt, tm, name):
    t = x1.shape[0]
    np_tiles = n_prompt // tm
    p_map, s_map = _pair_maps(np_tiles)
    row = lambda i: (i, 0)
    full = lambda i: (0, 0)
    return pl.pallas_call(
        functools.partial(_combine_kernel, np_tiles=np_tiles),
        grid=(t // tm,),
        in_specs=[pl.BlockSpec((TOP_K, tm, ROW_CHUNKS, LANES), lambda i: (0, i, 0, 0)),
                  pl.BlockSpec((tm, D_MODEL), row),
                  pl.BlockSpec((tm, TOP_K, LANES), lambda i: (i, 0, 0)),
                  pl.BlockSpec((D_MODEL, SHARED_FF), full),
                  pl.BlockSpec((D_MODEL, SHARED_FF), full),
                  pl.BlockSpec((SHARED_FF, D_MODEL), full),
                  pl.BlockSpec((1, D_MODEL), full),
                  pl.BlockSpec((1, D_MODEL), full)],
        out_specs=[pl.BlockSpec((tm, D_MODEL), p_map), pl.BlockSpec((tm, D_MODEL), s_map)],
        out_shape=[jax.ShapeDtypeStruct((n_prompt, D_MODEL), F32),
                   jax.ShapeDtypeStruct((t - n_prompt, D_MODEL), F32)],
        scratch_shapes=[pltpu.VMEM((tm, ROW_CHUNKS, LANES), F32)],
        compiler_params=_cparams(("arbitrary",)),
        name=name,
    )(ys, x1, w_tok, wsg, wsu, wsd, g, b)


def _pack_w_in(w_in):
    sizes = (GLA_KEY_DIM, GLA_KEY_DIM, GLA_VAL_DIM, GLA_VAL_DIM, GLA_GATE_RANK,
             SWA_Q_DIM, SWA_KV_DIM, SWA_KV_DIM, D_MODEL, D_MODEL)
    offs = [0]
    for s in sizes:
        offs.append(offs[-1] + s)
    qa, ka, va, ga, gk, qb, kb, vb, gate_a, gate_b = (w_in[:, offs[i]:offs[i + 1]] for i in range(10))
    pad = lambda w, n: jnp.pad(w, ((0, 0), (0, n - w.shape[1])))
    packed = jnp.concatenate([qa, ka, va, ga, qb, gate_a, gate_b, kb, vb, pad(gk, 2 * LANES)], axis=1)
    assert packed.shape[1] == H_WIDTH
    return packed.astype(BF16), jnp.concatenate([kb, vb], axis=1).astype(BF16)


def kernel(x_prompt, x_sample, state_gla, cache_swa_k, cache_swa_v, w_in, w_gk_up, b_gk, gla_norm_g,
           attn_sinks, w_proj_a, w_proj_b, w_out, ln1_g, ln1_b, w_router, router_bias,
           w_expert_gate, w_expert_up, w_expert_down, w_shared_gate, w_shared_up, w_shared_down,
           ln2_g, ln2_b):
    assert w_in.shape[0] == 1, "single-layer trunk"
    bp, lp, d = x_prompt.shape
    bs, ls, _ = x_sample.shape
    assert d == D_MODEL and ls == SUBLANES and cache_swa_k.shape[2] == WINDOW
    tp, ts = bp * lp, bs * ls
    t = tp + ts

    xp = x_prompt.reshape(tp, d)
    xs = x_sample.reshape(ts, d)
    w_main, w_kv = _pack_w_in(w_in[0])
    h = _matmul(xp, xs, w_main, BF16, _pair_tile(tp, ts, 1024), H_TN, "proj_in")

    xp_tail = x_prompt[:, lp - WINDOW:].reshape(bp * WINDOW, d)
    kv_tail = _matmul(xp_tail, xs, w_kv, F32, _pair_tile(bp * WINDOW, ts, 512), 2 * SWA_KV_DIM,
                      "proj_kv_tail")

    wup = jnp.pad(w_gk_up[0], ((0, LANES - GLA_GATE_RANK), (0, 0))).astype(BF16)
    bgk = b_gk[0].reshape(1, GLA_KEY_DIM)
    gn = gla_norm_g[0].reshape(1, GLA_DV)
    oa_p, s_prompt = _gla(h, wup, bgk, gn, None, row0=0, n_seq=bp, seq_len=lp, par=math.gcd(bp, 4), nb=1,
                          c=GLA_CHUNK, sub=2, name="gla_prompt")
    oa_s, s_sample = _gla(h, wup, bgk, gn, state_gla[0], row0=tp, n_seq=bs, seq_len=ls, par=1, nb=8,
                          c=math.gcd(ls, GLA_CHUNK), sub=1, name="gla_sample")
    oa_p = oa_p.reshape(tp, GLA_VAL_DIM)
    oa_s = oa_s.reshape(ts, GLA_VAL_DIM)

    sinks = attn_sinks[0]
    k_past = cache_swa_k[0].reshape(bs, WINDOW, SWA_KV_DIM)
    v_past = cache_swa_v[0].reshape(bs, WINDOW, SWA_KV_DIM)
    ob_p = _swa_prompt(h, sinks, n_seq=bp, seq_len=lp, name="swa_prompt")
    ob_s = _swa_sample(h, sinks, k_past, v_past, row0=tp, n_seq=bs, lq=ls, nb=8, name="swa_sample")

    x1, xw = _merge((oa_p, oa_s), (ob_p, ob_s), h, (xp, xs), w_proj_a[0].astype(BF16),
                    w_proj_b[0].astype(BF16), w_out[0].astype(BF16), ln1_g[0].reshape(1, d),
                    ln1_b[0].reshape(1, d), tm=_pair_tile(tp, ts, 512), name="merge_ln1")

    wr_t = w_router[0].T
    wr_hi = wr_t.astype(BF16)
    wr_lo = (wr_t - wr_hi.astype(F32)).astype(BF16)
    tm_r = _tile(t, 512)
    idx, w_lanes, rank, cnt_col, cnt_row = _router(x1, wr_hi, wr_lo, router_bias[0].reshape(N_EXPERTS, 1),
                                                   tm=tm_r, name="router")
    meta, items = _plan(cnt_col, cnt_row, idx, rank, tm=tm_r, name="plan")
    ys = _experts(items.reshape(-1), xw, meta.reshape(-1), w_expert_gate[0], w_expert_up[0],
                  w_expert_down[0], n_tok=t, name="experts")
    y_p, y_s = _combine(ys.reshape(TOP_K, t, ROW_CHUNKS, LANES), x1, w_lanes.reshape(t, TOP_K, LANES),
                        w_shared_gate[0].astype(BF16), w_shared_up[0].astype(BF16),
                        w_shared_down[0].astype(BF16), ln2_g[0].reshape(1, d), ln2_b[0].reshape(1, d),
                        n_prompt=tp, tm=_pair_tile(tp, ts, 128), name="combine_ln2")

    y_prompt = y_p.reshape(bp, lp, d)
    y_sample = y_s.reshape(bs, ls, d)
    k_tail = kv_tail[:, :SWA_KV_DIM]
    v_tail = kv_tail[:, SWA_KV_DIM:]
    kv_shape = (SWA_KV_HEADS, SWA_HEAD_DIM)
    k_prompt = k_tail[:bp * WINDOW].reshape(1, bp, WINDOW, *kv_shape)
    v_prompt = v_tail[:bp * WINDOW].reshape(1, bp, WINDOW, *kv_shape)
    k_new = k_tail[bp * WINDOW:].reshape(bs, ls, *kv_shape)
    v_new = v_tail[bp * WINDOW:].reshape(bs, ls, *kv_shape)
    k_sample = jnp.concatenate([cache_swa_k[0][:, ls:], k_new], axis=1)[None]
    v_sample = jnp.concatenate([cache_swa_v[0][:, ls:], v_new], axis=1)[None]
    return (y_prompt, y_sample, s_prompt[None], s_sample[None], k_prompt, v_prompt, k_sample, v_sample)
```

```python
import functools
import math

import jax
import jax.numpy as jnp
from jax import lax
from jax.experimental import pallas as pl
from jax.experimental.pallas import tpu as pltpu

F32 = jnp.float32
BF16 = jnp.bfloat16
I32 = jnp.int32

D_MODEL = 1024
GLA_HEADS = 4
GLA_DK = 128
GLA_DV = 256
GLA_KEY_DIM = GLA_HEADS * GLA_DK
GLA_VAL_DIM = GLA_HEADS * GLA_DV
GLA_GATE_RANK = 16
GLA_GATE_NORMALIZER = 16.0
GLA_CHUNK = 64
SWA_HEADS = 16
SWA_KV_HEADS = 4
SWA_GROUP = SWA_HEADS // SWA_KV_HEADS
SWA_HEAD_DIM = 64
SWA_Q_DIM = SWA_HEADS * SWA_HEAD_DIM
SWA_KV_DIM = SWA_KV_HEADS * SWA_HEAD_DIM
WINDOW = 128
N_EXPERTS = 256
TOP_K = 8
N_GROUPS = 8
GROUP_SIZE = N_EXPERTS // N_GROUPS
TOPK_GROUPS = 4
EXPERT_FF = 256
SHARED_FF = 256
ROUTED_SCALE = 2.5
DEEPNORM_ALPHA = 2.0 ** 0.25
EPS = 1e-5

LANES = 128
SUBLANES = 8
ROW_CHUNKS = D_MODEL // LANES
VMEM_LIMIT = 56 * 1024 * 1024

H_QA, H_KA, H_VA, H_GA, H_QB, H_GATE_A, H_GATE_B, H_KB, H_VB, H_GK = (
    0, 512, 1024, 2048, 3072, 4096, 5120, 6144, 6400, 6656)
H_WIDTH = 6912
H_TN = 2304

EXPERT_ROWS = 256
PACKED_ROWS = ROW_CHUNKS // 2
HIGH_HALF = -65536


def _cparams(sem, vmem=VMEM_LIMIT):
    return pltpu.CompilerParams(dimension_semantics=sem, vmem_limit_bytes=vmem)


def _tile(n, pref):
    t = min(n, pref)
    while n % t:
        t -= LANES
    assert t > 0 and t % LANES == 0, (n, pref)
    return t


def _pair_tile(n_p, n_s, pref):
    return _tile(math.gcd(n_p, n_s), pref)


def _pair_maps(np_tiles, col=0, extra=0):
    def p_map(i, *_):
        return (jnp.minimum(i, np_tiles - 1), col)

    def s_map(i, *_):
        return (jnp.maximum(i - np_tiles, 0), col)

    return p_map, s_map


def _pair_value(i, np_tiles, p_ref, s_ref):
    return jnp.where(i < np_tiles, p_ref[...], s_ref[...])


def _mm_kernel(xp_ref, xs_ref, w_ref, o_ref, *, np_tiles):
    x = _pair_value(pl.program_id(0), np_tiles, xp_ref, xs_ref)
    o_ref[...] = jnp.dot(x.astype(BF16), w_ref[...], preferred_element_type=F32).astype(o_ref.dtype)


def _matmul(xp, xs, w, out_dtype, tm, tn, name):
    k = xp.shape[1]
    m = xp.shape[0] + xs.shape[0]
    n = w.shape[1]
    np_tiles = xp.shape[0] // tm
    p_map, s_map = _pair_maps(np_tiles)
    return pl.pallas_call(
        functools.partial(_mm_kernel, np_tiles=np_tiles),
        grid=(m // tm, n // tn),
        in_specs=[pl.BlockSpec((tm, k), p_map),
                  pl.BlockSpec((tm, k), s_map),
                  pl.BlockSpec((k, tn), lambda i, j: (0, j))],
        out_specs=pl.BlockSpec((tm, tn), lambda i, j: (i, j)),
        out_shape=jax.ShapeDtypeStruct((m, n), out_dtype),
        compiler_params=_cparams(("parallel", "arbitrary")),
        name=name,
    )(xp, xs, w)


def _split_bf16(x):
    hi = x.astype(BF16)
    lo = (x - hi.astype(F32)).astype(BF16)
    return hi, lo


def _gla_kernel(*refs, par, nb, c, sub, has_s0):
    q_refs, k_refs, v_refs, ga_refs, gk_refs = (refs[n * par:(n + 1) * par] for n in range(5))
    rest = refs[5 * par:]
    n_state = par * GLA_HEADS
    s_scr = [rest[len(rest) - n_state + p * GLA_HEADS:len(rest) - n_state + (p + 1) * GLA_HEADS]
             for p in range(par)]
    rest = rest[:len(rest) - n_state]
    if has_s0:
        wup_ref, bgk_ref, gn_ref, s0_ref, o_ref, sout_ref = rest
    else:
        wup_ref, bgk_ref, gn_ref, o_ref, sout_ref = rest
        s0_ref = None
    ci = pl.program_id(1)
    rows = nb * c

    @pl.when(ci == 0)
    def _():
        for p in range(par):
            for h in range(GLA_HEADS):
                if has_s0:
                    s_scr[p][h][...] = s0_ref[p * nb:(p + 1) * nb, h]
                else:
                    s_scr[p][h][...] = jnp.zeros_like(s_scr[p][h])

    r_i = lax.broadcasted_iota(I32, (rows, rows), 0)
    c_i = lax.broadcasted_iota(I32, (rows, rows), 1)
    same_seq = (r_i // c) == (c_i // c)
    causal = same_seq & (c_i <= r_i)
    tri = jnp.where(causal, 1.0, 0.0).astype(BF16)
    seg = jnp.where(same_seq, 1.0, 0.0).astype(BF16)
    ones_kv = jnp.ones((rows, GLA_DV), BF16)
    seq_of_row = lax.broadcasted_iota(I32, (rows, 1), 0) // c
    tn = (((0,), (0,)), ((), ()))

    def decays(p, rs):
        pre = jnp.dot(gk_refs[p][rs, :], wup_ref[...], preferred_element_type=F32) + bgk_ref[...]
        log_a = jax.nn.log_sigmoid(pre) / GLA_GATE_NORMALIZER
        la_hi, la_lo = _split_bf16(log_a)
        b = (jnp.dot(tri, la_hi, preferred_element_type=F32)
             + jnp.dot(tri, la_lo, preferred_element_type=F32))
        b_last = (jnp.dot(seg, la_hi, preferred_element_type=F32)
                  + jnp.dot(seg, la_lo, preferred_element_type=F32))
        q = q_refs[p][rs, :].astype(F32) * (GLA_DK ** -0.5)
        k = k_refs[p][rs, :].astype(F32)
        q_dec = (q * jnp.exp(b)).astype(BF16)
        k_dec = (k * jnp.exp(-b)).astype(BF16)
        k_rem = k * jnp.exp(b_last - b)
        return q_dec, k_dec, k_rem, la_hi, la_lo

    def head(p, rs, h, q_dec, k_dec, k_rem, la_hi, la_lo):
        ks = slice(h * GLA_DK, (h + 1) * GLA_DK)
        vh = v_refs[p][rs, h * GLA_DV:(h + 1) * GLA_DV]
        att = lax.dot_general(q_dec[:, ks], k_dec[:, ks], (((1,), (1,)), ((), ())),
                              preferred_element_type=F32)
        att = jnp.where(causal, att, 0.0).astype(BF16)
        o_h = jnp.dot(att, vh, preferred_element_type=F32)
        for j in range(nb):
            mine = (seq_of_row == j) if nb > 1 else None
            pick = (lambda a: jnp.where(mine, a, 0.0)) if nb > 1 else (lambda a: a)
            s_old = s_scr[p][h][j]
            o_h = o_h + pick(jnp.dot(q_dec[:, ks], s_old.astype(BF16), preferred_element_type=F32))
            dec = (lax.dot_general(pick(la_hi[:, ks].astype(F32)).astype(BF16), ones_kv, tn,
                                   preferred_element_type=F32)
                   + lax.dot_general(pick(la_lo[:, ks].astype(F32)).astype(BF16), ones_kv, tn,
                                     preferred_element_type=F32))
            upd = lax.dot_general(pick(k_rem[:, ks]).astype(BF16), vh, tn, preferred_element_type=F32)
            s_scr[p][h][j] = jnp.exp(dec) * s_old + upd
        return o_h * lax.rsqrt(jnp.mean(jnp.square(o_h), axis=-1, keepdims=True) + EPS) * gn_ref[...]

    for s in range(sub):
        rs = slice(s * rows, (s + 1) * rows)
        staged = [decays(p, rs) for p in range(par)]
        outs = [[] for _ in range(par)]
        for h in range(GLA_HEADS):
            for p in range(par):
                outs[p].append(head(p, rs, h, *staged[p]))
        for p in range(par):
            o = jnp.concatenate(outs[p], axis=-1) * jax.nn.silu(ga_refs[p][rs, :].astype(F32))
            o_ref[p, rs, :] = o.astype(o_ref.dtype)

    @pl.when(ci == pl.num_programs(1) - 1)
    def _():
        for p in range(par):
            for h in range(GLA_HEADS):
                sout_ref[p * nb:(p + 1) * nb, h] = s_scr[p][h][...]


def _gla(h, wup, bgk, gn, s0, *, row0, n_seq, seq_len, par, nb, c, sub, name):
    rows = nb * c * sub
    n_blocks = n_seq // nb
    n_steps = seq_len // (c * sub)
    rb0 = row0 // rows

    def rmap(p, col):
        return lambda g, i: (rb0 + (g * par + p) * n_steps + i, col)

    fields = ((GLA_KEY_DIM, H_QA), (GLA_KEY_DIM, H_KA), (GLA_VAL_DIM, H_VA), (GLA_VAL_DIM, H_GA), (LANES, H_GK))
    in_specs = [pl.BlockSpec((rows, width), rmap(p, off // width)) for width, off in fields for p in range(par)]
    in_specs += [pl.BlockSpec((LANES, GLA_KEY_DIM), lambda g, i: (0, 0)),
                 pl.BlockSpec((1, GLA_KEY_DIM), lambda g, i: (0, 0)),
                 pl.BlockSpec((1, GLA_DV), lambda g, i: (0, 0))]
    args = [h] * (5 * par) + [wup, bgk, gn]
    state_spec = pl.BlockSpec((par * nb, GLA_HEADS, GLA_DK, GLA_DV), lambda g, i: (g, 0, 0, 0))
    if s0 is not None:
        in_specs.append(state_spec)
        args.append(s0)
    kern = functools.partial(_gla_kernel, par=par, nb=nb, c=c, sub=sub, has_s0=s0 is not None)
    return pl.pallas_call(
        kern,
        grid=(n_blocks // par, n_steps),
        in_specs=in_specs,
        out_specs=[pl.BlockSpec((par, rows, GLA_VAL_DIM), lambda g, i: (g, i, 0)), state_spec],
        out_shape=[jax.ShapeDtypeStruct((n_blocks, nb * seq_len, GLA_VAL_DIM), BF16),
                   jax.ShapeDtypeStruct((n_seq, GLA_HEADS, GLA_DK, GLA_DV), F32)],
        scratch_shapes=[pltpu.VMEM((nb, GLA_DK, GLA_DV), F32)] * (par * GLA_HEADS),
        compiler_params=_cparams(("parallel", "arbitrary")),
        name=name,
    )(*args)


def _alibi_slope(head):
    return 2.0 ** (-8.0 * (head + 1) / SWA_HEADS)


def _swa_softmax_pv(parts, sink):
    m = sink
    for s, _ in parts:
        m = jnp.maximum(m, jnp.max(s, axis=-1, keepdims=True))
    denom = jnp.exp(sink - m)
    acc = None
    for s, v in parts:
        p = jnp.exp(s - m)
        denom = denom + jnp.sum(p, axis=-1, keepdims=True)
        pv = jnp.dot(p.astype(BF16), v, preferred_element_type=F32)
        acc = pv if acc is None else acc + pv
    return acc / denom


def _swa_prompt_kernel(sink_ref, q_ref, kp_ref, vp_ref, kc_ref, vc_ref, o_ref, bias):
    i = pl.program_id(1)
    span = 2 * WINDOW
    col = lax.broadcasted_iota(I32, (WINDOW, span), 1)

    @pl.when(i == 0)
    def _():
        dist_i = lax.broadcasted_iota(I32, (WINDOW, span), 0) + WINDOW - col
        in_window = (dist_i >= 0) & (dist_i < WINDOW)
        dist = dist_i.astype(F32)
        for hh in range(SWA_HEADS):
            bias[hh] = jnp.where(in_window, -_alibi_slope(hh) * dist, -jnp.inf)

    kcat = jnp.concatenate([kp_ref[...], kc_ref[...]], axis=0)
    vcat = jnp.concatenate([vp_ref[...], vc_ref[...]], axis=0)
    no_past = (col < WINDOW) & (i == 0)
    scale = SWA_HEAD_DIM ** -0.5
    outs = []
    for hh in range(SWA_HEADS):
        g = hh // SWA_GROUP
        gs = slice(g * SWA_HEAD_DIM, (g + 1) * SWA_HEAD_DIM)
        qh = q_ref[:, hh * SWA_HEAD_DIM:(hh + 1) * SWA_HEAD_DIM] * scale
        s = lax.dot_general(qh, kcat[:, gs], (((1,), (1,)), ((), ())), preferred_element_type=F32)
        s = jnp.where(no_past, -jnp.inf, s + bias[hh])
        outs.append(_swa_softmax_pv([(s, vcat[:, gs])], sink_ref[hh]))
    o_ref[...] = jnp.concatenate(outs, axis=-1).astype(o_ref.dtype)


def _swa_prompt(h, sinks, *, n_seq, seq_len, name):
    nq = seq_len // WINDOW
    qcol = H_QB // SWA_Q_DIM
    kcol = H_KB // SWA_KV_DIM
    vcol = H_VB // SWA_KV_DIM

    def cur(col):
        return lambda b, i, sk: (b * nq + i, col)

    def prev(col):
        return lambda b, i, sk: (b * nq + jnp.maximum(i - 1, 0), col)

    return pl.pallas_call(
        _swa_prompt_kernel,
        grid_spec=pltpu.PrefetchScalarGridSpec(
            num_scalar_prefetch=1,
            grid=(n_seq, nq),
            in_specs=[pl.BlockSpec((WINDOW, SWA_Q_DIM), cur(qcol)),
                      pl.BlockSpec((WINDOW, SWA_KV_DIM), prev(kcol)),
                      pl.BlockSpec((WINDOW, SWA_KV_DIM), prev(vcol)),
                      pl.BlockSpec((WINDOW, SWA_KV_DIM), cur(kcol)),
                      pl.BlockSpec((WINDOW, SWA_KV_DIM), cur(vcol))],
            out_specs=pl.BlockSpec((WINDOW, SWA_Q_DIM), lambda b, i, sk: (b * nq + i, 0)),
            scratch_shapes=[pltpu.VMEM((SWA_HEADS, WINDOW, 2 * WINDOW), F32)],
        ),
        out_shape=jax.ShapeDtypeStruct((n_seq * seq_len, SWA_Q_DIM), BF16),
        compiler_params=_cparams(("parallel", "arbitrary")),
        name=name,
    )(sinks, h, h, h, h, h)


def _swa_sample_kernel(sink_ref, q_ref, kc_ref, vc_ref, kp_ref, vp_ref, o_ref, *, nb, lq):
    rows = SWA_HEADS * lq
    grp_rows = SWA_GROUP * lq
    hd = SWA_HEAD_DIM
    nt = (((1,), (1,)), ((), ()))
    head_of_row = lax.broadcasted_iota(I32, (rows, 1), 0) // lq
    slope = jnp.zeros((rows, 1), F32)
    sink = jnp.zeros((rows, 1), F32)
    for hh in range(SWA_HEADS):
        slope = jnp.where(head_of_row == hh, _alibi_slope(hh), slope)
        sink = jnp.where(head_of_row == hh, sink_ref[hh], sink)
    qi = lax.broadcasted_iota(I32, (rows, WINDOW), 0) % lq
    dist_p = qi + WINDOW - lax.broadcasted_iota(I32, (rows, WINDOW), 1)
    valid_p = dist_p < WINDOW
    bias_p = slope * dist_p.astype(F32)
    dist_c = lax.broadcasted_iota(I32, (rows, lq), 0) % lq - lax.broadcasted_iota(I32, (rows, lq), 1)
    valid_c = dist_c >= 0
    bias_c = slope * dist_c.astype(F32)
    scale = hd ** -0.5

    q_all = q_ref[...].astype(F32)
    kc_all = kc_ref[...].astype(F32)
    vc_all = vc_ref[...].astype(F32)
    seq_outs = []
    for j in range(nb):
        js = slice(j * lq, (j + 1) * lq)
        pieces = []
        for hh in range(SWA_HEADS):
            g = hh // SWA_GROUP
            parts = []
            if g:
                parts.append(jnp.zeros((lq, g * hd), F32))
            parts.append(q_all[js, hh * hd:(hh + 1) * hd])
            if g < SWA_KV_HEADS - 1:
                parts.append(jnp.zeros((lq, (SWA_KV_HEADS - 1 - g) * hd), F32))
            pieces.append(jnp.concatenate(parts, axis=-1))
        q_big = jnp.concatenate(pieces, axis=0).astype(BF16)
        kp = kp_ref[j].astype(BF16)
        vp = vp_ref[j].astype(BF16)
        kcj = kc_all[js, :].astype(BF16)
        vcj = vc_all[js, :].astype(BF16)
        s_p = lax.dot_general(q_big, kp, nt, preferred_element_type=F32)
        s_p = jnp.where(valid_p, s_p * scale - bias_p, -jnp.inf)
        s_c = lax.dot_general(q_big, kcj, nt, preferred_element_type=F32)
        s_c = jnp.where(valid_c, s_c * scale - bias_c, -jnp.inf)
        o_big = _swa_softmax_pv([(s_p, vp), (s_c, vcj)], sink)
        o_grp = [o_big[g * grp_rows:(g + 1) * grp_rows, g * hd:(g + 1) * hd] for g in range(SWA_KV_HEADS)]
        o_heads = jnp.concatenate(o_grp, axis=0)
        seq_outs.append(jnp.concatenate([o_heads[hh * lq:(hh + 1) * lq, :] for hh in range(SWA_HEADS)],
                                        axis=-1))
    o_ref[...] = jnp.concatenate(seq_outs, axis=0).astype(o_ref.dtype)


def _swa_sample(h, sinks, k_past, v_past, *, row0, n_seq, lq, nb, name):
    rows = nb * lq
    rb0 = row0 // rows
    qcol = H_QB // SWA_Q_DIM
    kcol = H_KB // SWA_KV_DIM
    vcol = H_VB // SWA_KV_DIM
    kern = functools.partial(_swa_sample_kernel, nb=nb, lq=lq)
    past_spec = pl.BlockSpec((nb, WINDOW, SWA_KV_DIM), lambda g, sk: (g, 0, 0))
    return pl.pallas_call(
        kern,
        grid_spec=pltpu.PrefetchScalarGridSpec(
            num_scalar_prefetch=1,
            grid=(n_seq // nb,),
            in_specs=[pl.BlockSpec((rows, SWA_Q_DIM), lambda g, sk: (rb0 + g, qcol)),
                      pl.BlockSpec((rows, SWA_KV_DIM), lambda g, sk: (rb0 + g, kcol)),
                      pl.BlockSpec((rows, SWA_KV_DIM), lambda g, sk: (rb0 + g, vcol)),
                      past_spec, past_spec],
            out_specs=pl.BlockSpec((rows, SWA_Q_DIM), lambda g, sk: (g, 0)),
        ),
        out_shape=jax.ShapeDtypeStruct((n_seq * lq, SWA_Q_DIM), BF16),
        compiler_params=_cparams(("parallel",)),
        name=name,
    )(sinks, h, h, h, k_past, v_past)


def _layer_norm(x, g, b):
    mu = jnp.mean(x, axis=-1, keepdims=True)
    xc = x - mu
    var = jnp.mean(jnp.square(xc), axis=-1, keepdims=True)
    return xc * lax.rsqrt(var + EPS) * g + b


def _pack_bf16_pairs(x, s):
    lo = lax.bitcast_convert_type(x[:, (2 * s) * LANES:(2 * s + 1) * LANES].astype(BF16).astype(F32), I32)
    hi = lax.bitcast_convert_type(x[:, (2 * s + 1) * LANES:(2 * s + 2) * LANES].astype(BF16).astype(F32), I32)
    return lax.shift_right_logical(lo, 16) | (hi & HIGH_HALF)


def _unpack_bf16_pairs(w):
    lo = lax.bitcast_convert_type(lax.shift_left(w, 16), F32).astype(BF16)
    hi = lax.bitcast_convert_type(w & HIGH_HALF, F32).astype(BF16)
    return lo, hi


def _merge_kernel(oap_ref, oas_ref, obp_ref, obs_ref, ga_ref, gb_ref, xp_ref, xs_ref,
                  wpa_ref, wpb_ref, wout_ref, g_ref, b_ref, o_ref, xw_ref, *, np_tiles):
    i = pl.program_id(0)
    tm = o_ref.shape[0]
    br_a = jnp.dot(_pair_value(i, np_tiles, oap_ref, oas_ref), wpa_ref[...], preferred_element_type=F32)
    br_b = jnp.dot(_pair_value(i, np_tiles, obp_ref, obs_ref), wpb_ref[...], preferred_element_type=F32)
    merged = (jax.nn.sigmoid(ga_ref[...].astype(F32)) * br_a
              + jax.nn.sigmoid(gb_ref[...].astype(F32)) * br_b)
    mix = jnp.dot(merged.astype(BF16), wout_ref[...], preferred_element_type=F32)
    x = _pair_value(i, np_tiles, xp_ref, xs_ref)
    x1 = _layer_norm(DEEPNORM_ALPHA * x + mix, g_ref[...], b_ref[...])
    o_ref[...] = x1
    for s in range(PACKED_ROWS):
        xw_ref[pl.ds(s, tm, stride=PACKED_ROWS), :] = _pack_bf16_pairs(x1, s)


def _merge(oa, ob, h, x, wpa, wpb, wout, g, b, *, tm, name):
    t = h.shape[0]
    np_tiles = x[0].shape[0] // tm
    p_map, s_map = _pair_maps(np_tiles)
    row = lambda i: (i, 0)
    full = lambda i: (0, 0)
    pair = [pl.BlockSpec((tm, D_MODEL), p_map), pl.BlockSpec((tm, D_MODEL), s_map)]
    return pl.pallas_call(
        functools.partial(_merge_kernel, np_tiles=np_tiles),
        grid=(t // tm,),
        in_specs=pair + pair + [
                  pl.BlockSpec((tm, D_MODEL), lambda i: (i, H_GATE_A // D_MODEL)),
                  pl.BlockSpec((tm, D_MODEL), lambda i: (i, H_GATE_B // D_MODEL))] + pair + [
                  pl.BlockSpec((D_MODEL, D_MODEL), full),
                  pl.BlockSpec((D_MODEL, D_MODEL), full),
                  pl.BlockSpec((D_MODEL, D_MODEL), full),
                  pl.BlockSpec((1, D_MODEL), full),
                  pl.BlockSpec((1, D_MODEL), full)],
        out_specs=[pl.BlockSpec((tm, D_MODEL), row),
                   pl.BlockSpec((tm * PACKED_ROWS, LANES), row)],
        out_shape=[jax.ShapeDtypeStruct((t, D_MODEL), F32),
                   jax.ShapeDtypeStruct((t * PACKED_ROWS, LANES), I32)],
        compiler_params=_cparams(("parallel",)),
        name=name,
    )(*oa, *ob, h, h, *x, wpa, wpb, wout, g, b)


def _router_kernel(x_ref, whi_ref, wlo_ref, bias_ref, idx_ref, wl_ref, rank_ref, cntc_ref, cntr_ref,
                   cntc_scr, cntr_scr):
    i = pl.program_id(0)
    tm = x_ref.shape[0]

    @pl.when(i == 0)
    def _():
        cntc_scr[...] = jnp.zeros_like(cntc_scr)
        cntr_scr[...] = jnp.zeros_like(cntr_scr)

    x_hi, x_lo = _split_bf16(x_ref[...])
    nt = (((1,), (1,)), ((), ()))
    logits = (lax.dot_general(whi_ref[...], x_hi, nt, preferred_element_type=F32)
              + lax.dot_general(whi_ref[...], x_lo, nt, preferred_element_type=F32)
              + lax.dot_general(wlo_ref[...], x_hi, nt, preferred_element_type=F32))
    scores = jax.nn.sigmoid(logits)
    biased = scores + bias_ref[...]

    grouped = biased.reshape(N_GROUPS, GROUP_SIZE, tm)
    m1 = jnp.max(grouped, axis=1)
    n_top = jnp.sum(jnp.where(grouped == m1[:, None, :], 1.0, 0.0), axis=1)
    m2 = jnp.max(jnp.where(grouped < m1[:, None, :], grouped, -jnp.inf), axis=1)
    gscore = m1 + jnp.where(n_top >= 2.0, m1, m2)

    g_iota = lax.broadcasted_iota(I32, (N_GROUPS, tm), 0)
    beaten = jnp.zeros((N_GROUPS, tm), I32)
    for g in range(N_GROUPS):
        other = gscore[g:g + 1, :]
        ahead = (other > gscore) | ((other == gscore) & (g < g_iota))
        beaten = beaten + jnp.where(ahead, 1, 0)
    keep = jnp.where(beaten < TOPK_GROUPS, 1.0, 0.0)
    masked = jnp.where(keep[:, None, :] > 0.5, grouped, -jnp.inf).reshape(N_EXPERTS, tm)

    e_iota = lax.broadcasted_iota(I32, (N_EXPERTS, tm), 0)
    sel_f = jnp.zeros((N_EXPERTS, tm), F32)
    ids = []
    for _ in range(TOP_K):
        best = jnp.max(masked, axis=0, keepdims=True)
        idx = jnp.min(jnp.where(masked == best, e_iota, N_EXPERTS), axis=0, keepdims=True)
        hit = e_iota == idx
        sel_f = sel_f + jnp.where(hit, 1.0, 0.0)
        masked = jnp.where(hit, -jnp.inf, masked)
        ids.append(idx)

    top_sum = jnp.sum(sel_f * scores, axis=0, keepdims=True)

    t_r = lax.broadcasted_iota(I32, (tm, tm), 0)
    t_c = lax.broadcasted_iota(I32, (tm, tm), 1)
    before = jnp.where(t_r < t_c, 1.0, 0.0).astype(BF16)
    sel_b = sel_f.astype(BF16)
    rank = jnp.dot(sel_b, before, preferred_element_type=F32) + cntc_scr[...]
    cntc_scr[...] = cntc_scr[...] + jnp.sum(sel_f, axis=1, keepdims=True)
    cntr_scr[...] = cntr_scr[...] + lax.dot_general(jnp.ones((SUBLANES, tm), BF16), sel_b, nt,
                                                    preferred_element_type=F32)

    idx_rows, w_rows, rank_rows = [], [], []
    for idx in ids:
        hit = e_iota == idx
        w = jnp.sum(jnp.where(hit, scores, 0.0), axis=0, keepdims=True)
        w_rows.append(w / top_sum * ROUTED_SCALE)
        rank_rows.append(jnp.sum(jnp.where(hit, rank, 0.0), axis=0, keepdims=True))
        idx_rows.append(idx)
    idx_ref[...] = jnp.concatenate(idx_rows, axis=0)
    rank_ref[...] = jnp.concatenate(rank_rows, axis=0).astype(I32)

    w_all = jnp.concatenate(w_rows, axis=0)
    w_1 = w_all.astype(BF16)
    r_1 = w_all - w_1.astype(F32)
    w_2 = r_1.astype(BF16)
    w_3 = (r_1 - w_2.astype(F32)).astype(BF16)
    terms = jnp.concatenate([w_1, w_2, w_3, jnp.zeros_like(w_1)], axis=0)
    t_row = lax.broadcasted_iota(I32, (4 * TOP_K, TOP_K * LANES), 0)
    t_col = lax.broadcasted_iota(I32, (4 * TOP_K, TOP_K * LANES), 1)
    spread = jnp.where((t_row % TOP_K) == (t_col // LANES), 1.0, 0.0).astype(BF16)
    w_lanes = lax.dot_general(terms, spread, (((0,), (0,)), ((), ())), preferred_element_type=F32)
    for k in range(TOP_K):
        wl_ref[pl.ds(k, tm, stride=TOP_K), :] = w_lanes[:, k * LANES:(k + 1) * LANES]

    @pl.when(i == pl.num_programs(0) - 1)
    def _():
        cntc_ref[...] = cntc_scr[...].astype(I32)
        cntr_ref[...] = cntr_scr[...].astype(I32)


def _router(x1, w_hi, w_lo, bias, *, tm, name):
    t = x1.shape[0]
    full = lambda i: (0, 0)
    tok = lambda i: (0, i)
    return pl.pallas_call(
        _router_kernel,
        grid=(t // tm,),
        in_specs=[pl.BlockSpec((tm, D_MODEL), lambda i: (i, 0)),
                  pl.BlockSpec((N_EXPERTS, D_MODEL), full),
                  pl.BlockSpec((N_EXPERTS, D_MODEL), full),
                  pl.BlockSpec((N_EXPERTS, 1), full)],
        out_specs=[pl.BlockSpec((TOP_K, tm), tok),
                   pl.BlockSpec((TOP_K * tm, LANES), lambda i: (i, 0)),
                   pl.BlockSpec((TOP_K, tm), tok),
                   pl.BlockSpec((N_EXPERTS, 1), full),
                   pl.BlockSpec((SUBLANES, N_EXPERTS), full)],
        out_shape=[jax.ShapeDtypeStruct((TOP_K, t), I32),
                   jax.ShapeDtypeStruct((TOP_K * t, LANES), F32),
                   jax.ShapeDtypeStruct((TOP_K, t), I32),
                   jax.ShapeDtypeStruct((N_EXPERTS, 1), I32),
                   jax.ShapeDtypeStruct((SUBLANES, N_EXPERTS), I32)],
        scratch_shapes=[pltpu.VMEM((N_EXPERTS, 1), F32),
                        pltpu.VMEM((SUBLANES, N_EXPERTS), F32)],
        compiler_params=_cparams(("arbitrary",)),
        name=name,
    )(x1, w_hi, w_lo, bias)


ITEM_FIELDS = 8


def _byte_split(v):
    return lax.shift_right_logical(v, 8).astype(F32), (v & 255).astype(F32)


def _plan_kernel(cntc_ref, cntr_ref, idx_ref, rank_ref, meta_ref, items_ref, m_scr, *, n_tok):
    i = pl.program_id(0)
    tm = idx_ref.shape[1]
    nbm = m_scr.shape[0]
    nip = items_ref.shape[1]
    nbp = nip - N_EXPERTS
    n_rows = n_tok * TOP_K
    n_blk = n_rows // EXPERT_ROWS
    nt = (((1,), (1,)), ((), ()))

    r_i = lax.broadcasted_iota(I32, (N_EXPERTS, N_EXPERTS), 0)
    c_i = lax.broadcasted_iota(I32, (N_EXPERTS, N_EXPERTS), 1)
    below = jnp.where(c_i < r_i, 1.0, 0.0).astype(BF16)
    ones_c = jnp.ones((N_EXPERTS, LANES), F32)
    c_hi, c_lo = _byte_split(cntc_ref[...])
    start_col = (256.0 * jnp.dot(below, (c_hi * ones_c).astype(BF16), preferred_element_type=F32)
                 + jnp.dot(below, (c_lo * ones_c).astype(BF16), preferred_element_type=F32))[:, 0:1]

    @pl.when(i == 0)
    def _():
        m_scr[...] = jnp.zeros_like(m_scr)
        above = jnp.where(r_i < c_i, 1.0, 0.0).astype(BF16)
        r_hi, r_lo = _byte_split(cntr_ref[...])
        start_row = (256.0 * jnp.dot(r_hi.astype(BF16), above, preferred_element_type=F32)
                     + jnp.dot(r_lo.astype(BF16), above, preferred_element_type=F32))[0:1, :]
        b_col = lax.broadcasted_iota(I32, (nbp, 1), 0)
        b_row = lax.broadcasted_iota(I32, (1, nbp), 1)
        blk_col = jnp.where(b_col < n_blk, b_col * EXPERT_ROWS, n_rows).astype(F32)
        blk_row = jnp.where(b_row < n_blk, b_row * EXPERT_ROWS, n_rows).astype(F32)
        v_col = jnp.concatenate([blk_col, start_col], axis=0)
        v_row = jnp.concatenate([blk_row, start_row], axis=1)
        j_col = lax.broadcasted_iota(I32, (nip, 1), 0)
        k_row = lax.broadcasted_iota(I32, (1, nip), 1)
        ahead = (v_row < v_col) | ((v_row == v_col) & (k_row < j_col))
        order_col = jnp.sum(jnp.where(ahead, 1.0, 0.0), axis=1, keepdims=True)
        pos_row = k_row.astype(F32)
        lo_abs = jnp.sum(jnp.where(order_col == pos_row, v_col, 0.0), axis=0, keepdims=True)
        hi_abs = jnp.sum(jnp.where(order_col == pos_row + 1.0, v_col, 0.0), axis=0, keepdims=True)
        hi_abs = jnp.where(k_row == nip - 1, float(n_rows), hi_abs)
        blk = jnp.minimum(jnp.floor(lo_abs * (1.0 / EXPERT_ROWS)), n_blk - 1.0)
        expert = jnp.sum(jnp.where(start_col <= lo_abs, 1.0, 0.0), axis=0, keepdims=True) - 1.0
        base = blk * EXPERT_ROWS
        e_col = lax.broadcasted_iota(I32, (N_EXPERTS, 1), 0).astype(F32)
        end_col = start_col + cntc_ref[...].astype(F32)
        seg_end = jnp.sum(jnp.where(e_col == expert, end_col, 0.0), axis=0, keepdims=True)
        follower = jnp.sum(jnp.where(start_col <= seg_end, 1.0, 0.0), axis=0, keepdims=True) - 1.0
        follower = jnp.where(seg_end < float(n_rows), follower, -1.0)
        fields = [blk, expert, lo_abs - base, hi_abs - base, follower]
        fields.append(jnp.zeros((ITEM_FIELDS - len(fields), nip), F32))
        items_ref[...] = jnp.concatenate(fields, axis=0).astype(I32)

    e_iota = lax.broadcasted_iota(I32, (N_EXPERTS, tm), 0)
    b_iota = lax.broadcasted_iota(I32, (nbm, tm), 0)
    l_iota = lax.broadcasted_iota(I32, (EXPERT_ROWS, tm), 0)
    tok = i * tm + lax.broadcasted_iota(I32, (1, tm), 1)
    tok_hi, tok_lo = _byte_split(tok)
    acc = jnp.zeros(m_scr.shape, F32)
    for k in range(TOP_K):
        hit = e_iota == idx_ref[k:k + 1, :]
        dest = (jnp.sum(jnp.where(hit, start_col, 0.0), axis=0, keepdims=True).astype(I32)
                + rank_ref[k:k + 1, :])
        oh_blk = jnp.where(b_iota == lax.shift_right_logical(dest, 8), 1.0, 0.0).astype(BF16)
        in_blk = l_iota == (dest & (EXPERT_ROWS - 1))
        vals = jnp.concatenate([jnp.where(in_blk, tok_hi, 0.0), jnp.where(in_blk, tok_lo, 0.0),
                                jnp.where(in_blk, float(k), 0.0)], axis=0).astype(BF16)
        acc = acc + lax.dot_general(oh_blk, vals, nt, preferred_element_type=F32)
    m_scr[...] = m_scr[...] + acc

    @pl.when(i == pl.num_programs(0) - 1)
    def _():
        m = m_scr[...]
        row_tok = m[:, 0:EXPERT_ROWS] * 256.0 + m[:, EXPERT_ROWS:2 * EXPERT_ROWS]
        row_slot = m[:, 2 * EXPERT_ROWS:3 * EXPERT_ROWS] * float(n_tok) + row_tok
        meta_ref[...] = jnp.concatenate([row_tok * float(PACKED_ROWS), row_slot * float(ROW_CHUNKS)],
                                        axis=1).astype(I32)


def _plan(cnt_col, cnt_row, idx, rank, *, tm, name):
    t = idx.shape[1]
    assert EXPERT_ROWS == 256 and (t * TOP_K) % (2 * EXPERT_ROWS) == 0
    n_blk = t * TOP_K // EXPERT_ROWS
    nbm = -(-n_blk // 16) * 16
    nip = -(-n_blk // LANES) * LANES + N_EXPERTS
    tok = lambda i: (0, i)
    full = lambda i: (0, 0)
    return pl.pallas_call(
        functools.partial(_plan_kernel, n_tok=t),
        grid=(t // tm,),
        in_specs=[pl.BlockSpec((N_EXPERTS, 1), full),
                  pl.BlockSpec((SUBLANES, N_EXPERTS), full),
                  pl.BlockSpec((TOP_K, tm), tok),
                  pl.BlockSpec((TOP_K, tm), tok)],
        out_specs=[pl.BlockSpec((nbm, 2 * EXPERT_ROWS), full),
                   pl.BlockSpec((ITEM_FIELDS, nip), full)],
        out_shape=[jax.ShapeDtypeStruct((nbm, 2 * EXPERT_ROWS), I32),
                   jax.ShapeDtypeStruct((ITEM_FIELDS, nip), I32)],
        scratch_shapes=[pltpu.VMEM((nbm, 3 * EXPERT_ROWS), F32)],
        compiler_params=_cparams(("arbitrary",)),
        name=name,
    )(cnt_col, cnt_row, idx, rank)


META_CHUNK = 4 * EXPERT_ROWS
GATHER_STRIDE = EXPERT_ROWS + 8


def _experts_kernel(items_ref, xw_hbm, meta_hbm, wg_hbm, wu_hbm, wd_hbm, ys_hbm,
                    xw, meta, tile, lhs, wg_f, wu_f, wd_f, wg_b, wu_b, wd_b, yacc, ybuf, state,
                    sem_x, sem_m, sem_y, sem_w, *, nip, n_blk):
    i = pl.program_id(0)
    blk = items_ref[i]
    expert = items_ref[nip + i]
    lo = items_ref[2 * nip + i]
    hi = items_ref[3 * nip + i]
    follower = items_ref[4 * nip + i]
    nonempty = hi > lo
    par = blk & 1
    chunk = lax.shift_right_logical(blk, 1)
    cpar = chunk & 1
    mbase = cpar * META_CHUNK + par * (2 * EXPERT_ROWS)

    def meta_copy(c, slot):
        return pltpu.make_async_copy(meta_hbm.at[pl.ds(pl.multiple_of(c * META_CHUNK, META_CHUNK), META_CHUNK)],
                                     meta.at[pl.ds(pl.multiple_of(slot * META_CHUNK, META_CHUNK), META_CHUNK)],
                                     sem_m.at[slot])

    def ybuf_drain(slot):
        pltpu.make_async_copy(ys_hbm.at[pl.ds(0, EXPERT_ROWS * ROW_CHUNKS), :], ybuf.at[slot],
                              sem_y.at[slot]).wait()

    def ybuf_fill_and_send(slot):
        for j in range(ROW_CHUNKS):
            ybuf[slot, pl.ds(j, EXPERT_ROWS, stride=ROW_CHUNKS), :] = yacc[:, j * LANES:(j + 1) * LANES]

        for r in range(EXPERT_ROWS):
            src = ybuf.at[slot, pl.ds(r * ROW_CHUNKS, ROW_CHUNKS), :]
            dst_row = pl.multiple_of(meta[mbase + EXPERT_ROWS + r], ROW_CHUNKS)
            pltpu.make_async_copy(src, ys_hbm.at[pl.ds(dst_row, ROW_CHUNKS), :],
                                  sem_y.at[slot]).start(priority=r % 2)

    def weight_copies(e, slot):
        return [pltpu.make_async_copy(src.at[e], dst.at[slot], sem_w.at[slot])
                for src, dst in ((wg_hbm, wg_f), (wu_hbm, wu_f), (wd_hbm, wd_f))]

    @pl.when(i == 0)
    def _():
        resident = pltpu.make_async_copy(xw_hbm, xw, sem_x)
        resident.start()
        meta_copy(0, 0).start()
        for cp in weight_copies(expert, 0):
            cp.start()
        state[0] = -1
        state[1] = 1
        resident.wait()

    @pl.when(nonempty & (expert != state[0]))
    def _():
        slot = 1 - state[1]
        for cp in weight_copies(expert, slot):
            cp.wait()
        state[0] = expert
        state[1] = slot

        @pl.when(follower >= 0)
        def _():
            for cp in weight_copies(follower, 1 - slot):
                cp.start()

        wg_b[...] = wg_f[slot].astype(BF16)
        wu_b[...] = wu_f[slot].astype(BF16)
        wd_b[...] = wd_f[slot].astype(BF16)

    @pl.when(nonempty & (lo == 0))
    def _():
        @pl.when(par == 0)
        def _():
            meta_copy(chunk, cpar).wait()

            @pl.when(2 * (chunk + 1) < n_blk)
            def _():
                meta_copy(chunk + 1, 1 - cpar).start()

        for r in range(EXPERT_ROWS):
            t4 = pl.multiple_of(meta[mbase + r], PACKED_ROWS)
            tile[pl.ds(r, PACKED_ROWS, stride=GATHER_STRIDE), :] = xw[pl.ds(t4, PACKED_ROWS), :]
        cols = []
        for s in range(PACKED_ROWS):
            cols.extend(_unpack_bf16_pairs(tile[pl.ds(s * GATHER_STRIDE, EXPERT_ROWS), :]))
        lhs[...] = jnp.concatenate(cols, axis=-1)

    def ffn(x):
        gate = jnp.dot(x, wg_b[...], preferred_element_type=F32)
        up = jnp.dot(x, wu_b[...], preferred_element_type=F32)
        hid = (jax.nn.silu(gate) * up).astype(BF16)
        return jnp.dot(hid, wd_b[...], preferred_element_type=F32)

    whole = (lo == 0) & (hi == EXPERT_ROWS)

    @pl.when(whole)
    def _():
        yacc[...] = ffn(lhs[...])

    @pl.when(nonempty & jnp.logical_not(whole))
    def _():
        row = lax.broadcasted_iota(I32, (EXPERT_ROWS, 1), 0)
        y = jnp.where((row >= lo) & (row < hi), ffn(lhs[...]), 0.0)

        @pl.when(lo == 0)
        def _():
            yacc[...] = y

        @pl.when(lo > 0)
        def _():
            yacc[...] = yacc[...] + y

    @pl.when(nonempty & (hi == EXPERT_ROWS))
    def _():
        @pl.when(blk >= 2)
        def _():
            ybuf_drain(par)

        for slot in range(2):
            @pl.when(par == slot)
            def _():
                ybuf_fill_and_send(slot)

    @pl.when(i == pl.num_programs(0) - 1)
    def _():
        ybuf_drain(0)
        ybuf_drain(1)


def _experts(items, xw, meta, wg, wu, wd, *, n_tok, name):
    nip = items.shape[0] // ITEM_FIELDS
    n_rows = n_tok * TOP_K
    n_blk = n_rows // EXPERT_ROWS
    n_items = n_blk + N_EXPERTS
    kern = functools.partial(_experts_kernel, nip=nip, n_blk=n_blk)
    any_spec = pl.BlockSpec(memory_space=pl.ANY)
    return pl.pallas_call(
        kern,
        grid_spec=pltpu.PrefetchScalarGridSpec(
            num_scalar_prefetch=1,
            grid=(n_items,),
            in_specs=[any_spec] * 5,
            out_specs=any_spec,
            scratch_shapes=[pltpu.VMEM(xw.shape, I32),
                            pltpu.SMEM((2 * META_CHUNK,), I32),
                            pltpu.VMEM((PACKED_ROWS * GATHER_STRIDE, LANES), I32),
                            pltpu.VMEM((EXPERT_ROWS, D_MODEL), BF16),
                            pltpu.VMEM((2, D_MODEL, EXPERT_FF), F32),
                            pltpu.VMEM((2, D_MODEL, EXPERT_FF), F32),
                            pltpu.VMEM((2, EXPERT_FF, D_MODEL), F32),
                            pltpu.VMEM((D_MODEL, EXPERT_FF), BF16),
                            pltpu.VMEM((D_MODEL, EXPERT_FF), BF16),
                            pltpu.VMEM((EXPERT_FF, D_MODEL), BF16),
                            pltpu.VMEM((EXPERT_ROWS, D_MODEL), F32),
                            pltpu.VMEM((2, EXPERT_ROWS * ROW_CHUNKS, LANES), F32),
                            pltpu.SMEM((2,), I32),
                            pltpu.SemaphoreType.DMA,
                            pltpu.SemaphoreType.DMA((2,)),
                            pltpu.SemaphoreType.DMA((2,)),
                            pltpu.SemaphoreType.DMA((2,))],
        ),
        out_shape=jax.ShapeDtypeStruct((n_rows * ROW_CHUNKS, LANES), F32),
        compiler_params=_cparams(("arbitrary",)),
        name=name,
    )(items, xw, meta, wg, wu, wd)


def _combine_kernel(ys_ref, x_ref, wl_ref, wsg_ref, wsu_ref, wsd_ref, g_ref, b_ref, op_ref, os_ref,
                    routed, *, np_tiles):
    x = x_ref[...]
    xb = x.astype(BF16)
    hid = (jax.nn.silu(jnp.dot(xb, wsg_ref[...], preferred_element_type=F32))
           * jnp.dot(xb, wsu_ref[...], preferred_element_type=F32))
    shared = jnp.dot(hid.astype(BF16), wsd_ref[...], preferred_element_type=F32)
    acc = None
    for k in range(TOP_K):
        part = wl_ref[:, k:k + 1, :] * ys_ref[k]
        acc = part if acc is None else acc + part
    routed[...] = acc
    ffn = jnp.concatenate([routed[:, j, :] for j in range(ROW_CHUNKS)], axis=-1) + shared
    out = _layer_norm(DEEPNORM_ALPHA * x + ffn, g_ref[...], b_ref[...])
    i = pl.program_id(0)

    @pl.when(i < np_tiles)
    def _():
        op_ref[...] = out

    @pl.when(i >= np_tiles)
    def _():
        os_ref[...] = out


def _combine(ys, x1, w_tok, wsg, wsu, wsd, g, b, *, n_prompt, tm, name):
    t = x1.shape[0]
    np_tiles = n_prompt // tm
    p_map, s_map = _pair_maps(np_tiles)
    row = lambda i: (i, 0)
    full = lambda i: (0, 0)
    return pl.pallas_call(
        functools.partial(_combine_kernel, np_tiles=np_tiles),
        grid=(t // tm,),
        in_specs=[pl.BlockSpec((TOP_K, tm, ROW_CHUNKS, LANES), lambda i: (0, i, 0, 0)),
                  pl.BlockSpec((tm, D_MODEL), row),
                  pl.BlockSpec((tm, TOP_K, LANES), lambda i: (i, 0, 0)),
                  pl.BlockSpec((D_MODEL, SHARED_FF), full),
                  pl.BlockSpec((D_MODEL, SHARED_FF), full),
                  pl.BlockSpec((SHARED_FF, D_MODEL), full),
                  pl.BlockSpec((1, D_MODEL), full),
                  pl.BlockSpec((1, D_MODEL), full)],
        out_specs=[pl.BlockSpec((tm, D_MODEL), p_map), pl.BlockSpec((tm, D_MODEL), s_map)],
        out_shape=[jax.ShapeDtypeStruct((n_prompt, D_MODEL), F32),
                   jax.ShapeDtypeStruct((t - n_prompt, D_MODEL), F32)],
        scratch_shapes=[pltpu.VMEM((tm, ROW_CHUNKS, LANES), F32)],
        compiler_params=_cparams(("arbitrary",)),
        name=name,
    )(ys, x1, w_tok, wsg, wsu, wsd, g, b)


def _pack_w_in(w_in):
    sizes = (GLA_KEY_DIM, GLA_KEY_DIM, GLA_VAL_DIM, GLA_VAL_DIM, GLA_GATE_RANK,
             SWA_Q_DIM, SWA_KV_DIM, SWA_KV_DIM, D_MODEL, D_MODEL)
    offs = [0]
    for s in sizes:
        offs.append(offs[-1] + s)
    qa, ka, va, ga, gk, qb, kb, vb, gate_a, gate_b = (w_in[:, offs[i]:offs[i + 1]] for i in range(10))
    pad = lambda w, n: jnp.pad(w, ((0, 0), (0, n - w.shape[1])))
    packed = jnp.concatenate([qa, ka, va, ga, qb, gate_a, gate_b, kb, vb, pad(gk, 2 * LANES)], axis=1)
    assert packed.shape[1] == H_WIDTH
    return packed.astype(BF16), jnp.concatenate([kb, vb], axis=1).astype(BF16)


def kernel(x_prompt, x_sample, state_gla, cache_swa_k, cache_swa_v, w_in, w_gk_up, b_gk, gla_norm_g,
           attn_sinks, w_proj_a, w_proj_b, w_out, ln1_g, ln1_b, w_router, router_bias,
           w_expert_gate, w_expert_up, w_expert_down, w_shared_gate, w_shared_up, w_shared_down,
           ln2_g, ln2_b):
    assert w_in.shape[0] == 1, "single-layer trunk"
    bp, lp, d = x_prompt.shape
    bs, ls, _ = x_sample.shape
    assert d == D_MODEL and ls == SUBLANES and cache_swa_k.shape[2] == WINDOW
    tp, ts = bp * lp, bs * ls
    t = tp + ts

    xp = x_prompt.reshape(tp, d)
    xs = x_sample.reshape(ts, d)
    w_main, w_kv = _pack_w_in(w_in[0])
    h = _matmul(xp, xs, w_main, BF16, _pair_tile(tp, ts, 1024), H_TN, "proj_in")

    xp_tail = x_prompt[:, lp - WINDOW:].reshape(bp * WINDOW, d)
    kv_tail = _matmul(xp_tail, xs, w_kv, F32, _pair_tile(bp * WINDOW, ts, 512), 2 * SWA_KV_DIM,
                      "proj_kv_tail")

    wup = jnp.pad(w_gk_up[0], ((0, LANES - GLA_GATE_RANK), (0, 0))).astype(BF16)
    bgk = b_gk[0].reshape(1, GLA_KEY_DIM)
    gn = gla_norm_g[0].reshape(1, GLA_DV)
    oa_p, s_prompt = _gla(h, wup, bgk, gn, None, row0=0, n_seq=bp, seq_len=lp, par=math.gcd(bp, 4), nb=1,
                          c=GLA_CHUNK, sub=2, name="gla_prompt")
    oa_s, s_sample = _gla(h, wup, bgk, gn, state_gla[0], row0=tp, n_seq=bs, seq_len=ls, par=1, nb=8,
                          c=math.gcd(ls, GLA_CHUNK), sub=1, name="gla_sample")
    oa_p = oa_p.reshape(tp, GLA_VAL_DIM)
    oa_s = oa_s.reshape(ts, GLA_VAL_DIM)

    sinks = attn_sinks[0]
    k_past = cache_swa_k[0].reshape(bs, WINDOW, SWA_KV_DIM)
    v_past = cache_swa_v[0].reshape(bs, WINDOW, SWA_KV_DIM)
    ob_p = _swa_prompt(h, sinks, n_seq=bp, seq_len=lp, name="swa_prompt")
    ob_s = _swa_sample(h, sinks, k_past, v_past, row0=tp, n_seq=bs, lq=ls, nb=8, name="swa_sample")

    x1, xw = _merge((oa_p, oa_s), (ob_p, ob_s), h, (xp, xs), w_proj_a[0].astype(BF16),
                    w_proj_b[0].astype(BF16), w_out[0].astype(BF16), ln1_g[0].reshape(1, d),
                    ln1_b[0].reshape(1, d), tm=_pair_tile(tp, ts, 512), name="merge_ln1")

    wr_t = w_router[0].T
    wr_hi = wr_t.astype(BF16)
    wr_lo = (wr_t - wr_hi.astype(F32)).astype(BF16)
    tm_r = _tile(t, 512)
    idx, w_lanes, rank, cnt_col, cnt_row = _router(x1, wr_hi, wr_lo, router_bias[0].reshape(N_EXPERTS, 1),
                                                   tm=tm_r, name="router")
    meta, items = _plan(cnt_col, cnt_row, idx, rank, tm=tm_r, name="plan")
    ys = _experts(items.reshape(-1), xw, meta.reshape(-1), w_expert_gate[0], w_expert_up[0],
                  w_expert_down[0], n_tok=t, name="experts")
    y_p, y_s = _combine(ys.reshape(TOP_K, t, ROW_CHUNKS, LANES), x1, w_lanes.reshape(t, TOP_K, LANES),
                        w_shared_gate[0].astype(BF16), w_shared_up[0].astype(BF16),
                        w_shared_down[0].astype(BF16), ln2_g[0].reshape(1, d), ln2_b[0].reshape(1, d),
                        n_prompt=tp, tm=_pair_tile(tp, ts, 128), name="combine_ln2")

    y_prompt = y_p.reshape(bp, lp, d)
    y_sample = y_s.reshape(bs, ls, d)
    k_tail = kv_tail[:, :SWA_KV_DIM]
    v_tail = kv_tail[:, SWA_KV_DIM:]
    kv_shape = (SWA_KV_HEADS, SWA_HEAD_DIM)
    k_prompt = k_tail[:bp * WINDOW].reshape(1, bp, WINDOW, *kv_shape)
    v_prompt = v_tail[:bp * WINDOW].reshape(1, bp, WINDOW, *kv_shape)
    k_new = k_tail[bp * WINDOW:].reshape(bs, ls, *kv_shape)
    v_new = v_tail[bp * WINDOW:].reshape(bs, ls, *kv_shape)
    k_sample = jnp.concatenate([cache_swa_k[0][:, ls:], k_new], axis=1)[None]
    v_sample = jnp.concatenate([cache_swa_v[0][:, ls:], v_new], axis=1)[None]
    return (y_prompt, y_sample, s_prompt[None], s_sample[None], k_prompt, v_prompt, k_sample, v_sample)
```

```python
import functools
import math

import jax
import jax.numpy as jnp
from jax import lax
from jax.experimental import pallas as pl
from jax.experimental.pallas import tpu as pltpu

F32 = jnp.float32
BF16 = jnp.bfloat16
I32 = jnp.int32

D_MODEL = 1024
GLA_HEADS = 4
GLA_DK = 128
GLA_DV = 256
GLA_KEY_DIM = GLA_HEADS * GLA_DK
GLA_VAL_DIM = GLA_HEADS * GLA_DV
GLA_GATE_RANK = 16
GLA_GATE_NORMALIZER = 16.0
GLA_CHUNK = 64
SWA_HEADS = 16
SWA_KV_HEADS = 4
SWA_GROUP = SWA_HEADS // SWA_KV_HEADS
SWA_HEAD_DIM = 64
SWA_Q_DIM = SWA_HEADS * SWA_HEAD_DIM
SWA_KV_DIM = SWA_KV_HEADS * SWA_HEAD_DIM
WINDOW = 128
N_EXPERTS = 256
TOP_K = 8
N_GROUPS = 8
GROUP_SIZE = N_EXPERTS // N_GROUPS
TOPK_GROUPS = 4
EXPERT_FF = 256
SHARED_FF = 256
ROUTED_SCALE = 2.5
DEEPNORM_ALPHA = 2.0 ** 0.25
EPS = 1e-5

LANES = 128
SUBLANES = 8
ROW_CHUNKS = D_MODEL // LANES
VMEM_LIMIT = 56 * 1024 * 1024

H_QA, H_KA, H_VA, H_GA, H_QB, H_GATE_A, H_GATE_B, H_KB, H_VB, H_GK = (
    0, 512, 1024, 2048, 3072, 4096, 5120, 6144, 6400, 6656)
H_WIDTH = 6912
H_TN = 2304

EXPERT_ROWS = 256
PACKED_ROWS = ROW_CHUNKS // 2
HIGH_HALF = -65536


def _cparams(sem, vmem=VMEM_LIMIT):
    return pltpu.CompilerParams(dimension_semantics=sem, vmem_limit_bytes=vmem)


def _tile(n, pref):
    t = min(n, pref)
    while n % t:
        t -= LANES
    assert t > 0 and t % LANES == 0, (n, pref)
    return t


def _pair_tile(n_p, n_s, pref):
    return _tile(math.gcd(n_p, n_s), pref)


def _pair_maps(np_tiles, col=0, extra=0):
    def p_map(i, *_):
        return (jnp.minimum(i, np_tiles - 1), col)

    def s_map(i, *_):
        return (jnp.maximum(i - np_tiles, 0), col)

    return p_map, s_map


def _pair_value(i, np_tiles, p_ref, s_ref):
    return jnp.where(i < np_tiles, p_ref[...], s_ref[...])


def _mm_kernel(xp_ref, xs_ref, w_ref, o_ref, *, np_tiles):
    x = _pair_value(pl.program_id(0), np_tiles, xp_ref, xs_ref)
    o_ref[...] = jnp.dot(x.astype(BF16), w_ref[...], preferred_element_type=F32).astype(o_ref.dtype)


def _matmul(xp, xs, w, out_dtype, tm, tn, name):
    k = xp.shape[1]
    m = xp.shape[0] + xs.shape[0]
    n = w.shape[1]
    np_tiles = xp.shape[0] // tm
    p_map, s_map = _pair_maps(np_tiles)
    return pl.pallas_call(
        functools.partial(_mm_kernel, np_tiles=np_tiles),
        grid=(m // tm, n // tn),
        in_specs=[pl.BlockSpec((tm, k), p_map),
                  pl.BlockSpec((tm, k), s_map),
                  pl.BlockSpec((k, tn), lambda i, j: (0, j))],
        out_specs=pl.BlockSpec((tm, tn), lambda i, j: (i, j)),
        out_shape=jax.ShapeDtypeStruct((m, n), out_dtype),
        compiler_params=_cparams(("parallel", "arbitrary")),
        name=name,
    )(xp, xs, w)


def _split_bf16(x):
    hi = x.astype(BF16)
    lo = (x - hi.astype(F32)).astype(BF16)
    return hi, lo


def _gla_kernel(*refs, par, nb, c, sub, has_s0):
    q_refs, k_refs, v_refs, ga_refs, gk_refs = (refs[n * par:(n + 1) * par] for n in range(5))
    rest = refs[5 * par:]
    n_state = par * GLA_HEADS
    s_scr = [rest[len(rest) - n_state + p * GLA_HEADS:len(rest) - n_state + (p + 1) * GLA_HEADS]
             for p in range(par)]
    rest = rest[:len(rest) - n_state]
    if has_s0:
        wup_ref, bgk_ref, gn_ref, s0_ref, o_ref, sout_ref = rest
    else:
        wup_ref, bgk_ref, gn_ref, o_ref, sout_ref = rest
        s0_ref = None
    ci = pl.program_id(1)
    rows = nb * c

    @pl.when(ci == 0)
    def _():
        for p in range(par):
            for h in range(GLA_HEADS):
                if has_s0:
                    s_scr[p][h][...] = s0_ref[p * nb:(p + 1) * nb, h]
                else:
                    s_scr[p][h][...] = jnp.zeros_like(s_scr[p][h])

    r_i = lax.broadcasted_iota(I32, (rows, rows), 0)
    c_i = lax.broadcasted_iota(I32, (rows, rows), 1)
    same_seq = (r_i // c) == (c_i // c)
    causal = same_seq & (c_i <= r_i)
    tri = jnp.where(causal, 1.0, 0.0).astype(BF16)
    seg = jnp.where(same_seq, 1.0, 0.0).astype(BF16)
    ones_kv = jnp.ones((rows, GLA_DV), BF16)
    seq_of_row = lax.broadcasted_iota(I32, (rows, 1), 0) // c
    tn = (((0,), (0,)), ((), ()))

    def decays(p, rs):
        pre = jnp.dot(gk_refs[p][rs, :], wup_ref[...], preferred_element_type=F32) + bgk_ref[...]
        log_a = jax.nn.log_sigmoid(pre) / GLA_GATE_NORMALIZER
        la_hi, la_lo = _split_bf16(log_a)
        b = (jnp.dot(tri, la_hi, preferred_element_type=F32)
             + jnp.dot(tri, la_lo, preferred_element_type=F32))
        b_last = (jnp.dot(seg, la_hi, preferred_element_type=F32)
                  + jnp.dot(seg, la_lo, preferred_element_type=F32))
        q = q_refs[p][rs, :].astype(F32) * (GLA_DK ** -0.5)
        k = k_refs[p][rs, :].astype(F32)
        q_dec = (q * jnp.exp(b)).astype(BF16)
        k_dec = (k * jnp.exp(-b)).astype(BF16)
        k_rem = k * jnp.exp(b_last - b)
        return q_dec, k_dec, k_rem, la_hi, la_lo

    def head(p, rs, h, q_dec, k_dec, k_rem, la_hi, la_lo):
        ks = slice(h * GLA_DK, (h + 1) * GLA_DK)
        vh = v_refs[p][rs, h * GLA_DV:(h + 1) * GLA_DV]
        att = lax.dot_general(q_dec[:, ks], k_dec[:, ks], (((1,), (1,)), ((), ())),
                              preferred_element_type=F32)
        att = jnp.where(causal, att, 0.0).astype(BF16)
        o_h = jnp.dot(att, vh, preferred_element_type=F32)
        for j in range(nb):
            mine = (seq_of_row == j) if nb > 1 else None
            pick = (lambda a: jnp.where(mine, a, 0.0)) if nb > 1 else (lambda a: a)
            s_old = s_scr[p][h][j]
            o_h = o_h + pick(jnp.dot(q_dec[:, ks], s_old.astype(BF16), preferred_element_type=F32))
            dec = (lax.dot_general(pick(la_hi[:, ks].astype(F32)).astype(BF16), ones_kv, tn,
                                   preferred_element_type=F32)
                   + lax.dot_general(pick(la_lo[:, ks].astype(F32)).astype(BF16), ones_kv, tn,
                                     preferred_element_type=F32))
            upd = lax.dot_general(pick(k_rem[:, ks]).astype(BF16), vh, tn, preferred_element_type=F32)
            s_scr[p][h][j] = jnp.exp(dec) * s_old + upd
        return o_h * lax.rsqrt(jnp.mean(jnp.square(o_h), axis=-1, keepdims=True) + EPS) * gn_ref[...]

    for s in range(sub):
        rs = slice(s * rows, (s + 1) * rows)
        staged = [decays(p, rs) for p in range(par)]
        outs = [[] for _ in range(par)]
        for h in range(GLA_HEADS):
            for p in range(par):
                outs[p].append(head(p, rs, h, *staged[p]))
        for p in range(par):
            o = jnp.concatenate(outs[p], axis=-1) * jax.nn.silu(ga_refs[p][rs, :].astype(F32))
            o_ref[p, rs, :] = o.astype(o_ref.dtype)

    @pl.when(ci == pl.num_programs(1) - 1)
    def _():
        for p in range(par):
            for h in range(GLA_HEADS):
                sout_ref[p * nb:(p + 1) * nb, h] = s_scr[p][h][...]


def _gla(h, wup, bgk, gn, s0, *, row0, n_seq, seq_len, par, nb, c, sub, name):
    rows = nb * c * sub
    n_blocks = n_seq // nb
    n_steps = seq_len // (c * sub)
    rb0 = row0 // rows

    def rmap(p, col):
        return lambda g, i: (rb0 + (g * par + p) * n_steps + i, col)

    fields = ((GLA_KEY_DIM, H_QA), (GLA_KEY_DIM, H_KA), (GLA_VAL_DIM, H_VA), (GLA_VAL_DIM, H_GA), (LANES, H_GK))
    in_specs = [pl.BlockSpec((rows, width), rmap(p, off // width)) for width, off in fields for p in range(par)]
    in_specs += [pl.BlockSpec((LANES, GLA_KEY_DIM), lambda g, i: (0, 0)),
                 pl.BlockSpec((1, GLA_KEY_DIM), lambda g, i: (0, 0)),
                 pl.BlockSpec((1, GLA_DV), lambda g, i: (0, 0))]
    args = [h] * (5 * par) + [wup, bgk, gn]
    state_spec = pl.BlockSpec((par * nb, GLA_HEADS, GLA_DK, GLA_DV), lambda g, i: (g, 0, 0, 0))
    if s0 is not None:
        in_specs.append(state_spec)
        args.append(s0)
    kern = functools.partial(_gla_kernel, par=par, nb=nb, c=c, sub=sub, has_s0=s0 is not None)
    return pl.pallas_call(
        kern,
        grid=(n_blocks // par, n_steps),
        in_specs=in_specs,
        out_specs=[pl.BlockSpec((par, rows, GLA_VAL_DIM), lambda g, i: (g, i, 0)), state_spec],
        out_shape=[jax.ShapeDtypeStruct((n_blocks, nb * seq_len, GLA_VAL_DIM), BF16),
                   jax.ShapeDtypeStruct((n_seq, GLA_HEADS, GLA_DK, GLA_DV), F32)],
        scratch_shapes=[pltpu.VMEM((nb, GLA_DK, GLA_DV), F32)] * (par * GLA_HEADS),
        compiler_params=_cparams(("parallel", "arbitrary")),
        name=name,
    )(*args)


def _alibi_slope(head):
    return 2.0 ** (-8.0 * (head + 1) / SWA_HEADS)


def _swa_softmax_pv(parts, sink):
    m = sink
    for s, _ in parts:
        m = jnp.maximum(m, jnp.max(s, axis=-1, keepdims=True))
    denom = jnp.exp(sink - m)
    acc = None
    for s, v in parts:
        p = jnp.exp(s - m)
        denom = denom + jnp.sum(p, axis=-1, keepdims=True)
        pv = jnp.dot(p.astype(BF16), v, preferred_element_type=F32)
        acc = pv if acc is None else acc + pv
    return acc / denom


def _swa_prompt_kernel(sink_ref, q_ref, kp_ref, vp_ref, kc_ref, vc_ref, o_ref, bias):
    i = pl.program_id(1)
    span = 2 * WINDOW
    col = lax.broadcasted_iota(I32, (WINDOW, span), 1)

    @pl.when(i == 0)
    def _():
        dist_i = lax.broadcasted_iota(I32, (WINDOW, span), 0) + WINDOW - col
        in_window = (dist_i >= 0) & (dist_i < WINDOW)
        dist = dist_i.astype(F32)
        for hh in range(SWA_HEADS):
            bias[hh] = jnp.where(in_window, -_alibi_slope(hh) * dist, -jnp.inf)

    kcat = jnp.concatenate([kp_ref[...], kc_ref[...]], axis=0)
    vcat = jnp.concatenate([vp_ref[...], vc_ref[...]], axis=0)
    no_past = (col < WINDOW) & (i == 0)
    scale = SWA_HEAD_DIM ** -0.5
    outs = []
    for hh in range(SWA_HEADS):
        g = hh // SWA_GROUP
        gs = slice(g * SWA_HEAD_DIM, (g + 1) * SWA_HEAD_DIM)
        qh = q_ref[:, hh * SWA_HEAD_DIM:(hh + 1) * SWA_HEAD_DIM] * scale
        s = lax.dot_general(qh, kcat[:, gs], (((1,), (1,)), ((), ())), preferred_element_type=F32)
        s = jnp.where(no_past, -jnp.inf, s + bias[hh])
        outs.append(_swa_softmax_pv([(s, vcat[:, gs])], sink_ref[hh]))
    o_ref[...] = jnp.concatenate(outs, axis=-1).astype(o_ref.dtype)


def _swa_prompt(h, sinks, *, n_seq, seq_len, name):
    nq = seq_len // WINDOW
    qcol = H_QB // SWA_Q_DIM
    kcol = H_KB // SWA_KV_DIM
    vcol = H_VB // SWA_KV_DIM

    def cur(col):
        return lambda b, i, sk: (b * nq + i, col)

    def prev(col):
        return lambda b, i, sk: (b * nq + jnp.maximum(i - 1, 0), col)

    return pl.pallas_call(
        _swa_prompt_kernel,
        grid_spec=pltpu.PrefetchScalarGridSpec(
            num_scalar_prefetch=1,
            grid=(n_seq, nq),
            in_specs=[pl.BlockSpec((WINDOW, SWA_Q_DIM), cur(qcol)),
                      pl.BlockSpec((WINDOW, SWA_KV_DIM), prev(kcol)),
                      pl.BlockSpec((WINDOW, SWA_KV_DIM), prev(vcol)),
                      pl.BlockSpec((WINDOW, SWA_KV_DIM), cur(kcol)),
                      pl.BlockSpec((WINDOW, SWA_KV_DIM), cur(vcol))],
            out_specs=pl.BlockSpec((WINDOW, SWA_Q_DIM), lambda b, i, sk: (b * nq + i, 0)),
            scratch_shapes=[pltpu.VMEM((SWA_HEADS, WINDOW, 2 * WINDOW), F32)],
        ),
        out_shape=jax.ShapeDtypeStruct((n_seq * seq_len, SWA_Q_DIM), BF16),
        compiler_params=_cparams(("parallel", "arbitrary")),
        name=name,
    )(sinks, h, h, h, h, h)


def _swa_sample_kernel(sink_ref, q_ref, kc_ref, vc_ref, kp_ref, vp_ref, o_ref, *, nb, lq):
    rows = SWA_HEADS * lq
    grp_rows = SWA_GROUP * lq
    hd = SWA_HEAD_DIM
    nt = (((1,), (1,)), ((), ()))
    head_of_row = lax.broadcasted_iota(I32, (rows, 1), 0) // lq
    slope = jnp.zeros((rows, 1), F32)
    sink = jnp.zeros((rows, 1), F32)
    for hh in range(SWA_HEADS):
        slope = jnp.where(head_of_row == hh, _alibi_slope(hh), slope)
        sink = jnp.where(head_of_row == hh, sink_ref[hh], sink)
    qi = lax.broadcasted_iota(I32, (rows, WINDOW), 0) % lq
    dist_p = qi + WINDOW - lax.broadcasted_iota(I32, (rows, WINDOW), 1)
    valid_p = dist_p < WINDOW
    bias_p = slope * dist_p.astype(F32)
    dist_c = lax.broadcasted_iota(I32, (rows, lq), 0) % lq - lax.broadcasted_iota(I32, (rows, lq), 1)
    valid_c = dist_c >= 0
    bias_c = slope * dist_c.astype(F32)
    scale = hd ** -0.5

    q_all = q_ref[...].astype(F32)
    kc_all = kc_ref[...].astype(F32)
    vc_all = vc_ref[...].astype(F32)
    seq_outs = []
    for j in range(nb):
        js = slice(j * lq, (j + 1) * lq)
        pieces = []
        for hh in range(SWA_HEADS):
            g = hh // SWA_GROUP
            parts = []
            if g:
                parts.append(jnp.zeros((lq, g * hd), F32))
            parts.append(q_all[js, hh * hd:(hh + 1) * hd])
            if g < SWA_KV_HEADS - 1:
                parts.append(jnp.zeros((lq, (SWA_KV_HEADS - 1 - g) * hd), F32))
            pieces.append(jnp.concatenate(parts, axis=-1))
        q_big = jnp.concatenate(pieces, axis=0).astype(BF16)
        kp = kp_ref[j].astype(BF16)
        vp = vp_ref[j].astype(BF16)
        kcj = kc_all[js, :].astype(BF16)
        vcj = vc_all[js, :].astype(BF16)
        s_p = lax.dot_general(q_big, kp, nt, preferred_element_type=F32)
        s_p = jnp.where(valid_p, s_p * scale - bias_p, -jnp.inf)
        s_c = lax.dot_general(q_big, kcj, nt, preferred_element_type=F32)
        s_c = jnp.where(valid_c, s_c * scale - bias_c, -jnp.inf)
        o_big = _swa_softmax_pv([(s_p, vp), (s_c, vcj)], sink)
        o_grp = [o_big[g * grp_rows:(g + 1) * grp_rows, g * hd:(g + 1) * hd] for g in range(SWA_KV_HEADS)]
        o_heads = jnp.concatenate(o_grp, axis=0)
        seq_outs.append(jnp.concatenate([o_heads[hh * lq:(hh + 1) * lq, :] for hh in range(SWA_HEADS)],
                                        axis=-1))
    o_ref[...] = jnp.concatenate(seq_outs, axis=0).astype(o_ref.dtype)


def _swa_sample(h, sinks, k_past, v_past, *, row0, n_seq, lq, nb, name):
    rows = nb * lq
    rb0 = row0 // rows
    qcol = H_QB // SWA_Q_DIM
    kcol = H_KB // SWA_KV_DIM
    vcol = H_VB // SWA_KV_DIM
    kern = functools.partial(_swa_sample_kernel, nb=nb, lq=lq)
    past_spec = pl.BlockSpec((nb, WINDOW, SWA_KV_DIM), lambda g, sk: (g, 0, 0))
    return pl.pallas_call(
        kern,
        grid_spec=pltpu.PrefetchScalarGridSpec(
            num_scalar_prefetch=1,
            grid=(n_seq // nb,),
            in_specs=[pl.BlockSpec((rows, SWA_Q_DIM), lambda g, sk: (rb0 + g, qcol)),
                      pl.BlockSpec((rows, SWA_KV_DIM), lambda g, sk: (rb0 + g, kcol)),
                      pl.BlockSpec((rows, SWA_KV_DIM), lambda g, sk: (rb0 + g, vcol)),
                      past_spec, past_spec],
            out_specs=pl.BlockSpec((rows, SWA_Q_DIM), lambda g, sk: (g, 0)),
        ),
        out_shape=jax.ShapeDtypeStruct((n_seq * lq, SWA_Q_DIM), BF16),
        compiler_params=_cparams(("parallel",)),
        name=name,
    )(sinks, h, h, h, k_past, v_past)


def _layer_norm(x, g, b):
    mu = jnp.mean(x, axis=-1, keepdims=True)
    xc = x - mu
    var = jnp.mean(jnp.square(xc), axis=-1, keepdims=True)
    return xc * lax.rsqrt(var + EPS) * g + b


def _pack_bf16_pairs(x, s):
    lo = lax.bitcast_convert_type(x[:, (2 * s) * LANES:(2 * s + 1) * LANES].astype(BF16).astype(F32), I32)
    hi = lax.bitcast_convert_type(x[:, (2 * s + 1) * LANES:(2 * s + 2) * LANES].astype(BF16).astype(F32), I32)
    return lax.shift_right_logical(lo, 16) | (hi & HIGH_HALF)


def _unpack_bf16_pairs(w):
    lo = lax.bitcast_convert_type(lax.shift_left(w, 16), F32).astype(BF16)
    hi = lax.bitcast_convert_type(w & HIGH_HALF, F32).astype(BF16)
    return lo, hi


def _merge_kernel(oap_ref, oas_ref, obp_ref, obs_ref, ga_ref, gb_ref, xp_ref, xs_ref,
                  wpa_ref, wpb_ref, wout_ref, g_ref, b_ref, o_ref, xw_ref, *, np_tiles):
    i = pl.program_id(0)
    tm = o_ref.shape[0]
    br_a = jnp.dot(_pair_value(i, np_tiles, oap_ref, oas_ref), wpa_ref[...], preferred_element_type=F32)
    br_b = jnp.dot(_pair_value(i, np_tiles, obp_ref, obs_ref), wpb_ref[...], preferred_element_type=F32)
    merged = (jax.nn.sigmoid(ga_ref[...].astype(F32)) * br_a
              + jax.nn.sigmoid(gb_ref[...].astype(F32)) * br_b)
    mix = jnp.dot(merged.astype(BF16), wout_ref[...], preferred_element_type=F32)
    x = _pair_value(i, np_tiles, xp_ref, xs_ref)
    x1 = _layer_norm(DEEPNORM_ALPHA * x + mix, g_ref[...], b_ref[...])
    o_ref[...] = x1
    for s in range(PACKED_ROWS):
        xw_ref[pl.ds(s, tm, stride=PACKED_ROWS), :] = _pack_bf16_pairs(x1, s)


def _merge(oa, ob, h, x, wpa, wpb, wout, g, b, *, tm, name):
    t = h.shape[0]
    np_tiles = x[0].shape[0] // tm
    p_map, s_map = _pair_maps(np_tiles)
    row = lambda i: (i, 0)
    full = lambda i: (0, 0)
    pair = [pl.BlockSpec((tm, D_MODEL), p_map), pl.BlockSpec((tm, D_MODEL), s_map)]
    return pl.pallas_call(
        functools.partial(_merge_kernel, np_tiles=np_tiles),
        grid=(t // tm,),
        in_specs=pair + pair + [
                  pl.BlockSpec((tm, D_MODEL), lambda i: (i, H_GATE_A // D_MODEL)),
                  pl.BlockSpec((tm, D_MODEL), lambda i: (i, H_GATE_B // D_MODEL))] + pair + [
                  pl.BlockSpec((D_MODEL, D_MODEL), full),
                  pl.BlockSpec((D_MODEL, D_MODEL), full),
                  pl.BlockSpec((D_MODEL, D_MODEL), full),
                  pl.BlockSpec((1, D_MODEL), full),
                  pl.BlockSpec((1, D_MODEL), full)],
        out_specs=[pl.BlockSpec((tm, D_MODEL), row),
                   pl.BlockSpec((tm * PACKED_ROWS, LANES), row)],
        out_shape=[jax.ShapeDtypeStruct((t, D_MODEL), F32),
                   jax.ShapeDtypeStruct((t * PACKED_ROWS, LANES), I32)],
        compiler_params=_cparams(("parallel",)),
        name=name,
    )(*oa, *ob, h, h, *x, wpa, wpb, wout, g, b)


def _router_kernel(x_ref, whi_ref, wlo_ref, bias_ref, idx_ref, wl_ref, rank_ref, cntc_ref, cntr_ref,
                   cntc_scr, cntr_scr):
    i = pl.program_id(0)
    tm = x_ref.shape[0]

    @pl.when(i == 0)
    def _():
        cntc_scr[...] = jnp.zeros_like(cntc_scr)
        cntr_scr[...] = jnp.zeros_like(cntr_scr)

    x_hi, x_lo = _split_bf16(x_ref[...])
    nt = (((1,), (1,)), ((), ()))
    logits = (lax.dot_general(whi_ref[...], x_hi, nt, preferred_element_type=F32)
              + lax.dot_general(whi_ref[...], x_lo, nt, preferred_element_type=F32)
              + lax.dot_general(wlo_ref[...], x_hi, nt, preferred_element_type=F32))
    scores = jax.nn.sigmoid(logits)
    biased = scores + bias_ref[...]

    grouped = biased.reshape(N_GROUPS, GROUP_SIZE, tm)
    m1 = jnp.max(grouped, axis=1)
    n_top = jnp.sum(jnp.where(grouped == m1[:, None, :], 1.0, 0.0), axis=1)
    m2 = jnp.max(jnp.where(grouped < m1[:, None, :], grouped, -jnp.inf), axis=1)
    gscore = m1 + jnp.where(n_top >= 2.0, m1, m2)

    g_iota = lax.broadcasted_iota(I32, (N_GROUPS, tm), 0)
    beaten = jnp.zeros((N_GROUPS, tm), I32)
    for g in range(N_GROUPS):
        other = gscore[g:g + 1, :]
        ahead = (other > gscore) | ((other == gscore) & (g < g_iota))
        beaten = beaten + jnp.where(ahead, 1, 0)
    keep = jnp.where(beaten < TOPK_GROUPS, 1.0, 0.0)
    masked = jnp.where(keep[:, None, :] > 0.5, grouped, -jnp.inf).reshape(N_EXPERTS, tm)

    e_iota = lax.broadcasted_iota(I32, (N_EXPERTS, tm), 0)
    sel_f = jnp.zeros((N_EXPERTS, tm), F32)
    ids = []
    for _ in range(TOP_K):
        best = jnp.max(masked, axis=0, keepdims=True)
        idx = jnp.min(jnp.where(masked == best, e_iota, N_EXPERTS), axis=0, keepdims=True)
        hit = e_iota == idx
        sel_f = sel_f + jnp.where(hit, 1.0, 0.0)
        masked = jnp.where(hit, -jnp.inf, masked)
        ids.append(idx)

    top_sum = jnp.sum(sel_f * scores, axis=0, keepdims=True)

    t_r = lax.broadcasted_iota(I32, (tm, tm), 0)
    t_c = lax.broadcasted_iota(I32, (tm, tm), 1)
    before = jnp.where(t_r < t_c, 1.0, 0.0).astype(BF16)
    sel_b = sel_f.astype(BF16)
    rank = jnp.dot(sel_b, before, preferred_element_type=F32) + cntc_scr[...]
    cntc_scr[...] = cntc_scr[...] + jnp.sum(sel_f, axis=1, keepdims=True)
    cntr_scr[...] = cntr_scr[...] + lax.dot_general(jnp.ones((SUBLANES, tm), BF16), sel_b, nt,
                                                    preferred_element_type=F32)

    idx_rows, w_rows, rank_rows = [], [], []
    for idx in ids:
        hit = e_iota == idx
        w = jnp.sum(jnp.where(hit, scores, 0.0), axis=0, keepdims=True)
        w_rows.append(w / top_sum * ROUTED_SCALE)
        rank_rows.append(jnp.sum(jnp.where(hit, rank, 0.0), axis=0, keepdims=True))
        idx_rows.append(idx)
    idx_ref[...] = jnp.concatenate(idx_rows, axis=0)
    rank_ref[...] = jnp.concatenate(rank_rows, axis=0).astype(I32)

    w_all = jnp.concatenate(w_rows, axis=0)
    w_1 = w_all.astype(BF16)
    r_1 = w_all - w_1.astype(F32)
    w_2 = r_1.astype(BF16)
    w_3 = (r_1 - w_2.astype(F32)).astype(BF16)
    terms = jnp.concatenate([w_1, w_2, w_3, jnp.zeros_like(w_1)], axis=0)
    t_row = lax.broadcasted_iota(I32, (4 * TOP_K, TOP_K * LANES), 0)
    t_col = lax.broadcasted_iota(I32, (4 * TOP_K, TOP_K * LANES), 1)
    spread = jnp.where((t_row % TOP_K) == (t_col // LANES), 1.0, 0.0).astype(BF16)
    w_lanes = lax.dot_general(terms, spread, (((0,), (0,)), ((), ())), preferred_element_type=F32)
    for k in range(TOP_K):
        wl_ref[pl.ds(k, tm, stride=TOP_K), :] = w_lanes[:, k * LANES:(k + 1) * LANES]

    @pl.when(i == pl.num_programs(0) - 1)
    def _():
        cntc_ref[...] = cntc_scr[...].astype(I32)
        cntr_ref[...] = cntr_scr[...].astype(I32)


def _router(x1, w_hi, w_lo, bias, *, tm, name):
    t = x1.shape[0]
    full = lambda i: (0, 0)
    tok = lambda i: (0, i)
    return pl.pallas_call(
        _router_kernel,
        grid=(t // tm,),
        in_specs=[pl.BlockSpec((tm, D_MODEL), lambda i: (i, 0)),
                  pl.BlockSpec((N_EXPERTS, D_MODEL), full),
                  pl.BlockSpec((N_EXPERTS, D_MODEL), full),
                  pl.BlockSpec((N_EXPERTS, 1), full)],
        out_specs=[pl.BlockSpec((TOP_K, tm), tok),
                   pl.BlockSpec((TOP_K * tm, LANES), lambda i: (i, 0)),
                   pl.BlockSpec((TOP_K, tm), tok),
                   pl.BlockSpec((N_EXPERTS, 1), full),
                   pl.BlockSpec((SUBLANES, N_EXPERTS), full)],
        out_shape=[jax.ShapeDtypeStruct((TOP_K, t), I32),
                   jax.ShapeDtypeStruct((TOP_K * t, LANES), F32),
                   jax.ShapeDtypeStruct((TOP_K, t), I32),
                   jax.ShapeDtypeStruct((N_EXPERTS, 1), I32),
                   jax.ShapeDtypeStruct((SUBLANES, N_EXPERTS), I32)],
        scratch_shapes=[pltpu.VMEM((N_EXPERTS, 1), F32),
                        pltpu.VMEM((SUBLANES, N_EXPERTS), F32)],
        compiler_params=_cparams(("arbitrary",)),
        name=name,
    )(x1, w_hi, w_lo, bias)


ITEM_FIELDS = 8


def _byte_split(v):
    return lax.shift_right_logical(v, 8).astype(F32), (v & 255).astype(F32)


def _plan_kernel(cntc_ref, cntr_ref, idx_ref, rank_ref, meta_ref, items_ref, m_scr, *, n_tok):
    i = pl.program_id(0)
    tm = idx_ref.shape[1]
    nbm = m_scr.shape[0]
    nip = items_ref.shape[1]
    nbp = nip - N_EXPERTS
    n_rows = n_tok * TOP_K
    n_blk = n_rows // EXPERT_ROWS
    nt = (((1,), (1,)), ((), ()))

    r_i = lax.broadcasted_iota(I32, (N_EXPERTS, N_EXPERTS), 0)
    c_i = lax.broadcasted_iota(I32, (N_EXPERTS, N_EXPERTS), 1)
    below = jnp.where(c_i < r_i, 1.0, 0.0).astype(BF16)
    ones_c = jnp.ones((N_EXPERTS, LANES), F32)
    c_hi, c_lo = _byte_split(cntc_ref[...])
    start_col = (256.0 * jnp.dot(below, (c_hi * ones_c).astype(BF16), preferred_element_type=F32)
                 + jnp.dot(below, (c_lo * ones_c).astype(BF16), preferred_element_type=F32))[:, 0:1]

    @pl.when(i == 0)
    def _():
        m_scr[...] = jnp.zeros_like(m_scr)
        above = jnp.where(r_i < c_i, 1.0, 0.0).astype(BF16)
        r_hi, r_lo = _byte_split(cntr_ref[...])
        start_row = (256.0 * jnp.dot(r_hi.astype(BF16), above, preferred_element_type=F32)
                     + jnp.dot(r_lo.astype(BF16), above, preferred_element_type=F32))[0:1, :]
        b_col = lax.broadcasted_iota(I32, (nbp, 1), 0)
        b_row = lax.broadcasted_iota(I32, (1, nbp), 1)
        blk_col = jnp.where(b_col < n_blk, b_col * EXPERT_ROWS, n_rows).astype(F32)
        blk_row = jnp.where(b_row < n_blk, b_row * EXPERT_ROWS, n_rows).astype(F32)
        v_col = jnp.concatenate([blk_col, start_col], axis=0)
        v_row = jnp.concatenate([blk_row, start_row], axis=1)
        j_col = lax.broadcasted_iota(I32, (nip, 1), 0)
        k_row = lax.broadcasted_iota(I32, (1, nip), 1)
        ahead = (v_row < v_col) | ((v_row == v_col) & (k_row < j_col))
        order_col = jnp.sum(jnp.where(ahead, 1.0, 0.0), axis=1, keepdims=True)
        pos_row = k_row.astype(F32)
        lo_abs = jnp.sum(jnp.where(order_col == pos_row, v_col, 0.0), axis=0, keepdims=True)
        hi_abs = jnp.sum(jnp.where(order_col == pos_row + 1.0, v_col, 0.0), axis=0, keepdims=True)
        hi_abs = jnp.where(k_row == nip - 1, float(n_rows), hi_abs)
        blk = jnp.minimum(jnp.floor(lo_abs * (1.0 / EXPERT_ROWS)), n_blk - 1.0)
        expert = jnp.sum(jnp.where(start_col <= lo_abs, 1.0, 0.0), axis=0, keepdims=True) - 1.0
        base = blk * EXPERT_ROWS
        e_col = lax.broadcasted_iota(I32, (N_EXPERTS, 1), 0).astype(F32)
        end_col = start_col + cntc_ref[...].astype(F32)
        seg_end = jnp.sum(jnp.where(e_col == expert, end_col, 0.0), axis=0, keepdims=True)
        follower = jnp.sum(jnp.where(start_col <= seg_end, 1.0, 0.0), axis=0, keepdims=True) - 1.0
        follower = jnp.where(seg_end < float(n_rows), follower, -1.0)
        fields = [blk, expert, lo_abs - base, hi_abs - base, follower]
        fields.append(jnp.zeros((ITEM_FIELDS - len(fields), nip), F32))
        items_ref[...] = jnp.concatenate(fields, axis=0).astype(I32)

    e_iota = lax.broadcasted_iota(I32, (N_EXPERTS, tm), 0)
    b_iota = lax.broadcasted_iota(I32, (nbm, tm), 0)
    l_iota = lax.broadcasted_iota(I32, (EXPERT_ROWS, tm), 0)
    tok = i * tm + lax.broadcasted_iota(I32, (1, tm), 1)
    tok_hi, tok_lo = _byte_split(tok)
    acc = jnp.zeros(m_scr.shape, F32)
    for k in range(TOP_K):
        hit = e_iota == idx_ref[k:k + 1, :]
        dest = (jnp.sum(jnp.where(hit, start_col, 0.0), axis=0, keepdims=True).astype(I32)
                + rank_ref[k:k + 1, :])
        oh_blk = jnp.where(b_iota == lax.shift_right_logical(dest, 8), 1.0, 0.0).astype(BF16)
        in_blk = l_iota == (dest & (EXPERT_ROWS - 1))
        vals = jnp.concatenate([jnp.where(in_blk, tok_hi, 0.0), jnp.where(in_blk, tok_lo, 0.0),
                                jnp.where(in_blk, float(k), 0.0)], axis=0).astype(BF16)
        acc = acc + lax.dot_general(oh_blk, vals, nt, preferred_element_type=F32)
    m_scr[...] = m_scr[...] + acc

    @pl.when(i == pl.num_programs(0) - 1)
    def _():
        m = m_scr[...]
        row_tok = m[:, 0:EXPERT_ROWS] * 256.0 + m[:, EXPERT_ROWS:2 * EXPERT_ROWS]
        row_slot = m[:, 2 * EXPERT_ROWS:3 * EXPERT_ROWS] * float(n_tok) + row_tok
        meta_ref[...] = jnp.concatenate([row_tok * float(PACKED_ROWS), row_slot * float(ROW_CHUNKS)],
                                        axis=1).astype(I32)


def _plan(cnt_col, cnt_row, idx, rank, *, tm, name):
    t = idx.shape[1]
    assert EXPERT_ROWS == 256 and (t * TOP_K) % (2 * EXPERT_ROWS) == 0
    n_blk = t * TOP_K // EXPERT_ROWS
    nbm = -(-n_blk // 16) * 16
    nip = -(-n_blk // LANES) * LANES + N_EXPERTS
    tok = lambda i: (0, i)
    full = lambda i: (0, 0)
    return pl.pallas_call(
        functools.partial(_plan_kernel, n_tok=t),
        grid=(t // tm,),
        in_specs=[pl.BlockSpec((N_EXPERTS, 1), full),
                  pl.BlockSpec((SUBLANES, N_EXPERTS), full),
                  pl.BlockSpec((TOP_K, tm), tok),
                  pl.BlockSpec((TOP_K, tm), tok)],
        out_specs=[pl.BlockSpec((nbm, 2 * EXPERT_ROWS), full),
                   pl.BlockSpec((ITEM_FIELDS, nip), full)],
        out_shape=[jax.ShapeDtypeStruct((nbm, 2 * EXPERT_ROWS), I32),
                   jax.ShapeDtypeStruct((ITEM_FIELDS, nip), I32)],
        scratch_shapes=[pltpu.VMEM((nbm, 3 * EXPERT_ROWS), F32)],
        compiler_params=_cparams(("arbitrary",)),
        name=name,
    )(cnt_col, cnt_row, idx, rank)


META_CHUNK = 4 * EXPERT_ROWS
GATHER_STRIDE = EXPERT_ROWS + 8
META_RING = 3
FFN_PIECES = 4


def _experts_kernel(items_ref, xw_hbm, meta_hbm, wg_hbm, wu_hbm, wd_hbm, ys_hbm,
                    xw, meta, tile, lhs, wg_f, wu_f, wd_f, wg_b, wu_b, wd_b, yacc, ybuf, state,
                    sem_x, sem_m, sem_y, sem_w, *, nip, n_blk):
    i = pl.program_id(0)
    blk = items_ref[i]
    expert = items_ref[nip + i]
    lo = items_ref[2 * nip + i]
    hi = items_ref[3 * nip + i]
    follower = items_ref[4 * nip + i]
    nonempty = hi > lo
    par = blk & 1
    chunk = lax.shift_right_logical(blk, 1)

    def meta_base(b):
        return (lax.shift_right_logical(b, 1) % META_RING) * META_CHUNK + (b & 1) * (2 * EXPERT_ROWS)

    mbase = meta_base(blk)
    prev_base = meta_base(jnp.maximum(blk - 1, 0))

    def meta_copy(c):
        slot = c % META_RING
        return pltpu.make_async_copy(meta_hbm.at[pl.ds(pl.multiple_of(c * META_CHUNK, META_CHUNK), META_CHUNK)],
                                     meta.at[pl.ds(pl.multiple_of(slot * META_CHUNK, META_CHUNK), META_CHUNK)],
                                     sem_m.at[slot])

    def ybuf_drain(slot):
        pltpu.make_async_copy(ys_hbm.at[pl.ds(0, EXPERT_ROWS * ROW_CHUNKS), :], ybuf.at[slot],
                              sem_y.at[slot]).wait()

    def ybuf_fill(slot):
        for j in range(ROW_CHUNKS):
            ybuf[slot, pl.ds(j, EXPERT_ROWS, stride=ROW_CHUNKS), :] = yacc[:, j * LANES:(j + 1) * LANES]

    def send_rows(slot, base, r0, r1):
        for r in range(r0, r1):
            src = ybuf.at[slot, pl.ds(r * ROW_CHUNKS, ROW_CHUNKS), :]
            dst_row = pl.multiple_of(meta[base + EXPERT_ROWS + r], ROW_CHUNKS)
            pltpu.make_async_copy(src, ys_hbm.at[pl.ds(dst_row, ROW_CHUNKS), :],
                                  sem_y.at[slot]).start(priority=r % 2)

    def weight_copies(e, slot):
        return [pltpu.make_async_copy(src.at[e], dst.at[slot], sem_w.at[slot])
                for src, dst in ((wg_hbm, wg_f), (wu_hbm, wu_f), (wd_hbm, wd_f))]

    @pl.when(i == 0)
    def _():
        resident = pltpu.make_async_copy(xw_hbm, xw, sem_x)
        resident.start()
        meta_copy(0).start()
        for cp in weight_copies(expert, 0):
            cp.start()
        state[0] = -1
        state[1] = 1
        resident.wait()

    @pl.when(nonempty & (expert != state[0]))
    def _():
        slot = 1 - state[1]
        for cp in weight_copies(expert, slot):
            cp.wait()
        state[0] = expert
        state[1] = slot

        @pl.when(follower >= 0)
        def _():
            for cp in weight_copies(follower, 1 - slot):
                cp.start()

        wg_b[...] = wg_f[slot].astype(BF16)
        wu_b[...] = wu_f[slot].astype(BF16)
        wd_b[...] = wd_f[slot].astype(BF16)

    @pl.when(nonempty & (lo == 0))
    def _():
        @pl.when(par == 0)
        def _():
            meta_copy(chunk).wait()

            @pl.when(2 * (chunk + 1) < n_blk)
            def _():
                meta_copy(chunk + 1).start()

        for r in range(EXPERT_ROWS):
            t4 = pl.multiple_of(meta[mbase + r], PACKED_ROWS)
            tile[pl.ds(r, PACKED_ROWS, stride=GATHER_STRIDE), :] = xw[pl.ds(t4, PACKED_ROWS), :]
        cols = []
        for s in range(PACKED_ROWS):
            cols.extend(_unpack_bf16_pairs(tile[pl.ds(s * GATHER_STRIDE, EXPERT_ROWS), :]))
        lhs[...] = jnp.concatenate(cols, axis=-1)

    def ffn(x, between=None):
        half_ff = EXPERT_FF // 2
        half_d = D_MODEL // 2
        hid = []
        for c in range(2):
            gate = jnp.dot(x, wg_b[:, c * half_ff:(c + 1) * half_ff], preferred_element_type=F32)
            up = jnp.dot(x, wu_b[:, c * half_ff:(c + 1) * half_ff], preferred_element_type=F32)
            hid.append((jax.nn.silu(gate) * up).astype(BF16))
            if between is not None:
                between(c)
        hid = jnp.concatenate(hid, axis=-1)
        out = []
        for c in range(2):
            out.append(jnp.dot(hid, wd_b[:, c * half_d:(c + 1) * half_d], preferred_element_type=F32))
            if between is not None:
                between(2 + c)
        return jnp.concatenate(out, axis=-1)

    def run_mask(y):
        row = lax.broadcasted_iota(I32, (EXPERT_ROWS, 1), 0)
        return jnp.where((row >= lo) & (row < hi), y, 0.0)

    whole = (lo == 0) & (hi == EXPERT_ROWS)
    first = nonempty & (lo == 0)
    has_prev = blk > 0
    rows_per_piece = EXPERT_ROWS // FFN_PIECES

    for prev_slot in range(2):
        def send_piece(g, prev_slot=prev_slot):
            send_rows(prev_slot, prev_base, g * rows_per_piece, (g + 1) * rows_per_piece)

        sends = has_prev & (par == 1 - prev_slot)

        @pl.when(whole & sends)
        def _():
            yacc[...] = ffn(lhs[...], send_piece)

        @pl.when(first & jnp.logical_not(whole) & sends)
        def _():
            yacc[...] = run_mask(ffn(lhs[...], send_piece))

    @pl.when(whole & jnp.logical_not(has_prev))
    def _():
        yacc[...] = ffn(lhs[...])

    @pl.when(nonempty & jnp.logical_not(whole) & jnp.logical_not(first & has_prev))
    def _():
        y = run_mask(ffn(lhs[...]))

        @pl.when(lo == 0)
        def _():
            yacc[...] = y

        @pl.when(lo > 0)
        def _():
            yacc[...] = yacc[...] + y

    @pl.when(nonempty & (hi == EXPERT_ROWS))
    def _():
        @pl.when(blk >= 2)
        def _():
            ybuf_drain(par)

        for slot in range(2):
            @pl.when(par == slot)
            def _():
                ybuf_fill(slot)

                @pl.when(blk == n_blk - 1)
                def _():
                    send_rows(slot, mbase, 0, EXPERT_ROWS)

    @pl.when(i == pl.num_programs(0) - 1)
    def _():
        ybuf_drain(0)
        ybuf_drain(1)


def _experts(items, xw, meta, wg, wu, wd, *, n_tok, name):
    nip = items.shape[0] // ITEM_FIELDS
    n_rows = n_tok * TOP_K
    n_blk = n_rows // EXPERT_ROWS
    n_items = n_blk + N_EXPERTS
    kern = functools.partial(_experts_kernel, nip=nip, n_blk=n_blk)
    any_spec = pl.BlockSpec(memory_space=pl.ANY)
    return pl.pallas_call(
        kern,
        grid_spec=pltpu.PrefetchScalarGridSpec(
            num_scalar_prefetch=1,
            grid=(n_items,),
            in_specs=[any_spec] * 5,
            out_specs=any_spec,
            scratch_shapes=[pltpu.VMEM(xw.shape, I32),
                            pltpu.SMEM((META_RING * META_CHUNK,), I32),
                            pltpu.VMEM((PACKED_ROWS * GATHER_STRIDE, LANES), I32),
                            pltpu.VMEM((EXPERT_ROWS, D_MODEL), BF16),
                            pltpu.VMEM((2, D_MODEL, EXPERT_FF), F32),
                            pltpu.VMEM((2, D_MODEL, EXPERT_FF), F32),
                            pltpu.VMEM((2, EXPERT_FF, D_MODEL), F32),
                            pltpu.VMEM((D_MODEL, EXPERT_FF), BF16),
                            pltpu.VMEM((D_MODEL, EXPERT_FF), BF16),
                            pltpu.VMEM((EXPERT_FF, D_MODEL), BF16),
                            pltpu.VMEM((EXPERT_ROWS, D_MODEL), F32),
                            pltpu.VMEM((2, EXPERT_ROWS * ROW_CHUNKS, LANES), F32),
                            pltpu.SMEM((2,), I32),
                            pltpu.SemaphoreType.DMA,
                            pltpu.SemaphoreType.DMA((META_RING,)),
                            pltpu.SemaphoreType.DMA((2,)),
                            pltpu.SemaphoreType.DMA((2,))],
        ),
        out_shape=jax.ShapeDtypeStruct((n_rows * ROW_CHUNKS, LANES), F32),
        compiler_params=_cparams(("arbitrary",)),
        name=name,
    )(items, xw, meta, wg, wu, wd)


def _combine_kernel(ys_ref, x_ref, wl_ref, wsg_ref, wsu_ref, wsd_ref, g_ref, b_ref, op_ref, os_ref,
                    routed, *, np_tiles):
    x = x_ref[...]
    xb = x.astype(BF16)
    hid = (jax.nn.silu(jnp.dot(xb, wsg_ref[...], preferred_element_type=F32))
           * jnp.dot(xb, wsu_ref[...], preferred_element_type=F32))
    shared = jnp.dot(hid.astype(BF16), wsd_ref[...], preferred_element_type=F32)
    acc = None
    for k in range(TOP_K):
        part = wl_ref[:, k:k + 1, :] * ys_ref[k]
        acc = part if acc is None else acc + part
    routed[...] = acc
    ffn = jnp.concatenate([routed[:, j, :] for j in range(ROW_CHUNKS)], axis=-1) + shared
    out = _layer_norm(DEEPNORM_ALPHA * x + ffn, g_ref[...], b_ref[...])
    i = pl.program_id(0)

    @pl.when(i < np_tiles)
    def _():
        op_ref[...] = out

    @pl.when(i >= np_tiles)
    def _():
        os_ref[...] = out


def _combine(ys, x1, w_tok, wsg, wsu, wsd, g, b, *, n_prompt, tm, name):
    t = x1.shape[0]
    np_tiles = n_prompt // tm
    p_map, s_map = _pair_maps(np_tiles)
    row = lambda i: (i, 0)
    full = lambda i: (0, 0)
    return pl.pallas_call(
        functools.partial(_combine_kernel, np_tiles=np_tiles),
        grid=(t // tm,),
        in_specs=[pl.BlockSpec((TOP_K, tm, ROW_CHUNKS, LANES), lambda i: (0, i, 0, 0)),
                  pl.BlockSpec((tm, D_MODEL), row),
                  pl.BlockSpec((tm, TOP_K, LANES), lambda i: (i, 0, 0)),
                  pl.BlockSpec((D_MODEL, SHARED_FF), full),
                  pl.BlockSpec((D_MODEL, SHARED_FF), full),
                  pl.BlockSpec((SHARED_FF, D_MODEL), full),
                  pl.BlockSpec((1, D_MODEL), full),
                  pl.BlockSpec((1, D_MODEL), full)],
        out_specs=[pl.BlockSpec((tm, D_MODEL), p_map), pl.BlockSpec((tm, D_MODEL), s_map)],
        out_shape=[jax.ShapeDtypeStruct((n_prompt, D_MODEL), F32),
                   jax.ShapeDtypeStruct((t - n_prompt, D_MODEL), F32)],
        scratch_shapes=[pltpu.VMEM((tm, ROW_CHUNKS, LANES), F32)],
        compiler_params=_cparams(("arbitrary",)),
        name=name,
    )(ys, x1, w_tok, wsg, wsu, wsd, g, b)


def _pack_w_in(w_in):
    sizes = (GLA_KEY_DIM, GLA_KEY_DIM, GLA_VAL_DIM, GLA_VAL_DIM, GLA_GATE_RANK,
             SWA_Q_DIM, SWA_KV_DIM, SWA_KV_DIM, D_MODEL, D_MODEL)
    offs = [0]
    for s in sizes:
        offs.append(offs[-1] + s)
    qa, ka, va, ga, gk, qb, kb, vb, gate_a, gate_b = (w_in[:, offs[i]:offs[i + 1]] for i in range(10))
    pad = lambda w, n: jnp.pad(w, ((0, 0), (0, n - w.shape[1])))
    packed = jnp.concatenate([qa, ka, va, ga, qb, gate_a, gate_b, kb, vb, pad(gk, 2 * LANES)], axis=1)
    assert packed.shape[1] == H_WIDTH
    return packed.astype(BF16), jnp.concatenate([kb, vb], axis=1).astype(BF16)


def kernel(x_prompt, x_sample, state_gla, cache_swa_k, cache_swa_v, w_in, w_gk_up, b_gk, gla_norm_g,
           attn_sinks, w_proj_a, w_proj_b, w_out, ln1_g, ln1_b, w_router, router_bias,
           w_expert_gate, w_expert_up, w_expert_down, w_shared_gate, w_shared_up, w_shared_down,
           ln2_g, ln2_b):
    assert w_in.shape[0] == 1, "single-layer trunk"
    bp, lp, d = x_prompt.shape
    bs, ls, _ = x_sample.shape
    assert d == D_MODEL and ls == SUBLANES and cache_swa_k.shape[2] == WINDOW
    tp, ts = bp * lp, bs * ls
    t = tp + ts

    xp = x_prompt.reshape(tp, d)
    xs = x_sample.reshape(ts, d)
    w_main, w_kv = _pack_w_in(w_in[0])
    h = _matmul(xp, xs, w_main, BF16, _pair_tile(tp, ts, 1024), H_TN, "proj_in")

    xp_tail = x_prompt[:, lp - WINDOW:].reshape(bp * WINDOW, d)
    kv_tail = _matmul(xp_tail, xs, w_kv, F32, _pair_tile(bp * WINDOW, ts, 512), 2 * SWA_KV_DIM,
                      "proj_kv_tail")

    wup = jnp.pad(w_gk_up[0], ((0, LANES - GLA_GATE_RANK), (0, 0))).astype(BF16)
    bgk = b_gk[0].reshape(1, GLA_KEY_DIM)
    gn = gla_norm_g[0].reshape(1, GLA_DV)
    oa_p, s_prompt = _gla(h, wup, bgk, gn, None, row0=0, n_seq=bp, seq_len=lp, par=math.gcd(bp, 4), nb=1,
                          c=GLA_CHUNK, sub=2, name="gla_prompt")
    oa_s, s_sample = _gla(h, wup, bgk, gn, state_gla[0], row0=tp, n_seq=bs, seq_len=ls, par=1, nb=8,
                          c=math.gcd(ls, GLA_CHUNK), sub=1, name="gla_sample")
    oa_p = oa_p.reshape(tp, GLA_VAL_DIM)
    oa_s = oa_s.reshape(ts, GLA_VAL_DIM)

    sinks = attn_sinks[0]
    k_past = cache_swa_k[0].reshape(bs, WINDOW, SWA_KV_DIM)
    v_past = cache_swa_v[0].reshape(bs, WINDOW, SWA_KV_DIM)
    ob_p = _swa_prompt(h, sinks, n_seq=bp, seq_len=lp, name="swa_prompt")
    ob_s = _swa_sample(h, sinks, k_past, v_past, row0=tp, n_seq=bs, lq=ls, nb=8, name="swa_sample")

    x1, xw = _merge((oa_p, oa_s), (ob_p, ob_s), h, (xp, xs), w_proj_a[0].astype(BF16),
                    w_proj_b[0].astype(BF16), w_out[0].astype(BF16), ln1_g[0].reshape(1, d),
                    ln1_b[0].reshape(1, d), tm=_pair_tile(tp, ts, 512), name="merge_ln1")

    wr_t = w_router[0].T
    wr_hi = wr_t.astype(BF16)
    wr_lo = (wr_t - wr_hi.astype(F32)).astype(BF16)
    tm_r = _tile(t, 512)
    idx, w_lanes, rank, cnt_col, cnt_row = _router(x1, wr_hi, wr_lo, router_bias[0].reshape(N_EXPERTS, 1),
                                                   tm=tm_r, name="router")
    meta, items = _plan(cnt_col, cnt_row, idx, rank, tm=tm_r, name="plan")
    ys = _experts(items.reshape(-1), xw, meta.reshape(-1), w_expert_gate[0], w_expert_up[0],
                  w_expert_down[0], n_tok=t, name="experts")
    y_p, y_s = _combine(ys.reshape(TOP_K, t, ROW_CHUNKS, LANES), x1, w_lanes.reshape(t, TOP_K, LANES),
                        w_shared_gate[0].astype(BF16), w_shared_up[0].astype(BF16),
                        w_shared_down[0].astype(BF16), ln2_g[0].reshape(1, d), ln2_b[0].reshape(1, d),
                        n_prompt=tp, tm=_pair_tile(tp, ts, 128), name="combine_ln2")

    y_prompt = y_p.reshape(bp, lp, d)
    y_sample = y_s.reshape(bs, ls, d)
    k_tail = kv_tail[:, :SWA_KV_DIM]
    v_tail = kv_tail[:, SWA_KV_DIM:]
    kv_shape = (SWA_KV_HEADS, SWA_HEAD_DIM)
    k_prompt = k_tail[:bp * WINDOW].reshape(1, bp, WINDOW, *kv_shape)
    v_prompt = v_tail[:bp * WINDOW].reshape(1, bp, WINDOW, *kv_shape)
    k_new = k_tail[bp * WINDOW:].reshape(bs, ls, *kv_shape)
    v_new = v_tail[bp * WINDOW:].reshape(bs, ls, *kv_shape)
    k_sample = jnp.concatenate([cache_swa_k[0][:, ls:], k_new], axis=1)[None]
    v_sample = jnp.concatenate([cache_swa_v[0][:, ls:], v_new], axis=1)[None]
    return (y_prompt, y_sample, s_prompt[None], s_sample[None], k_prompt, v_prompt, k_sample, v_sample)
```

```python
import functools
import math

import jax
import jax.numpy as jnp
from jax import lax
from jax.experimental import pallas as pl
from jax.experimental.pallas import tpu as pltpu

F32 = jnp.float32
BF16 = jnp.bfloat16
I32 = jnp.int32

D_MODEL = 1024
GLA_HEADS = 4
GLA_DK = 128
GLA_DV = 256
GLA_KEY_DIM = GLA_HEADS * GLA_DK
GLA_VAL_DIM = GLA_HEADS * GLA_DV
GLA_GATE_RANK = 16
GLA_GATE_NORMALIZER = 16.0
GLA_CHUNK = 64
SWA_HEADS = 16
SWA_KV_HEADS = 4
SWA_GROUP = SWA_HEADS // SWA_KV_HEADS
SWA_HEAD_DIM = 64
SWA_Q_DIM = SWA_HEADS * SWA_HEAD_DIM
SWA_KV_DIM = SWA_KV_HEADS * SWA_HEAD_DIM
WINDOW = 128
N_EXPERTS = 256
TOP_K = 8
N_GROUPS = 8
GROUP_SIZE = N_EXPERTS // N_GROUPS
TOPK_GROUPS = 4
EXPERT_FF = 256
SHARED_FF = 256
ROUTED_SCALE = 2.5
DEEPNORM_ALPHA = 2.0 ** 0.25
EPS = 1e-5

LANES = 128
SUBLANES = 8
ROW_CHUNKS = D_MODEL // LANES
VMEM_LIMIT = 56 * 1024 * 1024

H_QA, H_KA, H_VA, H_GA, H_QB, H_GATE_A, H_GATE_B, H_KB, H_VB, H_GK = (
    0, 512, 1024, 2048, 3072, 4096, 5120, 6144, 6400, 6656)
H_WIDTH = 6912
H_TN = 2304

EXPERT_ROWS = 256
PACKED_ROWS = ROW_CHUNKS // 2
HIGH_HALF = -65536


def _cparams(sem, vmem=VMEM_LIMIT):
    return pltpu.CompilerParams(dimension_semantics=sem, vmem_limit_bytes=vmem)


def _tile(n, pref):
    t = min(n, pref)
    while n % t:
        t -= LANES
    assert t > 0 and t % LANES == 0, (n, pref)
    return t


def _pair_tile(n_p, n_s, pref):
    return _tile(math.gcd(n_p, n_s), pref)


def _pair_maps(np_tiles, col=0, extra=0):
    def p_map(i, *_):
        return (jnp.minimum(i, np_tiles - 1), col)

    def s_map(i, *_):
        return (jnp.maximum(i - np_tiles, 0), col)

    return p_map, s_map


def _pair_value(i, np_tiles, p_ref, s_ref):
    return jnp.where(i < np_tiles, p_ref[...], s_ref[...])


def _mm_kernel(xp_ref, xs_ref, w_ref, o_ref, *, np_tiles):
    x = _pair_value(pl.program_id(0), np_tiles, xp_ref, xs_ref)
    o_ref[...] = jnp.dot(x.astype(BF16), w_ref[...], preferred_element_type=F32).astype(o_ref.dtype)


def _matmul(xp, xs, w, out_dtype, tm, tn, name):
    k = xp.shape[1]
    m = xp.shape[0] + xs.shape[0]
    n = w.shape[1]
    np_tiles = xp.shape[0] // tm
    p_map, s_map = _pair_maps(np_tiles)
    return pl.pallas_call(
        functools.partial(_mm_kernel, np_tiles=np_tiles),
        grid=(m // tm, n // tn),
        in_specs=[pl.BlockSpec((tm, k), p_map),
                  pl.BlockSpec((tm, k), s_map),
                  pl.BlockSpec((k, tn), lambda i, j: (0, j))],
        out_specs=pl.BlockSpec((tm, tn), lambda i, j: (i, j)),
        out_shape=jax.ShapeDtypeStruct((m, n), out_dtype),
        compiler_params=_cparams(("parallel", "arbitrary")),
        name=name,
    )(xp, xs, w)


def _split_bf16(x):
    hi = x.astype(BF16)
    lo = (x - hi.astype(F32)).astype(BF16)
    return hi, lo


def _gla_kernel(*refs, par, nb, c, sub, has_s0):
    q_refs, k_refs, v_refs, ga_refs, gk_refs = (refs[n * par:(n + 1) * par] for n in range(5))
    rest = refs[5 * par:]
    n_state = par * GLA_HEADS
    s_scr = [rest[len(rest) - n_state + p * GLA_HEADS:len(rest) - n_state + (p + 1) * GLA_HEADS]
             for p in range(par)]
    rest = rest[:len(rest) - n_state]
    if has_s0:
        wup_ref, bgk_ref, gn_ref, s0_ref, o_ref, sout_ref = rest
    else:
        wup_ref, bgk_ref, gn_ref, o_ref, sout_ref = rest
        s0_ref = None
    ci = pl.program_id(1)
    rows = nb * c

    @pl.when(ci == 0)
    def _():
        for p in range(par):
            for h in range(GLA_HEADS):
                if has_s0:
                    s_scr[p][h][...] = s0_ref[p * nb:(p + 1) * nb, h]
                else:
                    s_scr[p][h][...] = jnp.zeros_like(s_scr[p][h])

    r_i = lax.broadcasted_iota(I32, (rows, rows), 0)
    c_i = lax.broadcasted_iota(I32, (rows, rows), 1)
    same_seq = (r_i // c) == (c_i // c)
    causal = same_seq & (c_i <= r_i)
    tri = jnp.where(causal, 1.0, 0.0).astype(BF16)
    seg = jnp.where(same_seq, 1.0, 0.0).astype(BF16)
    ones_kv = jnp.ones((rows, GLA_DV), BF16)
    seq_of_row = lax.broadcasted_iota(I32, (rows, 1), 0) // c
    tn = (((0,), (0,)), ((), ()))

    def decays(p, rs):
        pre = jnp.dot(gk_refs[p][rs, :], wup_ref[...], preferred_element_type=F32) + bgk_ref[...]
        log_a = jax.nn.log_sigmoid(pre) / GLA_GATE_NORMALIZER
        la_hi, la_lo = _split_bf16(log_a)
        b = (jnp.dot(tri, la_hi, preferred_element_type=F32)
             + jnp.dot(tri, la_lo, preferred_element_type=F32))
        b_last = (jnp.dot(seg, la_hi, preferred_element_type=F32)
                  + jnp.dot(seg, la_lo, preferred_element_type=F32))
        q = q_refs[p][rs, :].astype(F32) * (GLA_DK ** -0.5)
        k = k_refs[p][rs, :].astype(F32)
        q_dec = (q * jnp.exp(b)).astype(BF16)
        k_dec = (k * jnp.exp(-b)).astype(BF16)
        k_rem = k * jnp.exp(b_last - b)
        return q_dec, k_dec, k_rem, la_hi, la_lo

    def head(p, rs, h, q_dec, k_dec, k_rem, la_hi, la_lo):
        ks = slice(h * GLA_DK, (h + 1) * GLA_DK)
        vh = v_refs[p][rs, h * GLA_DV:(h + 1) * GLA_DV]
        att = lax.dot_general(q_dec[:, ks], k_dec[:, ks], (((1,), (1,)), ((), ())),
                              preferred_element_type=F32)
        att = jnp.where(causal, att, 0.0).astype(BF16)
        o_h = jnp.dot(att, vh, preferred_element_type=F32)
        for j in range(nb):
            mine = (seq_of_row == j) if nb > 1 else None
            pick = (lambda a: jnp.where(mine, a, 0.0)) if nb > 1 else (lambda a: a)
            s_old = s_scr[p][h][j]
            o_h = o_h + pick(jnp.dot(q_dec[:, ks], s_old.astype(BF16), preferred_element_type=F32))
            dec = (lax.dot_general(pick(la_hi[:, ks].astype(F32)).astype(BF16), ones_kv, tn,
                                   preferred_element_type=F32)
                   + lax.dot_general(pick(la_lo[:, ks].astype(F32)).astype(BF16), ones_kv, tn,
                                     preferred_element_type=F32))
            upd = lax.dot_general(pick(k_rem[:, ks]).astype(BF16), vh, tn, preferred_element_type=F32)
            s_scr[p][h][j] = jnp.exp(dec) * s_old + upd
        return o_h * lax.rsqrt(jnp.mean(jnp.square(o_h), axis=-1, keepdims=True) + EPS) * gn_ref[...]

    for s in range(sub):
        rs = slice(s * rows, (s + 1) * rows)
        staged = [decays(p, rs) for p in range(par)]
        outs = [[] for _ in range(par)]
        for h in range(GLA_HEADS):
            for p in range(par):
                outs[p].append(head(p, rs, h, *staged[p]))
        for p in range(par):
            o = jnp.concatenate(outs[p], axis=-1) * jax.nn.silu(ga_refs[p][rs, :].astype(F32))
            o_ref[p, rs, :] = o.astype(o_ref.dtype)

    @pl.when(ci == pl.num_programs(1) - 1)
    def _():
        for p in range(par):
            for h in range(GLA_HEADS):
                sout_ref[p * nb:(p + 1) * nb, h] = s_scr[p][h][...]


def _gla(h, wup, bgk, gn, s0, *, row0, n_seq, seq_len, par, nb, c, sub, name):
    rows = nb * c * sub
    n_blocks = n_seq // nb
    n_steps = seq_len // (c * sub)
    rb0 = row0 // rows

    def rmap(p, col):
        return lambda g, i: (rb0 + (g * par + p) * n_steps + i, col)

    fields = ((GLA_KEY_DIM, H_QA), (GLA_KEY_DIM, H_KA), (GLA_VAL_DIM, H_VA), (GLA_VAL_DIM, H_GA), (LANES, H_GK))
    in_specs = [pl.BlockSpec((rows, width), rmap(p, off // width)) for width, off in fields for p in range(par)]
    in_specs += [pl.BlockSpec((LANES, GLA_KEY_DIM), lambda g, i: (0, 0)),
                 pl.BlockSpec((1, GLA_KEY_DIM), lambda g, i: (0, 0)),
                 pl.BlockSpec((1, GLA_DV), lambda g, i: (0, 0))]
    args = [h] * (5 * par) + [wup, bgk, gn]
    state_spec = pl.BlockSpec((par * nb, GLA_HEADS, GLA_DK, GLA_DV), lambda g, i: (g, 0, 0, 0))
    if s0 is not None:
        in_specs.append(state_spec)
        args.append(s0)
    kern = functools.partial(_gla_kernel, par=par, nb=nb, c=c, sub=sub, has_s0=s0 is not None)
    return pl.pallas_call(
        kern,
        grid=(n_blocks // par, n_steps),
        in_specs=in_specs,
        out_specs=[pl.BlockSpec((par, rows, GLA_VAL_DIM), lambda g, i: (g, i, 0)), state_spec],
        out_shape=[jax.ShapeDtypeStruct((n_blocks, nb * seq_len, GLA_VAL_DIM), BF16),
                   jax.ShapeDtypeStruct((n_seq, GLA_HEADS, GLA_DK, GLA_DV), F32)],
        scratch_shapes=[pltpu.VMEM((nb, GLA_DK, GLA_DV), F32)] * (par * GLA_HEADS),
        compiler_params=_cparams(("parallel", "arbitrary")),
        name=name,
    )(*args)


def _alibi_slope(head):
    return 2.0 ** (-8.0 * (head + 1) / SWA_HEADS)


def _swa_softmax_pv(parts, sink):
    m = sink
    for s, _ in parts:
        m = jnp.maximum(m, jnp.max(s, axis=-1, keepdims=True))
    denom = jnp.exp(sink - m)
    acc = None
    for s, v in parts:
        p = jnp.exp(s - m)
        denom = denom + jnp.sum(p, axis=-1, keepdims=True)
        pv = jnp.dot(p.astype(BF16), v, preferred_element_type=F32)
        acc = pv if acc is None else acc + pv
    return acc / denom


def _swa_prompt_kernel(sink_ref, q_ref, kp_ref, vp_ref, kc_ref, vc_ref, o_ref, bias):
    i = pl.program_id(1)
    span = 2 * WINDOW
    col = lax.broadcasted_iota(I32, (WINDOW, span), 1)

    @pl.when(i == 0)
    def _():
        dist_i = lax.broadcasted_iota(I32, (WINDOW, span), 0) + WINDOW - col
        in_window = (dist_i >= 0) & (dist_i < WINDOW)
        dist = dist_i.astype(F32)
        for hh in range(SWA_HEADS):
            bias[hh] = jnp.where(in_window, -_alibi_slope(hh) * dist, -jnp.inf)

    kcat = jnp.concatenate([kp_ref[...], kc_ref[...]], axis=0)
    vcat = jnp.concatenate([vp_ref[...], vc_ref[...]], axis=0)
    no_past = (col < WINDOW) & (i == 0)
    scale = SWA_HEAD_DIM ** -0.5
    outs = []
    for hh in range(SWA_HEADS):
        g = hh // SWA_GROUP
        gs = slice(g * SWA_HEAD_DIM, (g + 1) * SWA_HEAD_DIM)
        qh = q_ref[:, hh * SWA_HEAD_DIM:(hh + 1) * SWA_HEAD_DIM] * scale
        s = lax.dot_general(qh, kcat[:, gs], (((1,), (1,)), ((), ())), preferred_element_type=F32)
        s = jnp.where(no_past, -jnp.inf, s + bias[hh])
        outs.append(_swa_softmax_pv([(s, vcat[:, gs])], sink_ref[hh]))
    o_ref[...] = jnp.concatenate(outs, axis=-1).astype(o_ref.dtype)


def _swa_prompt(h, sinks, *, n_seq, seq_len, name):
    nq = seq_len // WINDOW
    qcol = H_QB // SWA_Q_DIM
    kcol = H_KB // SWA_KV_DIM
    vcol = H_VB // SWA_KV_DIM

    def cur(col):
        return lambda b, i, sk: (b * nq + i, col)

    def prev(col):
        return lambda b, i, sk: (b * nq + jnp.maximum(i - 1, 0), col)

    return pl.pallas_call(
        _swa_prompt_kernel,
        grid_spec=pltpu.PrefetchScalarGridSpec(
            num_scalar_prefetch=1,
            grid=(n_seq, nq),
            in_specs=[pl.BlockSpec((WINDOW, SWA_Q_DIM), cur(qcol)),
                      pl.BlockSpec((WINDOW, SWA_KV_DIM), prev(kcol)),
                      pl.BlockSpec((WINDOW, SWA_KV_DIM), prev(vcol)),
                      pl.BlockSpec((WINDOW, SWA_KV_DIM), cur(kcol)),
                      pl.BlockSpec((WINDOW, SWA_KV_DIM), cur(vcol))],
            out_specs=pl.BlockSpec((WINDOW, SWA_Q_DIM), lambda b, i, sk: (b * nq + i, 0)),
            scratch_shapes=[pltpu.VMEM((SWA_HEADS, WINDOW, 2 * WINDOW), F32)],
        ),
        out_shape=jax.ShapeDtypeStruct((n_seq * seq_len, SWA_Q_DIM), BF16),
        compiler_params=_cparams(("parallel", "arbitrary")),
        name=name,
    )(sinks, h, h, h, h, h)


def _swa_sample_kernel(sink_ref, q_ref, kc_ref, vc_ref, kvt_ref, kp_ref, vp_ref, o_ref, kout_ref, vout_ref,
                       *, nb, lq):
    rows = SWA_HEADS * lq
    grp_rows = SWA_GROUP * lq
    hd = SWA_HEAD_DIM
    nt = (((1,), (1,)), ((), ()))
    head_of_row = lax.broadcasted_iota(I32, (rows, 1), 0) // lq
    slope = jnp.zeros((rows, 1), F32)
    sink = jnp.zeros((rows, 1), F32)
    for hh in range(SWA_HEADS):
        slope = jnp.where(head_of_row == hh, _alibi_slope(hh), slope)
        sink = jnp.where(head_of_row == hh, sink_ref[hh], sink)
    qi = lax.broadcasted_iota(I32, (rows, WINDOW), 0) % lq
    dist_p = qi + WINDOW - lax.broadcasted_iota(I32, (rows, WINDOW), 1)
    valid_p = dist_p < WINDOW
    bias_p = slope * dist_p.astype(F32)
    dist_c = lax.broadcasted_iota(I32, (rows, lq), 0) % lq - lax.broadcasted_iota(I32, (rows, lq), 1)
    valid_c = dist_c >= 0
    bias_c = slope * dist_c.astype(F32)
    scale = hd ** -0.5

    q_all = q_ref[...].astype(F32)
    kc_all = kc_ref[...].astype(F32)
    vc_all = vc_ref[...].astype(F32)
    seq_outs = []
    for j in range(nb):
        js = slice(j * lq, (j + 1) * lq)
        pieces = []
        for hh in range(SWA_HEADS):
            g = hh // SWA_GROUP
            parts = []
            if g:
                parts.append(jnp.zeros((lq, g * hd), F32))
            parts.append(q_all[js, hh * hd:(hh + 1) * hd])
            if g < SWA_KV_HEADS - 1:
                parts.append(jnp.zeros((lq, (SWA_KV_HEADS - 1 - g) * hd), F32))
            pieces.append(jnp.concatenate(parts, axis=-1))
        q_big = jnp.concatenate(pieces, axis=0).astype(BF16)
        kp = jnp.concatenate([kp_ref[j, :, g, :] for g in range(SWA_KV_HEADS)], axis=-1).astype(BF16)
        vp = jnp.concatenate([vp_ref[j, :, g, :] for g in range(SWA_KV_HEADS)], axis=-1).astype(BF16)
        for past_ref, out_ref, col0 in ((kp_ref, kout_ref, 0), (vp_ref, vout_ref, SWA_KV_DIM)):
            out_ref[j, 0:WINDOW - lq] = past_ref[j, lq:WINDOW]
            for g in range(SWA_KV_HEADS):
                out_ref[j, WINDOW - lq:WINDOW, g, :] = kvt_ref[js, col0 + g * hd:col0 + (g + 1) * hd]
        kcj = kc_all[js, :].astype(BF16)
        vcj = vc_all[js, :].astype(BF16)
        s_p = lax.dot_general(q_big, kp, nt, preferred_element_type=F32)
        s_p = jnp.where(valid_p, s_p * scale - bias_p, -jnp.inf)
        s_c = lax.dot_general(q_big, kcj, nt, preferred_element_type=F32)
        s_c = jnp.where(valid_c, s_c * scale - bias_c, -jnp.inf)
        o_big = _swa_softmax_pv([(s_p, vp), (s_c, vcj)], sink)
        o_grp = [o_big[g * grp_rows:(g + 1) * grp_rows, g * hd:(g + 1) * hd] for g in range(SWA_KV_HEADS)]
        o_heads = jnp.concatenate(o_grp, axis=0)
        seq_outs.append(jnp.concatenate([o_heads[hh * lq:(hh + 1) * lq, :] for hh in range(SWA_HEADS)],
                                        axis=-1))
    o_ref[...] = jnp.concatenate(seq_outs, axis=0).astype(o_ref.dtype)


def _swa_sample(h, sinks, kv_tail, k_cache, v_cache, *, row0, tail_row0, n_seq, lq, nb, name):
    rows = nb * lq
    rb0 = row0 // rows
    tb0 = tail_row0 // rows
    qcol = H_QB // SWA_Q_DIM
    kcol = H_KB // SWA_KV_DIM
    vcol = H_VB // SWA_KV_DIM
    kern = functools.partial(_swa_sample_kernel, nb=nb, lq=lq)
    cache_spec = pl.BlockSpec((None, nb, WINDOW, SWA_KV_HEADS, SWA_HEAD_DIM), lambda g, sk: (0, g, 0, 0, 0))
    return pl.pallas_call(
        kern,
        grid_spec=pltpu.PrefetchScalarGridSpec(
            num_scalar_prefetch=1,
            grid=(n_seq // nb,),
            in_specs=[pl.BlockSpec((rows, SWA_Q_DIM), lambda g, sk: (rb0 + g, qcol)),
                      pl.BlockSpec((rows, SWA_KV_DIM), lambda g, sk: (rb0 + g, kcol)),
                      pl.BlockSpec((rows, SWA_KV_DIM), lambda g, sk: (rb0 + g, vcol)),
                      pl.BlockSpec((rows, 2 * SWA_KV_DIM), lambda g, sk: (tb0 + g, 0)),
                      cache_spec, cache_spec],
            out_specs=[pl.BlockSpec((rows, SWA_Q_DIM), lambda g, sk: (g, 0)), cache_spec, cache_spec],
        ),
        out_shape=[jax.ShapeDtypeStruct((n_seq * lq, SWA_Q_DIM), BF16),
                   jax.ShapeDtypeStruct(k_cache.shape, F32),
                   jax.ShapeDtypeStruct(v_cache.shape, F32)],
        compiler_params=_cparams(("parallel",)),
        name=name,
    )(sinks, h, h, h, kv_tail, k_cache, v_cache)


def _layer_norm(x, g, b):
    mu = jnp.mean(x, axis=-1, keepdims=True)
    xc = x - mu
    var = jnp.mean(jnp.square(xc), axis=-1, keepdims=True)
    return xc * lax.rsqrt(var + EPS) * g + b


def _pack_bf16_pairs(x, s):
    lo = lax.bitcast_convert_type(x[:, (2 * s) * LANES:(2 * s + 1) * LANES].astype(BF16).astype(F32), I32)
    hi = lax.bitcast_convert_type(x[:, (2 * s + 1) * LANES:(2 * s + 2) * LANES].astype(BF16).astype(F32), I32)
    return lax.shift_right_logical(lo, 16) | (hi & HIGH_HALF)


def _unpack_bf16_pairs(w):
    lo = lax.bitcast_convert_type(lax.shift_left(w, 16), F32).astype(BF16)
    hi = lax.bitcast_convert_type(w & HIGH_HALF, F32).astype(BF16)
    return lo, hi


def _merge_kernel(oap_ref, oas_ref, obp_ref, obs_ref, ga_ref, gb_ref, xp_ref, xs_ref,
                  wpa_ref, wpb_ref, wout_ref, g_ref, b_ref, o_ref, xw_ref, *, np_tiles):
    i = pl.program_id(0)
    tm = o_ref.shape[0]
    br_a = jnp.dot(_pair_value(i, np_tiles, oap_ref, oas_ref), wpa_ref[...], preferred_element_type=F32)
    br_b = jnp.dot(_pair_value(i, np_tiles, obp_ref, obs_ref), wpb_ref[...], preferred_element_type=F32)
    merged = (jax.nn.sigmoid(ga_ref[...].astype(F32)) * br_a
              + jax.nn.sigmoid(gb_ref[...].astype(F32)) * br_b)
    mix = jnp.dot(merged.astype(BF16), wout_ref[...], preferred_element_type=F32)
    x = _pair_value(i, np_tiles, xp_ref, xs_ref)
    x1 = _layer_norm(DEEPNORM_ALPHA * x + mix, g_ref[...], b_ref[...])
    o_ref[...] = x1
    for s in range(PACKED_ROWS):
        xw_ref[pl.ds(s, tm, stride=PACKED_ROWS), :] = _pack_bf16_pairs(x1, s)


def _merge(oa, ob, h, x, wpa, wpb, wout, g, b, *, tm, name):
    t = h.shape[0]
    np_tiles = x[0].shape[0] // tm
    p_map, s_map = _pair_maps(np_tiles)
    row = lambda i: (i, 0)
    full = lambda i: (0, 0)
    pair = [pl.BlockSpec((tm, D_MODEL), p_map), pl.BlockSpec((tm, D_MODEL), s_map)]
    return pl.pallas_call(
        functools.partial(_merge_kernel, np_tiles=np_tiles),
        grid=(t // tm,),
        in_specs=pair + pair + [
                  pl.BlockSpec((tm, D_MODEL), lambda i: (i, H_GATE_A // D_MODEL)),
                  pl.BlockSpec((tm, D_MODEL), lambda i: (i, H_GATE_B // D_MODEL))] + pair + [
                  pl.BlockSpec((D_MODEL, D_MODEL), full),
                  pl.BlockSpec((D_MODEL, D_MODEL), full),
                  pl.BlockSpec((D_MODEL, D_MODEL), full),
                  pl.BlockSpec((1, D_MODEL), full),
                  pl.BlockSpec((1, D_MODEL), full)],
        out_specs=[pl.BlockSpec((tm, D_MODEL), row),
                   pl.BlockSpec((tm * PACKED_ROWS, LANES), row)],
        out_shape=[jax.ShapeDtypeStruct((t, D_MODEL), F32),
                   jax.ShapeDtypeStruct((t * PACKED_ROWS, LANES), I32)],
        compiler_params=_cparams(("parallel",)),
        name=name,
    )(*oa, *ob, h, h, *x, wpa, wpb, wout, g, b)


def _router_kernel(x_ref, whi_ref, wlo_ref, bias_ref, idx_ref, wl_ref, rank_ref, cntc_ref, cntr_ref,
                   cntc_scr, cntr_scr):
    i = pl.program_id(0)
    tm = x_ref.shape[0]

    @pl.when(i == 0)
    def _():
        cntc_scr[...] = jnp.zeros_like(cntc_scr)
        cntr_scr[...] = jnp.zeros_like(cntr_scr)

    x_hi, x_lo = _split_bf16(x_ref[...])
    nt = (((1,), (1,)), ((), ()))
    logits = (lax.dot_general(whi_ref[...], x_hi, nt, preferred_element_type=F32)
              + lax.dot_general(whi_ref[...], x_lo, nt, preferred_element_type=F32)
              + lax.dot_general(wlo_ref[...], x_hi, nt, preferred_element_type=F32))
    scores = jax.nn.sigmoid(logits)
    biased = scores + bias_ref[...]

    grouped = biased.reshape(N_GROUPS, GROUP_SIZE, tm)
    m1 = jnp.max(grouped, axis=1)
    n_top = jnp.sum(jnp.where(grouped == m1[:, None, :], 1.0, 0.0), axis=1)
    m2 = jnp.max(jnp.where(grouped < m1[:, None, :], grouped, -jnp.inf), axis=1)
    gscore = m1 + jnp.where(n_top >= 2.0, m1, m2)

    g_iota = lax.broadcasted_iota(I32, (N_GROUPS, tm), 0)
    beaten = jnp.zeros((N_GROUPS, tm), I32)
    for g in range(N_GROUPS):
        other = gscore[g:g + 1, :]
        ahead = (other > gscore) | ((other == gscore) & (g < g_iota))
        beaten = beaten + jnp.where(ahead, 1, 0)
    keep = jnp.where(beaten < TOPK_GROUPS, 1.0, 0.0)
    masked = jnp.where(keep[:, None, :] > 0.5, grouped, -jnp.inf).reshape(N_EXPERTS, tm)

    e_iota = lax.broadcasted_iota(I32, (N_EXPERTS, tm), 0)
    sel_f = jnp.zeros((N_EXPERTS, tm), F32)
    ids = []
    for _ in range(TOP_K):
        best = jnp.max(masked, axis=0, keepdims=True)
        idx = jnp.min(jnp.where(masked == best, e_iota, N_EXPERTS), axis=0, keepdims=True)
        hit = e_iota == idx
        sel_f = sel_f + jnp.where(hit, 1.0, 0.0)
        masked = jnp.where(hit, -jnp.inf, masked)
        ids.append(idx)

    top_sum = jnp.sum(sel_f * scores, axis=0, keepdims=True)

    t_r = lax.broadcasted_iota(I32, (tm, tm), 0)
    t_c = lax.broadcasted_iota(I32, (tm, tm), 1)
    before = jnp.where(t_r < t_c, 1.0, 0.0).astype(BF16)
    sel_b = sel_f.astype(BF16)
    rank = jnp.dot(sel_b, before, preferred_element_type=F32) + cntc_scr[...]
    cntc_scr[...] = cntc_scr[...] + jnp.sum(sel_f, axis=1, keepdims=True)
    cntr_scr[...] = cntr_scr[...] + lax.dot_general(jnp.ones((SUBLANES, tm), BF16), sel_b, nt,
                                                    preferred_element_type=F32)

    idx_rows, w_rows, rank_rows = [], [], []
    for idx in ids:
        hit = e_iota == idx
        w = jnp.sum(jnp.where(hit, scores, 0.0), axis=0, keepdims=True)
        w_rows.append(w / top_sum * ROUTED_SCALE)
        rank_rows.append(jnp.sum(jnp.where(hit, rank, 0.0), axis=0, keepdims=True))
        idx_rows.append(idx)
    idx_ref[...] = jnp.concatenate(idx_rows, axis=0)
    rank_ref[...] = jnp.concatenate(rank_rows, axis=0).astype(I32)

    w_all = jnp.concatenate(w_rows, axis=0)
    w_1 = w_all.astype(BF16)
    r_1 = w_all - w_1.astype(F32)
    w_2 = r_1.astype(BF16)
    w_3 = (r_1 - w_2.astype(F32)).astype(BF16)
    terms = jnp.concatenate([w_1, w_2, w_3, jnp.zeros_like(w_1)], axis=0)
    t_row = lax.broadcasted_iota(I32, (4 * TOP_K, TOP_K * LANES), 0)
    t_col = lax.broadcasted_iota(I32, (4 * TOP_K, TOP_K * LANES), 1)
    spread = jnp.where((t_row % TOP_K) == (t_col // LANES), 1.0, 0.0).astype(BF16)
    w_lanes = lax.dot_general(terms, spread, (((0,), (0,)), ((), ())), preferred_element_type=F32)
    for k in range(TOP_K):
        wl_ref[pl.ds(k, tm, stride=TOP_K), :] = w_lanes[:, k * LANES:(k + 1) * LANES]

    @pl.when(i == pl.num_programs(0) - 1)
    def _():
        cntc_ref[...] = cntc_scr[...].astype(I32)
        cntr_ref[...] = cntr_scr[...].astype(I32)


def _router(x1, w_hi, w_lo, bias, *, tm, name):
    t = x1.shape[0]
    full = lambda i: (0, 0)
    tok = lambda i: (0, i)
    return pl.pallas_call(
        _router_kernel,
        grid=(t // tm,),
        in_specs=[pl.BlockSpec((tm, D_MODEL), lambda i: (i, 0)),
                  pl.BlockSpec((N_EXPERTS, D_MODEL), full),
                  pl.BlockSpec((N_EXPERTS, D_MODEL), full),
                  pl.BlockSpec((N_EXPERTS, 1), full)],
        out_specs=[pl.BlockSpec((TOP_K, tm), tok),
                   pl.BlockSpec((TOP_K * tm, LANES), lambda i: (i, 0)),
                   pl.BlockSpec((TOP_K, tm), tok),
                   pl.BlockSpec((N_EXPERTS, 1), full),
                   pl.BlockSpec((SUBLANES, N_EXPERTS), full)],
        out_shape=[jax.ShapeDtypeStruct((TOP_K, t), I32),
                   jax.ShapeDtypeStruct((TOP_K * t, LANES), F32),
                   jax.ShapeDtypeStruct((TOP_K, t), I32),
                   jax.ShapeDtypeStruct((N_EXPERTS, 1), I32),
                   jax.ShapeDtypeStruct((SUBLANES, N_EXPERTS), I32)],
        scratch_shapes=[pltpu.VMEM((N_EXPERTS, 1), F32),
                        pltpu.VMEM((SUBLANES, N_EXPERTS), F32)],
        compiler_params=_cparams(("arbitrary",)),
        name=name,
    )(x1, w_hi, w_lo, bias)


ITEM_FIELDS = 8


def _byte_split(v):
    return lax.shift_right_logical(v, 8).astype(F32), (v & 255).astype(F32)


def _plan_kernel(cntc_ref, cntr_ref, idx_ref, rank_ref, meta_ref, items_ref, m_scr, *, n_tok):
    i = pl.program_id(0)
    tm = idx_ref.shape[1]
    nbm = m_scr.shape[0]
    nip = items_ref.shape[1]
    nbp = nip - N_EXPERTS
    n_rows = n_tok * TOP_K
    n_blk = n_rows // EXPERT_ROWS
    nt = (((1,), (1,)), ((), ()))

    r_i = lax.broadcasted_iota(I32, (N_EXPERTS, N_EXPERTS), 0)
    c_i = lax.broadcasted_iota(I32, (N_EXPERTS, N_EXPERTS), 1)
    below = jnp.where(c_i < r_i, 1.0, 0.0).astype(BF16)
    ones_c = jnp.ones((N_EXPERTS, LANES), F32)
    c_hi, c_lo = _byte_split(cntc_ref[...])
    start_col = (256.0 * jnp.dot(below, (c_hi * ones_c).astype(BF16), preferred_element_type=F32)
                 + jnp.dot(below, (c_lo * ones_c).astype(BF16), preferred_element_type=F32))[:, 0:1]

    @pl.when(i == 0)
    def _():
        m_scr[...] = jnp.zeros_like(m_scr)
        above = jnp.where(r_i < c_i, 1.0, 0.0).astype(BF16)
        r_hi, r_lo = _byte_split(cntr_ref[...])
        start_row = (256.0 * jnp.dot(r_hi.astype(BF16), above, preferred_element_type=F32)
                     + jnp.dot(r_lo.astype(BF16), above, preferred_element_type=F32))[0:1, :]
        b_col = lax.broadcasted_iota(I32, (nbp, 1), 0)
        b_row = lax.broadcasted_iota(I32, (1, nbp), 1)
        blk_col = jnp.where(b_col < n_blk, b_col * EXPERT_ROWS, n_rows).astype(F32)
        blk_row = jnp.where(b_row < n_blk, b_row * EXPERT_ROWS, n_rows).astype(F32)
        v_col = jnp.concatenate([blk_col, start_col], axis=0)
        v_row = jnp.concatenate([blk_row, start_row], axis=1)
        j_col = lax.broadcasted_iota(I32, (nip, 1), 0)
        k_row = lax.broadcasted_iota(I32, (1, nip), 1)
        ahead = (v_row < v_col) | ((v_row == v_col) & (k_row < j_col))
        order_col = jnp.sum(jnp.where(ahead, 1.0, 0.0), axis=1, keepdims=True)
        pos_row = k_row.astype(F32)
        lo_abs = jnp.sum(jnp.where(order_col == pos_row, v_col, 0.0), axis=0, keepdims=True)
        hi_abs = jnp.sum(jnp.where(order_col == pos_row + 1.0, v_col, 0.0), axis=0, keepdims=True)
        hi_abs = jnp.where(k_row == nip - 1, float(n_rows), hi_abs)
        blk = jnp.minimum(jnp.floor(lo_abs * (1.0 / EXPERT_ROWS)), n_blk - 1.0)
        expert = jnp.sum(jnp.where(start_col <= lo_abs, 1.0, 0.0), axis=0, keepdims=True) - 1.0
        base = blk * EXPERT_ROWS
        e_col = lax.broadcasted_iota(I32, (N_EXPERTS, 1), 0).astype(F32)
        end_col = start_col + cntc_ref[...].astype(F32)
        seg_end = jnp.sum(jnp.where(e_col == expert, end_col, 0.0), axis=0, keepdims=True)
        follower = jnp.sum(jnp.where(start_col <= seg_end, 1.0, 0.0), axis=0, keepdims=True) - 1.0
        follower = jnp.where(seg_end < float(n_rows), follower, -1.0)
        fields = [blk, expert, lo_abs - base, hi_abs - base, follower]
        fields.append(jnp.zeros((ITEM_FIELDS - len(fields), nip), F32))
        items_ref[...] = jnp.concatenate(fields, axis=0).astype(I32)

    e_iota = lax.broadcasted_iota(I32, (N_EXPERTS, tm), 0)
    b_iota = lax.broadcasted_iota(I32, (nbm, tm), 0)
    l_iota = lax.broadcasted_iota(I32, (EXPERT_ROWS, tm), 0)
    tok = i * tm + lax.broadcasted_iota(I32, (1, tm), 1)
    tok_hi, tok_lo = _byte_split(tok)
    acc = jnp.zeros(m_scr.shape, F32)
    for k in range(TOP_K):
        hit = e_iota == idx_ref[k:k + 1, :]
        dest = (jnp.sum(jnp.where(hit, start_col, 0.0), axis=0, keepdims=True).astype(I32)
                + rank_ref[k:k + 1, :])
        oh_blk = jnp.where(b_iota == lax.shift_right_logical(dest, 8), 1.0, 0.0).astype(BF16)
        in_blk = l_iota == (dest & (EXPERT_ROWS - 1))
        vals = jnp.concatenate([jnp.where(in_blk, tok_hi, 0.0), jnp.where(in_blk, tok_lo, 0.0),
                                jnp.where(in_blk, float(k), 0.0)], axis=0).astype(BF16)
        acc = acc + lax.dot_general(oh_blk, vals, nt, preferred_element_type=F32)
    m_scr[...] = m_scr[...] + acc

    @pl.when(i == pl.num_programs(0) - 1)
    def _():
        m = m_scr[...]
        row_tok = m[:, 0:EXPERT_ROWS] * 256.0 + m[:, EXPERT_ROWS:2 * EXPERT_ROWS]
        row_slot = m[:, 2 * EXPERT_ROWS:3 * EXPERT_ROWS] * float(n_tok) + row_tok
        meta_ref[...] = jnp.concatenate([row_tok * float(PACKED_ROWS), row_slot * float(ROW_CHUNKS)],
                                        axis=1).astype(I32)


def _plan(cnt_col, cnt_row, idx, rank, *, tm, name):
    t = idx.shape[1]
    assert EXPERT_ROWS == 256 and (t * TOP_K) % (2 * EXPERT_ROWS) == 0
    n_blk = t * TOP_K // EXPERT_ROWS
    nbm = -(-n_blk // 16) * 16
    nip = -(-n_blk // LANES) * LANES + N_EXPERTS
    tok = lambda i: (0, i)
    full = lambda i: (0, 0)
    return pl.pallas_call(
        functools.partial(_plan_kernel, n_tok=t),
        grid=(t // tm,),
        in_specs=[pl.BlockSpec((N_EXPERTS, 1), full),
                  pl.BlockSpec((SUBLANES, N_EXPERTS), full),
                  pl.BlockSpec((TOP_K, tm), tok),
                  pl.BlockSpec((TOP_K, tm), tok)],
        out_specs=[pl.BlockSpec((nbm, 2 * EXPERT_ROWS), full),
                   pl.BlockSpec((ITEM_FIELDS, nip), full)],
        out_shape=[jax.ShapeDtypeStruct((nbm, 2 * EXPERT_ROWS), I32),
                   jax.ShapeDtypeStruct((ITEM_FIELDS, nip), I32)],
        scratch_shapes=[pltpu.VMEM((nbm, 3 * EXPERT_ROWS), F32)],
        compiler_params=_cparams(("arbitrary",)),
        name=name,
    )(cnt_col, cnt_row, idx, rank)


META_CHUNK = 4 * EXPERT_ROWS
GATHER_STRIDE = EXPERT_ROWS + 8
META_RING = 3
FFN_PIECES = 4


def _experts_kernel(items_ref, xw_hbm, meta_hbm, wg_hbm, wu_hbm, wd_hbm, ys_hbm,
                    xw, meta, tile, lhs, wg_f, wu_f, wd_f, wg_b, wu_b, wd_b, yacc, ybuf, state,
                    sem_x, sem_m, sem_y, sem_w, *, nip, n_blk):
    i = pl.program_id(0)
    blk = items_ref[i]
    expert = items_ref[nip + i]
    lo = items_ref[2 * nip + i]
    hi = items_ref[3 * nip + i]
    follower = items_ref[4 * nip + i]
    nonempty = hi > lo
    par = blk & 1
    chunk = lax.shift_right_logical(blk, 1)

    def meta_base(b):
        return (lax.shift_right_logical(b, 1) % META_RING) * META_CHUNK + (b & 1) * (2 * EXPERT_ROWS)

    mbase = meta_base(blk)
    prev_base = meta_base(jnp.maximum(blk - 1, 0))

    def meta_copy(c):
        slot = c % META_RING
        return pltpu.make_async_copy(meta_hbm.at[pl.ds(pl.multiple_of(c * META_CHUNK, META_CHUNK), META_CHUNK)],
                                     meta.at[pl.ds(pl.multiple_of(slot * META_CHUNK, META_CHUNK), META_CHUNK)],
                                     sem_m.at[slot])

    def ybuf_drain(slot):
        pltpu.make_async_copy(ys_hbm.at[pl.ds(0, EXPERT_ROWS * ROW_CHUNKS), :], ybuf.at[slot],
                              sem_y.at[slot]).wait()

    def ybuf_fill(slot):
        for j in range(ROW_CHUNKS):
            ybuf[slot, pl.ds(j, EXPERT_ROWS, stride=ROW_CHUNKS), :] = yacc[:, j * LANES:(j + 1) * LANES]

    def send_rows(slot, base, r0, r1):
        for r in range(r0, r1):
            src = ybuf.at[slot, pl.ds(r * ROW_CHUNKS, ROW_CHUNKS), :]
            dst_row = pl.multiple_of(meta[base + EXPERT_ROWS + r], ROW_CHUNKS)
            pltpu.make_async_copy(src, ys_hbm.at[pl.ds(dst_row, ROW_CHUNKS), :],
                                  sem_y.at[slot]).start(priority=r % 2)

    def weight_copies(e, slot):
        return [pltpu.make_async_copy(src.at[e], dst.at[slot], sem_w.at[slot])
                for src, dst in ((wg_hbm, wg_f), (wu_hbm, wu_f), (wd_hbm, wd_f))]

    @pl.when(i == 0)
    def _():
        resident = pltpu.make_async_copy(xw_hbm, xw, sem_x)
        resident.start()
        meta_copy(0).start()
        for cp in weight_copies(expert, 0):
            cp.start()
        state[0] = -1
        state[1] = 1
        resident.wait()

    @pl.when(nonempty & (expert != state[0]))
    def _():
        slot = 1 - state[1]
        for cp in weight_copies(expert, slot):
            cp.wait()
        state[0] = expert
        state[1] = slot

        @pl.when(follower >= 0)
        def _():
            for cp in weight_copies(follower, 1 - slot):
                cp.start()

        wg_b[...] = wg_f[slot].astype(BF16)
        wu_b[...] = wu_f[slot].astype(BF16)
        wd_b[...] = wd_f[slot].astype(BF16)

    @pl.when(nonempty & (lo == 0))
    def _():
        @pl.when(par == 0)
        def _():
            meta_copy(chunk).wait()

            @pl.when(2 * (chunk + 1) < n_blk)
            def _():
                meta_copy(chunk + 1).start()

        for r in range(EXPERT_ROWS):
            t4 = pl.multiple_of(meta[mbase + r], PACKED_ROWS)
            tile[pl.ds(r, PACKED_ROWS, stride=GATHER_STRIDE), :] = xw[pl.ds(t4, PACKED_ROWS), :]
        cols = []
        for s in range(PACKED_ROWS):
            cols.extend(_unpack_bf16_pairs(tile[pl.ds(s * GATHER_STRIDE, EXPERT_ROWS), :]))
        lhs[...] = jnp.concatenate(cols, axis=-1)

    def ffn(x, between=None):
        half_ff = EXPERT_FF // 2
        half_d = D_MODEL // 2
        hid = []
        for c in range(2):
            gate = jnp.dot(x, wg_b[:, c * half_ff:(c + 1) * half_ff], preferred_element_type=F32)
            up = jnp.dot(x, wu_b[:, c * half_ff:(c + 1) * half_ff], preferred_element_type=F32)
            hid.append((jax.nn.silu(gate) * up).astype(BF16))
            if between is not None:
                between(c)
        hid = jnp.concatenate(hid, axis=-1)
        out = []
        for c in range(2):
            out.append(jnp.dot(hid, wd_b[:, c * half_d:(c + 1) * half_d], preferred_element_type=F32))
            if between is not None:
                between(2 + c)
        return jnp.concatenate(out, axis=-1)

    def run_mask(y):
        row = lax.broadcasted_iota(I32, (EXPERT_ROWS, 1), 0)
        return jnp.where((row >= lo) & (row < hi), y, 0.0)

    whole = (lo == 0) & (hi == EXPERT_ROWS)
    first = nonempty & (lo == 0)
    has_prev = blk > 0
    rows_per_piece = EXPERT_ROWS // FFN_PIECES

    for prev_slot in range(2):
        def send_piece(g, prev_slot=prev_slot):
            send_rows(prev_slot, prev_base, g * rows_per_piece, (g + 1) * rows_per_piece)

        sends = has_prev & (par == 1 - prev_slot)

        @pl.when(whole & sends)
        def _():
            yacc[...] = ffn(lhs[...], send_piece)

        @pl.when(first & jnp.logical_not(whole) & sends)
        def _():
            yacc[...] = run_mask(ffn(lhs[...], send_piece))

    @pl.when(whole & jnp.logical_not(has_prev))
    def _():
        yacc[...] = ffn(lhs[...])

    @pl.when(nonempty & jnp.logical_not(whole) & jnp.logical_not(first & has_prev))
    def _():
        y = run_mask(ffn(lhs[...]))

        @pl.when(lo == 0)
        def _():
            yacc[...] = y

        @pl.when(lo > 0)
        def _():
            yacc[...] = yacc[...] + y

    @pl.when(nonempty & (hi == EXPERT_ROWS))
    def _():
        @pl.when(blk >= 2)
        def _():
            ybuf_drain(par)

        for slot in range(2):
            @pl.when(par == slot)
            def _():
                ybuf_fill(slot)

                @pl.when(blk == n_blk - 1)
                def _():
                    send_rows(slot, mbase, 0, EXPERT_ROWS)

    @pl.when(i == pl.num_programs(0) - 1)
    def _():
        ybuf_drain(0)
        ybuf_drain(1)


def _experts(items, xw, meta, wg, wu, wd, *, n_tok, name):
    nip = items.shape[0] // ITEM_FIELDS
    n_rows = n_tok * TOP_K
    n_blk = n_rows // EXPERT_ROWS
    n_items = n_blk + N_EXPERTS
    kern = functools.partial(_experts_kernel, nip=nip, n_blk=n_blk)
    any_spec = pl.BlockSpec(memory_space=pl.ANY)
    return pl.pallas_call(
        kern,
        grid_spec=pltpu.PrefetchScalarGridSpec(
            num_scalar_prefetch=1,
            grid=(n_items,),
            in_specs=[any_spec] * 5,
            out_specs=any_spec,
            scratch_shapes=[pltpu.VMEM(xw.shape, I32),
                            pltpu.SMEM((META_RING * META_CHUNK,), I32),
                            pltpu.VMEM((PACKED_ROWS * GATHER_STRIDE, LANES), I32),
                            pltpu.VMEM((EXPERT_ROWS, D_MODEL), BF16),
                            pltpu.VMEM((2, D_MODEL, EXPERT_FF), F32),
                            pltpu.VMEM((2, D_MODEL, EXPERT_FF), F32),
                            pltpu.VMEM((2, EXPERT_FF, D_MODEL), F32),
                            pltpu.VMEM((D_MODEL, EXPERT_FF), BF16),
                            pltpu.VMEM((D_MODEL, EXPERT_FF), BF16),
                            pltpu.VMEM((EXPERT_FF, D_MODEL), BF16),
                            pltpu.VMEM((EXPERT_ROWS, D_MODEL), F32),
                            pltpu.VMEM((2, EXPERT_ROWS * ROW_CHUNKS, LANES), F32),
                            pltpu.SMEM((2,), I32),
                            pltpu.SemaphoreType.DMA,
                            pltpu.SemaphoreType.DMA((META_RING,)),
                            pltpu.SemaphoreType.DMA((2,)),
                            pltpu.SemaphoreType.DMA((2,))],
        ),
        out_shape=jax.ShapeDtypeStruct((n_rows * ROW_CHUNKS, LANES), F32),
        compiler_params=_cparams(("arbitrary",)),
        name=name,
    )(items, xw, meta, wg, wu, wd)


def _combine_kernel(ys_ref, x_ref, wl_ref, wsg_ref, wsu_ref, wsd_ref, g_ref, b_ref, op_ref, os_ref,
                    routed, *, np_tiles):
    x = x_ref[...]
    xb = x.astype(BF16)
    hid = (jax.nn.silu(jnp.dot(xb, wsg_ref[...], preferred_element_type=F32))
           * jnp.dot(xb, wsu_ref[...], preferred_element_type=F32))
    shared = jnp.dot(hid.astype(BF16), wsd_ref[...], preferred_element_type=F32)
    acc = None
    for k in range(TOP_K):
        part = wl_ref[:, k:k + 1, :] * ys_ref[k]
        acc = part if acc is None else acc + part
    routed[...] = acc
    ffn = jnp.concatenate([routed[:, j, :] for j in range(ROW_CHUNKS)], axis=-1) + shared
    out = _layer_norm(DEEPNORM_ALPHA * x + ffn, g_ref[...], b_ref[...])
    i = pl.program_id(0)

    @pl.when(i < np_tiles)
    def _():
        op_ref[...] = out

    @pl.when(i >= np_tiles)
    def _():
        os_ref[...] = out


def _combine(ys, x1, w_tok, wsg, wsu, wsd, g, b, *, n_prompt, tm, name):
    t = x1.shape[0]
    np_tiles = n_prompt // tm
    p_map, s_map = _pair_maps(np_tiles)
    row = lambda i: (i, 0)
    full = lambda i: (0, 0)
    return pl.pallas_call(
        functools.partial(_combine_kernel, np_tiles=np_tiles),
        grid=(t // tm,),
        in_specs=[pl.BlockSpec((TOP_K, tm, ROW_CHUNKS, LANES), lambda i: (0, i, 0, 0)),
                  pl.BlockSpec((tm, D_MODEL), row),
                  pl.BlockSpec((tm, TOP_K, LANES), lambda i: (i, 0, 0)),
                  pl.BlockSpec((D_MODEL, SHARED_FF), full),
                  pl.BlockSpec((D_MODEL, SHARED_FF), full),
                  pl.BlockSpec((SHARED_FF, D_MODEL), full),
                  pl.BlockSpec((1, D_MODEL), full),
                  pl.BlockSpec((1, D_MODEL), full)],
        out_specs=[pl.BlockSpec((tm, D_MODEL), p_map), pl.BlockSpec((tm, D_MODEL), s_map)],
        out_shape=[jax.ShapeDtypeStruct((n_prompt, D_MODEL), F32),
                   jax.ShapeDtypeStruct((t - n_prompt, D_MODEL), F32)],
        scratch_shapes=[pltpu.VMEM((tm, ROW_CHUNKS, LANES), F32)],
        compiler_params=_cparams(("arbitrary",)),
        name=name,
    )(ys, x1, w_tok, wsg, wsu, wsd, g, b)


def _pack_w_in(w_in):
    sizes = (GLA_KEY_DIM, GLA_KEY_DIM, GLA_VAL_DIM, GLA_VAL_DIM, GLA_GATE_RANK,
             SWA_Q_DIM, SWA_KV_DIM, SWA_KV_DIM, D_MODEL, D_MODEL)
    offs = [0]
    for s in sizes:
        offs.append(offs[-1] + s)
    qa, ka, va, ga, gk, qb, kb, vb, gate_a, gate_b = (w_in[:, offs[i]:offs[i + 1]] for i in range(10))
    pad = lambda w, n: jnp.pad(w, ((0, 0), (0, n - w.shape[1])))
    packed = jnp.concatenate([qa, ka, va, ga, qb, gate_a, gate_b, kb, vb, pad(gk, 2 * LANES)], axis=1)
    assert packed.shape[1] == H_WIDTH
    return packed.astype(BF16), jnp.concatenate([kb, vb], axis=1).astype(BF16)


def kernel(x_prompt, x_sample, state_gla, cache_swa_k, cache_swa_v, w_in, w_gk_up, b_gk, gla_norm_g,
           attn_sinks, w_proj_a, w_proj_b, w_out, ln1_g, ln1_b, w_router, router_bias,
           w_expert_gate, w_expert_up, w_expert_down, w_shared_gate, w_shared_up, w_shared_down,
           ln2_g, ln2_b):
    assert w_in.shape[0] == 1, "single-layer trunk"
    bp, lp, d = x_prompt.shape
    bs, ls, _ = x_sample.shape
    assert d == D_MODEL and ls == SUBLANES and cache_swa_k.shape[2] == WINDOW
    tp, ts = bp * lp, bs * ls
    t = tp + ts

    xp = x_prompt.reshape(tp, d)
    xs = x_sample.reshape(ts, d)
    w_main, w_kv = _pack_w_in(w_in[0])
    h = _matmul(xp, xs, w_main, BF16, _pair_tile(tp, ts, 1024), H_TN, "proj_in")

    xp_tail = x_prompt[:, lp - WINDOW:].reshape(bp * WINDOW, d)
    kv_tail = _matmul(xp_tail, xs, w_kv, F32, _pair_tile(bp * WINDOW, ts, 512), 2 * SWA_KV_DIM,
                      "proj_kv_tail")

    wup = jnp.pad(w_gk_up[0], ((0, LANES - GLA_GATE_RANK), (0, 0))).astype(BF16)
    bgk = b_gk[0].reshape(1, GLA_KEY_DIM)
    gn = gla_norm_g[0].reshape(1, GLA_DV)
    oa_p, s_prompt = _gla(h, wup, bgk, gn, None, row0=0, n_seq=bp, seq_len=lp, par=math.gcd(bp, 4), nb=1,
                          c=GLA_CHUNK, sub=2, name="gla_prompt")
    oa_s, s_sample = _gla(h, wup, bgk, gn, state_gla[0], row0=tp, n_seq=bs, seq_len=ls, par=1, nb=8,
                          c=math.gcd(ls, GLA_CHUNK), sub=1, name="gla_sample")
    oa_p = oa_p.reshape(tp, GLA_VAL_DIM)
    oa_s = oa_s.reshape(ts, GLA_VAL_DIM)

    sinks = attn_sinks[0]
    ob_p = _swa_prompt(h, sinks, n_seq=bp, seq_len=lp, name="swa_prompt")
    ob_s, k_sample, v_sample = _swa_sample(h, sinks, kv_tail, cache_swa_k, cache_swa_v, row0=tp,
                                           tail_row0=bp * WINDOW, n_seq=bs, lq=ls, nb=8, name="swa_sample")

    x1, xw = _merge((oa_p, oa_s), (ob_p, ob_s), h, (xp, xs), w_proj_a[0].astype(BF16),
                    w_proj_b[0].astype(BF16), w_out[0].astype(BF16), ln1_g[0].reshape(1, d),
                    ln1_b[0].reshape(1, d), tm=_pair_tile(tp, ts, 512), name="merge_ln1")

    wr_t = w_router[0].T
    wr_hi = wr_t.astype(BF16)
    wr_lo = (wr_t - wr_hi.astype(F32)).astype(BF16)
    tm_r = _tile(t, 512)
    idx, w_lanes, rank, cnt_col, cnt_row = _router(x1, wr_hi, wr_lo, router_bias[0].reshape(N_EXPERTS, 1),
                                                   tm=tm_r, name="router")
    meta, items = _plan(cnt_col, cnt_row, idx, rank, tm=tm_r, name="plan")
    ys = _experts(items.reshape(-1), xw, meta.reshape(-1), w_expert_gate[0], w_expert_up[0],
                  w_expert_down[0], n_tok=t, name="experts")
    y_p, y_s = _combine(ys.reshape(TOP_K, t, ROW_CHUNKS, LANES), x1, w_lanes.reshape(t, TOP_K, LANES),
                        w_shared_gate[0].astype(BF16), w_shared_up[0].astype(BF16),
                        w_shared_down[0].astype(BF16), ln2_g[0].reshape(1, d), ln2_b[0].reshape(1, d),
                        n_prompt=tp, tm=_pair_tile(tp, ts, 128), name="combine_ln2")

    y_prompt = y_p.reshape(bp, lp, d)
    y_sample = y_s.reshape(bs, ls, d)
    k_tail = kv_tail[:, :SWA_KV_DIM]
    v_tail = kv_tail[:, SWA_KV_DIM:]
    kv_shape = (SWA_KV_HEADS, SWA_HEAD_DIM)
    k_prompt = k_tail[:bp * WINDOW].reshape(1, bp, WINDOW, *kv_shape)
    v_prompt = v_tail[:bp * WINDOW].reshape(1, bp, WINDOW, *kv_shape)
    return (y_prompt, y_sample, s_prompt[None], s_sample[None], k_prompt, v_prompt, k_sample, v_sample)
```

```python
import functools
import math

import jax
import jax.numpy as jnp
from jax import lax
from jax.experimental import pallas as pl
from jax.experimental.pallas import tpu as pltpu

F32 = jnp.float32
BF16 = jnp.bfloat16
I32 = jnp.int32

D_MODEL = 1024
GLA_HEADS = 4
GLA_DK = 128
GLA_DV = 256
GLA_KEY_DIM = GLA_HEADS * GLA_DK
GLA_VAL_DIM = GLA_HEADS * GLA_DV
GLA_GATE_RANK = 16
GLA_GATE_NORMALIZER = 16.0
GLA_CHUNK = 64
SWA_HEADS = 16
SWA_KV_HEADS = 4
SWA_GROUP = SWA_HEADS // SWA_KV_HEADS
SWA_HEAD_DIM = 64
SWA_Q_DIM = SWA_HEADS * SWA_HEAD_DIM
SWA_KV_DIM = SWA_KV_HEADS * SWA_HEAD_DIM
WINDOW = 128
N_EXPERTS = 256
TOP_K = 8
N_GROUPS = 8
GROUP_SIZE = N_EXPERTS // N_GROUPS
TOPK_GROUPS = 4
EXPERT_FF = 256
SHARED_FF = 256
ROUTED_SCALE = 2.5
DEEPNORM_ALPHA = 2.0 ** 0.25
EPS = 1e-5

LANES = 128
SUBLANES = 8
ROW_CHUNKS = D_MODEL // LANES
VMEM_LIMIT = 56 * 1024 * 1024

H_QA, H_KA, H_VA, H_GA, H_QB, H_GATE_A, H_GATE_B, H_KB, H_VB, H_GK = (
    0, 512, 1024, 2048, 3072, 4096, 5120, 6144, 6400, 6656)
H_WIDTH = 6912
H_TN = 2304

EXPERT_ROWS = 256
PACKED_ROWS = ROW_CHUNKS // 2
HIGH_HALF = -65536


def _cparams(sem, vmem=VMEM_LIMIT):
    return pltpu.CompilerParams(dimension_semantics=sem, vmem_limit_bytes=vmem)


def _tile(n, pref):
    t = min(n, pref)
    while n % t:
        t -= LANES
    assert t > 0 and t % LANES == 0, (n, pref)
    return t


def _pair_tile(n_p, n_s, pref):
    return _tile(math.gcd(n_p, n_s), pref)


def _pair_maps(np_tiles, col=0, extra=0):
    def p_map(i, *_):
        return (jnp.minimum(i, np_tiles - 1), col)

    def s_map(i, *_):
        return (jnp.maximum(i - np_tiles, 0), col)

    return p_map, s_map


def _pair_value(i, np_tiles, p_ref, s_ref):
    return jnp.where(i < np_tiles, p_ref[...], s_ref[...])


def _mm_kernel(xp_ref, xs_ref, w_ref, o_ref, *, np_tiles):
    x = _pair_value(pl.program_id(0), np_tiles, xp_ref, xs_ref)
    o_ref[...] = jnp.dot(x.astype(BF16), w_ref[...], preferred_element_type=F32).astype(o_ref.dtype)


def _matmul(xp, xs, w, out_dtype, tm, tn, name):
    k = xp.shape[1]
    m = xp.shape[0] + xs.shape[0]
    n = w.shape[1]
    np_tiles = xp.shape[0] // tm
    p_map, s_map = _pair_maps(np_tiles)
    return pl.pallas_call(
        functools.partial(_mm_kernel, np_tiles=np_tiles),
        grid=(m // tm, n // tn),
        in_specs=[pl.BlockSpec((tm, k), p_map),
                  pl.BlockSpec((tm, k), s_map),
                  pl.BlockSpec((k, tn), lambda i, j: (0, j))],
        out_specs=pl.BlockSpec((tm, tn), lambda i, j: (i, j)),
        out_shape=jax.ShapeDtypeStruct((m, n), out_dtype),
        compiler_params=_cparams(("parallel", "arbitrary")),
        name=name,
    )(xp, xs, w)


def _split_bf16(x):
    hi = x.astype(BF16)
    lo = (x - hi.astype(F32)).astype(BF16)
    return hi, lo


def _gla_kernel(*refs, par, nb, c, sub, has_s0):
    q_refs, k_refs, v_refs, ga_refs, gk_refs = (refs[n * par:(n + 1) * par] for n in range(5))
    rest = refs[5 * par:]
    n_state = par * GLA_HEADS
    s_scr = [rest[len(rest) - n_state + p * GLA_HEADS:len(rest) - n_state + (p + 1) * GLA_HEADS]
             for p in range(par)]
    rest = rest[:len(rest) - n_state]
    if has_s0:
        wup_ref, bgk_ref, gn_ref, s0_ref, o_ref, sout_ref = rest
    else:
        wup_ref, bgk_ref, gn_ref, o_ref, sout_ref = rest
        s0_ref = None
    ci = pl.program_id(1)
    rows = nb * c

    @pl.when(ci == 0)
    def _():
        for p in range(par):
            for h in range(GLA_HEADS):
                if has_s0:
                    s_scr[p][h][...] = s0_ref[p * nb:(p + 1) * nb, h]
                else:
                    s_scr[p][h][...] = jnp.zeros_like(s_scr[p][h])

    r_i = lax.broadcasted_iota(I32, (rows, rows), 0)
    c_i = lax.broadcasted_iota(I32, (rows, rows), 1)
    same_seq = (r_i // c) == (c_i // c)
    causal = same_seq & (c_i <= r_i)
    tri = jnp.where(causal, 1.0, 0.0).astype(BF16)
    seg = jnp.where(same_seq, 1.0, 0.0).astype(BF16)
    ones_kv = jnp.ones((rows, GLA_DV), BF16)
    seq_of_row = lax.broadcasted_iota(I32, (rows, 1), 0) // c
    tn = (((0,), (0,)), ((), ()))

    def decays(p, rs):
        pre = jnp.dot(gk_refs[p][rs, :], wup_ref[...], preferred_element_type=F32) + bgk_ref[...]
        log_a = jax.nn.log_sigmoid(pre) / GLA_GATE_NORMALIZER
        la_hi, la_lo = _split_bf16(log_a)
        b = (jnp.dot(tri, la_hi, preferred_element_type=F32)
             + jnp.dot(tri, la_lo, preferred_element_type=F32))
        b_last = (jnp.dot(seg, la_hi, preferred_element_type=F32)
                  + jnp.dot(seg, la_lo, preferred_element_type=F32))
        q = q_refs[p][rs, :].astype(F32) * (GLA_DK ** -0.5)
        k = k_refs[p][rs, :].astype(F32)
        q_dec = (q * jnp.exp(b)).astype(BF16)
        k_dec = (k * jnp.exp(-b)).astype(BF16)
        k_rem = k * jnp.exp(b_last - b)
        return q_dec, k_dec, k_rem, la_hi, la_lo

    def head(p, rs, h, q_dec, k_dec, k_rem, la_hi, la_lo):
        ks = slice(h * GLA_DK, (h + 1) * GLA_DK)
        vh = v_refs[p][rs, h * GLA_DV:(h + 1) * GLA_DV]
        att = lax.dot_general(q_dec[:, ks], k_dec[:, ks], (((1,), (1,)), ((), ())),
                              preferred_element_type=F32)
        att = jnp.where(causal, att, 0.0).astype(BF16)
        o_h = jnp.dot(att, vh, preferred_element_type=F32)
        for j in range(nb):
            mine = (seq_of_row == j) if nb > 1 else None
            pick = (lambda a: jnp.where(mine, a, 0.0)) if nb > 1 else (lambda a: a)
            s_old = s_scr[p][h][j]
            o_h = o_h + pick(jnp.dot(q_dec[:, ks], s_old.astype(BF16), preferred_element_type=F32))
            dec = (lax.dot_general(pick(la_hi[:, ks].astype(F32)).astype(BF16), ones_kv, tn,
                                   preferred_element_type=F32)
                   + lax.dot_general(pick(la_lo[:, ks].astype(F32)).astype(BF16), ones_kv, tn,
                                     preferred_element_type=F32))
            upd = lax.dot_general(pick(k_rem[:, ks]).astype(BF16), vh, tn, preferred_element_type=F32)
            s_scr[p][h][j] = jnp.exp(dec) * s_old + upd
        return o_h * lax.rsqrt(jnp.mean(jnp.square(o_h), axis=-1, keepdims=True) + EPS) * gn_ref[...]

    for s in range(sub):
        rs = slice(s * rows, (s + 1) * rows)
        staged = [decays(p, rs) for p in range(par)]
        outs = [[] for _ in range(par)]
        for h in range(GLA_HEADS):
            for p in range(par):
                outs[p].append(head(p, rs, h, *staged[p]))
        for p in range(par):
            o = jnp.concatenate(outs[p], axis=-1) * jax.nn.silu(ga_refs[p][rs, :].astype(F32))
            o_ref[p, rs, :] = o.astype(o_ref.dtype)

    @pl.when(ci == pl.num_programs(1) - 1)
    def _():
        for p in range(par):
            for h in range(GLA_HEADS):
                sout_ref[p * nb:(p + 1) * nb, h] = s_scr[p][h][...]


def _gla(h, wup, bgk, gn, s0, *, row0, n_seq, seq_len, par, nb, c, sub, name):
    rows = nb * c * sub
    n_blocks = n_seq // nb
    n_steps = seq_len // (c * sub)
    rb0 = row0 // rows

    def rmap(p, col):
        return lambda g, i: (rb0 + (g * par + p) * n_steps + i, col)

    fields = ((GLA_KEY_DIM, H_QA), (GLA_KEY_DIM, H_KA), (GLA_VAL_DIM, H_VA), (GLA_VAL_DIM, H_GA), (LANES, H_GK))
    in_specs = [pl.BlockSpec((rows, width), rmap(p, off // width)) for width, off in fields for p in range(par)]
    in_specs += [pl.BlockSpec((LANES, GLA_KEY_DIM), lambda g, i: (0, 0)),
                 pl.BlockSpec((1, GLA_KEY_DIM), lambda g, i: (0, 0)),
                 pl.BlockSpec((1, GLA_DV), lambda g, i: (0, 0))]
    args = [h] * (5 * par) + [wup, bgk, gn]
    state_spec = pl.BlockSpec((par * nb, GLA_HEADS, GLA_DK, GLA_DV), lambda g, i: (g, 0, 0, 0))
    if s0 is not None:
        in_specs.append(state_spec)
        args.append(s0)
    kern = functools.partial(_gla_kernel, par=par, nb=nb, c=c, sub=sub, has_s0=s0 is not None)
    return pl.pallas_call(
        kern,
        grid=(n_blocks // par, n_steps),
        in_specs=in_specs,
        out_specs=[pl.BlockSpec((par, rows, GLA_VAL_DIM), lambda g, i: (g, i, 0)), state_spec],
        out_shape=[jax.ShapeDtypeStruct((n_blocks, nb * seq_len, GLA_VAL_DIM), BF16),
                   jax.ShapeDtypeStruct((n_seq, GLA_HEADS, GLA_DK, GLA_DV), F32)],
        scratch_shapes=[pltpu.VMEM((nb, GLA_DK, GLA_DV), F32)] * (par * GLA_HEADS),
        compiler_params=_cparams(("parallel", "arbitrary")),
        name=name,
    )(*args)


def _alibi_slope(head):
    return 2.0 ** (-8.0 * (head + 1) / SWA_HEADS)


def _swa_softmax_pv(parts, sink):
    m = sink
    for s, _ in parts:
        m = jnp.maximum(m, jnp.max(s, axis=-1, keepdims=True))
    denom = jnp.exp(sink - m)
    acc = None
    for s, v in parts:
        p = jnp.exp(s - m)
        denom = denom + jnp.sum(p, axis=-1, keepdims=True)
        pv = jnp.dot(p.astype(BF16), v, preferred_element_type=F32)
        acc = pv if acc is None else acc + pv
    return acc / denom


def _swa_prompt_kernel(sink_ref, q_ref, kp_ref, vp_ref, kc_ref, vc_ref, o_ref, bias):
    i = pl.program_id(1)
    span = 2 * WINDOW
    col = lax.broadcasted_iota(I32, (WINDOW, span), 1)

    @pl.when(i == 0)
    def _():
        dist_i = lax.broadcasted_iota(I32, (WINDOW, span), 0) + WINDOW - col
        in_window = (dist_i >= 0) & (dist_i < WINDOW)
        dist = dist_i.astype(F32)
        for hh in range(SWA_HEADS):
            bias[hh] = jnp.where(in_window, -_alibi_slope(hh) * dist, -jnp.inf)

    kcat = jnp.concatenate([kp_ref[...], kc_ref[...]], axis=0)
    vcat = jnp.concatenate([vp_ref[...], vc_ref[...]], axis=0)
    no_past = (col < WINDOW) & (i == 0)
    scale = SWA_HEAD_DIM ** -0.5
    outs = []
    for hh in range(SWA_HEADS):
        g = hh // SWA_GROUP
        gs = slice(g * SWA_HEAD_DIM, (g + 1) * SWA_HEAD_DIM)
        qh = q_ref[:, hh * SWA_HEAD_DIM:(hh + 1) * SWA_HEAD_DIM] * scale
        s = lax.dot_general(qh, kcat[:, gs], (((1,), (1,)), ((), ())), preferred_element_type=F32)
        s = jnp.where(no_past, -jnp.inf, s + bias[hh])
        outs.append(_swa_softmax_pv([(s, vcat[:, gs])], sink_ref[hh]))
    o_ref[...] = jnp.concatenate(outs, axis=-1).astype(o_ref.dtype)


def _swa_prompt(h, sinks, *, n_seq, seq_len, name):
    nq = seq_len // WINDOW
    qcol = H_QB // SWA_Q_DIM
    kcol = H_KB // SWA_KV_DIM
    vcol = H_VB // SWA_KV_DIM

    def cur(col):
        return lambda b, i, sk: (b * nq + i, col)

    def prev(col):
        return lambda b, i, sk: (b * nq + jnp.maximum(i - 1, 0), col)

    return pl.pallas_call(
        _swa_prompt_kernel,
        grid_spec=pltpu.PrefetchScalarGridSpec(
            num_scalar_prefetch=1,
            grid=(n_seq, nq),
            in_specs=[pl.BlockSpec((WINDOW, SWA_Q_DIM), cur(qcol)),
                      pl.BlockSpec((WINDOW, SWA_KV_DIM), prev(kcol)),
                      pl.BlockSpec((WINDOW, SWA_KV_DIM), prev(vcol)),
                      pl.BlockSpec((WINDOW, SWA_KV_DIM), cur(kcol)),
                      pl.BlockSpec((WINDOW, SWA_KV_DIM), cur(vcol))],
            out_specs=pl.BlockSpec((WINDOW, SWA_Q_DIM), lambda b, i, sk: (b * nq + i, 0)),
            scratch_shapes=[pltpu.VMEM((SWA_HEADS, WINDOW, 2 * WINDOW), F32)],
        ),
        out_shape=jax.ShapeDtypeStruct((n_seq * seq_len, SWA_Q_DIM), BF16),
        compiler_params=_cparams(("parallel", "arbitrary")),
        name=name,
    )(sinks, h, h, h, h, h)


def _swa_sample_kernel(sink_ref, q_ref, kc_ref, vc_ref, kp_ref, vp_ref, o_ref, *, nb, lq):
    rows = SWA_HEADS * lq
    grp_rows = SWA_GROUP * lq
    hd = SWA_HEAD_DIM
    nt = (((1,), (1,)), ((), ()))
    head_of_row = lax.broadcasted_iota(I32, (rows, 1), 0) // lq
    slope = jnp.zeros((rows, 1), F32)
    sink = jnp.zeros((rows, 1), F32)
    for hh in range(SWA_HEADS):
        slope = jnp.where(head_of_row == hh, _alibi_slope(hh), slope)
        sink = jnp.where(head_of_row == hh, sink_ref[hh], sink)
    qi = lax.broadcasted_iota(I32, (rows, WINDOW), 0) % lq
    dist_p = qi + WINDOW - lax.broadcasted_iota(I32, (rows, WINDOW), 1)
    valid_p = dist_p < WINDOW
    bias_p = slope * dist_p.astype(F32)
    dist_c = lax.broadcasted_iota(I32, (rows, lq), 0) % lq - lax.broadcasted_iota(I32, (rows, lq), 1)
    valid_c = dist_c >= 0
    bias_c = slope * dist_c.astype(F32)
    scale = hd ** -0.5

    q_all = q_ref[...].astype(F32)
    kc_all = kc_ref[...].astype(F32)
    vc_all = vc_ref[...].astype(F32)
    seq_outs = []
    for j in range(nb):
        js = slice(j * lq, (j + 1) * lq)
        pieces = []
        for hh in range(SWA_HEADS):
            g = hh // SWA_GROUP
            parts = []
            if g:
                parts.append(jnp.zeros((lq, g * hd), F32))
            parts.append(q_all[js, hh * hd:(hh + 1) * hd])
            if g < SWA_KV_HEADS - 1:
                parts.append(jnp.zeros((lq, (SWA_KV_HEADS - 1 - g) * hd), F32))
            pieces.append(jnp.concatenate(parts, axis=-1))
        q_big = jnp.concatenate(pieces, axis=0).astype(BF16)
        kp = kp_ref[j].astype(BF16)
        vp = vp_ref[j].astype(BF16)
        kcj = kc_all[js, :].astype(BF16)
        vcj = vc_all[js, :].astype(BF16)
        s_p = lax.dot_general(q_big, kp, nt, preferred_element_type=F32)
        s_p = jnp.where(valid_p, s_p * scale - bias_p, -jnp.inf)
        s_c = lax.dot_general(q_big, kcj, nt, preferred_element_type=F32)
        s_c = jnp.where(valid_c, s_c * scale - bias_c, -jnp.inf)
        o_big = _swa_softmax_pv([(s_p, vp), (s_c, vcj)], sink)
        o_grp = [o_big[g * grp_rows:(g + 1) * grp_rows, g * hd:(g + 1) * hd] for g in range(SWA_KV_HEADS)]
        o_heads = jnp.concatenate(o_grp, axis=0)
        seq_outs.append(jnp.concatenate([o_heads[hh * lq:(hh + 1) * lq, :] for hh in range(SWA_HEADS)],
                                        axis=-1))
    o_ref[...] = jnp.concatenate(seq_outs, axis=0).astype(o_ref.dtype)


def _swa_sample(h, sinks, k_past, v_past, *, row0, n_seq, lq, nb, name):
    rows = nb * lq
    rb0 = row0 // rows
    qcol = H_QB // SWA_Q_DIM
    kcol = H_KB // SWA_KV_DIM
    vcol = H_VB // SWA_KV_DIM
    kern = functools.partial(_swa_sample_kernel, nb=nb, lq=lq)
    past_spec = pl.BlockSpec((nb, WINDOW, SWA_KV_DIM), lambda g, sk: (g, 0, 0))
    return pl.pallas_call(
        kern,
        grid_spec=pltpu.PrefetchScalarGridSpec(
            num_scalar_prefetch=1,
            grid=(n_seq // nb,),
            in_specs=[pl.BlockSpec((rows, SWA_Q_DIM), lambda g, sk: (rb0 + g, qcol)),
                      pl.BlockSpec((rows, SWA_KV_DIM), lambda g, sk: (rb0 + g, kcol)),
                      pl.BlockSpec((rows, SWA_KV_DIM), lambda g, sk: (rb0 + g, vcol)),
                      past_spec, past_spec],
            out_specs=pl.BlockSpec((rows, SWA_Q_DIM), lambda g, sk: (g, 0)),
        ),
        out_shape=jax.ShapeDtypeStruct((n_seq * lq, SWA_Q_DIM), BF16),
        compiler_params=_cparams(("parallel",)),
        name=name,
    )(sinks, h, h, h, k_past, v_past)


def _layer_norm(x, g, b):
    mu = jnp.mean(x, axis=-1, keepdims=True)
    xc = x - mu
    var = jnp.mean(jnp.square(xc), axis=-1, keepdims=True)
    return xc * lax.rsqrt(var + EPS) * g + b


def _pack_bf16_pairs(x, s):
    lo = lax.bitcast_convert_type(x[:, (2 * s) * LANES:(2 * s + 1) * LANES].astype(BF16).astype(F32), I32)
    hi = lax.bitcast_convert_type(x[:, (2 * s + 1) * LANES:(2 * s + 2) * LANES].astype(BF16).astype(F32), I32)
    return lax.shift_right_logical(lo, 16) | (hi & HIGH_HALF)


def _unpack_bf16_pairs(w):
    lo = lax.bitcast_convert_type(lax.shift_left(w, 16), F32).astype(BF16)
    hi = lax.bitcast_convert_type(w & HIGH_HALF, F32).astype(BF16)
    return lo, hi


def _merge_kernel(oap_ref, oas_ref, obp_ref, obs_ref, ga_ref, gb_ref, xp_ref, xs_ref,
                  wpa_ref, wpb_ref, wout_ref, g_ref, b_ref, o_ref, xw_ref, *, np_tiles):
    i = pl.program_id(0)
    tm = o_ref.shape[0]
    br_a = jnp.dot(_pair_value(i, np_tiles, oap_ref, oas_ref), wpa_ref[...], preferred_element_type=F32)
    br_b = jnp.dot(_pair_value(i, np_tiles, obp_ref, obs_ref), wpb_ref[...], preferred_element_type=F32)
    merged = (jax.nn.sigmoid(ga_ref[...].astype(F32)) * br_a
              + jax.nn.sigmoid(gb_ref[...].astype(F32)) * br_b)
    mix = jnp.dot(merged.astype(BF16), wout_ref[...], preferred_element_type=F32)
    x = _pair_value(i, np_tiles, xp_ref, xs_ref)
    x1 = _layer_norm(DEEPNORM_ALPHA * x + mix, g_ref[...], b_ref[...])
    o_ref[...] = x1
    for s in range(PACKED_ROWS):
        xw_ref[pl.ds(s, tm, stride=PACKED_ROWS), :] = _pack_bf16_pairs(x1, s)


def _merge(oa, ob, h, x, wpa, wpb, wout, g, b, *, tm, name):
    t = h.shape[0]
    np_tiles = x[0].shape[0] // tm
    p_map, s_map = _pair_maps(np_tiles)
    row = lambda i: (i, 0)
    full = lambda i: (0, 0)
    pair = [pl.BlockSpec((tm, D_MODEL), p_map), pl.BlockSpec((tm, D_MODEL), s_map)]
    return pl.pallas_call(
        functools.partial(_merge_kernel, np_tiles=np_tiles),
        grid=(t // tm,),
        in_specs=pair + pair + [
                  pl.BlockSpec((tm, D_MODEL), lambda i: (i, H_GATE_A // D_MODEL)),
                  pl.BlockSpec((tm, D_MODEL), lambda i: (i, H_GATE_B // D_MODEL))] + pair + [
                  pl.BlockSpec((D_MODEL, D_MODEL), full),
                  pl.BlockSpec((D_MODEL, D_MODEL), full),
                  pl.BlockSpec((D_MODEL, D_MODEL), full),
                  pl.BlockSpec((1, D_MODEL), full),
                  pl.BlockSpec((1, D_MODEL), full)],
        out_specs=[pl.BlockSpec((tm, D_MODEL), row),
                   pl.BlockSpec((tm * PACKED_ROWS, LANES), row)],
        out_shape=[jax.ShapeDtypeStruct((t, D_MODEL), F32),
                   jax.ShapeDtypeStruct((t * PACKED_ROWS, LANES), I32)],
        compiler_params=_cparams(("parallel",)),
        name=name,
    )(*oa, *ob, h, h, *x, wpa, wpb, wout, g, b)


def _router_kernel(x_ref, whi_ref, wlo_ref, bias_ref, idx_ref, wl_ref, rank_ref, cntc_ref, cntr_ref,
                   cntc_scr, cntr_scr):
    i = pl.program_id(0)
    tm = x_ref.shape[0]

    @pl.when(i == 0)
    def _():
        cntc_scr[...] = jnp.zeros_like(cntc_scr)
        cntr_scr[...] = jnp.zeros_like(cntr_scr)

    x_hi, x_lo = _split_bf16(x_ref[...])
    nt = (((1,), (1,)), ((), ()))
    logits = (lax.dot_general(whi_ref[...], x_hi, nt, preferred_element_type=F32)
              + lax.dot_general(whi_ref[...], x_lo, nt, preferred_element_type=F32)
              + lax.dot_general(wlo_ref[...], x_hi, nt, preferred_element_type=F32))
    scores = jax.nn.sigmoid(logits)
    biased = scores + bias_ref[...]

    grouped = biased.reshape(N_GROUPS, GROUP_SIZE, tm)
    m1 = jnp.max(grouped, axis=1)
    n_top = jnp.sum(jnp.where(grouped == m1[:, None, :], 1.0, 0.0), axis=1)
    m2 = jnp.max(jnp.where(grouped < m1[:, None, :], grouped, -jnp.inf), axis=1)
    gscore = m1 + jnp.where(n_top >= 2.0, m1, m2)

    g_iota = lax.broadcasted_iota(I32, (N_GROUPS, tm), 0)
    beaten = jnp.zeros((N_GROUPS, tm), I32)
    for g in range(N_GROUPS):
        other = gscore[g:g + 1, :]
        ahead = (other > gscore) | ((other == gscore) & (g < g_iota))
        beaten = beaten + jnp.where(ahead, 1, 0)
    keep = jnp.where(beaten < TOPK_GROUPS, 1.0, 0.0)
    masked = jnp.where(keep[:, None, :] > 0.5, grouped, -jnp.inf).reshape(N_EXPERTS, tm)

    e_iota = lax.broadcasted_iota(I32, (N_EXPERTS, tm), 0)
    sel_f = jnp.zeros((N_EXPERTS, tm), F32)
    ids = []
    for _ in range(TOP_K):
        best = jnp.max(masked, axis=0, keepdims=True)
        idx = jnp.min(jnp.where(masked == best, e_iota, N_EXPERTS), axis=0, keepdims=True)
        hit = e_iota == idx
        sel_f = sel_f + jnp.where(hit, 1.0, 0.0)
        masked = jnp.where(hit, -jnp.inf, masked)
        ids.append(idx)

    top_sum = jnp.sum(sel_f * scores, axis=0, keepdims=True)

    t_r = lax.broadcasted_iota(I32, (tm, tm), 0)
    t_c = lax.broadcasted_iota(I32, (tm, tm), 1)
    before = jnp.where(t_r < t_c, 1.0, 0.0).astype(BF16)
    sel_b = sel_f.astype(BF16)
    rank = jnp.dot(sel_b, before, preferred_element_type=F32) + cntc_scr[...]
    cntc_scr[...] = cntc_scr[...] + jnp.sum(sel_f, axis=1, keepdims=True)
    cntr_scr[...] = cntr_scr[...] + lax.dot_general(jnp.ones((SUBLANES, tm), BF16), sel_b, nt,
                                                    preferred_element_type=F32)

    idx_rows, w_rows, rank_rows = [], [], []
    for idx in ids:
        hit = e_iota == idx
        w = jnp.sum(jnp.where(hit, scores, 0.0), axis=0, keepdims=True)
        w_rows.append(w / top_sum * ROUTED_SCALE)
        rank_rows.append(jnp.sum(jnp.where(hit, rank, 0.0), axis=0, keepdims=True))
        idx_rows.append(idx)
    idx_ref[...] = jnp.concatenate(idx_rows, axis=0)
    rank_ref[...] = jnp.concatenate(rank_rows, axis=0).astype(I32)

    w_all = jnp.concatenate(w_rows, axis=0)
    w_1 = w_all.astype(BF16)
    r_1 = w_all - w_1.astype(F32)
    w_2 = r_1.astype(BF16)
    w_3 = (r_1 - w_2.astype(F32)).astype(BF16)
    terms = jnp.concatenate([w_1, w_2, w_3, jnp.zeros_like(w_1)], axis=0)
    t_row = lax.broadcasted_iota(I32, (4 * TOP_K, TOP_K * LANES), 0)
    t_col = lax.broadcasted_iota(I32, (4 * TOP_K, TOP_K * LANES), 1)
    spread = jnp.where((t_row % TOP_K) == (t_col // LANES), 1.0, 0.0).astype(BF16)
    w_lanes = lax.dot_general(terms, spread, (((0,), (0,)), ((), ())), preferred_element_type=F32)
    for k in range(TOP_K):
        wl_ref[pl.ds(k, tm, stride=TOP_K), :] = w_lanes[:, k * LANES:(k + 1) * LANES]

    @pl.when(i == pl.num_programs(0) - 1)
    def _():
        cntc_ref[...] = cntc_scr[...].astype(I32)
        cntr_ref[...] = cntr_scr[...].astype(I32)


def _router(x1, w_hi, w_lo, bias, *, tm, name):
    t = x1.shape[0]
    full = lambda i: (0, 0)
    tok = lambda i: (0, i)
    return pl.pallas_call(
        _router_kernel,
        grid=(t // tm,),
        in_specs=[pl.BlockSpec((tm, D_MODEL), lambda i: (i, 0)),
                  pl.BlockSpec((N_EXPERTS, D_MODEL), full),
                  pl.BlockSpec((N_EXPERTS, D_MODEL), full),
                  pl.BlockSpec((N_EXPERTS, 1), full)],
        out_specs=[pl.BlockSpec((TOP_K, tm), tok),
                   pl.BlockSpec((TOP_K * tm, LANES), lambda i: (i, 0)),
                   pl.BlockSpec((TOP_K, tm), tok),
                   pl.BlockSpec((N_EXPERTS, 1), full),
                   pl.BlockSpec((SUBLANES, N_EXPERTS), full)],
        out_shape=[jax.ShapeDtypeStruct((TOP_K, t), I32),
                   jax.ShapeDtypeStruct((TOP_K * t, LANES), F32),
                   jax.ShapeDtypeStruct((TOP_K, t), I32),
                   jax.ShapeDtypeStruct((N_EXPERTS, 1), I32),
                   jax.ShapeDtypeStruct((SUBLANES, N_EXPERTS), I32)],
        scratch_shapes=[pltpu.VMEM((N_EXPERTS, 1), F32),
                        pltpu.VMEM((SUBLANES, N_EXPERTS), F32)],
        compiler_params=_cparams(("arbitrary",)),
        name=name,
    )(x1, w_hi, w_lo, bias)


ITEM_FIELDS = 8


def _byte_split(v):
    return lax.shift_right_logical(v, 8).astype(F32), (v & 255).astype(F32)


def _plan_kernel(cntc_ref, cntr_ref, idx_ref, rank_ref, meta_ref, items_ref, m_scr, *, n_tok):
    i = pl.program_id(0)
    tm = idx_ref.shape[1]
    nbm = m_scr.shape[0]
    nip = items_ref.shape[1]
    nbp = nip - N_EXPERTS
    n_rows = n_tok * TOP_K
    n_blk = n_rows // EXPERT_ROWS
    nt = (((1,), (1,)), ((), ()))

    r_i = lax.broadcasted_iota(I32, (N_EXPERTS, N_EXPERTS), 0)
    c_i = lax.broadcasted_iota(I32, (N_EXPERTS, N_EXPERTS), 1)
    below = jnp.where(c_i < r_i, 1.0, 0.0).astype(BF16)
    ones_c = jnp.ones((N_EXPERTS, LANES), F32)
    c_hi, c_lo = _byte_split(cntc_ref[...])
    start_col = (256.0 * jnp.dot(below, (c_hi * ones_c).astype(BF16), preferred_element_type=F32)
                 + jnp.dot(below, (c_lo * ones_c).astype(BF16), preferred_element_type=F32))[:, 0:1]

    @pl.when(i == 0)
    def _():
        m_scr[...] = jnp.zeros_like(m_scr)
        above = jnp.where(r_i < c_i, 1.0, 0.0).astype(BF16)
        r_hi, r_lo = _byte_split(cntr_ref[...])
        start_row = (256.0 * jnp.dot(r_hi.astype(BF16), above, preferred_element_type=F32)
                     + jnp.dot(r_lo.astype(BF16), above, preferred_element_type=F32))[0:1, :]
        b_col = lax.broadcasted_iota(I32, (nbp, 1), 0)
        b_row = lax.broadcasted_iota(I32, (1, nbp), 1)
        blk_col = jnp.where(b_col < n_blk, b_col * EXPERT_ROWS, n_rows).astype(F32)
        blk_row = jnp.where(b_row < n_blk, b_row * EXPERT_ROWS, n_rows).astype(F32)
        v_col = jnp.concatenate([blk_col, start_col], axis=0)
        v_row = jnp.concatenate([blk_row, start_row], axis=1)
        j_col = lax.broadcasted_iota(I32, (nip, 1), 0)
        k_row = lax.broadcasted_iota(I32, (1, nip), 1)
        ahead = (v_row < v_col) | ((v_row == v_col) & (k_row < j_col))
        order_col = jnp.sum(jnp.where(ahead, 1.0, 0.0), axis=1, keepdims=True)
        pos_row = k_row.astype(F32)
        lo_abs = jnp.sum(jnp.where(order_col == pos_row, v_col, 0.0), axis=0, keepdims=True)
        hi_abs = jnp.sum(jnp.where(order_col == pos_row + 1.0, v_col, 0.0), axis=0, keepdims=True)
        hi_abs = jnp.where(k_row == nip - 1, float(n_rows), hi_abs)
        blk = jnp.minimum(jnp.floor(lo_abs * (1.0 / EXPERT_ROWS)), n_blk - 1.0)
        expert = jnp.sum(jnp.where(start_col <= lo_abs, 1.0, 0.0), axis=0, keepdims=True) - 1.0
        base = blk * EXPERT_ROWS
        e_col = lax.broadcasted_iota(I32, (N_EXPERTS, 1), 0).astype(F32)
        end_col = start_col + cntc_ref[...].astype(F32)
        seg_end = jnp.sum(jnp.where(e_col == expert, end_col, 0.0), axis=0, keepdims=True)
        follower = jnp.sum(jnp.where(start_col <= seg_end, 1.0, 0.0), axis=0, keepdims=True) - 1.0
        follower = jnp.where(seg_end < float(n_rows), follower, -1.0)
        fields = [blk, expert, lo_abs - base, hi_abs - base, follower]
        fields.append(jnp.zeros((ITEM_FIELDS - len(fields), nip), F32))
        items_ref[...] = jnp.concatenate(fields, axis=0).astype(I32)

    e_iota = lax.broadcasted_iota(I32, (N_EXPERTS, tm), 0)
    b_iota = lax.broadcasted_iota(I32, (nbm, tm), 0)
    l_iota = lax.broadcasted_iota(I32, (EXPERT_ROWS, tm), 0)
    tok = i * tm + lax.broadcasted_iota(I32, (1, tm), 1)
    tok_hi, tok_lo = _byte_split(tok)
    acc = jnp.zeros(m_scr.shape, F32)
    for k in range(TOP_K):
        hit = e_iota == idx_ref[k:k + 1, :]
        dest = (jnp.sum(jnp.where(hit, start_col, 0.0), axis=0, keepdims=True).astype(I32)
                + rank_ref[k:k + 1, :])
        oh_blk = jnp.where(b_iota == lax.shift_right_logical(dest, 8), 1.0, 0.0).astype(BF16)
        in_blk = l_iota == (dest & (EXPERT_ROWS - 1))
        vals = jnp.concatenate([jnp.where(in_blk, tok_hi, 0.0), jnp.where(in_blk, tok_lo, 0.0),
                                jnp.where(in_blk, float(k), 0.0)], axis=0).astype(BF16)
        acc = acc + lax.dot_general(oh_blk, vals, nt, preferred_element_type=F32)
    m_scr[...] = m_scr[...] + acc

    @pl.when(i == pl.num_programs(0) - 1)
    def _():
        m = m_scr[...]
        row_tok = m[:, 0:EXPERT_ROWS] * 256.0 + m[:, EXPERT_ROWS:2 * EXPERT_ROWS]
        row_slot = m[:, 2 * EXPERT_ROWS:3 * EXPERT_ROWS] * float(n_tok) + row_tok
        meta_ref[...] = jnp.concatenate([row_tok * float(PACKED_ROWS), row_slot * float(ROW_CHUNKS)],
                                        axis=1).astype(I32)


def _plan(cnt_col, cnt_row, idx, rank, *, tm, name):
    t = idx.shape[1]
    assert EXPERT_ROWS == 256 and (t * TOP_K) % (2 * EXPERT_ROWS) == 0
    n_blk = t * TOP_K // EXPERT_ROWS
    nbm = -(-n_blk // 16) * 16
    nip = -(-n_blk // LANES) * LANES + N_EXPERTS
    tok = lambda i: (0, i)
    full = lambda i: (0, 0)
    return pl.pallas_call(
        functools.partial(_plan_kernel, n_tok=t),
        grid=(t // tm,),
        in_specs=[pl.BlockSpec((N_EXPERTS, 1), full),
                  pl.BlockSpec((SUBLANES, N_EXPERTS), full),
                  pl.BlockSpec((TOP_K, tm), tok),
                  pl.BlockSpec((TOP_K, tm), tok)],
        out_specs=[pl.BlockSpec((nbm, 2 * EXPERT_ROWS), full),
                   pl.BlockSpec((ITEM_FIELDS, nip), full)],
        out_shape=[jax.ShapeDtypeStruct((nbm, 2 * EXPERT_ROWS), I32),
                   jax.ShapeDtypeStruct((ITEM_FIELDS, nip), I32)],
        scratch_shapes=[pltpu.VMEM((nbm, 3 * EXPERT_ROWS), F32)],
        compiler_params=_cparams(("arbitrary",)),
        name=name,
    )(cnt_col, cnt_row, idx, rank)


META_CHUNK = 4 * EXPERT_ROWS
GATHER_STRIDE = EXPERT_ROWS + 8
META_RING = 3
FFN_PIECES = 2 + D_MODEL // EXPERT_FF


def _experts_kernel(items_ref, xw_hbm, meta_hbm, wg_hbm, wu_hbm, wd_hbm, ys_hbm,
                    xw, meta, tile, lhs, wg_f, wu_f, wd_f, wg_b, wu_b, wd_b, yacc, ybuf, state,
                    sem_x, sem_m, sem_y, sem_w, *, nip, n_blk):
    i = pl.program_id(0)
    blk = items_ref[i]
    expert = items_ref[nip + i]
    lo = items_ref[2 * nip + i]
    hi = items_ref[3 * nip + i]
    follower = items_ref[4 * nip + i]
    nonempty = hi > lo
    par = blk & 1
    chunk = lax.shift_right_logical(blk, 1)

    def meta_base(b):
        return (lax.shift_right_logical(b, 1) % META_RING) * META_CHUNK + (b & 1) * (2 * EXPERT_ROWS)

    mbase = meta_base(blk)
    prev_base = meta_base(jnp.maximum(blk - 1, 0))

    def meta_copy(c):
        slot = c % META_RING
        return pltpu.make_async_copy(meta_hbm.at[pl.ds(pl.multiple_of(c * META_CHUNK, META_CHUNK), META_CHUNK)],
                                     meta.at[pl.ds(pl.multiple_of(slot * META_CHUNK, META_CHUNK), META_CHUNK)],
                                     sem_m.at[slot])

    def ybuf_drain(slot):
        pltpu.make_async_copy(ys_hbm.at[pl.ds(0, EXPERT_ROWS * ROW_CHUNKS), :], ybuf.at[slot],
                              sem_y.at[slot]).wait()

    def ybuf_fill(slot):
        for j in range(ROW_CHUNKS):
            ybuf[slot, pl.ds(j, EXPERT_ROWS, stride=ROW_CHUNKS), :] = yacc[:, j * LANES:(j + 1) * LANES]

    def send_rows(slot, base, r0, r1):
        for r in range(r0, r1):
            src = ybuf.at[slot, pl.ds(r * ROW_CHUNKS, ROW_CHUNKS), :]
            dst_row = pl.multiple_of(meta[base + EXPERT_ROWS + r], ROW_CHUNKS)
            pltpu.make_async_copy(src, ys_hbm.at[pl.ds(dst_row, ROW_CHUNKS), :],
                                  sem_y.at[slot]).start(priority=r % 2)

    def weight_copies(e, slot):
        return [pltpu.make_async_copy(src.at[e], dst.at[slot], sem_w.at[slot])
                for src, dst in ((wg_hbm, wg_f), (wu_hbm, wu_f), (wd_hbm, wd_f))]

    @pl.when(i == 0)
    def _():
        resident = pltpu.make_async_copy(xw_hbm, xw, sem_x)
        resident.start()
        meta_copy(0).start()
        for cp in weight_copies(expert, 0):
            cp.start()
        state[0] = -1
        state[1] = 1
        resident.wait()

    @pl.when(nonempty & (expert != state[0]))
    def _():
        slot = 1 - state[1]
        for cp in weight_copies(expert, slot):
            cp.wait()
        state[0] = expert
        state[1] = slot

        @pl.when(follower >= 0)
        def _():
            for cp in weight_copies(follower, 1 - slot):
                cp.start()

        wg_b[...] = wg_f[slot].astype(BF16)
        wu_b[...] = wu_f[slot].astype(BF16)
        wd_b[...] = wd_f[slot].astype(BF16)

    @pl.when(nonempty & (lo == 0))
    def _():
        @pl.when(par == 0)
        def _():
            meta_copy(chunk).wait()

            @pl.when(2 * (chunk + 1) < n_blk)
            def _():
                meta_copy(chunk + 1).start()

        for r in range(EXPERT_ROWS):
            t4 = pl.multiple_of(meta[mbase + r], PACKED_ROWS)
            tile[pl.ds(r, PACKED_ROWS, stride=GATHER_STRIDE), :] = xw[pl.ds(t4, PACKED_ROWS), :]
        cols = []
        for s in range(PACKED_ROWS):
            cols.extend(_unpack_bf16_pairs(tile[pl.ds(s * GATHER_STRIDE, EXPERT_ROWS), :]))
        lhs[...] = jnp.concatenate(cols, axis=-1)

    def ffn(x, between=None):
        done = [0]

        def piece_done():
            if between is not None:
                between(done[0])
            done[0] += 1

        gate = jnp.dot(x, wg_b[...], preferred_element_type=F32)
        piece_done()
        up = jnp.dot(x, wu_b[...], preferred_element_type=F32)
        piece_done()
        hid = (jax.nn.silu(gate) * up).astype(BF16)
        out = []
        for c in range(D_MODEL // EXPERT_FF):
            out.append(jnp.dot(hid, wd_b[:, c * EXPERT_FF:(c + 1) * EXPERT_FF], preferred_element_type=F32))
            piece_done()
        assert done[0] == FFN_PIECES
        return jnp.concatenate(out, axis=-1)

    def run_mask(y):
        row = lax.broadcasted_iota(I32, (EXPERT_ROWS, 1), 0)
        return jnp.where((row >= lo) & (row < hi), y, 0.0)

    whole = (lo == 0) & (hi == EXPERT_ROWS)
    first = nonempty & (lo == 0)
    has_prev = blk > 0

    for prev_slot in range(2):
        def send_piece(g, prev_slot=prev_slot):
            send_rows(prev_slot, prev_base, g * EXPERT_ROWS // FFN_PIECES, (g + 1) * EXPERT_ROWS // FFN_PIECES)

        sends = has_prev & (par == 1 - prev_slot)

        @pl.when(whole & sends)
        def _():
            yacc[...] = ffn(lhs[...], send_piece)

        @pl.when(first & jnp.logical_not(whole) & sends)
        def _():
            yacc[...] = run_mask(ffn(lhs[...], send_piece))

    @pl.when(whole & jnp.logical_not(has_prev))
    def _():
        yacc[...] = ffn(lhs[...])

    @pl.when(nonempty & jnp.logical_not(whole) & jnp.logical_not(first & has_prev))
    def _():
        y = run_mask(ffn(lhs[...]))

        @pl.when(lo == 0)
        def _():
            yacc[...] = y

        @pl.when(lo > 0)
        def _():
            yacc[...] = yacc[...] + y

    @pl.when(nonempty & (hi == EXPERT_ROWS))
    def _():
        @pl.when(blk >= 2)
        def _():
            ybuf_drain(par)

        for slot in range(2):
            @pl.when(par == slot)
            def _():
                ybuf_fill(slot)

                @pl.when(blk == n_blk - 1)
                def _():
                    send_rows(slot, mbase, 0, EXPERT_ROWS)

    @pl.when(i == pl.num_programs(0) - 1)
    def _():
        ybuf_drain(0)
        ybuf_drain(1)


def _experts(items, xw, meta, wg, wu, wd, *, n_tok, name):
    nip = items.shape[0] // ITEM_FIELDS
    n_rows = n_tok * TOP_K
    n_blk = n_rows // EXPERT_ROWS
    n_items = n_blk + N_EXPERTS
    assert n_blk >= 2 and n_blk % 2 == 0
    kern = functools.partial(_experts_kernel, nip=nip, n_blk=n_blk)
    any_spec = pl.BlockSpec(memory_space=pl.ANY)
    return pl.pallas_call(
        kern,
        grid_spec=pltpu.PrefetchScalarGridSpec(
            num_scalar_prefetch=1,
            grid=(n_items,),
            in_specs=[any_spec] * 5,
            out_specs=any_spec,
            scratch_shapes=[pltpu.VMEM(xw.shape, I32),
                            pltpu.SMEM((META_RING * META_CHUNK,), I32),
                            pltpu.VMEM((PACKED_ROWS * GATHER_STRIDE, LANES), I32),
                            pltpu.VMEM((EXPERT_ROWS, D_MODEL), BF16),
                            pltpu.VMEM((2, D_MODEL, EXPERT_FF), F32),
                            pltpu.VMEM((2, D_MODEL, EXPERT_FF), F32),
                            pltpu.VMEM((2, EXPERT_FF, D_MODEL), F32),
                            pltpu.VMEM((D_MODEL, EXPERT_FF), BF16),
                            pltpu.VMEM((D_MODEL, EXPERT_FF), BF16),
                            pltpu.VMEM((EXPERT_FF, D_MODEL), BF16),
                            pltpu.VMEM((EXPERT_ROWS, D_MODEL), F32),
                            pltpu.VMEM((2, EXPERT_ROWS * ROW_CHUNKS, LANES), F32),
                            pltpu.SMEM((2,), I32),
                            pltpu.SemaphoreType.DMA,
                            pltpu.SemaphoreType.DMA((META_RING,)),
                            pltpu.SemaphoreType.DMA((2,)),
                            pltpu.SemaphoreType.DMA((2,))],
        ),
        out_shape=jax.ShapeDtypeStruct((n_rows * ROW_CHUNKS, LANES), F32),
        compiler_params=_cparams(("arbitrary",)),
        name=name,
    )(items, xw, meta, wg, wu, wd)


def _combine_kernel(ys_ref, x_ref, wl_ref, wsg_ref, wsu_ref, wsd_ref, g_ref, b_ref, op_ref, os_ref,
                    routed, *, np_tiles):
    x = x_ref[...]
    xb = x.astype(BF16)
    hid = (jax.nn.silu(jnp.dot(xb, wsg_ref[...], preferred_element_type=F32))
           * jnp.dot(xb, wsu_ref[...], preferred_element_type=F32))
    shared = jnp.dot(hid.astype(BF16), wsd_ref[...], preferred_element_type=F32)
    acc = None
    for k in range(TOP_K):
        part = wl_ref[:, k:k + 1, :] * ys_ref[k]
        acc = part if acc is None else acc + part
    routed[...] = acc
    ffn = jnp.concatenate([routed[:, j, :] for j in range(ROW_CHUNKS)], axis=-1) + shared
    out = _layer_norm(DEEPNORM_ALPHA * x + ffn, g_ref[...], b_ref[...])
    i = pl.program_id(0)

    @pl.when(i < np_tiles)
    def _():
        op_ref[...] = out

    @pl.when(i >= np_tiles)
    def _():
        os_ref[...] = out


def _combine(ys, x1, w_tok, wsg, wsu, wsd, g, b, *, n_prompt, tm, name):
    t = x1.shape[0]
    np_tiles = n_prompt // tm
    p_map, s_map = _pair_maps(np_tiles)
    row = lambda i: (i, 0)
    full = lambda i: (0, 0)
    return pl.pallas_call(
        functools.partial(_combine_kernel, np_tiles=np_tiles),
        grid=(t // tm,),
        in_specs=[pl.BlockSpec((TOP_K, tm, ROW_CHUNKS, LANES), lambda i: (0, i, 0, 0)),
                  pl.BlockSpec((tm, D_MODEL), row),
                  pl.BlockSpec((tm, TOP_K, LANES), lambda i: (i, 0, 0)),
                  pl.BlockSpec((D_MODEL, SHARED_FF), full),
                  pl.BlockSpec((D_MODEL, SHARED_FF), full),
                  pl.BlockSpec((SHARED_FF, D_MODEL), full),
                  pl.BlockSpec((1, D_MODEL), full),
                  pl.BlockSpec((1, D_MODEL), full)],
        out_specs=[pl.BlockSpec((tm, D_MODEL), p_map), pl.BlockSpec((tm, D_MODEL), s_map)],
        out_shape=[jax.ShapeDtypeStruct((n_prompt, D_MODEL), F32),
                   jax.ShapeDtypeStruct((t - n_prompt, D_MODEL), F32)],
        scratch_shapes=[pltpu.VMEM((tm, ROW_CHUNKS, LANES), F32)],
        compiler_params=_cparams(("arbitrary",)),
        name=name,
    )(ys, x1, w_tok, wsg, wsu, wsd, g, b)


def _pack_w_in(w_in):
    sizes = (GLA_KEY_DIM, GLA_KEY_DIM, GLA_VAL_DIM, GLA_VAL_DIM, GLA_GATE_RANK,
             SWA_Q_DIM, SWA_KV_DIM, SWA_KV_DIM, D_MODEL, D_MODEL)
    offs = [0]
    for s in sizes:
        offs.append(offs[-1] + s)
    qa, ka, va, ga, gk, qb, kb, vb, gate_a, gate_b = (w_in[:, offs[i]:offs[i + 1]] for i in range(10))
    pad = lambda w, n: jnp.pad(w, ((0, 0), (0, n - w.shape[1])))
    packed = jnp.concatenate([qa, ka, va, ga, qb, gate_a, gate_b, kb, vb, pad(gk, 2 * LANES)], axis=1)
    assert packed.shape[1] == H_WIDTH
    return packed.astype(BF16), jnp.concatenate([kb, vb], axis=1).astype(BF16)


def kernel(x_prompt, x_sample, state_gla, cache_swa_k, cache_swa_v, w_in, w_gk_up, b_gk, gla_norm_g,
           attn_sinks, w_proj_a, w_proj_b, w_out, ln1_g, ln1_b, w_router, router_bias,
           w_expert_gate, w_expert_up, w_expert_down, w_shared_gate, w_shared_up, w_shared_down,
           ln2_g, ln2_b):
    assert w_in.shape[0] == 1, "single-layer trunk"
    bp, lp, d = x_prompt.shape
    bs, ls, _ = x_sample.shape
    assert d == D_MODEL and ls == SUBLANES and cache_swa_k.shape[2] == WINDOW
    tp, ts = bp * lp, bs * ls
    t = tp + ts

    xp = x_prompt.reshape(tp, d)
    xs = x_sample.reshape(ts, d)
    w_main, w_kv = _pack_w_in(w_in[0])
    h = _matmul(xp, xs, w_main, BF16, _pair_tile(tp, ts, 1024), H_TN, "proj_in")

    xp_tail = x_prompt[:, lp - WINDOW:].reshape(bp * WINDOW, d)
    kv_tail = _matmul(xp_tail, xs, w_kv, F32, _pair_tile(bp * WINDOW, ts, 512), 2 * SWA_KV_DIM,
                      "proj_kv_tail")

    wup = jnp.pad(w_gk_up[0], ((0, LANES - GLA_GATE_RANK), (0, 0))).astype(BF16)
    bgk = b_gk[0].reshape(1, GLA_KEY_DIM)
    gn = gla_norm_g[0].reshape(1, GLA_DV)
    oa_p, s_prompt = _gla(h, wup, bgk, gn, None, row0=0, n_seq=bp, seq_len=lp, par=math.gcd(bp, 4), nb=1,
                          c=GLA_CHUNK, sub=2, name="gla_prompt")
    oa_s, s_sample = _gla(h, wup, bgk, gn, state_gla[0], row0=tp, n_seq=bs, seq_len=ls, par=1, nb=8,
                          c=math.gcd(ls, GLA_CHUNK), sub=1, name="gla_sample")
    oa_p = oa_p.reshape(tp, GLA_VAL_DIM)
    oa_s = oa_s.reshape(ts, GLA_VAL_DIM)

    sinks = attn_sinks[0]
    k_past = cache_swa_k[0].reshape(bs, WINDOW, SWA_KV_DIM)
    v_past = cache_swa_v[0].reshape(bs, WINDOW, SWA_KV_DIM)
    ob_p = _swa_prompt(h, sinks, n_seq=bp, seq_len=lp, name="swa_prompt")
    ob_s = _swa_sample(h, sinks, k_past, v_past, row0=tp, n_seq=bs, lq=ls, nb=8, name="swa_sample")

    x1, xw = _merge((oa_p, oa_s), (ob_p, ob_s), h, (xp, xs), w_proj_a[0].astype(BF16),
                    w_proj_b[0].astype(BF16), w_out[0].astype(BF16), ln1_g[0].reshape(1, d),
                    ln1_b[0].reshape(1, d), tm=_pair_tile(tp, ts, 512), name="merge_ln1")

    wr_t = w_router[0].T
    wr_hi = wr_t.astype(BF16)
    wr_lo = (wr_t - wr_hi.astype(F32)).astype(BF16)
    tm_r = _tile(t, 512)
    idx, w_lanes, rank, cnt_col, cnt_row = _router(x1, wr_hi, wr_lo, router_bias[0].reshape(N_EXPERTS, 1),
                                                   tm=tm_r, name="router")
    meta, items = _plan(cnt_col, cnt_row, idx, rank, tm=tm_r, name="plan")
    ys = _experts(items.reshape(-1), xw, meta.reshape(-1), w_expert_gate[0], w_expert_up[0],
                  w_expert_down[0], n_tok=t, name="experts")
    y_p, y_s = _combine(ys.reshape(TOP_K, t, ROW_CHUNKS, LANES), x1, w_lanes.reshape(t, TOP_K, LANES),
                        w_shared_gate[0].astype(BF16), w_shared_up[0].astype(BF16),
                        w_shared_down[0].astype(BF16), ln2_g[0].reshape(1, d), ln2_b[0].reshape(1, d),
                        n_prompt=tp, tm=_pair_tile(tp, ts, 128), name="combine_ln2")

    y_prompt = y_p.reshape(bp, lp, d)
    y_sample = y_s.reshape(bs, ls, d)
    k_tail = kv_tail[:, :SWA_KV_DIM]
    v_tail = kv_tail[:, SWA_KV_DIM:]
    kv_shape = (SWA_KV_HEADS, SWA_HEAD_DIM)
    k_prompt = k_tail[:bp * WINDOW].reshape(1, bp, WINDOW, *kv_shape)
    v_prompt = v_tail[:bp * WINDOW].reshape(1, bp, WINDOW, *kv_shape)
    k_new = k_tail[bp * WINDOW:].reshape(bs, ls, *kv_shape)
    v_new = v_tail[bp * WINDOW:].reshape(bs, ls, *kv_shape)
    k_sample = jnp.concatenate([cache_swa_k[0][:, ls:], k_new], axis=1)[None]
    v_sample = jnp.concatenate([cache_swa_v[0][:, ls:], v_new], axis=1)[None]
    return (y_prompt, y_sample, s_prompt[None], s_sample[None], k_prompt, v_prompt, k_sample, v_sample)
```

```python
import functools
import math

import jax
import jax.numpy as jnp
from jax import lax
from jax.experimental import pallas as pl
from jax.experimental.pallas import tpu as pltpu

F32 = jnp.float32
BF16 = jnp.bfloat16
I32 = jnp.int32

D_MODEL = 1024
GLA_HEADS = 4
GLA_DK = 128
GLA_DV = 256
GLA_KEY_DIM = GLA_HEADS * GLA_DK
GLA_VAL_DIM = GLA_HEADS * GLA_DV
GLA_GATE_RANK = 16
GLA_GATE_NORMALIZER = 16.0
GLA_CHUNK = 64
SWA_HEADS = 16
SWA_KV_HEADS = 4
SWA_GROUP = SWA_HEADS // SWA_KV_HEADS
SWA_HEAD_DIM = 64
SWA_Q_DIM = SWA_HEADS * SWA_HEAD_DIM
SWA_KV_DIM = SWA_KV_HEADS * SWA_HEAD_DIM
WINDOW = 128
N_EXPERTS = 256
TOP_K = 8
N_GROUPS = 8
GROUP_SIZE = N_EXPERTS // N_GROUPS
TOPK_GROUPS = 4
EXPERT_FF = 256
SHARED_FF = 256
ROUTED_SCALE = 2.5
DEEPNORM_ALPHA = 2.0 ** 0.25
EPS = 1e-5

LANES = 128
SUBLANES = 8
ROW_CHUNKS = D_MODEL // LANES
VMEM_LIMIT = 56 * 1024 * 1024

H_QA, H_KA, H_VA, H_GA, H_QB, H_GATE_A, H_GATE_B, H_KB, H_VB, H_GK = (
    0, 512, 1024, 2048, 3072, 4096, 5120, 6144, 6400, 6656)
H_WIDTH = 6912
H_TN = 2304

EXPERT_ROWS = 256
PACKED_ROWS = ROW_CHUNKS // 2
HIGH_HALF = -65536


def _cparams(sem, vmem=VMEM_LIMIT):
    return pltpu.CompilerParams(dimension_semantics=sem, vmem_limit_bytes=vmem)


def _tile(n, pref):
    t = min(n, pref)
    while n % t:
        t -= LANES
    assert t > 0 and t % LANES == 0, (n, pref)
    return t


def _pair_tile(n_p, n_s, pref):
    return _tile(math.gcd(n_p, n_s), pref)


def _pair_maps(np_tiles, col=0, extra=0):
    def p_map(i, *_):
        return (jnp.minimum(i, np_tiles - 1), col)

    def s_map(i, *_):
        return (jnp.maximum(i - np_tiles, 0), col)

    return p_map, s_map


def _pair_value(i, np_tiles, p_ref, s_ref):
    return jnp.where(i < np_tiles, p_ref[...], s_ref[...])


def _mm_kernel(xp_ref, xs_ref, w_ref, o_ref, *, np_tiles):
    x = _pair_value(pl.program_id(0), np_tiles, xp_ref, xs_ref)
    o_ref[...] = jnp.dot(x.astype(BF16), w_ref[...], preferred_element_type=F32).astype(o_ref.dtype)


def _matmul(xp, xs, w, out_dtype, tm, tn, name):
    k = xp.shape[1]
    m = xp.shape[0] + xs.shape[0]
    n = w.shape[1]
    np_tiles = xp.shape[0] // tm
    p_map, s_map = _pair_maps(np_tiles)
    return pl.pallas_call(
        functools.partial(_mm_kernel, np_tiles=np_tiles),
        grid=(m // tm, n // tn),
        in_specs=[pl.BlockSpec((tm, k), p_map),
                  pl.BlockSpec((tm, k), s_map),
                  pl.BlockSpec((k, tn), lambda i, j: (0, j))],
        out_specs=pl.BlockSpec((tm, tn), lambda i, j: (i, j)),
        out_shape=jax.ShapeDtypeStruct((m, n), out_dtype),
        compiler_params=_cparams(("parallel", "arbitrary")),
        name=name,
    )(xp, xs, w)


def _split_bf16(x):
    hi = x.astype(BF16)
    lo = (x - hi.astype(F32)).astype(BF16)
    return hi, lo


def _gla_kernel(*refs, par, nb, c, sub, has_s0):
    q_refs, k_refs, v_refs, ga_refs, gk_refs = (refs[n * par:(n + 1) * par] for n in range(5))
    rest = refs[5 * par:]
    n_state = par * GLA_HEADS
    s_scr = [rest[len(rest) - n_state + p * GLA_HEADS:len(rest) - n_state + (p + 1) * GLA_HEADS]
             for p in range(par)]
    rest = rest[:len(rest) - n_state]
    if has_s0:
        wup_ref, bgk_ref, gn_ref, s0_ref, o_ref, sout_ref = rest
    else:
        wup_ref, bgk_ref, gn_ref, o_ref, sout_ref = rest
        s0_ref = None
    ci = pl.program_id(1)
    rows = nb * c

    @pl.when(ci == 0)
    def _():
        for p in range(par):
            for h in range(GLA_HEADS):
                if has_s0:
                    s_scr[p][h][...] = s0_ref[p * nb:(p + 1) * nb, h]
                else:
                    s_scr[p][h][...] = jnp.zeros_like(s_scr[p][h])

    r_i = lax.broadcasted_iota(I32, (rows, rows), 0)
    c_i = lax.broadcasted_iota(I32, (rows, rows), 1)
    same_seq = (r_i // c) == (c_i // c)
    causal = same_seq & (c_i <= r_i)
    tri = jnp.where(causal, 1.0, 0.0).astype(BF16)
    seg = jnp.where(same_seq, 1.0, 0.0).astype(BF16)
    ones_kv = jnp.ones((rows, GLA_DV), BF16)
    seq_of_row = lax.broadcasted_iota(I32, (rows, 1), 0) // c
    tn = (((0,), (0,)), ((), ()))

    def decays(p, rs):
        pre = jnp.dot(gk_refs[p][rs, :], wup_ref[...], preferred_element_type=F32) + bgk_ref[...]
        log_a = jax.nn.log_sigmoid(pre) / GLA_GATE_NORMALIZER
        la_hi, la_lo = _split_bf16(log_a)
        b = (jnp.dot(tri, la_hi, preferred_element_type=F32)
             + jnp.dot(tri, la_lo, preferred_element_type=F32))
        b_last = (jnp.dot(seg, la_hi, preferred_element_type=F32)
                  + jnp.dot(seg, la_lo, preferred_element_type=F32))
        q = q_refs[p][rs, :].astype(F32) * (GLA_DK ** -0.5)
        k = k_refs[p][rs, :].astype(F32)
        q_dec = (q * jnp.exp(b)).astype(BF16)
        k_dec = (k * jnp.exp(-b)).astype(BF16)
        k_rem = k * jnp.exp(b_last - b)
        return q_dec, k_dec, k_rem, la_hi, la_lo

    def head(p, rs, h, q_dec, k_dec, k_rem, la_hi, la_lo):
        ks = slice(h * GLA_DK, (h + 1) * GLA_DK)
        vh = v_refs[p][rs, h * GLA_DV:(h + 1) * GLA_DV]
        att = lax.dot_general(q_dec[:, ks], k_dec[:, ks], (((1,), (1,)), ((), ())),
                              preferred_element_type=F32)
        att = jnp.where(causal, att, 0.0).astype(BF16)
        o_h = jnp.dot(att, vh, preferred_element_type=F32)
        for j in range(nb):
            mine = (seq_of_row == j) if nb > 1 else None
            pick = (lambda a: jnp.where(mine, a, 0.0)) if nb > 1 else (lambda a: a)
            s_old = s_scr[p][h][j]
            o_h = o_h + pick(jnp.dot(q_dec[:, ks], s_old.astype(BF16), preferred_element_type=F32))
            dec = (lax.dot_general(pick(la_hi[:, ks].astype(F32)).astype(BF16), ones_kv, tn,
                                   preferred_element_type=F32)
                   + lax.dot_general(pick(la_lo[:, ks].astype(F32)).astype(BF16), ones_kv, tn,
                                     preferred_element_type=F32))
            upd = lax.dot_general(pick(k_rem[:, ks]).astype(BF16), vh, tn, preferred_element_type=F32)
            s_scr[p][h][j] = jnp.exp(dec) * s_old + upd
        return o_h * lax.rsqrt(jnp.mean(jnp.square(o_h), axis=-1, keepdims=True) + EPS) * gn_ref[...]

    for s in range(sub):
        rs = slice(s * rows, (s + 1) * rows)
        staged = [decays(p, rs) for p in range(par)]
        outs = [[] for _ in range(par)]
        for h in range(GLA_HEADS):
            for p in range(par):
                outs[p].append(head(p, rs, h, *staged[p]))
        for p in range(par):
            o = jnp.concatenate(outs[p], axis=-1) * jax.nn.silu(ga_refs[p][rs, :].astype(F32))
            o_ref[p, rs, :] = o.astype(o_ref.dtype)

    @pl.when(ci == pl.num_programs(1) - 1)
    def _():
        for p in range(par):
            for h in range(GLA_HEADS):
                sout_ref[p * nb:(p + 1) * nb, h] = s_scr[p][h][...]


def _gla(h, wup, bgk, gn, s0, *, row0, n_seq, seq_len, par, nb, c, sub, name):
    rows = nb * c * sub
    n_blocks = n_seq // nb
    n_steps = seq_len // (c * sub)
    rb0 = row0 // rows

    def rmap(p, col):
        return lambda g, i: (rb0 + (g * par + p) * n_steps + i, col)

    fields = ((GLA_KEY_DIM, H_QA), (GLA_KEY_DIM, H_KA), (GLA_VAL_DIM, H_VA), (GLA_VAL_DIM, H_GA), (LANES, H_GK))
    in_specs = [pl.BlockSpec((rows, width), rmap(p, off // width)) for width, off in fields for p in range(par)]
    in_specs += [pl.BlockSpec((LANES, GLA_KEY_DIM), lambda g, i: (0, 0)),
                 pl.BlockSpec((1, GLA_KEY_DIM), lambda g, i: (0, 0)),
                 pl.BlockSpec((1, GLA_DV), lambda g, i: (0, 0))]
    args = [h] * (5 * par) + [wup, bgk, gn]
    state_spec = pl.BlockSpec((par * nb, GLA_HEADS, GLA_DK, GLA_DV), lambda g, i: (g, 0, 0, 0))
    if s0 is not None:
        in_specs.append(state_spec)
        args.append(s0)
    kern = functools.partial(_gla_kernel, par=par, nb=nb, c=c, sub=sub, has_s0=s0 is not None)
    return pl.pallas_call(
        kern,
        grid=(n_blocks // par, n_steps),
        in_specs=in_specs,
        out_specs=[pl.BlockSpec((par, rows, GLA_VAL_DIM), lambda g, i: (g, i, 0)), state_spec],
        out_shape=[jax.ShapeDtypeStruct((n_blocks, nb * seq_len, GLA_VAL_DIM), BF16),
                   jax.ShapeDtypeStruct((n_seq, GLA_HEADS, GLA_DK, GLA_DV), F32)],
        scratch_shapes=[pltpu.VMEM((nb, GLA_DK, GLA_DV), F32)] * (par * GLA_HEADS),
        compiler_params=_cparams(("parallel", "arbitrary")),
        name=name,
    )(*args)


def _alibi_slope(head):
    return 2.0 ** (-8.0 * (head + 1) / SWA_HEADS)


def _swa_softmax_pv(parts, sink):
    m = sink
    for s, _ in parts:
        m = jnp.maximum(m, jnp.max(s, axis=-1, keepdims=True))
    denom = jnp.exp(sink - m)
    acc = None
    for s, v in parts:
        p = jnp.exp(s - m)
        denom = denom + jnp.sum(p, axis=-1, keepdims=True)
        pv = jnp.dot(p.astype(BF16), v, preferred_element_type=F32)
        acc = pv if acc is None else acc + pv
    return acc / denom


def _swa_prompt_kernel(sink_ref, q_ref, kp_ref, vp_ref, kc_ref, vc_ref, o_ref, bias):
    i = pl.program_id(1)
    span = 2 * WINDOW
    col = lax.broadcasted_iota(I32, (WINDOW, span), 1)

    @pl.when(i == 0)
    def _():
        dist_i = lax.broadcasted_iota(I32, (WINDOW, span), 0) + WINDOW - col
        in_window = (dist_i >= 0) & (dist_i < WINDOW)
        dist = dist_i.astype(F32)
        for hh in range(SWA_HEADS):
            bias[hh] = jnp.where(in_window, -_alibi_slope(hh) * dist, -jnp.inf)

    kcat = jnp.concatenate([kp_ref[...], kc_ref[...]], axis=0)
    vcat = jnp.concatenate([vp_ref[...], vc_ref[...]], axis=0)
    no_past = (col < WINDOW) & (i == 0)
    scale = SWA_HEAD_DIM ** -0.5
    outs = []
    for hh in range(SWA_HEADS):
        g = hh // SWA_GROUP
        gs = slice(g * SWA_HEAD_DIM, (g + 1) * SWA_HEAD_DIM)
        qh = q_ref[:, hh * SWA_HEAD_DIM:(hh + 1) * SWA_HEAD_DIM] * scale
        s = lax.dot_general(qh, kcat[:, gs], (((1,), (1,)), ((), ())), preferred_element_type=F32)
        s = jnp.where(no_past, -jnp.inf, s + bias[hh])
        outs.append(_swa_softmax_pv([(s, vcat[:, gs])], sink_ref[hh]))
    o_ref[...] = jnp.concatenate(outs, axis=-1).astype(o_ref.dtype)


def _swa_prompt(h, sinks, *, n_seq, seq_len, name):
    nq = seq_len // WINDOW
    qcol = H_QB // SWA_Q_DIM
    kcol = H_KB // SWA_KV_DIM
    vcol = H_VB // SWA_KV_DIM

    def cur(col):
        return lambda b, i, sk: (b * nq + i, col)

    def prev(col):
        return lambda b, i, sk: (b * nq + jnp.maximum(i - 1, 0), col)

    return pl.pallas_call(
        _swa_prompt_kernel,
        grid_spec=pltpu.PrefetchScalarGridSpec(
            num_scalar_prefetch=1,
            grid=(n_seq, nq),
            in_specs=[pl.BlockSpec((WINDOW, SWA_Q_DIM), cur(qcol)),
                      pl.BlockSpec((WINDOW, SWA_KV_DIM), prev(kcol)),
                      pl.BlockSpec((WINDOW, SWA_KV_DIM), prev(vcol)),
                      pl.BlockSpec((WINDOW, SWA_KV_DIM), cur(kcol)),
                      pl.BlockSpec((WINDOW, SWA_KV_DIM), cur(vcol))],
            out_specs=pl.BlockSpec((WINDOW, SWA_Q_DIM), lambda b, i, sk: (b * nq + i, 0)),
            scratch_shapes=[pltpu.VMEM((SWA_HEADS, WINDOW, 2 * WINDOW), F32)],
        ),
        out_shape=jax.ShapeDtypeStruct((n_seq * seq_len, SWA_Q_DIM), BF16),
        compiler_params=_cparams(("parallel", "arbitrary")),
        name=name,
    )(sinks, h, h, h, h, h)


def _swa_sample_kernel(sink_ref, q_ref, kc_ref, vc_ref, kp_ref, vp_ref, o_ref, *, nb, lq):
    rows = SWA_HEADS * lq
    grp_rows = SWA_GROUP * lq
    hd = SWA_HEAD_DIM
    nt = (((1,), (1,)), ((), ()))
    head_of_row = lax.broadcasted_iota(I32, (rows, 1), 0) // lq
    slope = jnp.zeros((rows, 1), F32)
    sink = jnp.zeros((rows, 1), F32)
    for hh in range(SWA_HEADS):
        slope = jnp.where(head_of_row == hh, _alibi_slope(hh), slope)
        sink = jnp.where(head_of_row == hh, sink_ref[hh], sink)
    qi = lax.broadcasted_iota(I32, (rows, WINDOW), 0) % lq
    dist_p = qi + WINDOW - lax.broadcasted_iota(I32, (rows, WINDOW), 1)
    valid_p = dist_p < WINDOW
    bias_p = slope * dist_p.astype(F32)
    dist_c = lax.broadcasted_iota(I32, (rows, lq), 0) % lq - lax.broadcasted_iota(I32, (rows, lq), 1)
    valid_c = dist_c >= 0
    bias_c = slope * dist_c.astype(F32)
    scale = hd ** -0.5

    q_all = q_ref[...].astype(F32)
    kc_all = kc_ref[...].astype(F32)
    vc_all = vc_ref[...].astype(F32)
    seq_outs = []
    for j in range(nb):
        js = slice(j * lq, (j + 1) * lq)
        pieces = []
        for hh in range(SWA_HEADS):
            g = hh // SWA_GROUP
            parts = []
            if g:
                parts.append(jnp.zeros((lq, g * hd), F32))
            parts.append(q_all[js, hh * hd:(hh + 1) * hd])
            if g < SWA_KV_HEADS - 1:
                parts.append(jnp.zeros((lq, (SWA_KV_HEADS - 1 - g) * hd), F32))
            pieces.append(jnp.concatenate(parts, axis=-1))
        q_big = jnp.concatenate(pieces, axis=0).astype(BF16)
        kp = kp_ref[j].astype(BF16)
        vp = vp_ref[j].astype(BF16)
        kcj = kc_all[js, :].astype(BF16)
        vcj = vc_all[js, :].astype(BF16)
        s_p = lax.dot_general(q_big, kp, nt, preferred_element_type=F32)
        s_p = jnp.where(valid_p, s_p * scale - bias_p, -jnp.inf)
        s_c = lax.dot_general(q_big, kcj, nt, preferred_element_type=F32)
        s_c = jnp.where(valid_c, s_c * scale - bias_c, -jnp.inf)
        o_big = _swa_softmax_pv([(s_p, vp), (s_c, vcj)], sink)
        o_grp = [o_big[g * grp_rows:(g + 1) * grp_rows, g * hd:(g + 1) * hd] for g in range(SWA_KV_HEADS)]
        o_heads = jnp.concatenate(o_grp, axis=0)
        seq_outs.append(jnp.concatenate([o_heads[hh * lq:(hh + 1) * lq, :] for hh in range(SWA_HEADS)],
                                        axis=-1))
    o_ref[...] = jnp.concatenate(seq_outs, axis=0).astype(o_ref.dtype)


def _swa_sample(h, sinks, k_past, v_past, *, row0, n_seq, lq, nb, name):
    rows = nb * lq
    rb0 = row0 // rows
    qcol = H_QB // SWA_Q_DIM
    kcol = H_KB // SWA_KV_DIM
    vcol = H_VB // SWA_KV_DIM
    kern = functools.partial(_swa_sample_kernel, nb=nb, lq=lq)
    past_spec = pl.BlockSpec((nb, WINDOW, SWA_KV_DIM), lambda g, sk: (g, 0, 0))
    return pl.pallas_call(
        kern,
        grid_spec=pltpu.PrefetchScalarGridSpec(
            num_scalar_prefetch=1,
            grid=(n_seq // nb,),
            in_specs=[pl.BlockSpec((rows, SWA_Q_DIM), lambda g, sk: (rb0 + g, qcol)),
                      pl.BlockSpec((rows, SWA_KV_DIM), lambda g, sk: (rb0 + g, kcol)),
                      pl.BlockSpec((rows, SWA_KV_DIM), lambda g, sk: (rb0 + g, vcol)),
                      past_spec, past_spec],
            out_specs=pl.BlockSpec((rows, SWA_Q_DIM), lambda g, sk: (g, 0)),
        ),
        out_shape=jax.ShapeDtypeStruct((n_seq * lq, SWA_Q_DIM), BF16),
        compiler_params=_cparams(("parallel",)),
        name=name,
    )(sinks, h, h, h, k_past, v_past)


def _layer_norm(x, g, b):
    mu = jnp.mean(x, axis=-1, keepdims=True)
    xc = x - mu
    var = jnp.mean(jnp.square(xc), axis=-1, keepdims=True)
    return xc * lax.rsqrt(var + EPS) * g + b


def _pack_bf16_pairs(x, s):
    lo = lax.bitcast_convert_type(x[:, (2 * s) * LANES:(2 * s + 1) * LANES].astype(BF16).astype(F32), I32)
    hi = lax.bitcast_convert_type(x[:, (2 * s + 1) * LANES:(2 * s + 2) * LANES].astype(BF16).astype(F32), I32)
    return lax.shift_right_logical(lo, 16) | (hi & HIGH_HALF)


def _unpack_bf16_pairs(w):
    lo = lax.bitcast_convert_type(lax.shift_left(w, 16), F32).astype(BF16)
    hi = lax.bitcast_convert_type(w & HIGH_HALF, F32).astype(BF16)
    return lo, hi


def _merge_kernel(oap_ref, oas_ref, obp_ref, obs_ref, ga_ref, gb_ref, xp_ref, xs_ref,
                  wpa_ref, wpb_ref, wout_ref, g_ref, b_ref, o_ref, xw_ref, *, np_tiles):
    i = pl.program_id(0)
    tm = o_ref.shape[0]
    br_a = jnp.dot(_pair_value(i, np_tiles, oap_ref, oas_ref), wpa_ref[...], preferred_element_type=F32)
    br_b = jnp.dot(_pair_value(i, np_tiles, obp_ref, obs_ref), wpb_ref[...], preferred_element_type=F32)
    merged = (jax.nn.sigmoid(ga_ref[...].astype(F32)) * br_a
              + jax.nn.sigmoid(gb_ref[...].astype(F32)) * br_b)
    mix = jnp.dot(merged.astype(BF16), wout_ref[...], preferred_element_type=F32)
    x = _pair_value(i, np_tiles, xp_ref, xs_ref)
    x1 = _layer_norm(DEEPNORM_ALPHA * x + mix, g_ref[...], b_ref[...])
    o_ref[...] = x1
    for s in range(PACKED_ROWS):
        xw_ref[pl.ds(s, tm, stride=PACKED_ROWS), :] = _pack_bf16_pairs(x1, s)


def _merge(oa, ob, h, x, wpa, wpb, wout, g, b, *, tm, name):
    t = h.shape[0]
    np_tiles = x[0].shape[0] // tm
    p_map, s_map = _pair_maps(np_tiles)
    row = lambda i: (i, 0)
    full = lambda i: (0, 0)
    pair = [pl.BlockSpec((tm, D_MODEL), p_map), pl.BlockSpec((tm, D_MODEL), s_map)]
    return pl.pallas_call(
        functools.partial(_merge_kernel, np_tiles=np_tiles),
        grid=(t // tm,),
        in_specs=pair + pair + [
                  pl.BlockSpec((tm, D_MODEL), lambda i: (i, H_GATE_A // D_MODEL)),
                  pl.BlockSpec((tm, D_MODEL), lambda i: (i, H_GATE_B // D_MODEL))] + pair + [
                  pl.BlockSpec((D_MODEL, D_MODEL), full),
                  pl.BlockSpec((D_MODEL, D_MODEL), full),
                  pl.BlockSpec((D_MODEL, D_MODEL), full),
                  pl.BlockSpec((1, D_MODEL), full),
                  pl.BlockSpec((1, D_MODEL), full)],
        out_specs=[pl.BlockSpec((tm, D_MODEL), row),
                   pl.BlockSpec((tm * PACKED_ROWS, LANES), row)],
        out_shape=[jax.ShapeDtypeStruct((t, D_MODEL), F32),
                   jax.ShapeDtypeStruct((t * PACKED_ROWS, LANES), I32)],
        compiler_params=_cparams(("parallel",)),
        name=name,
    )(*oa, *ob, h, h, *x, wpa, wpb, wout, g, b)


def _router_kernel(x_ref, whi_ref, wlo_ref, bias_ref, idx_ref, wl_ref, rank_ref, cntc_ref, cntr_ref,
                   cntc_scr, cntr_scr):
    i = pl.program_id(0)
    tm = x_ref.shape[0]

    @pl.when(i == 0)
    def _():
        cntc_scr[...] = jnp.zeros_like(cntc_scr)
        cntr_scr[...] = jnp.zeros_like(cntr_scr)

    x_hi, x_lo = _split_bf16(x_ref[...])
    nt = (((1,), (1,)), ((), ()))
    logits = (lax.dot_general(whi_ref[...], x_hi, nt, preferred_element_type=F32)
              + lax.dot_general(whi_ref[...], x_lo, nt, preferred_element_type=F32)
              + lax.dot_general(wlo_ref[...], x_hi, nt, preferred_element_type=F32))
    scores = jax.nn.sigmoid(logits)
    biased = scores + bias_ref[...]

    grouped = biased.reshape(N_GROUPS, GROUP_SIZE, tm)
    m1 = jnp.max(grouped, axis=1)
    n_top = jnp.sum(jnp.where(grouped == m1[:, None, :], 1.0, 0.0), axis=1)
    m2 = jnp.max(jnp.where(grouped < m1[:, None, :], grouped, -jnp.inf), axis=1)
    gscore = m1 + jnp.where(n_top >= 2.0, m1, m2)

    g_iota = lax.broadcasted_iota(I32, (N_GROUPS, tm), 0)
    beaten = jnp.zeros((N_GROUPS, tm), I32)
    for g in range(N_GROUPS):
        other = gscore[g:g + 1, :]
        ahead = (other > gscore) | ((other == gscore) & (g < g_iota))
        beaten = beaten + jnp.where(ahead, 1, 0)
    keep = jnp.where(beaten < TOPK_GROUPS, 1.0, 0.0)
    masked = jnp.where(keep[:, None, :] > 0.5, grouped, -jnp.inf).reshape(N_EXPERTS, tm)

    e_iota = lax.broadcasted_iota(I32, (N_EXPERTS, tm), 0)
    sel_f = jnp.zeros((N_EXPERTS, tm), F32)
    ids = []
    for _ in range(TOP_K):
        best = jnp.max(masked, axis=0, keepdims=True)
        idx = jnp.min(jnp.where(masked == best, e_iota, N_EXPERTS), axis=0, keepdims=True)
        hit = e_iota == idx
        sel_f = sel_f + jnp.where(hit, 1.0, 0.0)
        masked = jnp.where(hit, -jnp.inf, masked)
        ids.append(idx)

    top_sum = jnp.sum(sel_f * scores, axis=0, keepdims=True)

    t_r = lax.broadcasted_iota(I32, (tm, tm), 0)
    t_c = lax.broadcasted_iota(I32, (tm, tm), 1)
    before = jnp.where(t_r < t_c, 1.0, 0.0).astype(BF16)
    sel_b = sel_f.astype(BF16)
    rank = jnp.dot(sel_b, before, preferred_element_type=F32) + cntc_scr[...]
    cntc_scr[...] = cntc_scr[...] + jnp.sum(sel_f, axis=1, keepdims=True)
    cntr_scr[...] = cntr_scr[...] + lax.dot_general(jnp.ones((SUBLANES, tm), BF16), sel_b, nt,
                                                    preferred_element_type=F32)

    idx_rows, w_rows, rank_rows = [], [], []
    for idx in ids:
        hit = e_iota == idx
        w = jnp.sum(jnp.where(hit, scores, 0.0), axis=0, keepdims=True)
        w_rows.append(w / top_sum * ROUTED_SCALE)
        rank_rows.append(jnp.sum(jnp.where(hit, rank, 0.0), axis=0, keepdims=True))
        idx_rows.append(idx)
    idx_ref[...] = jnp.concatenate(idx_rows, axis=0)
    rank_ref[...] = jnp.concatenate(rank_rows, axis=0).astype(I32)

    w_all = jnp.concatenate(w_rows, axis=0)
    w_1 = w_all.astype(BF16)
    r_1 = w_all - w_1.astype(F32)
    w_2 = r_1.astype(BF16)
    w_3 = (r_1 - w_2.astype(F32)).astype(BF16)
    terms = jnp.concatenate([w_1, w_2, w_3, jnp.zeros_like(w_1)], axis=0)
    t_row = lax.broadcasted_iota(I32, (4 * TOP_K, TOP_K * LANES), 0)
    t_col = lax.broadcasted_iota(I32, (4 * TOP_K, TOP_K * LANES), 1)
    spread = jnp.where((t_row % TOP_K) == (t_col // LANES), 1.0, 0.0).astype(BF16)
    w_lanes = lax.dot_general(terms, spread, (((0,), (0,)), ((), ())), preferred_element_type=F32)
    for k in range(TOP_K):
        wl_ref[pl.ds(k, tm, stride=TOP_K), :] = w_lanes[:, k * LANES:(k + 1) * LANES]

    @pl.when(i == pl.num_programs(0) - 1)
    def _():
        cntc_ref[...] = cntc_scr[...].astype(I32)
        cntr_ref[...] = cntr_scr[...].astype(I32)


def _router(x1, w_hi, w_lo, bias, *, tm, name):
    t = x1.shape[0]
    full = lambda i: (0, 0)
    tok = lambda i: (0, i)
    return pl.pallas_call(
        _router_kernel,
        grid=(t // tm,),
        in_specs=[pl.BlockSpec((tm, D_MODEL), lambda i: (i, 0)),
                  pl.BlockSpec((N_EXPERTS, D_MODEL), full),
                  pl.BlockSpec((N_EXPERTS, D_MODEL), full),
                  pl.BlockSpec((N_EXPERTS, 1), full)],
        out_specs=[pl.BlockSpec((TOP_K, tm), tok),
                   pl.BlockSpec((TOP_K * tm, LANES), lambda i: (i, 0)),
                   pl.BlockSpec((TOP_K, tm), tok),
                   pl.BlockSpec((N_EXPERTS, 1), full),
                   pl.BlockSpec((SUBLANES, N_EXPERTS), full)],
        out_shape=[jax.ShapeDtypeStruct((TOP_K, t), I32),
                   jax.ShapeDtypeStruct((TOP_K * t, LANES), F32),
                   jax.ShapeDtypeStruct((TOP_K, t), I32),
                   jax.ShapeDtypeStruct((N_EXPERTS, 1), I32),
                   jax.ShapeDtypeStruct((SUBLANES, N_EXPERTS), I32)],
        scratch_shapes=[pltpu.VMEM((N_EXPERTS, 1), F32),
                        pltpu.VMEM((SUBLANES, N_EXPERTS), F32)],
        compiler_params=_cparams(("arbitrary",)),
        name=name,
    )(x1, w_hi, w_lo, bias)


ITEM_FIELDS = 8


def _byte_split(v):
    return lax.shift_right_logical(v, 8).astype(F32), (v & 255).astype(F32)


def _plan_kernel(cntc_ref, cntr_ref, idx_ref, rank_ref, meta_ref, items_ref, m_scr, *, n_tok):
    i = pl.program_id(0)
    tm = idx_ref.shape[1]
    nbm = m_scr.shape[0]
    nip = items_ref.shape[1]
    nbp = nip - N_EXPERTS
    n_rows = n_tok * TOP_K
    n_blk = n_rows // EXPERT_ROWS
    nt = (((1,), (1,)), ((), ()))

    r_i = lax.broadcasted_iota(I32, (N_EXPERTS, N_EXPERTS), 0)
    c_i = lax.broadcasted_iota(I32, (N_EXPERTS, N_EXPERTS), 1)
    below = jnp.where(c_i < r_i, 1.0, 0.0).astype(BF16)
    ones_c = jnp.ones((N_EXPERTS, LANES), F32)
    c_hi, c_lo = _byte_split(cntc_ref[...])
    start_col = (256.0 * jnp.dot(below, (c_hi * ones_c).astype(BF16), preferred_element_type=F32)
                 + jnp.dot(below, (c_lo * ones_c).astype(BF16), preferred_element_type=F32))[:, 0:1]

    @pl.when(i == 0)
    def _():
        m_scr[...] = jnp.zeros_like(m_scr)
        above = jnp.where(r_i < c_i, 1.0, 0.0).astype(BF16)
        r_hi, r_lo = _byte_split(cntr_ref[...])
        start_row = (256.0 * jnp.dot(r_hi.astype(BF16), above, preferred_element_type=F32)
                     + jnp.dot(r_lo.astype(BF16), above, preferred_element_type=F32))[0:1, :]
        b_col = lax.broadcasted_iota(I32, (nbp, 1), 0)
        b_row = lax.broadcasted_iota(I32, (1, nbp), 1)
        blk_col = jnp.where(b_col < n_blk, b_col * EXPERT_ROWS, n_rows).astype(F32)
        blk_row = jnp.where(b_row < n_blk, b_row * EXPERT_ROWS, n_rows).astype(F32)
        v_col = jnp.concatenate([blk_col, start_col], axis=0)
        v_row = jnp.concatenate([blk_row, start_row], axis=1)
        j_col = lax.broadcasted_iota(I32, (nip, 1), 0)
        k_row = lax.broadcasted_iota(I32, (1, nip), 1)
        ahead = (v_row < v_col) | ((v_row == v_col) & (k_row < j_col))
        order_col = jnp.sum(jnp.where(ahead, 1.0, 0.0), axis=1, keepdims=True)
        pos_row = k_row.astype(F32)
        lo_abs = jnp.sum(jnp.where(order_col == pos_row, v_col, 0.0), axis=0, keepdims=True)
        hi_abs = jnp.sum(jnp.where(order_col == pos_row + 1.0, v_col, 0.0), axis=0, keepdims=True)
        hi_abs = jnp.where(k_row == nip - 1, float(n_rows), hi_abs)
        blk = jnp.minimum(jnp.floor(lo_abs * (1.0 / EXPERT_ROWS)), n_blk - 1.0)
        expert = jnp.sum(jnp.where(start_col <= lo_abs, 1.0, 0.0), axis=0, keepdims=True) - 1.0
        base = blk * EXPERT_ROWS
        e_col = lax.broadcasted_iota(I32, (N_EXPERTS, 1), 0).astype(F32)
        end_col = start_col + cntc_ref[...].astype(F32)
        seg_end = jnp.sum(jnp.where(e_col == expert, end_col, 0.0), axis=0, keepdims=True)
        follower = jnp.sum(jnp.where(start_col <= seg_end, 1.0, 0.0), axis=0, keepdims=True) - 1.0
        follower = jnp.where(seg_end < float(n_rows), follower, -1.0)
        fields = [blk, expert, lo_abs - base, hi_abs - base, follower]
        fields.append(jnp.zeros((ITEM_FIELDS - len(fields), nip), F32))
        items_ref[...] = jnp.concatenate(fields, axis=0).astype(I32)

    e_iota = lax.broadcasted_iota(I32, (N_EXPERTS, tm), 0)
    b_iota = lax.broadcasted_iota(I32, (nbm, tm), 0)
    l_iota = lax.broadcasted_iota(I32, (EXPERT_ROWS, tm), 0)
    tok = i * tm + lax.broadcasted_iota(I32, (1, tm), 1)
    tok_hi, tok_lo = _byte_split(tok)
    acc = jnp.zeros(m_scr.shape, F32)
    for k in range(TOP_K):
        hit = e_iota == idx_ref[k:k + 1, :]
        dest = (jnp.sum(jnp.where(hit, start_col, 0.0), axis=0, keepdims=True).astype(I32)
                + rank_ref[k:k + 1, :])
        oh_blk = jnp.where(b_iota == lax.shift_right_logical(dest, 8), 1.0, 0.0).astype(BF16)
        in_blk = l_iota == (dest & (EXPERT_ROWS - 1))
        vals = jnp.concatenate([jnp.where(in_blk, tok_hi, 0.0), jnp.where(in_blk, tok_lo, 0.0),
                                jnp.where(in_blk, float(k), 0.0)], axis=0).astype(BF16)
        acc = acc + lax.dot_general(oh_blk, vals, nt, preferred_element_type=F32)
    m_scr[...] = m_scr[...] + acc

    @pl.when(i == pl.num_programs(0) - 1)
    def _():
        m = m_scr[...]
        row_tok = m[:, 0:EXPERT_ROWS] * 256.0 + m[:, EXPERT_ROWS:2 * EXPERT_ROWS]
        row_slot = m[:, 2 * EXPERT_ROWS:3 * EXPERT_ROWS] * float(n_tok) + row_tok
        meta_ref[...] = jnp.concatenate([row_tok * float(PACKED_ROWS), row_slot * float(ROW_CHUNKS)],
                                        axis=1).astype(I32)


def _plan(cnt_col, cnt_row, idx, rank, *, tm, name):
    t = idx.shape[1]
    assert EXPERT_ROWS == 256 and (t * TOP_K) % (2 * EXPERT_ROWS) == 0
    n_blk = t * TOP_K // EXPERT_ROWS
    nbm = -(-n_blk // 16) * 16
    nip = -(-n_blk // LANES) * LANES + N_EXPERTS
    tok = lambda i: (0, i)
    full = lambda i: (0, 0)
    return pl.pallas_call(
        functools.partial(_plan_kernel, n_tok=t),
        grid=(t // tm,),
        in_specs=[pl.BlockSpec((N_EXPERTS, 1), full),
                  pl.BlockSpec((SUBLANES, N_EXPERTS), full),
                  pl.BlockSpec((TOP_K, tm), tok),
                  pl.BlockSpec((TOP_K, tm), tok)],
        out_specs=[pl.BlockSpec((nbm, 2 * EXPERT_ROWS), full),
                   pl.BlockSpec((ITEM_FIELDS, nip), full)],
        out_shape=[jax.ShapeDtypeStruct((nbm, 2 * EXPERT_ROWS), I32),
                   jax.ShapeDtypeStruct((ITEM_FIELDS, nip), I32)],
        scratch_shapes=[pltpu.VMEM((nbm, 3 * EXPERT_ROWS), F32)],
        compiler_params=_cparams(("arbitrary",)),
        name=name,
    )(cnt_col, cnt_row, idx, rank)


META_CHUNK = 4 * EXPERT_ROWS
GATHER_STRIDE = EXPERT_ROWS + 8
META_RING = 3
FFN_PIECES = 2 + D_MODEL // EXPERT_FF


def _experts_kernel(items_ref, xw_hbm, meta_hbm, wg_hbm, wu_hbm, wd_hbm, ys_hbm,
                    xw, meta, tile, lhs, wg_f, wu_f, wd_f, wg_b, wu_b, wd_b, yacc, ybuf_0, ybuf_1, state,
                    sem_x, sem_m, sem_y, sem_w, *, nip, n_blk):
    i = pl.program_id(0)
    ybuf = (ybuf_0, ybuf_1)
    blk = items_ref[i]
    expert = items_ref[nip + i]
    lo = items_ref[2 * nip + i]
    hi = items_ref[3 * nip + i]
    follower = items_ref[4 * nip + i]
    nonempty = hi > lo
    par = blk & 1
    chunk = lax.shift_right_logical(blk, 1)

    def meta_base(b):
        return (lax.shift_right_logical(b, 1) % META_RING) * META_CHUNK + (b & 1) * (2 * EXPERT_ROWS)

    mbase = meta_base(blk)
    prev_base = meta_base(jnp.maximum(blk - 1, 0))

    def meta_copy(c):
        slot = c % META_RING
        return pltpu.make_async_copy(meta_hbm.at[pl.ds(pl.multiple_of(c * META_CHUNK, META_CHUNK), META_CHUNK)],
                                     meta.at[pl.ds(pl.multiple_of(slot * META_CHUNK, META_CHUNK), META_CHUNK)],
                                     sem_m.at[slot])

    def ybuf_drain(slot):
        pltpu.make_async_copy(ys_hbm.at[pl.ds(0, EXPERT_ROWS * ROW_CHUNKS), :], ybuf[slot],
                              sem_y.at[slot]).wait()

    def ybuf_fill(slot):
        for j in range(ROW_CHUNKS):
            ybuf[slot][pl.ds(j, EXPERT_ROWS, stride=ROW_CHUNKS), :] = yacc[:, j * LANES:(j + 1) * LANES]

    def send_rows(slot, base, r0, r1):
        for r in range(r0, r1):
            src = ybuf[slot].at[pl.ds(r * ROW_CHUNKS, ROW_CHUNKS), :]
            dst_row = pl.multiple_of(meta[base + EXPERT_ROWS + r], ROW_CHUNKS)
            pltpu.make_async_copy(src, ys_hbm.at[pl.ds(dst_row, ROW_CHUNKS), :],
                                  sem_y.at[slot]).start(priority=r % 2)

    def weight_copies(e, slot):
        return [pltpu.make_async_copy(src.at[e], dst.at[slot], sem_w.at[slot])
                for src, dst in ((wg_hbm, wg_f), (wu_hbm, wu_f), (wd_hbm, wd_f))]

    @pl.when(i == 0)
    def _():
        resident = pltpu.make_async_copy(xw_hbm, xw, sem_x)
        resident.start()
        meta_copy(0).start()
        for cp in weight_copies(expert, 0):
            cp.start()
        state[0] = -1
        state[1] = 1
        resident.wait()

    @pl.when(nonempty & (expert != state[0]))
    def _():
        slot = 1 - state[1]
        for cp in weight_copies(expert, slot):
            cp.wait()
        state[0] = expert
        state[1] = slot

        @pl.when(follower >= 0)
        def _():
            for cp in weight_copies(follower, 1 - slot):
                cp.start()

        wg_b[...] = wg_f[slot].astype(BF16)
        wu_b[...] = wu_f[slot].astype(BF16)
        wd_b[...] = wd_f[slot].astype(BF16)

    @pl.when(nonempty & (lo == 0))
    def _():
        @pl.when(par == 0)
        def _():
            meta_copy(chunk).wait()

            @pl.when(2 * (chunk + 1) < n_blk)
            def _():
                meta_copy(chunk + 1).start()

        for r in range(EXPERT_ROWS):
            t4 = pl.multiple_of(meta[mbase + r], PACKED_ROWS)
            tile[pl.ds(r, PACKED_ROWS, stride=GATHER_STRIDE), :] = xw[pl.ds(t4, PACKED_ROWS), :]
        cols = []
        for s in range(PACKED_ROWS):
            cols.extend(_unpack_bf16_pairs(tile[pl.ds(s * GATHER_STRIDE, EXPERT_ROWS), :]))
        lhs[...] = jnp.concatenate(cols, axis=-1)

    def ffn(x, between=None, to_slot=None):
        done = [0]

        def piece_done():
            if between is not None:
                between(done[0])
            done[0] += 1

        gate = jnp.dot(x, wg_b[...], preferred_element_type=F32)
        piece_done()
        up = jnp.dot(x, wu_b[...], preferred_element_type=F32)
        piece_done()
        hid = (jax.nn.silu(gate) * up).astype(BF16)
        out = []
        for c in range(D_MODEL // EXPERT_FF):
            y_c = jnp.dot(hid, wd_b[:, c * EXPERT_FF:(c + 1) * EXPERT_FF], preferred_element_type=F32)
            if to_slot is None:
                out.append(y_c)
            else:
                for jj in range(EXPERT_FF // LANES):
                    j = c * (EXPERT_FF // LANES) + jj
                    ybuf[to_slot][pl.ds(j, EXPERT_ROWS, stride=ROW_CHUNKS), :] = y_c[:, jj * LANES:(jj + 1) * LANES]
            piece_done()
        assert done[0] == FFN_PIECES
        return jnp.concatenate(out, axis=-1) if to_slot is None else None

    def run_mask(y):
        row = lax.broadcasted_iota(I32, (EXPERT_ROWS, 1), 0)
        return jnp.where((row >= lo) & (row < hi), y, 0.0)

    whole = (lo == 0) & (hi == EXPERT_ROWS)
    first = nonempty & (lo == 0)
    has_prev = blk > 0

    for prev_slot in range(2):
        def send_piece(g, prev_slot=prev_slot):
            send_rows(prev_slot, prev_base, g * EXPERT_ROWS // FFN_PIECES, (g + 1) * EXPERT_ROWS // FFN_PIECES)

        sends = has_prev & (par == 1 - prev_slot)

        @pl.when(whole & sends)
        def _():
            @pl.when(blk >= 2)
            def _():
                ybuf_drain(1 - prev_slot)

            ffn(lhs[...], send_piece, to_slot=1 - prev_slot)

        @pl.when(first & jnp.logical_not(whole) & sends)
        def _():
            yacc[...] = run_mask(ffn(lhs[...], send_piece))

    @pl.when(whole & jnp.logical_not(has_prev))
    def _():
        ffn(lhs[...], to_slot=0)

    @pl.when(nonempty & jnp.logical_not(whole) & jnp.logical_not(first & has_prev))
    def _():
        y = run_mask(ffn(lhs[...]))

        @pl.when(lo == 0)
        def _():
            yacc[...] = y

        @pl.when(lo > 0)
        def _():
            yacc[...] = yacc[...] + y

    @pl.when(nonempty & (hi == EXPERT_ROWS))
    def _():
        @pl.when(jnp.logical_not(whole))
        def _():
            for slot in range(2):
                @pl.when(par == slot)
                def _():
                    @pl.when(blk >= 2)
                    def _():
                        ybuf_drain(slot)

                    ybuf_fill(slot)

        @pl.when(blk == n_blk - 1)
        def _():
            for slot in range(2):
                @pl.when(par == slot)
                def _():
                    send_rows(slot, mbase, 0, EXPERT_ROWS)

    @pl.when(i == pl.num_programs(0) - 1)
    def _():
        ybuf_drain(0)
        ybuf_drain(1)


def _experts(items, xw, meta, wg, wu, wd, *, n_tok, name):
    nip = items.shape[0] // ITEM_FIELDS
    n_rows = n_tok * TOP_K
    n_blk = n_rows // EXPERT_ROWS
    n_items = n_blk + N_EXPERTS
    assert n_blk >= 2 and n_blk % 2 == 0
    kern = functools.partial(_experts_kernel, nip=nip, n_blk=n_blk)
    any_spec = pl.BlockSpec(memory_space=pl.ANY)
    return pl.pallas_call(
        kern,
        grid_spec=pltpu.PrefetchScalarGridSpec(
            num_scalar_prefetch=1,
            grid=(n_items,),
            in_specs=[any_spec] * 5,
            out_specs=any_spec,
            scratch_shapes=[pltpu.VMEM(xw.shape, I32),
                            pltpu.SMEM((META_RING * META_CHUNK,), I32),
                            pltpu.VMEM((PACKED_ROWS * GATHER_STRIDE, LANES), I32),
                            pltpu.VMEM((EXPERT_ROWS, D_MODEL), BF16),
                            pltpu.VMEM((2, D_MODEL, EXPERT_FF), F32),
                            pltpu.VMEM((2, D_MODEL, EXPERT_FF), F32),
                            pltpu.VMEM((2, EXPERT_FF, D_MODEL), F32),
                            pltpu.VMEM((D_MODEL, EXPERT_FF), BF16),
                            pltpu.VMEM((D_MODEL, EXPERT_FF), BF16),
                            pltpu.VMEM((EXPERT_FF, D_MODEL), BF16),
                            pltpu.VMEM((EXPERT_ROWS, D_MODEL), F32),
                            pltpu.VMEM((EXPERT_ROWS * ROW_CHUNKS, LANES), F32),
                            pltpu.VMEM((EXPERT_ROWS * ROW_CHUNKS, LANES), F32),
                            pltpu.SMEM((2,), I32),
                            pltpu.SemaphoreType.DMA,
                            pltpu.SemaphoreType.DMA((META_RING,)),
                            pltpu.SemaphoreType.DMA((2,)),
                            pltpu.SemaphoreType.DMA((2,))],
        ),
        out_shape=jax.ShapeDtypeStruct((n_rows * ROW_CHUNKS, LANES), F32),
        compiler_params=_cparams(("arbitrary",)),
        name=name,
    )(items, xw, meta, wg, wu, wd)


def _combine_kernel(ys_ref, x_ref, wl_ref, wsg_ref, wsu_ref, wsd_ref, g_ref, b_ref, op_ref, os_ref,
                    routed, *, np_tiles):
    x = x_ref[...]
    xb = x.astype(BF16)
    hid = (jax.nn.silu(jnp.dot(xb, wsg_ref[...], preferred_element_type=F32))
           * jnp.dot(xb, wsu_ref[...], preferred_element_type=F32))
    shared = jnp.dot(hid.astype(BF16), wsd_ref[...], preferred_element_type=F32)
    acc = None
    for k in range(TOP_K):
        part = wl_ref[:, k:k + 1, :] * ys_ref[k]
        acc = part if acc is None else acc + part
    routed[...] = acc
    ffn = jnp.concatenate([routed[:, j, :] for j in range(ROW_CHUNKS)], axis=-1) + shared
    out = _layer_norm(DEEPNORM_ALPHA * x + ffn, g_ref[...], b_ref[...])
    i = pl.program_id(0)

    @pl.when(i < np_tiles)
    def _():
        op_ref[...] = out

    @pl.when(i >= np_tiles)
    def _():
        os_ref[...] = out


def _combine(ys, x1, w_tok, wsg, wsu, wsd, g, b, *, n_prompt, tm, name):
    t = x1.shape[0]
    np_tiles = n_prompt // tm
    p_map, s_map = _pair_maps(np_tiles)
    row = lambda i: (i, 0)
    full = lambda i: (0, 0)
    return pl.pallas_call(
        functools.partial(_combine_kernel, np_tiles=np_tiles),
        grid=(t // tm,),
        in_specs=[pl.BlockSpec((TOP_K, tm, ROW_CHUNKS, LANES), lambda i: (0, i, 0, 0)),
                  pl.BlockSpec((tm, D_MODEL), row),
                  pl.BlockSpec((tm, TOP_K, LANES), lambda i: (i, 0, 0)),
                  pl.BlockSpec((D_MODEL, SHARED_FF), full),
                  pl.BlockSpec((D_MODEL, SHARED_FF), full),
                  pl.BlockSpec((SHARED_FF, D_MODEL), full),
                  pl.BlockSpec((1, D_MODEL), full),
                  pl.BlockSpec((1, D_MODEL), full)],
        out_specs=[pl.BlockSpec((tm, D_MODEL), p_map), pl.BlockSpec((tm, D_MODEL), s_map)],
        out_shape=[jax.ShapeDtypeStruct((n_prompt, D_MODEL), F32),
                   jax.ShapeDtypeStruct((t - n_prompt, D_MODEL), F32)],
        scratch_shapes=[pltpu.VMEM((tm, ROW_CHUNKS, LANES), F32)],
        compiler_params=_cparams(("arbitrary",)),
        name=name,
    )(ys, x1, w_tok, wsg, wsu, wsd, g, b)


def _pack_w_in(w_in):
    sizes = (GLA_KEY_DIM, GLA_KEY_DIM, GLA_VAL_DIM, GLA_VAL_DIM, GLA_GATE_RANK,
             SWA_Q_DIM, SWA_KV_DIM, SWA_KV_DIM, D_MODEL, D_MODEL)
    offs = [0]
    for s in sizes:
        offs.append(offs[-1] + s)
    qa, ka, va, ga, gk, qb, kb, vb, gate_a, gate_b = (w_in[:, offs[i]:offs[i + 1]] for i in range(10))
    pad = lambda w, n: jnp.pad(w, ((0, 0), (0, n - w.shape[1])))
    packed = jnp.concatenate([qa, ka, va, ga, qb, gate_a, gate_b, kb, vb, pad(gk, 2 * LANES)], axis=1)
    assert packed.shape[1] == H_WIDTH
    return packed.astype(BF16), jnp.concatenate([kb, vb], axis=1).astype(BF16)


def kernel(x_prompt, x_sample, state_gla, cache_swa_k, cache_swa_v, w_in, w_gk_up, b_gk, gla_norm_g,
           attn_sinks, w_proj_a, w_proj_b, w_out, ln1_g, ln1_b, w_router, router_bias,
           w_expert_gate, w_expert_up, w_expert_down, w_shared_gate, w_shared_up, w_shared_down,
           ln2_g, ln2_b):
    assert w_in.shape[0] == 1, "single-layer trunk"
    bp, lp, d = x_prompt.shape
    bs, ls, _ = x_sample.shape
    assert d == D_MODEL and ls == SUBLANES and cache_swa_k.shape[2] == WINDOW
    tp, ts = bp * lp, bs * ls
    t = tp + ts

    xp = x_prompt.reshape(tp, d)
    xs = x_sample.reshape(ts, d)
    w_main, w_kv = _pack_w_in(w_in[0])
    h = _matmul(xp, xs, w_main, BF16, _pair_tile(tp, ts, 1024), H_TN, "proj_in")

    xp_tail = x_prompt[:, lp - WINDOW:].reshape(bp * WINDOW, d)
    kv_tail = _matmul(xp_tail, xs, w_kv, F32, _pair_tile(bp * WINDOW, ts, 512), 2 * SWA_KV_DIM,
                      "proj_kv_tail")

    wup = jnp.pad(w_gk_up[0], ((0, LANES - GLA_GATE_RANK), (0, 0))).astype(BF16)
    bgk = b_gk[0].reshape(1, GLA_KEY_DIM)
    gn = gla_norm_g[0].reshape(1, GLA_DV)
    oa_p, s_prompt = _gla(h, wup, bgk, gn, None, row0=0, n_seq=bp, seq_len=lp, par=math.gcd(bp, 4), nb=1,
                          c=GLA_CHUNK, sub=2, name="gla_prompt")
    oa_s, s_sample = _gla(h, wup, bgk, gn, state_gla[0], row0=tp, n_seq=bs, seq_len=ls, par=1, nb=8,
                          c=math.gcd(ls, GLA_CHUNK), sub=1, name="gla_sample")
    oa_p = oa_p.reshape(tp, GLA_VAL_DIM)
    oa_s = oa_s.reshape(ts, GLA_VAL_DIM)

    sinks = attn_sinks[0]
    k_past = cache_swa_k[0].reshape(bs, WINDOW, SWA_KV_DIM)
    v_past = cache_swa_v[0].reshape(bs, WINDOW, SWA_KV_DIM)
    ob_p = _swa_prompt(h, sinks, n_seq=bp, seq_len=lp, name="swa_prompt")
    ob_s = _swa_sample(h, sinks, k_past, v_past, row0=tp, n_seq=bs, lq=ls, nb=8, name="swa_sample")

    x1, xw = _merge((oa_p, oa_s), (ob_p, ob_s), h, (xp, xs), w_proj_a[0].astype(BF16),
                    w_proj_b[0].astype(BF16), w_out[0].astype(BF16), ln1_g[0].reshape(1, d),
                    ln1_b[0].reshape(1, d), tm=_pair_tile(tp, ts, 512), name="merge_ln1")

    wr_t = w_router[0].T
    wr_hi = wr_t.astype(BF16)
    wr_lo = (wr_t - wr_hi.astype(F32)).astype(BF16)
    tm_r = _tile(t, 512)
    idx, w_lanes, rank, cnt_col, cnt_row = _router(x1, wr_hi, wr_lo, router_bias[0].reshape(N_EXPERTS, 1),
                                                   tm=tm_r, name="router")
    meta, items = _plan(cnt_col, cnt_row, idx, rank, tm=tm_r, name="plan")
    ys = _experts(items.reshape(-1), xw, meta.reshape(-1), w_expert_gate[0], w_expert_up[0],
                  w_expert_down[0], n_tok=t, name="experts")
    y_p, y_s = _combine(ys.reshape(TOP_K, t, ROW_CHUNKS, LANES), x1, w_lanes.reshape(t, TOP_K, LANES),
                        w_shared_gate[0].astype(BF16), w_shared_up[0].astype(BF16),
                        w_shared_down[0].astype(BF16), ln2_g[0].reshape(1, d), ln2_b[0].reshape(1, d),
                        n_prompt=tp, tm=_pair_tile(tp, ts, 128), name="combine_ln2")

    y_prompt = y_p.reshape(bp, lp, d)
    y_sample = y_s.reshape(bs, ls, d)
    k_tail = kv_tail[:, :SWA_KV_DIM]
    v_tail = kv_tail[:, SWA_KV_DIM:]
    kv_shape = (SWA_KV_HEADS, SWA_HEAD_DIM)
    k_prompt = k_tail[:bp * WINDOW].reshape(1, bp, WINDOW, *kv_shape)
    v_prompt = v_tail[:bp * WINDOW].reshape(1, bp, WINDOW, *kv_shape)
    k_new = k_tail[bp * WINDOW:].reshape(bs, ls, *kv_shape)
    v_new = v_tail[bp * WINDOW:].reshape(bs, ls, *kv_shape)
    k_sample = jnp.concatenate([cache_swa_k[0][:, ls:], k_new], axis=1)[None]
    v_sample = jnp.concatenate([cache_swa_v[0][:, ls:], v_new], axis=1)[None]
    return (y_prompt, y_sample, s_prompt[None], s_sample[None], k_prompt, v_prompt, k_sample, v_sample)
```

```python
import functools
import math

import jax
import jax.numpy as jnp
from jax import lax
from jax.experimental import pallas as pl
from jax.experimental.pallas import tpu as pltpu

F32 = jnp.float32
BF16 = jnp.bfloat16
I32 = jnp.int32

D_MODEL = 1024
GLA_HEADS = 4
GLA_DK = 128
GLA_DV = 256
GLA_KEY_DIM = GLA_HEADS * GLA_DK
GLA_VAL_DIM = GLA_HEADS * GLA_DV
GLA_GATE_RANK = 16
GLA_GATE_NORMALIZER = 16.0
GLA_CHUNK = 64
SWA_HEADS = 16
SWA_KV_HEADS = 4
SWA_GROUP = SWA_HEADS // SWA_KV_HEADS
SWA_HEAD_DIM = 64
SWA_Q_DIM = SWA_HEADS * SWA_HEAD_DIM
SWA_KV_DIM = SWA_KV_HEADS * SWA_HEAD_DIM
WINDOW = 128
N_EXPERTS = 256
TOP_K = 8
N_GROUPS = 8
GROUP_SIZE = N_EXPERTS // N_GROUPS
TOPK_GROUPS = 4
EXPERT_FF = 256
SHARED_FF = 256
ROUTED_SCALE = 2.5
DEEPNORM_ALPHA = 2.0 ** 0.25
EPS = 1e-5

LANES = 128
SUBLANES = 8
ROW_CHUNKS = D_MODEL // LANES
VMEM_LIMIT = 56 * 1024 * 1024

H_QA, H_KA, H_VA, H_GA, H_QB, H_GATE_A, H_GATE_B, H_KB, H_VB, H_GK = (
    0, 512, 1024, 2048, 3072, 4096, 5120, 6144, 6400, 6656)
H_WIDTH = 6912
H_TN = 2304

EXPERT_ROWS = 256
PACKED_ROWS = ROW_CHUNKS // 2
HIGH_HALF = -65536


def _cparams(sem, vmem=VMEM_LIMIT):
    return pltpu.CompilerParams(dimension_semantics=sem, vmem_limit_bytes=vmem)


def _tile(n, pref):
    t = min(n, pref)
    while n % t:
        t -= LANES
    assert t > 0 and t % LANES == 0, (n, pref)
    return t


def _pair_tile(n_p, n_s, pref):
    return _tile(math.gcd(n_p, n_s), pref)


def _pair_maps(np_tiles, col=0, extra=0):
    def p_map(i, *_):
        return (jnp.minimum(i, np_tiles - 1), col)

    def s_map(i, *_):
        return (jnp.maximum(i - np_tiles, 0), col)

    return p_map, s_map


def _pair_value(i, np_tiles, p_ref, s_ref):
    return jnp.where(i < np_tiles, p_ref[...], s_ref[...])


def _mm_kernel(xp_ref, xs_ref, w_ref, o_ref, *, np_tiles):
    x = _pair_value(pl.program_id(0), np_tiles, xp_ref, xs_ref)
    o_ref[...] = jnp.dot(x.astype(BF16), w_ref[...], preferred_element_type=F32).astype(o_ref.dtype)


def _matmul(xp, xs, w, out_dtype, tm, tn, name):
    k = xp.shape[1]
    m = xp.shape[0] + xs.shape[0]
    n = w.shape[1]
    np_tiles = xp.shape[0] // tm
    p_map, s_map = _pair_maps(np_tiles)
    return pl.pallas_call(
        functools.partial(_mm_kernel, np_tiles=np_tiles),
        grid=(m // tm, n // tn),
        in_specs=[pl.BlockSpec((tm, k), p_map),
                  pl.BlockSpec((tm, k), s_map),
                  pl.BlockSpec((k, tn), lambda i, j: (0, j))],
        out_specs=pl.BlockSpec((tm, tn), lambda i, j: (i, j)),
        out_shape=jax.ShapeDtypeStruct((m, n), out_dtype),
        compiler_params=_cparams(("parallel", "arbitrary")),
        name=name,
    )(xp, xs, w)


def _split_bf16(x):
    hi = x.astype(BF16)
    lo = (x - hi.astype(F32)).astype(BF16)
    return hi, lo


def _gla_kernel(*refs, par, nb, c, sub, has_s0):
    q_refs, k_refs, v_refs, ga_refs, gk_refs = (refs[n * par:(n + 1) * par] for n in range(5))
    rest = refs[5 * par:]
    n_state = par * GLA_HEADS
    s_scr = [rest[len(rest) - n_state + p * GLA_HEADS:len(rest) - n_state + (p + 1) * GLA_HEADS]
             for p in range(par)]
    rest = rest[:len(rest) - n_state]
    if has_s0:
        wup_ref, bgk_ref, gn_ref, s0_ref, o_ref, sout_ref = rest
    else:
        wup_ref, bgk_ref, gn_ref, o_ref, sout_ref = rest
        s0_ref = None
    ci = pl.program_id(1)
    rows = nb * c

    @pl.when(ci == 0)
    def _():
        for p in range(par):
            for h in range(GLA_HEADS):
                if has_s0:
                    s_scr[p][h][...] = s0_ref[p * nb:(p + 1) * nb, h]
                else:
                    s_scr[p][h][...] = jnp.zeros_like(s_scr[p][h])

    r_i = lax.broadcasted_iota(I32, (rows, rows), 0)
    c_i = lax.broadcasted_iota(I32, (rows, rows), 1)
    same_seq = (r_i // c) == (c_i // c)
    causal = same_seq & (c_i <= r_i)
    tri = jnp.where(causal, 1.0, 0.0).astype(BF16)
    seg = jnp.where(same_seq, 1.0, 0.0).astype(BF16)
    ones_kv = jnp.ones((rows, GLA_DV), BF16)
    seq_of_row = lax.broadcasted_iota(I32, (rows, 1), 0) // c
    tn = (((0,), (0,)), ((), ()))

    def decays(p, rs):
        pre = jnp.dot(gk_refs[p][rs, :], wup_ref[...], preferred_element_type=F32) + bgk_ref[...]
        log_a = jax.nn.log_sigmoid(pre) / GLA_GATE_NORMALIZER
        la_hi, la_lo = _split_bf16(log_a)
        b = (jnp.dot(tri, la_hi, preferred_element_type=F32)
             + jnp.dot(tri, la_lo, preferred_element_type=F32))
        b_last = (jnp.dot(seg, la_hi, preferred_element_type=F32)
                  + jnp.dot(seg, la_lo, preferred_element_type=F32))
        q = q_refs[p][rs, :].astype(F32) * (GLA_DK ** -0.5)
        k = k_refs[p][rs, :].astype(F32)
        q_dec = (q * jnp.exp(b)).astype(BF16)
        k_dec = (k * jnp.exp(-b)).astype(BF16)
        k_rem = k * jnp.exp(b_last - b)
        return q_dec, k_dec, k_rem, la_hi, la_lo

    def head(p, rs, h, q_dec, k_dec, k_rem, la_hi, la_lo):
        ks = slice(h * GLA_DK, (h + 1) * GLA_DK)
        vh = v_refs[p][rs, h * GLA_DV:(h + 1) * GLA_DV]
        att = lax.dot_general(q_dec[:, ks], k_dec[:, ks], (((1,), (1,)), ((), ())),
                              preferred_element_type=F32)
        att = jnp.where(causal, att, 0.0).astype(BF16)
        o_h = jnp.dot(att, vh, preferred_element_type=F32)
        for j in range(nb):
            mine = (seq_of_row == j) if nb > 1 else None
            pick = (lambda a: jnp.where(mine, a, 0.0)) if nb > 1 else (lambda a: a)
            s_old = s_scr[p][h][j]
            o_h = o_h + pick(jnp.dot(q_dec[:, ks], s_old.astype(BF16), preferred_element_type=F32))
            dec = (lax.dot_general(pick(la_hi[:, ks].astype(F32)).astype(BF16), ones_kv, tn,
                                   preferred_element_type=F32)
                   + lax.dot_general(pick(la_lo[:, ks].astype(F32)).astype(BF16), ones_kv, tn,
                                     preferred_element_type=F32))
            upd = lax.dot_general(pick(k_rem[:, ks]).astype(BF16), vh, tn, preferred_element_type=F32)
            s_scr[p][h][j] = jnp.exp(dec) * s_old + upd
        return o_h * lax.rsqrt(jnp.mean(jnp.square(o_h), axis=-1, keepdims=True) + EPS) * gn_ref[...]

    for s in range(sub):
        rs = slice(s * rows, (s + 1) * rows)
        staged = [decays(p, rs) for p in range(par)]
        outs = [[] for _ in range(par)]
        for h in range(GLA_HEADS):
            for p in range(par):
                outs[p].append(head(p, rs, h, *staged[p]))
        for p in range(par):
            o = jnp.concatenate(outs[p], axis=-1) * jax.nn.silu(ga_refs[p][rs, :].astype(F32))
            o_ref[p, rs, :] = o.astype(o_ref.dtype)

    @pl.when(ci == pl.num_programs(1) - 1)
    def _():
        for p in range(par):
            for h in range(GLA_HEADS):
                sout_ref[p * nb:(p + 1) * nb, h] = s_scr[p][h][...]


def _gla(h, wup, bgk, gn, s0, *, row0, n_seq, seq_len, par, nb, c, sub, name):
    rows = nb * c * sub
    n_blocks = n_seq // nb
    n_steps = seq_len // (c * sub)
    rb0 = row0 // rows

    def rmap(p, col):
        return lambda g, i: (rb0 + (g * par + p) * n_steps + i, col)

    fields = ((GLA_KEY_DIM, H_QA), (GLA_KEY_DIM, H_KA), (GLA_VAL_DIM, H_VA), (GLA_VAL_DIM, H_GA), (LANES, H_GK))
    in_specs = [pl.BlockSpec((rows, width), rmap(p, off // width)) for width, off in fields for p in range(par)]
    in_specs += [pl.BlockSpec((LANES, GLA_KEY_DIM), lambda g, i: (0, 0)),
                 pl.BlockSpec((1, GLA_KEY_DIM), lambda g, i: (0, 0)),
                 pl.BlockSpec((1, GLA_DV), lambda g, i: (0, 0))]
    args = [h] * (5 * par) + [wup, bgk, gn]
    state_spec = pl.BlockSpec((par * nb, GLA_HEADS, GLA_DK, GLA_DV), lambda g, i: (g, 0, 0, 0))
    if s0 is not None:
        in_specs.append(state_spec)
        args.append(s0)
    kern = functools.partial(_gla_kernel, par=par, nb=nb, c=c, sub=sub, has_s0=s0 is not None)
    return pl.pallas_call(
        kern,
        grid=(n_blocks // par, n_steps),
        in_specs=in_specs,
        out_specs=[pl.BlockSpec((par, rows, GLA_VAL_DIM), lambda g, i: (g, i, 0)), state_spec],
        out_shape=[jax.ShapeDtypeStruct((n_blocks, nb * seq_len, GLA_VAL_DIM), BF16),
                   jax.ShapeDtypeStruct((n_seq, GLA_HEADS, GLA_DK, GLA_DV), F32)],
        scratch_shapes=[pltpu.VMEM((nb, GLA_DK, GLA_DV), F32)] * (par * GLA_HEADS),
        compiler_params=_cparams(("parallel", "arbitrary")),
        name=name,
    )(*args)


def _alibi_slope(head):
    return 2.0 ** (-8.0 * (head + 1) / SWA_HEADS)


def _swa_softmax_pv(parts, sink):
    m = sink
    for s, _ in parts:
        m = jnp.maximum(m, jnp.max(s, axis=-1, keepdims=True))
    denom = jnp.exp(sink - m)
    acc = None
    for s, v in parts:
        p = jnp.exp(s - m)
        denom = denom + jnp.sum(p, axis=-1, keepdims=True)
        pv = jnp.dot(p.astype(BF16), v, preferred_element_type=F32)
        acc = pv if acc is None else acc + pv
    return acc / denom


def _swa_prompt_kernel(sink_ref, q_ref, kp_ref, vp_ref, kc_ref, vc_ref, o_ref, bias):
    i = pl.program_id(1)
    span = 2 * WINDOW
    col = lax.broadcasted_iota(I32, (WINDOW, span), 1)

    @pl.when(i == 0)
    def _():
        dist_i = lax.broadcasted_iota(I32, (WINDOW, span), 0) + WINDOW - col
        in_window = (dist_i >= 0) & (dist_i < WINDOW)
        dist = dist_i.astype(F32)
        for hh in range(SWA_HEADS):
            bias[hh] = jnp.where(in_window, -_alibi_slope(hh) * dist, -jnp.inf)

    kcat = jnp.concatenate([kp_ref[...], kc_ref[...]], axis=0)
    vcat = jnp.concatenate([vp_ref[...], vc_ref[...]], axis=0)
    no_past = (col < WINDOW) & (i == 0)
    scale = SWA_HEAD_DIM ** -0.5
    outs = []
    for hh in range(SWA_HEADS):
        g = hh // SWA_GROUP
        gs = slice(g * SWA_HEAD_DIM, (g + 1) * SWA_HEAD_DIM)
        qh = q_ref[:, hh * SWA_HEAD_DIM:(hh + 1) * SWA_HEAD_DIM] * scale
        s = lax.dot_general(qh, kcat[:, gs], (((1,), (1,)), ((), ())), preferred_element_type=F32)
        s = jnp.where(no_past, -jnp.inf, s + bias[hh])
        outs.append(_swa_softmax_pv([(s, vcat[:, gs])], sink_ref[hh]))
    o_ref[...] = jnp.concatenate(outs, axis=-1).astype(o_ref.dtype)


def _swa_prompt(h, sinks, *, n_seq, seq_len, name):
    nq = seq_len // WINDOW
    qcol = H_QB // SWA_Q_DIM
    kcol = H_KB // SWA_KV_DIM
    vcol = H_VB // SWA_KV_DIM

    def cur(col):
        return lambda b, i, sk: (b * nq + i, col)

    def prev(col):
        return lambda b, i, sk: (b * nq + jnp.maximum(i - 1, 0), col)

    return pl.pallas_call(
        _swa_prompt_kernel,
        grid_spec=pltpu.PrefetchScalarGridSpec(
            num_scalar_prefetch=1,
            grid=(n_seq, nq),
            in_specs=[pl.BlockSpec((WINDOW, SWA_Q_DIM), cur(qcol)),
                      pl.BlockSpec((WINDOW, SWA_KV_DIM), prev(kcol)),
                      pl.BlockSpec((WINDOW, SWA_KV_DIM), prev(vcol)),
                      pl.BlockSpec((WINDOW, SWA_KV_DIM), cur(kcol)),
                      pl.BlockSpec((WINDOW, SWA_KV_DIM), cur(vcol))],
            out_specs=pl.BlockSpec((WINDOW, SWA_Q_DIM), lambda b, i, sk: (b * nq + i, 0)),
            scratch_shapes=[pltpu.VMEM((SWA_HEADS, WINDOW, 2 * WINDOW), F32)],
        ),
        out_shape=jax.ShapeDtypeStruct((n_seq * seq_len, SWA_Q_DIM), BF16),
        compiler_params=_cparams(("parallel", "arbitrary")),
        name=name,
    )(sinks, h, h, h, h, h)


def _swa_sample_kernel(sink_ref, q_ref, kc_ref, vc_ref, kp_ref, vp_ref, o_ref, *, nb, lq):
    rows = SWA_HEADS * lq
    grp_rows = SWA_GROUP * lq
    hd = SWA_HEAD_DIM
    nt = (((1,), (1,)), ((), ()))
    head_of_row = lax.broadcasted_iota(I32, (rows, 1), 0) // lq
    slope = jnp.zeros((rows, 1), F32)
    sink = jnp.zeros((rows, 1), F32)
    for hh in range(SWA_HEADS):
        slope = jnp.where(head_of_row == hh, _alibi_slope(hh), slope)
        sink = jnp.where(head_of_row == hh, sink_ref[hh], sink)
    qi = lax.broadcasted_iota(I32, (rows, WINDOW), 0) % lq
    dist_p = qi + WINDOW - lax.broadcasted_iota(I32, (rows, WINDOW), 1)
    valid_p = dist_p < WINDOW
    bias_p = slope * dist_p.astype(F32)
    dist_c = lax.broadcasted_iota(I32, (rows, lq), 0) % lq - lax.broadcasted_iota(I32, (rows, lq), 1)
    valid_c = dist_c >= 0
    bias_c = slope * dist_c.astype(F32)
    scale = hd ** -0.5

    q_all = q_ref[...].astype(F32)
    kc_all = kc_ref[...].astype(F32)
    vc_all = vc_ref[...].astype(F32)
    seq_outs = []
    for j in range(nb):
        js = slice(j * lq, (j + 1) * lq)
        pieces = []
        for hh in range(SWA_HEADS):
            g = hh // SWA_GROUP
            parts = []
            if g:
                parts.append(jnp.zeros((lq, g * hd), F32))
            parts.append(q_all[js, hh * hd:(hh + 1) * hd])
            if g < SWA_KV_HEADS - 1:
                parts.append(jnp.zeros((lq, (SWA_KV_HEADS - 1 - g) * hd), F32))
            pieces.append(jnp.concatenate(parts, axis=-1))
        q_big = jnp.concatenate(pieces, axis=0).astype(BF16)
        kp = kp_ref[j].astype(BF16)
        vp = vp_ref[j].astype(BF16)
        kcj = kc_all[js, :].astype(BF16)
        vcj = vc_all[js, :].astype(BF16)
        s_p = lax.dot_general(q_big, kp, nt, preferred_element_type=F32)
        s_p = jnp.where(valid_p, s_p * scale - bias_p, -jnp.inf)
        s_c = lax.dot_general(q_big, kcj, nt, preferred_element_type=F32)
        s_c = jnp.where(valid_c, s_c * scale - bias_c, -jnp.inf)
        o_big = _swa_softmax_pv([(s_p, vp), (s_c, vcj)], sink)
        o_grp = [o_big[g * grp_rows:(g + 1) * grp_rows, g * hd:(g + 1) * hd] for g in range(SWA_KV_HEADS)]
        o_heads = jnp.concatenate(o_grp, axis=0)
        seq_outs.append(jnp.concatenate([o_heads[hh * lq:(hh + 1) * lq, :] for hh in range(SWA_HEADS)],
                                        axis=-1))
    o_ref[...] = jnp.concatenate(seq_outs, axis=0).astype(o_ref.dtype)


def _swa_sample(h, sinks, k_past, v_past, *, row0, n_seq, lq, nb, name):
    rows = nb * lq
    rb0 = row0 // rows
    qcol = H_QB // SWA_Q_DIM
    kcol = H_KB // SWA_KV_DIM
    vcol = H_VB // SWA_KV_DIM
    kern = functools.partial(_swa_sample_kernel, nb=nb, lq=lq)
    past_spec = pl.BlockSpec((nb, WINDOW, SWA_KV_DIM), lambda g, sk: (g, 0, 0))
    return pl.pallas_call(
        kern,
        grid_spec=pltpu.PrefetchScalarGridSpec(
            num_scalar_prefetch=1,
            grid=(n_seq // nb,),
            in_specs=[pl.BlockSpec((rows, SWA_Q_DIM), lambda g, sk: (rb0 + g, qcol)),
                      pl.BlockSpec((rows, SWA_KV_DIM), lambda g, sk: (rb0 + g, kcol)),
                      pl.BlockSpec((rows, SWA_KV_DIM), lambda g, sk: (rb0 + g, vcol)),
                      past_spec, past_spec],
            out_specs=pl.BlockSpec((rows, SWA_Q_DIM), lambda g, sk: (g, 0)),
        ),
        out_shape=jax.ShapeDtypeStruct((n_seq * lq, SWA_Q_DIM), BF16),
        compiler_params=_cparams(("parallel",)),
        name=name,
    )(sinks, h, h, h, k_past, v_past)


def _layer_norm(x, g, b):
    mu = jnp.mean(x, axis=-1, keepdims=True)
    xc = x - mu
    var = jnp.mean(jnp.square(xc), axis=-1, keepdims=True)
    return xc * lax.rsqrt(var + EPS) * g + b


def _pack_bf16_pairs(x, s):
    lo = lax.bitcast_convert_type(x[:, (2 * s) * LANES:(2 * s + 1) * LANES].astype(BF16).astype(F32), I32)
    hi = lax.bitcast_convert_type(x[:, (2 * s + 1) * LANES:(2 * s + 2) * LANES].astype(BF16).astype(F32), I32)
    return lax.shift_right_logical(lo, 16) | (hi & HIGH_HALF)


def _unpack_bf16_pairs(w):
    lo = lax.bitcast_convert_type(lax.shift_left(w, 16), F32).astype(BF16)
    hi = lax.bitcast_convert_type(w & HIGH_HALF, F32).astype(BF16)
    return lo, hi


def _merge_kernel(oap_ref, oas_ref, obp_ref, obs_ref, ga_ref, gb_ref, xp_ref, xs_ref,
                  wpa_ref, wpb_ref, wout_ref, g_ref, b_ref, o_ref, xw_ref, *, np_tiles):
    i = pl.program_id(0)
    tm = o_ref.shape[0]
    br_a = jnp.dot(_pair_value(i, np_tiles, oap_ref, oas_ref), wpa_ref[...], preferred_element_type=F32)
    br_b = jnp.dot(_pair_value(i, np_tiles, obp_ref, obs_ref), wpb_ref[...], preferred_element_type=F32)
    merged = (jax.nn.sigmoid(ga_ref[...].astype(F32)) * br_a
              + jax.nn.sigmoid(gb_ref[...].astype(F32)) * br_b)
    mix = jnp.dot(merged.astype(BF16), wout_ref[...], preferred_element_type=F32)
    x = _pair_value(i, np_tiles, xp_ref, xs_ref)
    x1 = _layer_norm(DEEPNORM_ALPHA * x + mix, g_ref[...], b_ref[...])
    o_ref[...] = x1
    for s in range(PACKED_ROWS):
        xw_ref[pl.ds(s, tm, stride=PACKED_ROWS), :] = _pack_bf16_pairs(x1, s)


def _merge(oa, ob, h, x, wpa, wpb, wout, g, b, *, tm, name):
    t = h.shape[0]
    np_tiles = x[0].shape[0] // tm
    p_map, s_map = _pair_maps(np_tiles)
    row = lambda i: (i, 0)
    full = lambda i: (0, 0)
    pair = [pl.BlockSpec((tm, D_MODEL), p_map), pl.BlockSpec((tm, D_MODEL), s_map)]
    return pl.pallas_call(
        functools.partial(_merge_kernel, np_tiles=np_tiles),
        grid=(t // tm,),
        in_specs=pair + pair + [
                  pl.BlockSpec((tm, D_MODEL), lambda i: (i, H_GATE_A // D_MODEL)),
                  pl.BlockSpec((tm, D_MODEL), lambda i: (i, H_GATE_B // D_MODEL))] + pair + [
                  pl.BlockSpec((D_MODEL, D_MODEL), full),
                  pl.BlockSpec((D_MODEL, D_MODEL), full),
                  pl.BlockSpec((D_MODEL, D_MODEL), full),
                  pl.BlockSpec((1, D_MODEL), full),
                  pl.BlockSpec((1, D_MODEL), full)],
        out_specs=[pl.BlockSpec((tm, D_MODEL), row),
                   pl.BlockSpec((tm * PACKED_ROWS, LANES), row)],
        out_shape=[jax.ShapeDtypeStruct((t, D_MODEL), F32),
                   jax.ShapeDtypeStruct((t * PACKED_ROWS, LANES), I32)],
        compiler_params=_cparams(("parallel",)),
        name=name,
    )(*oa, *ob, h, h, *x, wpa, wpb, wout, g, b)


def _router_kernel(x_ref, whi_ref, wlo_ref, bias_ref, idx_ref, wl_ref, rank_ref, cntc_ref, cntr_ref,
                   cntc_scr, cntr_scr):
    i = pl.program_id(0)
    tm = x_ref.shape[0]

    @pl.when(i == 0)
    def _():
        cntc_scr[...] = jnp.zeros_like(cntc_scr)
        cntr_scr[...] = jnp.zeros_like(cntr_scr)

    x_hi, x_lo = _split_bf16(x_ref[...])
    nt = (((1,), (1,)), ((), ()))
    logits = (lax.dot_general(whi_ref[...], x_hi, nt, preferred_element_type=F32)
              + lax.dot_general(whi_ref[...], x_lo, nt, preferred_element_type=F32)
              + lax.dot_general(wlo_ref[...], x_hi, nt, preferred_element_type=F32))
    scores = jax.nn.sigmoid(logits)
    biased = scores + bias_ref[...]

    grouped = biased.reshape(N_GROUPS, GROUP_SIZE, tm)
    m1 = jnp.max(grouped, axis=1)
    n_top = jnp.sum(jnp.where(grouped == m1[:, None, :], 1.0, 0.0), axis=1)
    m2 = jnp.max(jnp.where(grouped < m1[:, None, :], grouped, -jnp.inf), axis=1)
    gscore = m1 + jnp.where(n_top >= 2.0, m1, m2)

    g_iota = lax.broadcasted_iota(I32, (N_GROUPS, tm), 0)
    beaten = jnp.zeros((N_GROUPS, tm), I32)
    for g in range(N_GROUPS):
        other = gscore[g:g + 1, :]
        ahead = (other > gscore) | ((other == gscore) & (g < g_iota))
        beaten = beaten + jnp.where(ahead, 1, 0)
    keep = jnp.where(beaten < TOPK_GROUPS, 1.0, 0.0)
    masked = jnp.where(keep[:, None, :] > 0.5, grouped, -jnp.inf).reshape(N_EXPERTS, tm)

    e_iota = lax.broadcasted_iota(I32, (N_EXPERTS, tm), 0)
    sel_f = jnp.zeros((N_EXPERTS, tm), F32)
    ids = []
    for _ in range(TOP_K):
        best = jnp.max(masked, axis=0, keepdims=True)
        idx = jnp.min(jnp.where(masked == best, e_iota, N_EXPERTS), axis=0, keepdims=True)
        hit = e_iota == idx
        sel_f = sel_f + jnp.where(hit, 1.0, 0.0)
        masked = jnp.where(hit, -jnp.inf, masked)
        ids.append(idx)

    top_sum = jnp.sum(sel_f * scores, axis=0, keepdims=True)

    t_r = lax.broadcasted_iota(I32, (tm, tm), 0)
    t_c = lax.broadcasted_iota(I32, (tm, tm), 1)
    before = jnp.where(t_r < t_c, 1.0, 0.0).astype(BF16)
    sel_b = sel_f.astype(BF16)
    rank = jnp.dot(sel_b, before, preferred_element_type=F32) + cntc_scr[...]
    cntc_scr[...] = cntc_scr[...] + jnp.sum(sel_f, axis=1, keepdims=True)
    cntr_scr[...] = cntr_scr[...] + lax.dot_general(jnp.ones((SUBLANES, tm), BF16), sel_b, nt,
                                                    preferred_element_type=F32)

    idx_rows, w_rows, rank_rows = [], [], []
    for idx in ids:
        hit = e_iota == idx
        w = jnp.sum(jnp.where(hit, scores, 0.0), axis=0, keepdims=True)
        w_rows.append(w / top_sum * ROUTED_SCALE)
        rank_rows.append(jnp.sum(jnp.where(hit, rank, 0.0), axis=0, keepdims=True))
        idx_rows.append(idx)
    idx_ref[...] = jnp.concatenate(idx_rows, axis=0)
    rank_ref[...] = jnp.concatenate(rank_rows, axis=0).astype(I32)

    w_all = jnp.concatenate(w_rows, axis=0)
    w_1 = w_all.astype(BF16)
    r_1 = w_all - w_1.astype(F32)
    w_2 = r_1.astype(BF16)
    w_3 = (r_1 - w_2.astype(F32)).astype(BF16)
    terms = jnp.concatenate([w_1, w_2, w_3, jnp.zeros_like(w_1)], axis=0)
    t_row = lax.broadcasted_iota(I32, (4 * TOP_K, TOP_K * LANES), 0)
    t_col = lax.broadcasted_iota(I32, (4 * TOP_K, TOP_K * LANES), 1)
    spread = jnp.where((t_row % TOP_K) == (t_col // LANES), 1.0, 0.0).astype(BF16)
    w_lanes = lax.dot_general(terms, spread, (((0,), (0,)), ((), ())), preferred_element_type=F32)
    for k in range(TOP_K):
        wl_ref[pl.ds(k, tm, stride=TOP_K), :] = w_lanes[:, k * LANES:(k + 1) * LANES]

    @pl.when(i == pl.num_programs(0) - 1)
    def _():
        cntc_ref[...] = cntc_scr[...].astype(I32)
        cntr_ref[...] = cntr_scr[...].astype(I32)


def _router(x1, w_hi, w_lo, bias, *, tm, name):
    t = x1.shape[0]
    full = lambda i: (0, 0)
    tok = lambda i: (0, i)
    return pl.pallas_call(
        _router_kernel,
        grid=(t // tm,),
        in_specs=[pl.BlockSpec((tm, D_MODEL), lambda i: (i, 0)),
                  pl.BlockSpec((N_EXPERTS, D_MODEL), full),
                  pl.BlockSpec((N_EXPERTS, D_MODEL), full),
                  pl.BlockSpec((N_EXPERTS, 1), full)],
        out_specs=[pl.BlockSpec((TOP_K, tm), tok),
                   pl.BlockSpec((TOP_K * tm, LANES), lambda i: (i, 0)),
                   pl.BlockSpec((TOP_K, tm), tok),
                   pl.BlockSpec((N_EXPERTS, 1), full),
                   pl.BlockSpec((SUBLANES, N_EXPERTS), full)],
        out_shape=[jax.ShapeDtypeStruct((TOP_K, t), I32),
                   jax.ShapeDtypeStruct((TOP_K * t, LANES), F32),
                   jax.ShapeDtypeStruct((TOP_K, t), I32),
                   jax.ShapeDtypeStruct((N_EXPERTS, 1), I32),
                   jax.ShapeDtypeStruct((SUBLANES, N_EXPERTS), I32)],
        scratch_shapes=[pltpu.VMEM((N_EXPERTS, 1), F32),
                        pltpu.VMEM((SUBLANES, N_EXPERTS), F32)],
        compiler_params=_cparams(("arbitrary",)),
        name=name,
    )(x1, w_hi, w_lo, bias)


ITEM_FIELDS = 8


def _byte_split(v):
    return lax.shift_right_logical(v, 8).astype(F32), (v & 255).astype(F32)


def _plan_kernel(cntc_ref, cntr_ref, idx_ref, rank_ref, meta_ref, items_ref, m_scr, *, n_tok):
    i = pl.program_id(0)
    tm = idx_ref.shape[1]
    nbm = m_scr.shape[0]
    nip = items_ref.shape[1]
    nbp = nip - N_EXPERTS
    n_rows = n_tok * TOP_K
    n_blk = n_rows // EXPERT_ROWS
    nt = (((1,), (1,)), ((), ()))

    r_i = lax.broadcasted_iota(I32, (N_EXPERTS, N_EXPERTS), 0)
    c_i = lax.broadcasted_iota(I32, (N_EXPERTS, N_EXPERTS), 1)
    below = jnp.where(c_i < r_i, 1.0, 0.0).astype(BF16)
    ones_c = jnp.ones((N_EXPERTS, LANES), F32)
    c_hi, c_lo = _byte_split(cntc_ref[...])
    start_col = (256.0 * jnp.dot(below, (c_hi * ones_c).astype(BF16), preferred_element_type=F32)
                 + jnp.dot(below, (c_lo * ones_c).astype(BF16), preferred_element_type=F32))[:, 0:1]

    @pl.when(i == 0)
    def _():
        m_scr[...] = jnp.zeros_like(m_scr)
        above = jnp.where(r_i < c_i, 1.0, 0.0).astype(BF16)
        r_hi, r_lo = _byte_split(cntr_ref[...])
        start_row = (256.0 * jnp.dot(r_hi.astype(BF16), above, preferred_element_type=F32)
                     + jnp.dot(r_lo.astype(BF16), above, preferred_element_type=F32))[0:1, :]
        b_col = lax.broadcasted_iota(I32, (nbp, 1), 0)
        b_row = lax.broadcasted_iota(I32, (1, nbp), 1)
        blk_col = jnp.where(b_col < n_blk, b_col * EXPERT_ROWS, n_rows).astype(F32)
        blk_row = jnp.where(b_row < n_blk, b_row * EXPERT_ROWS, n_rows).astype(F32)
        v_col = jnp.concatenate([blk_col, start_col], axis=0)
        v_row = jnp.concatenate([blk_row, start_row], axis=1)
        j_col = lax.broadcasted_iota(I32, (nip, 1), 0)
        k_row = lax.broadcasted_iota(I32, (1, nip), 1)
        ahead = (v_row < v_col) | ((v_row == v_col) & (k_row < j_col))
        order_col = jnp.sum(jnp.where(ahead, 1.0, 0.0), axis=1, keepdims=True)
        pos_row = k_row.astype(F32)
        lo_abs = jnp.sum(jnp.where(order_col == pos_row, v_col, 0.0), axis=0, keepdims=True)
        hi_abs = jnp.sum(jnp.where(order_col == pos_row + 1.0, v_col, 0.0), axis=0, keepdims=True)
        hi_abs = jnp.where(k_row == nip - 1, float(n_rows), hi_abs)
        blk = jnp.minimum(jnp.floor(lo_abs * (1.0 / EXPERT_ROWS)), n_blk - 1.0)
        expert = jnp.sum(jnp.where(start_col <= lo_abs, 1.0, 0.0), axis=0, keepdims=True) - 1.0
        base = blk * EXPERT_ROWS
        e_col = lax.broadcasted_iota(I32, (N_EXPERTS, 1), 0).astype(F32)
        end_col = start_col + cntc_ref[...].astype(F32)
        seg_end = jnp.sum(jnp.where(e_col == expert, end_col, 0.0), axis=0, keepdims=True)
        follower = jnp.sum(jnp.where(start_col <= seg_end, 1.0, 0.0), axis=0, keepdims=True) - 1.0
        follower = jnp.where(seg_end < float(n_rows), follower, -1.0)
        fields = [blk, expert, lo_abs - base, hi_abs - base, follower]
        fields.append(jnp.zeros((ITEM_FIELDS - len(fields), nip), F32))
        items_ref[...] = jnp.concatenate(fields, axis=0).astype(I32)

    e_iota = lax.broadcasted_iota(I32, (N_EXPERTS, tm), 0)
    b_iota = lax.broadcasted_iota(I32, (nbm, tm), 0)
    l_iota = lax.broadcasted_iota(I32, (EXPERT_ROWS, tm), 0)
    tok = i * tm + lax.broadcasted_iota(I32, (1, tm), 1)
    tok_hi, tok_lo = _byte_split(tok)
    acc = jnp.zeros(m_scr.shape, F32)
    for k in range(TOP_K):
        hit = e_iota == idx_ref[k:k + 1, :]
        dest = (jnp.sum(jnp.where(hit, start_col, 0.0), axis=0, keepdims=True).astype(I32)
                + rank_ref[k:k + 1, :])
        oh_blk = jnp.where(b_iota == lax.shift_right_logical(dest, 8), 1.0, 0.0).astype(BF16)
        in_blk = l_iota == (dest & (EXPERT_ROWS - 1))
        vals = jnp.concatenate([jnp.where(in_blk, tok_hi, 0.0), jnp.where(in_blk, tok_lo, 0.0),
                                jnp.where(in_blk, float(k), 0.0)], axis=0).astype(BF16)
        acc = acc + lax.dot_general(oh_blk, vals, nt, preferred_element_type=F32)
    m_scr[...] = m_scr[...] + acc

    @pl.when(i == pl.num_programs(0) - 1)
    def _():
        m = m_scr[...]
        row_tok = m[:, 0:EXPERT_ROWS] * 256.0 + m[:, EXPERT_ROWS:2 * EXPERT_ROWS]
        row_slot = m[:, 2 * EXPERT_ROWS:3 * EXPERT_ROWS] * float(n_tok) + row_tok
        meta_ref[...] = jnp.concatenate([row_tok * float(PACKED_ROWS), row_slot * float(ROW_CHUNKS)],
                                        axis=1).astype(I32)


def _plan(cnt_col, cnt_row, idx, rank, *, tm, name):
    t = idx.shape[1]
    assert EXPERT_ROWS == 256 and (t * TOP_K) % (2 * EXPERT_ROWS) == 0
    n_blk = t * TOP_K // EXPERT_ROWS
    nbm = -(-n_blk // 16) * 16
    nip = -(-n_blk // LANES) * LANES + N_EXPERTS
    tok = lambda i: (0, i)
    full = lambda i: (0, 0)
    return pl.pallas_call(
        functools.partial(_plan_kernel, n_tok=t),
        grid=(t // tm,),
        in_specs=[pl.BlockSpec((N_EXPERTS, 1), full),
                  pl.BlockSpec((SUBLANES, N_EXPERTS), full),
                  pl.BlockSpec((TOP_K, tm), tok),
                  pl.BlockSpec((TOP_K, tm), tok)],
        out_specs=[pl.BlockSpec((nbm, 2 * EXPERT_ROWS), full),
                   pl.BlockSpec((ITEM_FIELDS, nip), full)],
        out_shape=[jax.ShapeDtypeStruct((nbm, 2 * EXPERT_ROWS), I32),
                   jax.ShapeDtypeStruct((ITEM_FIELDS, nip), I32)],
        scratch_shapes=[pltpu.VMEM((nbm, 3 * EXPERT_ROWS), F32)],
        compiler_params=_cparams(("arbitrary",)),
        name=name,
    )(cnt_col, cnt_row, idx, rank)


META_CHUNK = 4 * EXPERT_ROWS
GATHER_STRIDE = EXPERT_ROWS + 8
META_RING = 3
FFN_PIECES = 2 + D_MODEL // EXPERT_FF


def _experts_kernel(items_ref, xw_hbm, meta_hbm, wg_hbm, wu_hbm, wd_hbm, ys_hbm,
                    xw, meta, tile, lhs, wg_f, wu_f, wd_f, wg_b, wu_b, wd_b, yacc, ybuf_0, ybuf_1, state,
                    sem_x, sem_m, sem_y, sem_w, *, nip, n_blk):
    i = pl.program_id(0)
    ybuf = (ybuf_0, ybuf_1)
    blk = items_ref[i]
    expert = items_ref[nip + i]
    lo = items_ref[2 * nip + i]
    hi = items_ref[3 * nip + i]
    follower = items_ref[4 * nip + i]
    nonempty = hi > lo
    par = blk & 1
    chunk = lax.shift_right_logical(blk, 1)

    def meta_base(b):
        return (lax.shift_right_logical(b, 1) % META_RING) * META_CHUNK + (b & 1) * (2 * EXPERT_ROWS)

    mbase = meta_base(blk)
    prev_base = meta_base(jnp.maximum(blk - 1, 0))

    def meta_copy(c):
        slot = c % META_RING
        return pltpu.make_async_copy(meta_hbm.at[pl.ds(pl.multiple_of(c * META_CHUNK, META_CHUNK), META_CHUNK)],
                                     meta.at[pl.ds(pl.multiple_of(slot * META_CHUNK, META_CHUNK), META_CHUNK)],
                                     sem_m.at[slot])

    def ybuf_drain(slot):
        pltpu.make_async_copy(ys_hbm.at[pl.ds(0, EXPERT_ROWS * ROW_CHUNKS), :], ybuf[slot],
                              sem_y.at[slot]).wait()

    def ybuf_fill(slot):
        for j in range(ROW_CHUNKS):
            ybuf[slot][pl.ds(j, EXPERT_ROWS, stride=ROW_CHUNKS), :] = yacc[:, j * LANES:(j + 1) * LANES]

    def send_rows(slot, base, r0, r1):
        for r in range(r0, r1):
            src = ybuf[slot].at[pl.ds(r * ROW_CHUNKS, ROW_CHUNKS), :]
            dst_row = pl.multiple_of(meta[base + EXPERT_ROWS + r], ROW_CHUNKS)
            pltpu.make_async_copy(src, ys_hbm.at[pl.ds(dst_row, ROW_CHUNKS), :],
                                  sem_y.at[slot]).start(priority=r % 2)

    def weight_copies(e, slot):
        return [pltpu.make_async_copy(src.at[e], dst.at[slot], sem_w.at[slot])
                for src, dst in ((wg_hbm, wg_f), (wu_hbm, wu_f), (wd_hbm, wd_f))]

    @pl.when(i == 0)
    def _():
        resident = pltpu.make_async_copy(xw_hbm, xw, sem_x)
        resident.start()
        meta_copy(0).start()
        for cp in weight_copies(expert, 0):
            cp.start()
        state[0] = -1
        state[1] = 1
        resident.wait()

    @pl.when(nonempty & (expert != state[0]))
    def _():
        slot = 1 - state[1]
        for cp in weight_copies(expert, slot):
            cp.wait()
        state[0] = expert
        state[1] = slot

        @pl.when(follower >= 0)
        def _():
            for cp in weight_copies(follower, 1 - slot):
                cp.start()

        wg_b[...] = wg_f[slot].astype(BF16)
        wu_b[...] = wu_f[slot].astype(BF16)
        wd_b[...] = wd_f[slot].astype(BF16)

    @pl.when(nonempty & (lo == 0))
    def _():
        @pl.when(par == 0)
        def _():
            meta_copy(chunk).wait()

            @pl.when(2 * (chunk + 1) < n_blk)
            def _():
                meta_copy(chunk + 1).start()

        for r in range(EXPERT_ROWS):
            t4 = pl.multiple_of(meta[mbase + r], PACKED_ROWS)
            tile[pl.ds(r, PACKED_ROWS, stride=GATHER_STRIDE), :] = xw[pl.ds(t4, PACKED_ROWS), :]
        cols = []
        for s in range(PACKED_ROWS):
            cols.extend(_unpack_bf16_pairs(tile[pl.ds(s * GATHER_STRIDE, EXPERT_ROWS), :]))
        lhs[...] = jnp.concatenate(cols, axis=-1)

    def ffn(x, between=None, to_slot=None):
        done = [0]

        def piece_done():
            if between is not None:
                between(done[0])
            done[0] += 1

        gate = jnp.dot(x, wg_b[...], preferred_element_type=F32)
        piece_done()
        up = jnp.dot(x, wu_b[...], preferred_element_type=F32)
        piece_done()
        hid = (jax.nn.silu(gate) * up).astype(BF16)
        out = []
        for c in range(D_MODEL // EXPERT_FF):
            y_c = jnp.dot(hid, wd_b[:, c * EXPERT_FF:(c + 1) * EXPERT_FF], preferred_element_type=F32)
            if to_slot is None:
                out.append(y_c)
            else:
                for jj in range(EXPERT_FF // LANES):
                    j = c * (EXPERT_FF // LANES) + jj
                    ybuf[to_slot][pl.ds(j, EXPERT_ROWS, stride=ROW_CHUNKS), :] = y_c[:, jj * LANES:(jj + 1) * LANES]
            piece_done()
        assert done[0] == FFN_PIECES
        return jnp.concatenate(out, axis=-1) if to_slot is None else None

    def run_mask(y):
        row = lax.broadcasted_iota(I32, (EXPERT_ROWS, 1), 0)
        return jnp.where((row >= lo) & (row < hi), y, 0.0)

    whole = (lo == 0) & (hi == EXPERT_ROWS)
    first = nonempty & (lo == 0)
    has_prev = blk > 0

    for prev_slot in range(2):
        def send_piece(g, prev_slot=prev_slot):
            send_rows(prev_slot, prev_base, g * EXPERT_ROWS // FFN_PIECES, (g + 1) * EXPERT_ROWS // FFN_PIECES)

        sends = has_prev & (par == 1 - prev_slot)

        @pl.when(whole & sends)
        def _():
            @pl.when(blk >= 2)
            def _():
                ybuf_drain(1 - prev_slot)

            ffn(lhs[...], send_piece, to_slot=1 - prev_slot)

        @pl.when(first & jnp.logical_not(whole) & sends)
        def _():
            yacc[...] = run_mask(ffn(lhs[...], send_piece))

    @pl.when(whole & jnp.logical_not(has_prev))
    def _():
        ffn(lhs[...], to_slot=0)

    @pl.when(nonempty & jnp.logical_not(whole) & jnp.logical_not(first & has_prev))
    def _():
        y = run_mask(ffn(lhs[...]))

        @pl.when(lo == 0)
        def _():
            yacc[...] = y

        @pl.when(lo > 0)
        def _():
            yacc[...] = yacc[...] + y

    @pl.when(nonempty & (hi == EXPERT_ROWS))
    def _():
        @pl.when(jnp.logical_not(whole))
        def _():
            for slot in range(2):
                @pl.when(par == slot)
                def _():
                    @pl.when(blk >= 2)
                    def _():
                        ybuf_drain(slot)

                    ybuf_fill(slot)

        @pl.when(blk == n_blk - 1)
        def _():
            for slot in range(2):
                @pl.when(par == slot)
                def _():
                    send_rows(slot, mbase, 0, EXPERT_ROWS)

    @pl.when(i == pl.num_programs(0) - 1)
    def _():
        ybuf_drain(0)
        ybuf_drain(1)


def _experts(items, xw, meta, wg, wu, wd, *, n_tok, name):
    nip = items.shape[0] // ITEM_FIELDS
    n_rows = n_tok * TOP_K
    n_blk = n_rows // EXPERT_ROWS
    n_items = n_blk + N_EXPERTS
    assert n_blk >= 2 and n_blk % 2 == 0
    kern = functools.partial(_experts_kernel, nip=nip, n_blk=n_blk)
    any_spec = pl.BlockSpec(memory_space=pl.ANY)
    return pl.pallas_call(
        kern,
        grid_spec=pltpu.PrefetchScalarGridSpec(
            num_scalar_prefetch=1,
            grid=(n_items,),
            in_specs=[any_spec] * 5,
            out_specs=any_spec,
            scratch_shapes=[pltpu.VMEM(xw.shape, I32),
                            pltpu.SMEM((META_RING * META_CHUNK,), I32),
                            pltpu.VMEM((PACKED_ROWS * GATHER_STRIDE, LANES), I32),
                            pltpu.VMEM((EXPERT_ROWS, D_MODEL), BF16),
                            pltpu.VMEM((2, D_MODEL, EXPERT_FF), F32),
                            pltpu.VMEM((2, D_MODEL, EXPERT_FF), F32),
                            pltpu.VMEM((2, EXPERT_FF, D_MODEL), F32),
                            pltpu.VMEM((D_MODEL, EXPERT_FF), BF16),
                            pltpu.VMEM((D_MODEL, EXPERT_FF), BF16),
                            pltpu.VMEM((EXPERT_FF, D_MODEL), BF16),
                            pltpu.VMEM((EXPERT_ROWS, D_MODEL), F32),
                            pltpu.VMEM((EXPERT_ROWS * ROW_CHUNKS, LANES), F32),
                            pltpu.VMEM((EXPERT_ROWS * ROW_CHUNKS, LANES), F32),
                            pltpu.SMEM((2,), I32),
                            pltpu.SemaphoreType.DMA,
                            pltpu.SemaphoreType.DMA((META_RING,)),
                            pltpu.SemaphoreType.DMA((2,)),
                            pltpu.SemaphoreType.DMA((2,))],
        ),
        out_shape=jax.ShapeDtypeStruct((n_rows * ROW_CHUNKS, LANES), F32),
        compiler_params=_cparams(("arbitrary",)),
        name=name,
    )(items, xw, meta, wg, wu, wd)


def _combine_kernel(ys_ref, x_ref, wl_ref, wsg_ref, wsu_ref, wsd_ref, g_ref, b_ref, op_ref, os_ref,
                    routed, *, np_tiles):
    x = x_ref[...]
    xb = x.astype(BF16)
    hid = (jax.nn.silu(jnp.dot(xb, wsg_ref[...], preferred_element_type=F32))
           * jnp.dot(xb, wsu_ref[...], preferred_element_type=F32))
    shared = jnp.dot(hid.astype(BF16), wsd_ref[...], preferred_element_type=F32)
    acc = None
    for k in range(TOP_K):
        part = wl_ref[:, k:k + 1, :] * ys_ref[k]
        acc = part if acc is None else acc + part
    routed[...] = acc
    ffn = jnp.concatenate([routed[:, j, :] for j in range(ROW_CHUNKS)], axis=-1) + shared
    out = _layer_norm(DEEPNORM_ALPHA * x + ffn, g_ref[...], b_ref[...])
    i = pl.program_id(0)

    @pl.when(i < np_tiles)
    def _():
        op_ref[...] = out

    @pl.when(i >= np_tiles)
    def _():
        os_ref[...] = out


def _combine(ys, x1, w_tok, wsg, wsu, wsd, g, b, *, n_prompt, tm, name):
    t = x1.shape[0]
    np_tiles = n_prompt // tm
    p_map, s_map = _pair_maps(np_tiles)
    row = lambda i: (i, 0)
    full = lambda i: (0, 0)
    return pl.pallas_call(
        functools.partial(_combine_kernel, np_tiles=np_tiles),
        grid=(t // tm,),
        in_specs=[pl.BlockSpec((TOP_K, tm, ROW_CHUNKS, LANES), lambda i: (0, i, 0, 0)),
                  pl.BlockSpec((tm, D_MODEL), row),
                  pl.BlockSpec((tm, TOP_K, LANES), lambda i: (i, 0, 0)),
                  pl.BlockSpec((D_MODEL, SHARED_FF), full),
                  pl.BlockSpec((D_MODEL, SHARED_FF), full),
                  pl.BlockSpec((SHARED_FF, D_MODEL), full),
                  pl.BlockSpec((1, D_MODEL), full),
                  pl.BlockSpec((1, D_MODEL), full)],
        out_specs=[pl.BlockSpec((tm, D_MODEL), p_map), pl.BlockSpec((tm, D_MODEL), s_map)],
        out_shape=[jax.ShapeDtypeStruct((n_prompt, D_MODEL), F32),
                   jax.ShapeDtypeStruct((t - n_prompt, D_MODEL), F32)],
        scratch_shapes=[pltpu.VMEM((tm, ROW_CHUNKS, LANES), F32)],
        compiler_params=_cparams(("arbitrary",)),
        name=name,
    )(ys, x1, w_tok, wsg, wsu, wsd, g, b)


def _pack_w_in(w_in):
    sizes = (GLA_KEY_DIM, GLA_KEY_DIM, GLA_VAL_DIM, GLA_VAL_DIM, GLA_GATE_RANK,
             SWA_Q_DIM, SWA_KV_DIM, SWA_KV_DIM, D_MODEL, D_MODEL)
    offs = [0]
    for s in sizes:
        offs.append(offs[-1] + s)
    qa, ka, va, ga, gk, qb, kb, vb, gate_a, gate_b = (w_in[:, offs[i]:offs[i + 1]] for i in range(10))
    pad = lambda w, n: jnp.pad(w, ((0, 0), (0, n - w.shape[1])))
    packed = jnp.concatenate([qa, ka, va, ga, qb, gate_a, gate_b, kb, vb, pad(gk, 2 * LANES)], axis=1)
    assert packed.shape[1] == H_WIDTH
    return packed.astype(BF16), jnp.concatenate([kb, vb], axis=1).astype(BF16)


def kernel(x_prompt, x_sample, state_gla, cache_swa_k, cache_swa_v, w_in, w_gk_up, b_gk, gla_norm_g,
           attn_sinks, w_proj_a, w_proj_b, w_out, ln1_g, ln1_b, w_router, router_bias,
           w_expert_gate, w_expert_up, w_expert_down, w_shared_gate, w_shared_up, w_shared_down,
           ln2_g, ln2_b):
    assert w_in.shape[0] == 1, "single-layer trunk"
    bp, lp, d = x_prompt.shape
    bs, ls, _ = x_sample.shape
    assert d == D_MODEL and ls == SUBLANES and cache_swa_k.shape[2] == WINDOW
    tp, ts = bp * lp, bs * ls
    t = tp + ts

    xp = x_prompt.reshape(tp, d)
    xs = x_sample.reshape(ts, d)
    w_main, w_kv = _pack_w_in(w_in[0])
    h = _matmul(xp, xs, w_main, BF16, _pair_tile(tp, ts, 1024), H_TN, "proj_in")

    xp_tail = x_prompt[:, lp - WINDOW:].reshape(bp * WINDOW, d)
    kv_tail = _matmul(xp_tail, xs, w_kv, F32, _pair_tile(bp * WINDOW, ts, 512), 2 * SWA_KV_DIM,
                      "proj_kv_tail")

    wup = jnp.pad(w_gk_up[0], ((0, LANES - GLA_GATE_RANK), (0, 0))).astype(BF16)
    bgk = b_gk[0].reshape(1, GLA_KEY_DIM)
    gn = gla_norm_g[0].reshape(1, GLA_DV)
    oa_p, s_prompt = _gla(h, wup, bgk, gn, None, row0=0, n_seq=bp, seq_len=lp, par=math.gcd(bp, 4), nb=1,
                          c=GLA_CHUNK, sub=2, name="gla_prompt")
    oa_s, s_sample = _gla(h, wup, bgk, gn, state_gla[0], row0=tp, n_seq=bs, seq_len=ls, par=1, nb=8,
                          c=math.gcd(ls, GLA_CHUNK), sub=1, name="gla_sample")
    oa_p = oa_p.reshape(tp, GLA_VAL_DIM)
    oa_s = oa_s.reshape(ts, GLA_VAL_DIM)

    sinks = attn_sinks[0]
    k_past = cache_swa_k[0].reshape(bs, WINDOW, SWA_KV_DIM)
    v_past = cache_swa_v[0].reshape(bs, WINDOW, SWA_KV_DIM)
    ob_p = _swa_prompt(h, sinks, n_seq=bp, seq_len=lp, name="swa_prompt")
    ob_s = _swa_sample(h, sinks, k_past, v_past, row0=tp, n_seq=bs, lq=ls, nb=8, name="swa_sample")

    x1, xw = _merge((oa_p, oa_s), (ob_p, ob_s), h, (xp, xs), w_proj_a[0].astype(BF16),
                    w_proj_b[0].astype(BF16), w_out[0].astype(BF16), ln1_g[0].reshape(1, d),
                    ln1_b[0].reshape(1, d), tm=_pair_tile(tp, ts, 512), name="merge_ln1")

    wr_t = w_router[0].T
    wr_hi = wr_t.astype(BF16)
    wr_lo = (wr_t - wr_hi.astype(F32)).astype(BF16)
    tm_r = _tile(t, 512)
    idx, w_lanes, rank, cnt_col, cnt_row = _router(x1, wr_hi, wr_lo, router_bias[0].reshape(N_EXPERTS, 1),
                                                   tm=tm_r, name="router")
    meta, items = _plan(cnt_col, cnt_row, idx, rank, tm=tm_r, name="plan")
    ys = _experts(items.reshape(-1), xw, meta.reshape(-1), w_expert_gate[0], w_expert_up[0],
                  w_expert_down[0], n_tok=t, name="experts")
    y_p, y_s = _combine(ys.reshape(TOP_K, t, ROW_CHUNKS, LANES), x1, w_lanes.reshape(t, TOP_K, LANES),
                        w_shared_gate[0].astype(BF16), w_shared_up[0].astype(BF16),
                        w_shared_down[0].astype(BF16), ln2_g[0].reshape(1, d), ln2_b[0].reshape(1, d),
                        n_prompt=tp, tm=_pair_tile(tp, ts, 256), name="combine_ln2")

    y_prompt = y_p.reshape(bp, lp, d)
    y_sample = y_s.reshape(bs, ls, d)
    k_tail = kv_tail[:, :SWA_KV_DIM]
    v_tail = kv_tail[:, SWA_KV_DIM:]
    kv_shape = (SWA_KV_HEADS, SWA_HEAD_DIM)
    k_prompt = k_tail[:bp * WINDOW].reshape(1, bp, WINDOW, *kv_shape)
    v_prompt = v_tail[:bp * WINDOW].reshape(1, bp, WINDOW, *kv_shape)
    k_new = k_tail[bp * WINDOW:].reshape(bs, ls, *kv_shape)
    v_new = v_tail[bp * WINDOW:].reshape(bs, ls, *kv_shape)
    k_sample = jnp.concatenate([cache_swa_k[0][:, ls:], k_new], axis=1)[None]
    v_sample = jnp.concatenate([cache_swa_v[0][:, ls:], v_new], axis=1)[None]
    return (y_prompt, y_sample, s_prompt[None], s_sample[None], k_prompt, v_prompt, k_sample, v_sample)
```

```python
import functools
import math

import jax
import jax.numpy as jnp
from jax import lax
from jax.experimental import pallas as pl
from jax.experimental.pallas import tpu as pltpu

F32 = jnp.float32
BF16 = jnp.bfloat16
I32 = jnp.int32

D_MODEL = 1024
GLA_HEADS = 4
GLA_DK = 128
GLA_DV = 256
GLA_KEY_DIM = GLA_HEADS * GLA_DK
GLA_VAL_DIM = GLA_HEADS * GLA_DV
GLA_GATE_RANK = 16
GLA_GATE_NORMALIZER = 16.0
GLA_CHUNK = 64
SWA_HEADS = 16
SWA_KV_HEADS = 4
SWA_GROUP = SWA_HEADS // SWA_KV_HEADS
SWA_HEAD_DIM = 64
SWA_Q_DIM = SWA_HEADS * SWA_HEAD_DIM
SWA_KV_DIM = SWA_KV_HEADS * SWA_HEAD_DIM
WINDOW = 128
N_EXPERTS = 256
TOP_K = 8
N_GROUPS = 8
GROUP_SIZE = N_EXPERTS // N_GROUPS
TOPK_GROUPS = 4
EXPERT_FF = 256
SHARED_FF = 256
ROUTED_SCALE = 2.5
DEEPNORM_ALPHA = 2.0 ** 0.25
EPS = 1e-5

LANES = 128
SUBLANES = 8
ROW_CHUNKS = D_MODEL // LANES
VMEM_LIMIT = 56 * 1024 * 1024

H_QA, H_KA, H_VA, H_GA, H_QB, H_GATE_A, H_GATE_B, H_KB, H_VB, H_GK = (
    0, 512, 1024, 2048, 3072, 4096, 5120, 6144, 6400, 6656)
H_WIDTH = 6912
H_TN = 2304

EXPERT_ROWS = 256
PACKED_ROWS = ROW_CHUNKS // 2
HIGH_HALF = -65536


def _cparams(sem, vmem=VMEM_LIMIT):
    return pltpu.CompilerParams(dimension_semantics=sem, vmem_limit_bytes=vmem)


def _tile(n, pref):
    t = min(n, pref)
    while n % t:
        t -= LANES
    assert t > 0 and t % LANES == 0, (n, pref)
    return t


def _pair_tile(n_p, n_s, pref):
    return _tile(math.gcd(n_p, n_s), pref)


def _pair_maps(np_tiles, col=0, extra=0):
    def p_map(i, *_):
        return (jnp.minimum(i, np_tiles - 1), col)

    def s_map(i, *_):
        return (jnp.maximum(i - np_tiles, 0), col)

    return p_map, s_map


def _pair_value(i, np_tiles, p_ref, s_ref):
    return jnp.where(i < np_tiles, p_ref[...], s_ref[...])


def _mm_kernel(xp_ref, xs_ref, w_ref, o_ref, *, np_tiles):
    x = _pair_value(pl.program_id(0), np_tiles, xp_ref, xs_ref)
    o_ref[...] = jnp.dot(x.astype(BF16), w_ref[...], preferred_element_type=F32).astype(o_ref.dtype)


def _matmul(xp, xs, w, out_dtype, tm, tn, name):
    k = xp.shape[1]
    m = xp.shape[0] + xs.shape[0]
    n = w.shape[1]
    np_tiles = xp.shape[0] // tm
    p_map, s_map = _pair_maps(np_tiles)
    return pl.pallas_call(
        functools.partial(_mm_kernel, np_tiles=np_tiles),
        grid=(m // tm, n // tn),
        in_specs=[pl.BlockSpec((tm, k), p_map),
                  pl.BlockSpec((tm, k), s_map),
                  pl.BlockSpec((k, tn), lambda i, j: (0, j))],
        out_specs=pl.BlockSpec((tm, tn), lambda i, j: (i, j)),
        out_shape=jax.ShapeDtypeStruct((m, n), out_dtype),
        compiler_params=_cparams(("parallel", "arbitrary")),
        name=name,
    )(xp, xs, w)


def _split_bf16(x):
    hi = x.astype(BF16)
    lo = (x - hi.astype(F32)).astype(BF16)
    return hi, lo


def _gla_kernel(*refs, par, nb, c, sub, has_s0):
    q_refs, k_refs, v_refs, ga_refs, gk_refs = (refs[n * par:(n + 1) * par] for n in range(5))
    rest = refs[5 * par:]
    n_state = par * GLA_HEADS
    s_scr = [rest[len(rest) - n_state + p * GLA_HEADS:len(rest) - n_state + (p + 1) * GLA_HEADS]
             for p in range(par)]
    rest = rest[:len(rest) - n_state]
    if has_s0:
        wup_ref, bgk_ref, gn_ref, s0_ref, o_ref, sout_ref = rest
    else:
        wup_ref, bgk_ref, gn_ref, o_ref, sout_ref = rest
        s0_ref = None
    ci = pl.program_id(1)
    rows = nb * c

    @pl.when(ci == 0)
    def _():
        for p in range(par):
            for h in range(GLA_HEADS):
                if has_s0:
                    s_scr[p][h][...] = s0_ref[p * nb:(p + 1) * nb, h]
                else:
                    s_scr[p][h][...] = jnp.zeros_like(s_scr[p][h])

    r_i = lax.broadcasted_iota(I32, (rows, rows), 0)
    c_i = lax.broadcasted_iota(I32, (rows, rows), 1)
    same_seq = (r_i // c) == (c_i // c)
    causal = same_seq & (c_i <= r_i)
    tri = jnp.where(causal, 1.0, 0.0).astype(BF16)
    seg = jnp.where(same_seq, 1.0, 0.0).astype(BF16)
    ones_kv = jnp.ones((rows, GLA_DV), BF16)
    seq_of_row = lax.broadcasted_iota(I32, (rows, 1), 0) // c
    tn = (((0,), (0,)), ((), ()))

    def decays(p, rs):
        pre = jnp.dot(gk_refs[p][rs, :], wup_ref[...], preferred_element_type=F32) + bgk_ref[...]
        log_a = jax.nn.log_sigmoid(pre) / GLA_GATE_NORMALIZER
        la_hi, la_lo = _split_bf16(log_a)
        b = (jnp.dot(tri, la_hi, preferred_element_type=F32)
             + jnp.dot(tri, la_lo, preferred_element_type=F32))
        b_last = (jnp.dot(seg, la_hi, preferred_element_type=F32)
                  + jnp.dot(seg, la_lo, preferred_element_type=F32))
        q = q_refs[p][rs, :].astype(F32) * (GLA_DK ** -0.5)
        k = k_refs[p][rs, :].astype(F32)
        q_dec = (q * jnp.exp(b)).astype(BF16)
        k_dec = (k * jnp.exp(-b)).astype(BF16)
        k_rem = k * jnp.exp(b_last - b)
        return q_dec, k_dec, k_rem, la_hi, la_lo

    def head(p, rs, h, q_dec, k_dec, k_rem, la_hi, la_lo):
        ks = slice(h * GLA_DK, (h + 1) * GLA_DK)
        vh = v_refs[p][rs, h * GLA_DV:(h + 1) * GLA_DV]
        att = lax.dot_general(q_dec[:, ks], k_dec[:, ks], (((1,), (1,)), ((), ())),
                              preferred_element_type=F32)
        att = jnp.where(causal, att, 0.0).astype(BF16)
        o_h = jnp.dot(att, vh, preferred_element_type=F32)
        for j in range(nb):
            mine = (seq_of_row == j) if nb > 1 else None
            pick = (lambda a: jnp.where(mine, a, 0.0)) if nb > 1 else (lambda a: a)
            s_old = s_scr[p][h][j]
            o_h = o_h + pick(jnp.dot(q_dec[:, ks], s_old.astype(BF16), preferred_element_type=F32))
            dec = (lax.dot_general(pick(la_hi[:, ks].astype(F32)).astype(BF16), ones_kv, tn,
                                   preferred_element_type=F32)
                   + lax.dot_general(pick(la_lo[:, ks].astype(F32)).astype(BF16), ones_kv, tn,
                                     preferred_element_type=F32))
            upd = lax.dot_general(pick(k_rem[:, ks]).astype(BF16), vh, tn, preferred_element_type=F32)
            s_scr[p][h][j] = jnp.exp(dec) * s_old + upd
        return o_h * lax.rsqrt(jnp.mean(jnp.square(o_h), axis=-1, keepdims=True) + EPS) * gn_ref[...]

    for s in range(sub):
        rs = slice(s * rows, (s + 1) * rows)
        staged = [decays(p, rs) for p in range(par)]
        outs = [[] for _ in range(par)]
        for h in range(GLA_HEADS):
            for p in range(par):
                outs[p].append(head(p, rs, h, *staged[p]))
        for p in range(par):
            o = jnp.concatenate(outs[p], axis=-1) * jax.nn.silu(ga_refs[p][rs, :].astype(F32))
            o_ref[p, rs, :] = o.astype(o_ref.dtype)

    @pl.when(ci == pl.num_programs(1) - 1)
    def _():
        for p in range(par):
            for h in range(GLA_HEADS):
                sout_ref[p * nb:(p + 1) * nb, h] = s_scr[p][h][...]


def _gla(h, wup, bgk, gn, s0, *, row0, n_seq, seq_len, par, nb, c, sub, name):
    rows = nb * c * sub
    n_blocks = n_seq // nb
    n_steps = seq_len // (c * sub)
    rb0 = row0 // rows

    def rmap(p, col):
        return lambda g, i: (rb0 + (g * par + p) * n_steps + i, col)

    fields = ((GLA_KEY_DIM, H_QA), (GLA_KEY_DIM, H_KA), (GLA_VAL_DIM, H_VA), (GLA_VAL_DIM, H_GA), (LANES, H_GK))
    in_specs = [pl.BlockSpec((rows, width), rmap(p, off // width)) for width, off in fields for p in range(par)]
    in_specs += [pl.BlockSpec((LANES, GLA_KEY_DIM), lambda g, i: (0, 0)),
                 pl.BlockSpec((1, GLA_KEY_DIM), lambda g, i: (0, 0)),
                 pl.BlockSpec((1, GLA_DV), lambda g, i: (0, 0))]
    args = [h] * (5 * par) + [wup, bgk, gn]
    state_spec = pl.BlockSpec((par * nb, GLA_HEADS, GLA_DK, GLA_DV), lambda g, i: (g, 0, 0, 0))
    if s0 is not None:
        in_specs.append(state_spec)
        args.append(s0)
    kern = functools.partial(_gla_kernel, par=par, nb=nb, c=c, sub=sub, has_s0=s0 is not None)
    return pl.pallas_call(
        kern,
        grid=(n_blocks // par, n_steps),
        in_specs=in_specs,
        out_specs=[pl.BlockSpec((par, rows, GLA_VAL_DIM), lambda g, i: (g, i, 0)), state_spec],
        out_shape=[jax.ShapeDtypeStruct((n_blocks, nb * seq_len, GLA_VAL_DIM), BF16),
                   jax.ShapeDtypeStruct((n_seq, GLA_HEADS, GLA_DK, GLA_DV), F32)],
        scratch_shapes=[pltpu.VMEM((nb, GLA_DK, GLA_DV), F32)] * (par * GLA_HEADS),
        compiler_params=_cparams(("parallel", "arbitrary")),
        name=name,
    )(*args)


def _alibi_slope(head):
    return 2.0 ** (-8.0 * (head + 1) / SWA_HEADS)


def _swa_softmax_pv(parts, sink):
    m = sink
    for s, _ in parts:
        m = jnp.maximum(m, jnp.max(s, axis=-1, keepdims=True))
    denom = jnp.exp(sink - m)
    acc = None
    for s, v in parts:
        p = jnp.exp(s - m)
        denom = denom + jnp.sum(p, axis=-1, keepdims=True)
        pv = jnp.dot(p.astype(BF16), v, preferred_element_type=F32)
        acc = pv if acc is None else acc + pv
    return acc / denom


def _swa_prompt_kernel(sink_ref, q_ref, kp_ref, vp_ref, kc_ref, vc_ref, o_ref, bias):
    i = pl.program_id(1)
    span = 2 * WINDOW
    col = lax.broadcasted_iota(I32, (WINDOW, span), 1)

    @pl.when(i == 0)
    def _():
        dist_i = lax.broadcasted_iota(I32, (WINDOW, span), 0) + WINDOW - col
        in_window = (dist_i >= 0) & (dist_i < WINDOW)
        dist = dist_i.astype(F32)
        for hh in range(SWA_HEADS):
            bias[hh] = jnp.where(in_window, -_alibi_slope(hh) * dist, -jnp.inf)

    kcat = jnp.concatenate([kp_ref[...], kc_ref[...]], axis=0)
    vcat = jnp.concatenate([vp_ref[...], vc_ref[...]], axis=0)
    no_past = (col < WINDOW) & (i == 0)
    scale = SWA_HEAD_DIM ** -0.5
    outs = []
    for hh in range(SWA_HEADS):
        g = hh // SWA_GROUP
        gs = slice(g * SWA_HEAD_DIM, (g + 1) * SWA_HEAD_DIM)
        qh = q_ref[:, hh * SWA_HEAD_DIM:(hh + 1) * SWA_HEAD_DIM] * scale
        s = lax.dot_general(qh, kcat[:, gs], (((1,), (1,)), ((), ())), preferred_element_type=F32)
        s = jnp.where(no_past, -jnp.inf, s + bias[hh])
        outs.append(_swa_softmax_pv([(s, vcat[:, gs])], sink_ref[hh]))
    o_ref[...] = jnp.concatenate(outs, axis=-1).astype(o_ref.dtype)


def _swa_prompt(h, sinks, *, n_seq, seq_len, name):
    nq = seq_len // WINDOW
    qcol = H_QB // SWA_Q_DIM
    kcol = H_KB // SWA_KV_DIM
    vcol = H_VB // SWA_KV_DIM

    def cur(col):
        return lambda b, i, sk: (b * nq + i, col)

    def prev(col):
        return lambda b, i, sk: (b * nq + jnp.maximum(i - 1, 0), col)

    return pl.pallas_call(
        _swa_prompt_kernel,
        grid_spec=pltpu.PrefetchScalarGridSpec(
            num_scalar_prefetch=1,
            grid=(n_seq, nq),
            in_specs=[pl.BlockSpec((WINDOW, SWA_Q_DIM), cur(qcol)),
                      pl.BlockSpec((WINDOW, SWA_KV_DIM), prev(kcol)),
                      pl.BlockSpec((WINDOW, SWA_KV_DIM), prev(vcol)),
                      pl.BlockSpec((WINDOW, SWA_KV_DIM), cur(kcol)),
                      pl.BlockSpec((WINDOW, SWA_KV_DIM), cur(vcol))],
            out_specs=pl.BlockSpec((WINDOW, SWA_Q_DIM), lambda b, i, sk: (b * nq + i, 0)),
            scratch_shapes=[pltpu.VMEM((SWA_HEADS, WINDOW, 2 * WINDOW), F32)],
        ),
        out_shape=jax.ShapeDtypeStruct((n_seq * seq_len, SWA_Q_DIM), BF16),
        compiler_params=_cparams(("parallel", "arbitrary")),
        name=name,
    )(sinks, h, h, h, h, h)


def _swa_sample_kernel(sink_ref, q_ref, kc_ref, vc_ref, kp_ref, vp_ref, o_ref, *, nb, lq):
    rows = SWA_HEADS * lq
    grp_rows = SWA_GROUP * lq
    hd = SWA_HEAD_DIM
    nt = (((1,), (1,)), ((), ()))
    head_of_row = lax.broadcasted_iota(I32, (rows, 1), 0) // lq
    slope = jnp.zeros((rows, 1), F32)
    sink = jnp.zeros((rows, 1), F32)
    for hh in range(SWA_HEADS):
        slope = jnp.where(head_of_row == hh, _alibi_slope(hh), slope)
        sink = jnp.where(head_of_row == hh, sink_ref[hh], sink)
    qi = lax.broadcasted_iota(I32, (rows, WINDOW), 0) % lq
    dist_p = qi + WINDOW - lax.broadcasted_iota(I32, (rows, WINDOW), 1)
    valid_p = dist_p < WINDOW
    bias_p = slope * dist_p.astype(F32)
    dist_c = lax.broadcasted_iota(I32, (rows, lq), 0) % lq - lax.broadcasted_iota(I32, (rows, lq), 1)
    valid_c = dist_c >= 0
    bias_c = slope * dist_c.astype(F32)
    scale = hd ** -0.5

    q_all = q_ref[...].astype(F32)
    kc_all = kc_ref[...].astype(F32)
    vc_all = vc_ref[...].astype(F32)
    seq_outs = []
    for j in range(nb):
        js = slice(j * lq, (j + 1) * lq)
        pieces = []
        for hh in range(SWA_HEADS):
            g = hh // SWA_GROUP
            parts = []
            if g:
                parts.append(jnp.zeros((lq, g * hd), F32))
            parts.append(q_all[js, hh * hd:(hh + 1) * hd])
            if g < SWA_KV_HEADS - 1:
                parts.append(jnp.zeros((lq, (SWA_KV_HEADS - 1 - g) * hd), F32))
            pieces.append(jnp.concatenate(parts, axis=-1))
        q_big = jnp.concatenate(pieces, axis=0).astype(BF16)
        kp = kp_ref[j].astype(BF16)
        vp = vp_ref[j].astype(BF16)
        kcj = kc_all[js, :].astype(BF16)
        vcj = vc_all[js, :].astype(BF16)
        s_p = lax.dot_general(q_big, kp, nt, preferred_element_type=F32)
        s_p = jnp.where(valid_p, s_p * scale - bias_p, -jnp.inf)
        s_c = lax.dot_general(q_big, kcj, nt, preferred_element_type=F32)
        s_c = jnp.where(valid_c, s_c * scale - bias_c, -jnp.inf)
        o_big = _swa_softmax_pv([(s_p, vp), (s_c, vcj)], sink)
        o_grp = [o_big[g * grp_rows:(g + 1) * grp_rows, g * hd:(g + 1) * hd] for g in range(SWA_KV_HEADS)]
        o_heads = jnp.concatenate(o_grp, axis=0)
        seq_outs.append(jnp.concatenate([o_heads[hh * lq:(hh + 1) * lq, :] for hh in range(SWA_HEADS)],
                                        axis=-1))
    o_ref[...] = jnp.concatenate(seq_outs, axis=0).astype(o_ref.dtype)


def _swa_sample(h, sinks, k_past, v_past, *, row0, n_seq, lq, nb, name):
    rows = nb * lq
    rb0 = row0 // rows
    qcol = H_QB // SWA_Q_DIM
    kcol = H_KB // SWA_KV_DIM
    vcol = H_VB // SWA_KV_DIM
    kern = functools.partial(_swa_sample_kernel, nb=nb, lq=lq)
    past_spec = pl.BlockSpec((nb, WINDOW, SWA_KV_DIM), lambda g, sk: (g, 0, 0))
    return pl.pallas_call(
        kern,
        grid_spec=pltpu.PrefetchScalarGridSpec(
            num_scalar_prefetch=1,
            grid=(n_seq // nb,),
            in_specs=[pl.BlockSpec((rows, SWA_Q_DIM), lambda g, sk: (rb0 + g, qcol)),
                      pl.BlockSpec((rows, SWA_KV_DIM), lambda g, sk: (rb0 + g, kcol)),
                      pl.BlockSpec((rows, SWA_KV_DIM), lambda g, sk: (rb0 + g, vcol)),
                      past_spec, past_spec],
            out_specs=pl.BlockSpec((rows, SWA_Q_DIM), lambda g, sk: (g, 0)),
        ),
        out_shape=jax.ShapeDtypeStruct((n_seq * lq, SWA_Q_DIM), BF16),
        compiler_params=_cparams(("parallel",)),
        name=name,
    )(sinks, h, h, h, k_past, v_past)


def _layer_norm(x, g, b):
    mu = jnp.mean(x, axis=-1, keepdims=True)
    xc = x - mu
    var = jnp.mean(jnp.square(xc), axis=-1, keepdims=True)
    return xc * lax.rsqrt(var + EPS) * g + b


def _pack_bf16_pairs(x, s):
    lo = lax.bitcast_convert_type(x[:, (2 * s) * LANES:(2 * s + 1) * LANES].astype(BF16).astype(F32), I32)
    hi = lax.bitcast_convert_type(x[:, (2 * s + 1) * LANES:(2 * s + 2) * LANES].astype(BF16).astype(F32), I32)
    return lax.shift_right_logical(lo, 16) | (hi & HIGH_HALF)


def _unpack_bf16_pairs(w):
    lo = lax.bitcast_convert_type(lax.shift_left(w, 16), F32).astype(BF16)
    hi = lax.bitcast_convert_type(w & HIGH_HALF, F32).astype(BF16)
    return lo, hi


def _merge_kernel(oap_ref, oas_ref, obp_ref, obs_ref, ga_ref, gb_ref, xp_ref, xs_ref,
                  wpa_ref, wpb_ref, wout_ref, g_ref, b_ref, o_ref, xw_ref, *, np_tiles):
    i = pl.program_id(0)
    tm = o_ref.shape[0]
    br_a = jnp.dot(_pair_value(i, np_tiles, oap_ref, oas_ref), wpa_ref[...], preferred_element_type=F32)
    br_b = jnp.dot(_pair_value(i, np_tiles, obp_ref, obs_ref), wpb_ref[...], preferred_element_type=F32)
    merged = (jax.nn.sigmoid(ga_ref[...].astype(F32)) * br_a
              + jax.nn.sigmoid(gb_ref[...].astype(F32)) * br_b)
    mix = jnp.dot(merged.astype(BF16), wout_ref[...], preferred_element_type=F32)
    x = _pair_value(i, np_tiles, xp_ref, xs_ref)
    x1 = _layer_norm(DEEPNORM_ALPHA * x + mix, g_ref[...], b_ref[...])
    o_ref[...] = x1
    for s in range(PACKED_ROWS):
        xw_ref[pl.ds(s, tm, stride=PACKED_ROWS), :] = _pack_bf16_pairs(x1, s)


def _merge(oa, ob, h, x, wpa, wpb, wout, g, b, *, tm, name):
    t = h.shape[0]
    np_tiles = x[0].shape[0] // tm
    p_map, s_map = _pair_maps(np_tiles)
    row = lambda i: (i, 0)
    full = lambda i: (0, 0)
    pair = [pl.BlockSpec((tm, D_MODEL), p_map), pl.BlockSpec((tm, D_MODEL), s_map)]
    return pl.pallas_call(
        functools.partial(_merge_kernel, np_tiles=np_tiles),
        grid=(t // tm,),
        in_specs=pair + pair + [
                  pl.BlockSpec((tm, D_MODEL), lambda i: (i, H_GATE_A // D_MODEL)),
                  pl.BlockSpec((tm, D_MODEL), lambda i: (i, H_GATE_B // D_MODEL))] + pair + [
                  pl.BlockSpec((D_MODEL, D_MODEL), full),
                  pl.BlockSpec((D_MODEL, D_MODEL), full),
                  pl.BlockSpec((D_MODEL, D_MODEL), full),
                  pl.BlockSpec((1, D_MODEL), full),
                  pl.BlockSpec((1, D_MODEL), full)],
        out_specs=[pl.BlockSpec((tm, D_MODEL), row),
                   pl.BlockSpec((tm * PACKED_ROWS, LANES), row)],
        out_shape=[jax.ShapeDtypeStruct((t, D_MODEL), F32),
                   jax.ShapeDtypeStruct((t * PACKED_ROWS, LANES), I32)],
        compiler_params=_cparams(("parallel",)),
        name=name,
    )(*oa, *ob, h, h, *x, wpa, wpb, wout, g, b)


def _router_kernel(x_ref, whi_ref, wlo_ref, bias_ref, idx_ref, wl_ref, rank_ref, cntc_ref, cntr_ref,
                   cntc_scr, cntr_scr):
    i = pl.program_id(0)
    tm = x_ref.shape[0]

    @pl.when(i == 0)
    def _():
        cntc_scr[...] = jnp.zeros_like(cntc_scr)
        cntr_scr[...] = jnp.zeros_like(cntr_scr)

    x_hi, x_lo = _split_bf16(x_ref[...])
    nt = (((1,), (1,)), ((), ()))
    logits = (lax.dot_general(whi_ref[...], x_hi, nt, preferred_element_type=F32)
              + lax.dot_general(whi_ref[...], x_lo, nt, preferred_element_type=F32)
              + lax.dot_general(wlo_ref[...], x_hi, nt, preferred_element_type=F32))
    scores = jax.nn.sigmoid(logits)
    biased = scores + bias_ref[...]

    grouped = biased.reshape(N_GROUPS, GROUP_SIZE, tm)
    m1 = jnp.max(grouped, axis=1)
    n_top = jnp.sum(jnp.where(grouped == m1[:, None, :], 1.0, 0.0), axis=1)
    m2 = jnp.max(jnp.where(grouped < m1[:, None, :], grouped, -jnp.inf), axis=1)
    gscore = m1 + jnp.where(n_top >= 2.0, m1, m2)

    g_iota = lax.broadcasted_iota(I32, (N_GROUPS, tm), 0)
    beaten = jnp.zeros((N_GROUPS, tm), I32)
    for g in range(N_GROUPS):
        other = gscore[g:g + 1, :]
        ahead = (other > gscore) | ((other == gscore) & (g < g_iota))
        beaten = beaten + jnp.where(ahead, 1, 0)
    keep = jnp.where(beaten < TOPK_GROUPS, 1.0, 0.0)
    masked = jnp.where(keep[:, None, :] > 0.5, grouped, -jnp.inf).reshape(N_EXPERTS, tm)

    e_iota = lax.broadcasted_iota(I32, (N_EXPERTS, tm), 0)
    sel_f = jnp.zeros((N_EXPERTS, tm), F32)
    ids = []
    for _ in range(TOP_K):
        best = jnp.max(masked, axis=0, keepdims=True)
        idx = jnp.min(jnp.where(masked == best, e_iota, N_EXPERTS), axis=0, keepdims=True)
        hit = e_iota == idx
        sel_f = sel_f + jnp.where(hit, 1.0, 0.0)
        masked = jnp.where(hit, -jnp.inf, masked)
        ids.append(idx)

    top_sum = jnp.sum(sel_f * scores, axis=0, keepdims=True)

    t_r = lax.broadcasted_iota(I32, (tm, tm), 0)
    t_c = lax.broadcasted_iota(I32, (tm, tm), 1)
    before = jnp.where(t_r < t_c, 1.0, 0.0).astype(BF16)
    sel_b = sel_f.astype(BF16)
    rank = jnp.dot(sel_b, before, preferred_element_type=F32) + cntc_scr[...]
    cntc_scr[...] = cntc_scr[...] + jnp.sum(sel_f, axis=1, keepdims=True)
    cntr_scr[...] = cntr_scr[...] + lax.dot_general(jnp.ones((SUBLANES, tm), BF16), sel_b, nt,
                                                    preferred_element_type=F32)

    idx_rows, w_rows, rank_rows = [], [], []
    for idx in ids:
        hit = e_iota == idx
        w = jnp.sum(jnp.where(hit, scores, 0.0), axis=0, keepdims=True)
        w_rows.append(w / top_sum * ROUTED_SCALE)
        rank_rows.append(jnp.sum(jnp.where(hit, rank, 0.0), axis=0, keepdims=True))
        idx_rows.append(idx)
    idx_ref[...] = jnp.concatenate(idx_rows, axis=0)
    rank_ref[...] = jnp.concatenate(rank_rows, axis=0).astype(I32)

    w_all = jnp.concatenate(w_rows, axis=0)
    w_1 = w_all.astype(BF16)
    r_1 = w_all - w_1.astype(F32)
    w_2 = r_1.astype(BF16)
    w_3 = (r_1 - w_2.astype(F32)).astype(BF16)
    terms = jnp.concatenate([w_1, w_2, w_3, jnp.zeros_like(w_1)], axis=0)
    t_row = lax.broadcasted_iota(I32, (4 * TOP_K, TOP_K * LANES), 0)
    t_col = lax.broadcasted_iota(I32, (4 * TOP_K, TOP_K * LANES), 1)
    spread = jnp.where((t_row % TOP_K) == (t_col // LANES), 1.0, 0.0).astype(BF16)
    w_lanes = lax.dot_general(terms, spread, (((0,), (0,)), ((), ())), preferred_element_type=F32)
    for k in range(TOP_K):
        wl_ref[pl.ds(k, tm, stride=TOP_K), :] = w_lanes[:, k * LANES:(k + 1) * LANES]

    @pl.when(i == pl.num_programs(0) - 1)
    def _():
        cntc_ref[...] = cntc_scr[...].astype(I32)
        cntr_ref[...] = cntr_scr[...].astype(I32)


def _router(x1, w_hi, w_lo, bias, *, tm, name):
    t = x1.shape[0]
    full = lambda i: (0, 0)
    tok = lambda i: (0, i)
    return pl.pallas_call(
        _router_kernel,
        grid=(t // tm,),
        in_specs=[pl.BlockSpec((tm, D_MODEL), lambda i: (i, 0)),
                  pl.BlockSpec((N_EXPERTS, D_MODEL), full),
                  pl.BlockSpec((N_EXPERTS, D_MODEL), full),
                  pl.BlockSpec((N_EXPERTS, 1), full)],
        out_specs=[pl.BlockSpec((TOP_K, tm), tok),
                   pl.BlockSpec((TOP_K * tm, LANES), lambda i: (i, 0)),
                   pl.BlockSpec((TOP_K, tm), tok),
                   pl.BlockSpec((N_EXPERTS, 1), full),
                   pl.BlockSpec((SUBLANES, N_EXPERTS), full)],
        out_shape=[jax.ShapeDtypeStruct((TOP_K, t), I32),
                   jax.ShapeDtypeStruct((TOP_K * t, LANES), F32),
                   jax.ShapeDtypeStruct((TOP_K, t), I32),
                   jax.ShapeDtypeStruct((N_EXPERTS, 1), I32),
                   jax.ShapeDtypeStruct((SUBLANES, N_EXPERTS), I32)],
        scratch_shapes=[pltpu.VMEM((N_EXPERTS, 1), F32),
                        pltpu.VMEM((SUBLANES, N_EXPERTS), F32)],
        compiler_params=_cparams(("arbitrary",)),
        name=name,
    )(x1, w_hi, w_lo, bias)


ITEM_FIELDS = 8


def _byte_split(v):
    return lax.shift_right_logical(v, 8).astype(F32), (v & 255).astype(F32)


def _plan_kernel(cntc_ref, cntr_ref, idx_ref, rank_ref, meta_ref, items_ref, m_scr, *, n_tok):
    i = pl.program_id(0)
    tm = idx_ref.shape[1]
    nbm = m_scr.shape[0]
    nip = items_ref.shape[1]
    nbp = nip - N_EXPERTS
    n_rows = n_tok * TOP_K
    n_blk = n_rows // EXPERT_ROWS
    nt = (((1,), (1,)), ((), ()))

    r_i = lax.broadcasted_iota(I32, (N_EXPERTS, N_EXPERTS), 0)
    c_i = lax.broadcasted_iota(I32, (N_EXPERTS, N_EXPERTS), 1)
    below = jnp.where(c_i < r_i, 1.0, 0.0).astype(BF16)
    ones_c = jnp.ones((N_EXPERTS, LANES), F32)
    c_hi, c_lo = _byte_split(cntc_ref[...])
    start_col = (256.0 * jnp.dot(below, (c_hi * ones_c).astype(BF16), preferred_element_type=F32)
                 + jnp.dot(below, (c_lo * ones_c).astype(BF16), preferred_element_type=F32))[:, 0:1]

    @pl.when(i == 0)
    def _():
        m_scr[...] = jnp.zeros_like(m_scr)
        above = jnp.where(r_i < c_i, 1.0, 0.0).astype(BF16)
        r_hi, r_lo = _byte_split(cntr_ref[...])
        start_row = (256.0 * jnp.dot(r_hi.astype(BF16), above, preferred_element_type=F32)
                     + jnp.dot(r_lo.astype(BF16), above, preferred_element_type=F32))[0:1, :]
        b_col = lax.broadcasted_iota(I32, (nbp, 1), 0)
        b_row = lax.broadcasted_iota(I32, (1, nbp), 1)
        blk_col = jnp.where(b_col < n_blk, b_col * EXPERT_ROWS, n_rows).astype(F32)
        blk_row = jnp.where(b_row < n_blk, b_row * EXPERT_ROWS, n_rows).astype(F32)
        v_col = jnp.concatenate([blk_col, start_col], axis=0)
        v_row = jnp.concatenate([blk_row, start_row], axis=1)
        j_col = lax.broadcasted_iota(I32, (nip, 1), 0)
        k_row = lax.broadcasted_iota(I32, (1, nip), 1)
        ahead = (v_row < v_col) | ((v_row == v_col) & (k_row < j_col))
        order_col = jnp.sum(jnp.where(ahead, 1.0, 0.0), axis=1, keepdims=True)
        pos_row = k_row.astype(F32)
        lo_abs = jnp.sum(jnp.where(order_col == pos_row, v_col, 0.0), axis=0, keepdims=True)
        hi_abs = jnp.sum(jnp.where(order_col == pos_row + 1.0, v_col, 0.0), axis=0, keepdims=True)
        hi_abs = jnp.where(k_row == nip - 1, float(n_rows), hi_abs)
        blk = jnp.minimum(jnp.floor(lo_abs * (1.0 / EXPERT_ROWS)), n_blk - 1.0)
        expert = jnp.sum(jnp.where(start_col <= lo_abs, 1.0, 0.0), axis=0, keepdims=True) - 1.0
        base = blk * EXPERT_ROWS
        e_col = lax.broadcasted_iota(I32, (N_EXPERTS, 1), 0).astype(F32)
        end_col = start_col + cntc_ref[...].astype(F32)
        seg_end = jnp.sum(jnp.where(e_col == expert, end_col, 0.0), axis=0, keepdims=True)
        follower = jnp.sum(jnp.where(start_col <= seg_end, 1.0, 0.0), axis=0, keepdims=True) - 1.0
        follower = jnp.where(seg_end < float(n_rows), follower, -1.0)
        fields = [blk, expert, lo_abs - base, hi_abs - base, follower]
        fields.append(jnp.zeros((ITEM_FIELDS - len(fields), nip), F32))
        items_ref[...] = jnp.concatenate(fields, axis=0).astype(I32)

    e_iota = lax.broadcasted_iota(I32, (N_EXPERTS, tm), 0)
    b_iota = lax.broadcasted_iota(I32, (nbm, tm), 0)
    l_iota = lax.broadcasted_iota(I32, (EXPERT_ROWS, tm), 0)
    tok = i * tm + lax.broadcasted_iota(I32, (1, tm), 1)
    tok_hi, tok_lo = _byte_split(tok)
    acc = jnp.zeros(m_scr.shape, F32)
    for k in range(TOP_K):
        hit = e_iota == idx_ref[k:k + 1, :]
        dest = (jnp.sum(jnp.where(hit, start_col, 0.0), axis=0, keepdims=True).astype(I32)
                + rank_ref[k:k + 1, :])
        oh_blk = jnp.where(b_iota == lax.shift_right_logical(dest, 8), 1.0, 0.0).astype(BF16)
        in_blk = l_iota == (dest & (EXPERT_ROWS - 1))
        vals = jnp.concatenate([jnp.where(in_blk, tok_hi, 0.0), jnp.where(in_blk, tok_lo, 0.0),
                                jnp.where(in_blk, float(k), 0.0)], axis=0).astype(BF16)
        acc = acc + lax.dot_general(oh_blk, vals, nt, preferred_element_type=F32)
    m_scr[...] = m_scr[...] + acc

    @pl.when(i == pl.num_programs(0) - 1)
    def _():
        m = m_scr[...]
        row_tok = m[:, 0:EXPERT_ROWS] * 256.0 + m[:, EXPERT_ROWS:2 * EXPERT_ROWS]
        row_slot = m[:, 2 * EXPERT_ROWS:3 * EXPERT_ROWS] * float(n_tok) + row_tok
        meta_ref[...] = jnp.concatenate([row_tok * float(PACKED_ROWS), row_slot * float(ROW_CHUNKS)],
                                        axis=1).astype(I32)


def _plan(cnt_col, cnt_row, idx, rank, *, tm, name):
    t = idx.shape[1]
    assert EXPERT_ROWS == 256 and (t * TOP_K) % (2 * EXPERT_ROWS) == 0
    n_blk = t * TOP_K // EXPERT_ROWS
    nbm = -(-n_blk // 16) * 16
    nip = -(-n_blk // LANES) * LANES + N_EXPERTS
    tok = lambda i: (0, i)
    full = lambda i: (0, 0)
    return pl.pallas_call(
        functools.partial(_plan_kernel, n_tok=t),
        grid=(t // tm,),
        in_specs=[pl.BlockSpec((N_EXPERTS, 1), full),
                  pl.BlockSpec((SUBLANES, N_EXPERTS), full),
                  pl.BlockSpec((TOP_K, tm), tok),
                  pl.BlockSpec((TOP_K, tm), tok)],
        out_specs=[pl.BlockSpec((nbm, 2 * EXPERT_ROWS), full),
                   pl.BlockSpec((ITEM_FIELDS, nip), full)],
        out_shape=[jax.ShapeDtypeStruct((nbm, 2 * EXPERT_ROWS), I32),
                   jax.ShapeDtypeStruct((ITEM_FIELDS, nip), I32)],
        scratch_shapes=[pltpu.VMEM((nbm, 3 * EXPERT_ROWS), F32)],
        compiler_params=_cparams(("arbitrary",)),
        name=name,
    )(cnt_col, cnt_row, idx, rank)


META_CHUNK = 4 * EXPERT_ROWS
GATHER_STRIDE = EXPERT_ROWS + 8
META_RING = 3
FFN_PIECES = 2 + D_MODEL // EXPERT_FF


def _experts_kernel(items_ref, xw_hbm, meta_hbm, wg_hbm, wu_hbm, wd_hbm, ys_hbm,
                    xw, meta, tile, lhs, wg_f, wu_f, wd_f, wg_b, wu_b, wd_b, yacc, ybuf_0, ybuf_1, state,
                    sem_x, sem_m, sem_y, sem_w, *, nip, n_blk):
    i = pl.program_id(0)
    ybuf = (ybuf_0, ybuf_1)
    blk = items_ref[i]
    expert = items_ref[nip + i]
    lo = items_ref[2 * nip + i]
    hi = items_ref[3 * nip + i]
    follower = items_ref[4 * nip + i]
    nonempty = hi > lo
    par = blk & 1
    chunk = lax.shift_right_logical(blk, 1)

    def meta_base(b):
        return (lax.shift_right_logical(b, 1) % META_RING) * META_CHUNK + (b & 1) * (2 * EXPERT_ROWS)

    mbase = meta_base(blk)
    prev_base = meta_base(jnp.maximum(blk - 1, 0))

    def meta_copy(c):
        slot = c % META_RING
        return pltpu.make_async_copy(meta_hbm.at[pl.ds(pl.multiple_of(c * META_CHUNK, META_CHUNK), META_CHUNK)],
                                     meta.at[pl.ds(pl.multiple_of(slot * META_CHUNK, META_CHUNK), META_CHUNK)],
                                     sem_m.at[slot])

    def ybuf_drain(slot):
        pltpu.make_async_copy(ys_hbm.at[pl.ds(0, EXPERT_ROWS * ROW_CHUNKS), :], ybuf[slot],
                              sem_y.at[slot]).wait()

    def ybuf_fill(slot):
        for j in range(ROW_CHUNKS):
            ybuf[slot][pl.ds(j, EXPERT_ROWS, stride=ROW_CHUNKS), :] = yacc[:, j * LANES:(j + 1) * LANES]

    def send_rows(slot, base, r0, r1):
        for r in range(r0, r1):
            src = ybuf[slot].at[pl.ds(r * ROW_CHUNKS, ROW_CHUNKS), :]
            dst_row = pl.multiple_of(meta[base + EXPERT_ROWS + r], ROW_CHUNKS)
            pltpu.make_async_copy(src, ys_hbm.at[pl.ds(dst_row, ROW_CHUNKS), :],
                                  sem_y.at[slot]).start(priority=r % 2)

    def weight_copies(e, slot):
        return [pltpu.make_async_copy(src.at[e], dst.at[slot], sem_w.at[slot])
                for src, dst in ((wg_hbm, wg_f), (wu_hbm, wu_f), (wd_hbm, wd_f))]

    @pl.when(i == 0)
    def _():
        resident = pltpu.make_async_copy(xw_hbm, xw, sem_x)
        resident.start()
        meta_copy(0).start()
        for cp in weight_copies(expert, 0):
            cp.start()
        state[0] = -1
        state[1] = 1
        resident.wait()

    @pl.when(nonempty & (expert != state[0]))
    def _():
        slot = 1 - state[1]
        for cp in weight_copies(expert, slot):
            cp.wait()
        state[0] = expert
        state[1] = slot

        @pl.when(follower >= 0)
        def _():
            for cp in weight_copies(follower, 1 - slot):
                cp.start()

        wg_b[...] = wg_f[slot].astype(BF16)
        wu_b[...] = wu_f[slot].astype(BF16)
        wd_b[...] = wd_f[slot].astype(BF16)

    @pl.when(nonempty & (lo == 0))
    def _():
        @pl.when(par == 0)
        def _():
            meta_copy(chunk).wait()

            @pl.when(2 * (chunk + 1) < n_blk)
            def _():
                meta_copy(chunk + 1).start()

        for r in range(EXPERT_ROWS):
            t4 = pl.multiple_of(meta[mbase + r], PACKED_ROWS)
            tile[pl.ds(r, PACKED_ROWS, stride=GATHER_STRIDE), :] = xw[pl.ds(t4, PACKED_ROWS), :]
        cols = []
        for s in range(PACKED_ROWS):
            cols.extend(_unpack_bf16_pairs(tile[pl.ds(s * GATHER_STRIDE, EXPERT_ROWS), :]))
        lhs[...] = jnp.concatenate(cols, axis=-1)

    def ffn(x, between=None, to_slot=None):
        done = [0]

        def piece_done():
            if between is not None:
                between(done[0])
            done[0] += 1

        gate = jnp.dot(x, wg_b[...], preferred_element_type=F32)
        piece_done()
        up = jnp.dot(x, wu_b[...], preferred_element_type=F32)
        piece_done()
        hid = (jax.nn.silu(gate) * up).astype(BF16)
        out = []
        for c in range(D_MODEL // EXPERT_FF):
            y_c = jnp.dot(hid, wd_b[:, c * EXPERT_FF:(c + 1) * EXPERT_FF], preferred_element_type=F32)
            if to_slot is None:
                out.append(y_c)
            else:
                for jj in range(EXPERT_FF // LANES):
                    j = c * (EXPERT_FF // LANES) + jj
                    ybuf[to_slot][pl.ds(j, EXPERT_ROWS, stride=ROW_CHUNKS), :] = y_c[:, jj * LANES:(jj + 1) * LANES]
            piece_done()
        assert done[0] == FFN_PIECES
        return jnp.concatenate(out, axis=-1) if to_slot is None else None

    def run_mask(y):
        row = lax.broadcasted_iota(I32, (EXPERT_ROWS, 1), 0)
        return jnp.where((row >= lo) & (row < hi), y, 0.0)

    whole = (lo == 0) & (hi == EXPERT_ROWS)
    first = nonempty & (lo == 0)
    has_prev = blk > 0

    for prev_slot in range(2):
        def send_piece(g, prev_slot=prev_slot):
            send_rows(prev_slot, prev_base, g * EXPERT_ROWS // FFN_PIECES, (g + 1) * EXPERT_ROWS // FFN_PIECES)

        sends = has_prev & (par == 1 - prev_slot)

        @pl.when(whole & sends)
        def _():
            @pl.when(blk >= 2)
            def _():
                ybuf_drain(1 - prev_slot)

            ffn(lhs[...], send_piece, to_slot=1 - prev_slot)

        @pl.when(first & jnp.logical_not(whole) & sends)
        def _():
            yacc[...] = run_mask(ffn(lhs[...], send_piece))

    @pl.when(whole & jnp.logical_not(has_prev))
    def _():
        ffn(lhs[...], to_slot=0)

    @pl.when(nonempty & jnp.logical_not(whole) & jnp.logical_not(first & has_prev))
    def _():
        y = run_mask(ffn(lhs[...]))

        @pl.when(lo == 0)
        def _():
            yacc[...] = y

        @pl.when(lo > 0)
        def _():
            yacc[...] = yacc[...] + y

    @pl.when(nonempty & (hi == EXPERT_ROWS))
    def _():
        @pl.when(jnp.logical_not(whole))
        def _():
            for slot in range(2):
                @pl.when(par == slot)
                def _():
                    @pl.when(blk >= 2)
                    def _():
                        ybuf_drain(slot)

                    ybuf_fill(slot)

        @pl.when(blk == n_blk - 1)
        def _():
            for slot in range(2):
                @pl.when(par == slot)
                def _():
                    send_rows(slot, mbase, 0, EXPERT_ROWS)

    @pl.when(i == pl.num_programs(0) - 1)
    def _():
        ybuf_drain(0)
        ybuf_drain(1)


def _experts(items, xw, meta, wg, wu, wd, *, n_tok, name):
    nip = items.shape[0] // ITEM_FIELDS
    n_rows = n_tok * TOP_K
    n_blk = n_rows // EXPERT_ROWS
    n_items = n_blk + N_EXPERTS
    assert n_blk >= 2 and n_blk % 2 == 0
    kern = functools.partial(_experts_kernel, nip=nip, n_blk=n_blk)
    any_spec = pl.BlockSpec(memory_space=pl.ANY)
    return pl.pallas_call(
        kern,
        grid_spec=pltpu.PrefetchScalarGridSpec(
            num_scalar_prefetch=1,
            grid=(n_items,),
            in_specs=[any_spec] * 5,
            out_specs=any_spec,
            scratch_shapes=[pltpu.VMEM(xw.shape, I32),
                            pltpu.SMEM((META_RING * META_CHUNK,), I32),
                            pltpu.VMEM((PACKED_ROWS * GATHER_STRIDE, LANES), I32),
                            pltpu.VMEM((EXPERT_ROWS, D_MODEL), BF16),
                            pltpu.VMEM((2, D_MODEL, EXPERT_FF), F32),
                            pltpu.VMEM((2, D_MODEL, EXPERT_FF), F32),
                            pltpu.VMEM((2, EXPERT_FF, D_MODEL), F32),
                            pltpu.VMEM((D_MODEL, EXPERT_FF), BF16),
                            pltpu.VMEM((D_MODEL, EXPERT_FF), BF16),
                            pltpu.VMEM((EXPERT_FF, D_MODEL), BF16),
                            pltpu.VMEM((EXPERT_ROWS, D_MODEL), F32),
                            pltpu.VMEM((EXPERT_ROWS * ROW_CHUNKS, LANES), F32),
                            pltpu.VMEM((EXPERT_ROWS * ROW_CHUNKS, LANES), F32),
                            pltpu.SMEM((2,), I32),
                            pltpu.SemaphoreType.DMA,
                            pltpu.SemaphoreType.DMA((META_RING,)),
                            pltpu.SemaphoreType.DMA((2,)),
                            pltpu.SemaphoreType.DMA((2,))],
        ),
        out_shape=jax.ShapeDtypeStruct((n_rows * ROW_CHUNKS, LANES), F32),
        compiler_params=_cparams(("arbitrary",)),
        name=name,
    )(items, xw, meta, wg, wu, wd)


def _combine_kernel(ys_ref, x_ref, wl_ref, wsg_ref, wsu_ref, wsd_ref, g_ref, b_ref, op_ref, os_ref,
                    routed, *, np_tiles):
    x = x_ref[...]
    xb = x.astype(BF16)
    hid = (jax.nn.silu(jnp.dot(xb, wsg_ref[...], preferred_element_type=F32))
           * jnp.dot(xb, wsu_ref[...], preferred_element_type=F32))
    shared = jnp.dot(hid.astype(BF16), wsd_ref[...], preferred_element_type=F32)
    acc = None
    for k in range(TOP_K):
        part = wl_ref[:, k:k + 1, :] * ys_ref[k]
        acc = part if acc is None else acc + part
    routed[...] = acc
    ffn = jnp.concatenate([routed[:, j, :] for j in range(ROW_CHUNKS)], axis=-1) + shared
    out = _layer_norm(DEEPNORM_ALPHA * x + ffn, g_ref[...], b_ref[...])
    i = pl.program_id(0)

    @pl.when(i < np_tiles)
    def _():
        op_ref[...] = out

    @pl.when(i >= np_tiles)
    def _():
        os_ref[...] = out


def _combine(ys, x1, w_tok, wsg, wsu, wsd, g, b, *, n_prompt, tm, name):
    t = x1.shape[0]
    np_tiles = n_prompt // tm
    p_map, s_map = _pair_maps(np_tiles)
    row = lambda i: (i, 0)
    full = lambda i: (0, 0)
    return pl.pallas_call(
        functools.partial(_combine_kernel, np_tiles=np_tiles),
        grid=(t // tm,),
        in_specs=[pl.BlockSpec((TOP_K, tm, ROW_CHUNKS, LANES), lambda i: (0, i, 0, 0)),
                  pl.BlockSpec((tm, D_MODEL), row),
                  pl.BlockSpec((tm, TOP_K, LANES), lambda i: (i, 0, 0)),
                  pl.BlockSpec((D_MODEL, SHARED_FF), full),
                  pl.BlockSpec((D_MODEL, SHARED_FF), full),
                  pl.BlockSpec((SHARED_FF, D_MODEL), full),
                  pl.BlockSpec((1, D_MODEL), full),
                  pl.BlockSpec((1, D_MODEL), full)],
        out_specs=[pl.BlockSpec((tm, D_MODEL), p_map), pl.BlockSpec((tm, D_MODEL), s_map)],
        out_shape=[jax.ShapeDtypeStruct((n_prompt, D_MODEL), F32),
                   jax.ShapeDtypeStruct((t - n_prompt, D_MODEL), F32)],
        scratch_shapes=[pltpu.VMEM((tm, ROW_CHUNKS, LANES), F32)],
        compiler_params=_cparams(("arbitrary",)),
        name=name,
    )(ys, x1, w_tok, wsg, wsu, wsd, g, b)


def _pack_w_in(w_in):
    sizes = (GLA_KEY_DIM, GLA_KEY_DIM, GLA_VAL_DIM, GLA_VAL_DIM, GLA_GATE_RANK,
             SWA_Q_DIM, SWA_KV_DIM, SWA_KV_DIM, D_MODEL, D_MODEL)
    offs = [0]
    for s in sizes:
        offs.append(offs[-1] + s)
    qa, ka, va, ga, gk, qb, kb, vb, gate_a, gate_b = (w_in[:, offs[i]:offs[i + 1]] for i in range(10))
    pad = lambda w, n: jnp.pad(w, ((0, 0), (0, n - w.shape[1])))
    packed = jnp.concatenate([qa, ka, va, ga, qb, gate_a, gate_b, kb, vb, pad(gk, 2 * LANES)], axis=1)
    assert packed.shape[1] == H_WIDTH
    return packed.astype(BF16), jnp.concatenate([kb, vb], axis=1).astype(BF16)


def kernel(x_prompt, x_sample, state_gla, cache_swa_k, cache_swa_v, w_in, w_gk_up, b_gk, gla_norm_g,
           attn_sinks, w_proj_a, w_proj_b, w_out, ln1_g, ln1_b, w_router, router_bias,
           w_expert_gate, w_expert_up, w_expert_down, w_shared_gate, w_shared_up, w_shared_down,
           ln2_g, ln2_b):
    assert w_in.shape[0] == 1, "single-layer trunk"
    bp, lp, d = x_prompt.shape
    bs, ls, _ = x_sample.shape
    assert d == D_MODEL and ls == SUBLANES and cache_swa_k.shape[2] == WINDOW
    tp, ts = bp * lp, bs * ls
    t = tp + ts

    xp = x_prompt.reshape(tp, d)
    xs = x_sample.reshape(ts, d)
    w_main, w_kv = _pack_w_in(w_in[0])
    h = _matmul(xp, xs, w_main, BF16, _pair_tile(tp, ts, 1024), H_TN, "proj_in")

    xp_tail = x_prompt[:, lp - WINDOW:].reshape(bp * WINDOW, d)
    kv_tail = _matmul(xp_tail, xs, w_kv, F32, _pair_tile(bp * WINDOW, ts, 512), 2 * SWA_KV_DIM,
                      "proj_kv_tail")

    wup = jnp.pad(w_gk_up[0], ((0, LANES - GLA_GATE_RANK), (0, 0))).astype(BF16)
    bgk = b_gk[0].reshape(1, GLA_KEY_DIM)
    gn = gla_norm_g[0].reshape(1, GLA_DV)
    oa_p, s_prompt = _gla(h, wup, bgk, gn, None, row0=0, n_seq=bp, seq_len=lp, par=math.gcd(bp, 4), nb=1,
                          c=GLA_CHUNK, sub=2, name="gla_prompt")
    oa_s, s_sample = _gla(h, wup, bgk, gn, state_gla[0], row0=tp, n_seq=bs, seq_len=ls, par=1, nb=8,
                          c=math.gcd(ls, GLA_CHUNK), sub=1, name="gla_sample")
    oa_p = oa_p.reshape(tp, GLA_VAL_DIM)
    oa_s = oa_s.reshape(ts, GLA_VAL_DIM)

    sinks = attn_sinks[0]
    k_past = cache_swa_k[0].reshape(bs, WINDOW, SWA_KV_DIM)
    v_past = cache_swa_v[0].reshape(bs, WINDOW, SWA_KV_DIM)
    ob_p = _swa_prompt(h, sinks, n_seq=bp, seq_len=lp, name="swa_prompt")
    ob_s = _swa_sample(h, sinks, k_past, v_past, row0=tp, n_seq=bs, lq=ls, nb=8, name="swa_sample")

    x1, xw = _merge((oa_p, oa_s), (ob_p, ob_s), h, (xp, xs), w_proj_a[0].astype(BF16),
                    w_proj_b[0].astype(BF16), w_out[0].astype(BF16), ln1_g[0].reshape(1, d),
                    ln1_b[0].reshape(1, d), tm=_pair_tile(tp, ts, 512), name="merge_ln1")

    wr_t = w_router[0].T
    wr_hi = wr_t.astype(BF16)
    wr_lo = (wr_t - wr_hi.astype(F32)).astype(BF16)
    tm_r = _tile(t, 512)
    idx, w_lanes, rank, cnt_col, cnt_row = _router(x1, wr_hi, wr_lo, router_bias[0].reshape(N_EXPERTS, 1),
                                                   tm=tm_r, name="router")
    meta, items = _plan(cnt_col, cnt_row, idx, rank, tm=tm_r, name="plan")
    ys = _experts(items.reshape(-1), xw, meta.reshape(-1), w_expert_gate[0], w_expert_up[0],
                  w_expert_down[0], n_tok=t, name="experts")
    y_p, y_s = _combine(ys.reshape(TOP_K, t, ROW_CHUNKS, LANES), x1, w_lanes.reshape(t, TOP_K, LANES),
                        w_shared_gate[0].astype(BF16), w_shared_up[0].astype(BF16),
                        w_shared_down[0].astype(BF16), ln2_g[0].reshape(1, d), ln2_b[0].reshape(1, d),
                        n_prompt=tp, tm=_pair_tile(tp, ts, 512), name="combine_ln2")

    y_prompt = y_p.reshape(bp, lp, d)
    y_sample = y_s.reshape(bs, ls, d)
    k_tail = kv_tail[:, :SWA_KV_DIM]
    v_tail = kv_tail[:, SWA_KV_DIM:]
    kv_shape = (SWA_KV_HEADS, SWA_HEAD_DIM)
    k_prompt = k_tail[:bp * WINDOW].reshape(1, bp, WINDOW, *kv_shape)
    v_prompt = v_tail[:bp * WINDOW].reshape(1, bp, WINDOW, *kv_shape)
    k_new = k_tail[bp * WINDOW:].reshape(bs, ls, *kv_shape)
    v_new = v_tail[bp * WINDOW:].reshape(bs, ls, *kv_shape)
    k_sample = jnp.concatenate([cache_swa_k[0][:, ls:], k_new], axis=1)[None]
    v_sample = jnp.concatenate([cache_swa_v[0][:, ls:], v_new], axis=1)[None]
    return (y_prompt, y_sample, s_prompt[None], s_sample[None], k_prompt, v_prompt, k_sample, v_sample)
```

```python
import functools
import math

import jax
import jax.numpy as jnp
from jax import lax
from jax.experimental import pallas as pl
from jax.experimental.pallas import tpu as pltpu

F32 = jnp.float32
BF16 = jnp.bfloat16
I32 = jnp.int32

D_MODEL = 1024
GLA_HEADS = 4
GLA_DK = 128
GLA_DV = 256
GLA_KEY_DIM = GLA_HEADS * GLA_DK
GLA_VAL_DIM = GLA_HEADS * GLA_DV
GLA_GATE_RANK = 16
GLA_GATE_NORMALIZER = 16.0
GLA_CHUNK = 64
SWA_HEADS = 16
SWA_KV_HEADS = 4
SWA_GROUP = SWA_HEADS // SWA_KV_HEADS
SWA_HEAD_DIM = 64
SWA_Q_DIM = SWA_HEADS * SWA_HEAD_DIM
SWA_KV_DIM = SWA_KV_HEADS * SWA_HEAD_DIM
WINDOW = 128
N_EXPERTS = 256
TOP_K = 8
N_GROUPS = 8
GROUP_SIZE = N_EXPERTS // N_GROUPS
TOPK_GROUPS = 4
EXPERT_FF = 256
SHARED_FF = 256
ROUTED_SCALE = 2.5
DEEPNORM_ALPHA = 2.0 ** 0.25
EPS = 1e-5

LANES = 128
SUBLANES = 8
ROW_CHUNKS = D_MODEL // LANES
VMEM_LIMIT = 56 * 1024 * 1024

H_QA, H_KA, H_VA, H_GA, H_QB, H_GATE_A, H_GATE_B, H_KB, H_VB, H_GK = (
    0, 512, 1024, 2048, 3072, 4096, 5120, 6144, 6400, 6656)
H_WIDTH = 6912
H_TN = 2304

EXPERT_ROWS = 256
PACKED_ROWS = ROW_CHUNKS // 2
HIGH_HALF = -65536


def _cparams(sem, vmem=VMEM_LIMIT):
    return pltpu.CompilerParams(dimension_semantics=sem, vmem_limit_bytes=vmem)


def _tile(n, pref):
    t = min(n, pref)
    while n % t:
        t -= LANES
    assert t > 0 and t % LANES == 0, (n, pref)
    return t


def _pair_tile(n_p, n_s, pref):
    return _tile(math.gcd(n_p, n_s), pref)


def _pair_maps(np_tiles, col=0, extra=0):
    def p_map(i, *_):
        return (jnp.minimum(i, np_tiles - 1), col)

    def s_map(i, *_):
        return (jnp.maximum(i - np_tiles, 0), col)

    return p_map, s_map


def _pair_value(i, np_tiles, p_ref, s_ref):
    return jnp.where(i < np_tiles, p_ref[...], s_ref[...])


def _mm_kernel(xp_ref, xs_ref, w_ref, o_ref, *, np_tiles):
    x = _pair_value(pl.program_id(0), np_tiles, xp_ref, xs_ref)
    o_ref[...] = jnp.dot(x.astype(BF16), w_ref[...], preferred_element_type=F32).astype(o_ref.dtype)


def _matmul(xp, xs, w, out_dtype, tm, tn, name):
    k = xp.shape[1]
    m = xp.shape[0] + xs.shape[0]
    n = w.shape[1]
    np_tiles = xp.shape[0] // tm
    p_map, s_map = _pair_maps(np_tiles)
    return pl.pallas_call(
        functools.partial(_mm_kernel, np_tiles=np_tiles),
        grid=(m // tm, n // tn),
        in_specs=[pl.BlockSpec((tm, k), p_map),
                  pl.BlockSpec((tm, k), s_map),
                  pl.BlockSpec((k, tn), lambda i, j: (0, j))],
        out_specs=pl.BlockSpec((tm, tn), lambda i, j: (i, j)),
        out_shape=jax.ShapeDtypeStruct((m, n), out_dtype),
        compiler_params=_cparams(("parallel", "arbitrary")),
        name=name,
    )(xp, xs, w)


def _split_bf16(x):
    hi = x.astype(BF16)
    lo = (x - hi.astype(F32)).astype(BF16)
    return hi, lo


def _gla_kernel(*refs, par, nb, c, sub, has_s0):
    q_refs, k_refs, v_refs, ga_refs, gk_refs = (refs[n * par:(n + 1) * par] for n in range(5))
    rest = refs[5 * par:]
    n_state = par * GLA_HEADS
    s_scr = [rest[len(rest) - n_state + p * GLA_HEADS:len(rest) - n_state + (p + 1) * GLA_HEADS]
             for p in range(par)]
    rest = rest[:len(rest) - n_state]
    if has_s0:
        wup_ref, bgk_ref, gn_ref, s0_ref, o_ref, sout_ref = rest
    else:
        wup_ref, bgk_ref, gn_ref, o_ref, sout_ref = rest
        s0_ref = None
    ci = pl.program_id(1)
    rows = nb * c

    @pl.when(ci == 0)
    def _():
        for p in range(par):
            for h in range(GLA_HEADS):
                if has_s0:
                    s_scr[p][h][...] = s0_ref[p * nb:(p + 1) * nb, h]
                else:
                    s_scr[p][h][...] = jnp.zeros_like(s_scr[p][h])

    r_i = lax.broadcasted_iota(I32, (rows, rows), 0)
    c_i = lax.broadcasted_iota(I32, (rows, rows), 1)
    same_seq = (r_i // c) == (c_i // c)
    causal = same_seq & (c_i <= r_i)
    tri = jnp.where(causal, 1.0, 0.0).astype(BF16)
    seg = jnp.where(same_seq, 1.0, 0.0).astype(BF16)
    ones_kv = jnp.ones((rows, GLA_DV), BF16)
    seq_of_row = lax.broadcasted_iota(I32, (rows, 1), 0) // c
    tn = (((0,), (0,)), ((), ()))

    def decays(p, rs):
        pre = jnp.dot(gk_refs[p][rs, :], wup_ref[...], preferred_element_type=F32) + bgk_ref[...]
        log_a = jax.nn.log_sigmoid(pre) / GLA_GATE_NORMALIZER
        la_hi, la_lo = _split_bf16(log_a)
        b = (jnp.dot(tri, la_hi, preferred_element_type=F32)
             + jnp.dot(tri, la_lo, preferred_element_type=F32))
        b_last = (jnp.dot(seg, la_hi, preferred_element_type=F32)
                  + jnp.dot(seg, la_lo, preferred_element_type=F32))
        q = q_refs[p][rs, :].astype(F32) * (GLA_DK ** -0.5)
        k = k_refs[p][rs, :].astype(F32)
        q_dec = (q * jnp.exp(b)).astype(BF16)
        k_dec = (k * jnp.exp(-b)).astype(BF16)
        k_rem = k * jnp.exp(b_last - b)
        return q_dec, k_dec, k_rem, la_hi, la_lo

    def head(p, rs, h, q_dec, k_dec, k_rem, la_hi, la_lo):
        ks = slice(h * GLA_DK, (h + 1) * GLA_DK)
        vh = v_refs[p][rs, h * GLA_DV:(h + 1) * GLA_DV]
        att = lax.dot_general(q_dec[:, ks], k_dec[:, ks], (((1,), (1,)), ((), ())),
                              preferred_element_type=F32)
        att = jnp.where(causal, att, 0.0).astype(BF16)
        o_h = jnp.dot(att, vh, preferred_element_type=F32)
        for j in range(nb):
            mine = (seq_of_row == j) if nb > 1 else None
            pick = (lambda a: jnp.where(mine, a, 0.0)) if nb > 1 else (lambda a: a)
            s_old = s_scr[p][h][j]
            o_h = o_h + pick(jnp.dot(q_dec[:, ks], s_old.astype(BF16), preferred_element_type=F32))
            dec = (lax.dot_general(pick(la_hi[:, ks].astype(F32)).astype(BF16), ones_kv, tn,
                                   preferred_element_type=F32)
                   + lax.dot_general(pick(la_lo[:, ks].astype(F32)).astype(BF16), ones_kv, tn,
                                     preferred_element_type=F32))
            upd = lax.dot_general(pick(k_rem[:, ks]).astype(BF16), vh, tn, preferred_element_type=F32)
            s_scr[p][h][j] = jnp.exp(dec) * s_old + upd
        return o_h * lax.rsqrt(jnp.mean(jnp.square(o_h), axis=-1, keepdims=True) + EPS) * gn_ref[...]

    for s in range(sub):
        rs = slice(s * rows, (s + 1) * rows)
        staged = [decays(p, rs) for p in range(par)]
        outs = [[] for _ in range(par)]
        for h in range(GLA_HEADS):
            for p in range(par):
                outs[p].append(head(p, rs, h, *staged[p]))
        for p in range(par):
            o = jnp.concatenate(outs[p], axis=-1) * jax.nn.silu(ga_refs[p][rs, :].astype(F32))
            o_ref[p, rs, :] = o.astype(o_ref.dtype)

    @pl.when(ci == pl.num_programs(1) - 1)
    def _():
        for p in range(par):
            for h in range(GLA_HEADS):
                sout_ref[p * nb:(p + 1) * nb, h] = s_scr[p][h][...]


def _gla(h, wup, bgk, gn, s0, *, row0, n_seq, seq_len, par, nb, c, sub, name):
    rows = nb * c * sub
    n_blocks = n_seq // nb
    n_steps = seq_len // (c * sub)
    rb0 = row0 // rows

    def rmap(p, col):
        return lambda g, i: (rb0 + (g * par + p) * n_steps + i, col)

    fields = ((GLA_KEY_DIM, H_QA), (GLA_KEY_DIM, H_KA), (GLA_VAL_DIM, H_VA), (GLA_VAL_DIM, H_GA), (LANES, H_GK))
    in_specs = [pl.BlockSpec((rows, width), rmap(p, off // width)) for width, off in fields for p in range(par)]
    in_specs += [pl.BlockSpec((LANES, GLA_KEY_DIM), lambda g, i: (0, 0)),
                 pl.BlockSpec((1, GLA_KEY_DIM), lambda g, i: (0, 0)),
                 pl.BlockSpec((1, GLA_DV), lambda g, i: (0, 0))]
    args = [h] * (5 * par) + [wup, bgk, gn]
    state_spec = pl.BlockSpec((par * nb, GLA_HEADS, GLA_DK, GLA_DV), lambda g, i: (g, 0, 0, 0))
    if s0 is not None:
        in_specs.append(state_spec)
        args.append(s0)
    kern = functools.partial(_gla_kernel, par=par, nb=nb, c=c, sub=sub, has_s0=s0 is not None)
    return pl.pallas_call(
        kern,
        grid=(n_blocks // par, n_steps),
        in_specs=in_specs,
        out_specs=[pl.BlockSpec((par, rows, GLA_VAL_DIM), lambda g, i: (g, i, 0)), state_spec],
        out_shape=[jax.ShapeDtypeStruct((n_blocks, nb * seq_len, GLA_VAL_DIM), BF16),
                   jax.ShapeDtypeStruct((n_seq, GLA_HEADS, GLA_DK, GLA_DV), F32)],
        scratch_shapes=[pltpu.VMEM((nb, GLA_DK, GLA_DV), F32)] * (par * GLA_HEADS),
        compiler_params=_cparams(("parallel", "arbitrary")),
        name=name,
    )(*args)


def _alibi_slope(head):
    return 2.0 ** (-8.0 * (head + 1) / SWA_HEADS)


def _swa_softmax_pv(parts, sink):
    m = sink
    for s, _ in parts:
        m = jnp.maximum(m, jnp.max(s, axis=-1, keepdims=True))
    denom = jnp.exp(sink - m)
    acc = None
    for s, v in parts:
        p = jnp.exp(s - m)
        denom = denom + jnp.sum(p, axis=-1, keepdims=True)
        pv = jnp.dot(p.astype(BF16), v, preferred_element_type=F32)
        acc = pv if acc is None else acc + pv
    return acc / denom


def _swa_prompt_kernel(sink_ref, q_ref, kp_ref, vp_ref, kc_ref, vc_ref, o_ref, bias):
    i = pl.program_id(1)
    span = 2 * WINDOW
    col = lax.broadcasted_iota(I32, (WINDOW, span), 1)

    @pl.when(i == 0)
    def _():
        dist_i = lax.broadcasted_iota(I32, (WINDOW, span), 0) + WINDOW - col
        in_window = (dist_i >= 0) & (dist_i < WINDOW)
        dist = dist_i.astype(F32)
        for hh in range(SWA_HEADS):
            bias[hh] = jnp.where(in_window, -_alibi_slope(hh) * dist, -jnp.inf)

    kcat = jnp.concatenate([kp_ref[...], kc_ref[...]], axis=0)
    vcat = jnp.concatenate([vp_ref[...], vc_ref[...]], axis=0)
    no_past = (col < WINDOW) & (i == 0)
    scale = SWA_HEAD_DIM ** -0.5
    outs = []
    for hh in range(SWA_HEADS):
        g = hh // SWA_GROUP
        gs = slice(g * SWA_HEAD_DIM, (g + 1) * SWA_HEAD_DIM)
        qh = q_ref[:, hh * SWA_HEAD_DIM:(hh + 1) * SWA_HEAD_DIM] * scale
        s = lax.dot_general(qh, kcat[:, gs], (((1,), (1,)), ((), ())), preferred_element_type=F32)
        s = jnp.where(no_past, -jnp.inf, s + bias[hh])
        outs.append(_swa_softmax_pv([(s, vcat[:, gs])], sink_ref[hh]))
    o_ref[...] = jnp.concatenate(outs, axis=-1).astype(o_ref.dtype)


def _swa_prompt(h, sinks, *, n_seq, seq_len, name):
    nq = seq_len // WINDOW
    qcol = H_QB // SWA_Q_DIM
    kcol = H_KB // SWA_KV_DIM
    vcol = H_VB // SWA_KV_DIM

    def cur(col):
        return lambda b, i, sk: (b * nq + i, col)

    def prev(col):
        return lambda b, i, sk: (b * nq + jnp.maximum(i - 1, 0), col)

    return pl.pallas_call(
        _swa_prompt_kernel,
        grid_spec=pltpu.PrefetchScalarGridSpec(
            num_scalar_prefetch=1,
            grid=(n_seq, nq),
            in_specs=[pl.BlockSpec((WINDOW, SWA_Q_DIM), cur(qcol)),
                      pl.BlockSpec((WINDOW, SWA_KV_DIM), prev(kcol)),
                      pl.BlockSpec((WINDOW, SWA_KV_DIM), prev(vcol)),
                      pl.BlockSpec((WINDOW, SWA_KV_DIM), cur(kcol)),
                      pl.BlockSpec((WINDOW, SWA_KV_DIM), cur(vcol))],
            out_specs=pl.BlockSpec((WINDOW, SWA_Q_DIM), lambda b, i, sk: (b * nq + i, 0)),
            scratch_shapes=[pltpu.VMEM((SWA_HEADS, WINDOW, 2 * WINDOW), F32)],
        ),
        out_shape=jax.ShapeDtypeStruct((n_seq * seq_len, SWA_Q_DIM), BF16),
        compiler_params=_cparams(("parallel", "arbitrary")),
        name=name,
    )(sinks, h, h, h, h, h)


def _swa_sample_kernel(sink_ref, q_ref, kc_ref, vc_ref, kp_ref, vp_ref, o_ref, *, nb, lq):
    rows = SWA_HEADS * lq
    grp_rows = SWA_GROUP * lq
    hd = SWA_HEAD_DIM
    nt = (((1,), (1,)), ((), ()))
    head_of_row = lax.broadcasted_iota(I32, (rows, 1), 0) // lq
    slope = jnp.zeros((rows, 1), F32)
    sink = jnp.zeros((rows, 1), F32)
    for hh in range(SWA_HEADS):
        slope = jnp.where(head_of_row == hh, _alibi_slope(hh), slope)
        sink = jnp.where(head_of_row == hh, sink_ref[hh], sink)
    qi = lax.broadcasted_iota(I32, (rows, WINDOW), 0) % lq
    dist_p = qi + WINDOW - lax.broadcasted_iota(I32, (rows, WINDOW), 1)
    valid_p = dist_p < WINDOW
    bias_p = slope * dist_p.astype(F32)
    dist_c = lax.broadcasted_iota(I32, (rows, lq), 0) % lq - lax.broadcasted_iota(I32, (rows, lq), 1)
    valid_c = dist_c >= 0
    bias_c = slope * dist_c.astype(F32)
    scale = hd ** -0.5

    q_all = q_ref[...].astype(F32)
    kc_all = kc_ref[...].astype(F32)
    vc_all = vc_ref[...].astype(F32)
    seq_outs = []
    for j in range(nb):
        js = slice(j * lq, (j + 1) * lq)
        pieces = []
        for hh in range(SWA_HEADS):
            g = hh // SWA_GROUP
            parts = []
            if g:
                parts.append(jnp.zeros((lq, g * hd), F32))
            parts.append(q_all[js, hh * hd:(hh + 1) * hd])
            if g < SWA_KV_HEADS - 1:
                parts.append(jnp.zeros((lq, (SWA_KV_HEADS - 1 - g) * hd), F32))
            pieces.append(jnp.concatenate(parts, axis=-1))
        q_big = jnp.concatenate(pieces, axis=0).astype(BF16)
        kp = kp_ref[j].astype(BF16)
        vp = vp_ref[j].astype(BF16)
        kcj = kc_all[js, :].astype(BF16)
        vcj = vc_all[js, :].astype(BF16)
        s_p = lax.dot_general(q_big, kp, nt, preferred_element_type=F32)
        s_p = jnp.where(valid_p, s_p * scale - bias_p, -jnp.inf)
        s_c = lax.dot_general(q_big, kcj, nt, preferred_element_type=F32)
        s_c = jnp.where(valid_c, s_c * scale - bias_c, -jnp.inf)
        o_big = _swa_softmax_pv([(s_p, vp), (s_c, vcj)], sink)
        o_grp = [o_big[g * grp_rows:(g + 1) * grp_rows, g * hd:(g + 1) * hd] for g in range(SWA_KV_HEADS)]
        o_heads = jnp.concatenate(o_grp, axis=0)
        seq_outs.append(jnp.concatenate([o_heads[hh * lq:(hh + 1) * lq, :] for hh in range(SWA_HEADS)],
                                        axis=-1))
    o_ref[...] = jnp.concatenate(seq_outs, axis=0).astype(o_ref.dtype)


def _swa_sample(h, sinks, k_past, v_past, *, row0, n_seq, lq, nb, name):
    rows = nb * lq
    rb0 = row0 // rows
    qcol = H_QB // SWA_Q_DIM
    kcol = H_KB // SWA_KV_DIM
    vcol = H_VB // SWA_KV_DIM
    kern = functools.partial(_swa_sample_kernel, nb=nb, lq=lq)
    past_spec = pl.BlockSpec((nb, WINDOW, SWA_KV_DIM), lambda g, sk: (g, 0, 0))
    return pl.pallas_call(
        kern,
        grid_spec=pltpu.PrefetchScalarGridSpec(
            num_scalar_prefetch=1,
            grid=(n_seq // nb,),
            in_specs=[pl.BlockSpec((rows, SWA_Q_DIM), lambda g, sk: (rb0 + g, qcol)),
                      pl.BlockSpec((rows, SWA_KV_DIM), lambda g, sk: (rb0 + g, kcol)),
                      pl.BlockSpec((rows, SWA_KV_DIM), lambda g, sk: (rb0 + g, vcol)),
                      past_spec, past_spec],
            out_specs=pl.BlockSpec((rows, SWA_Q_DIM), lambda g, sk: (g, 0)),
        ),
        out_shape=jax.ShapeDtypeStruct((n_seq * lq, SWA_Q_DIM), BF16),
        compiler_params=_cparams(("parallel",)),
        name=name,
    )(sinks, h, h, h, k_past, v_past)


def _layer_norm(x, g, b):
    mu = jnp.mean(x, axis=-1, keepdims=True)
    xc = x - mu
    var = jnp.mean(jnp.square(xc), axis=-1, keepdims=True)
    return xc * lax.rsqrt(var + EPS) * g + b


def _pack_bf16_pairs(x, s):
    lo = lax.bitcast_convert_type(x[:, (2 * s) * LANES:(2 * s + 1) * LANES].astype(BF16).astype(F32), I32)
    hi = lax.bitcast_convert_type(x[:, (2 * s + 1) * LANES:(2 * s + 2) * LANES].astype(BF16).astype(F32), I32)
    return lax.shift_right_logical(lo, 16) | (hi & HIGH_HALF)


def _unpack_bf16_pairs(w):
    lo = lax.bitcast_convert_type(lax.shift_left(w, 16), F32).astype(BF16)
    hi = lax.bitcast_convert_type(w & HIGH_HALF, F32).astype(BF16)
    return lo, hi


def _merge_kernel(oap_ref, oas_ref, obp_ref, obs_ref, ga_ref, gb_ref, xp_ref, xs_ref,
                  wpa_ref, wpb_ref, wout_ref, g_ref, b_ref, o_ref, xw_ref, *, np_tiles):
    i = pl.program_id(0)
    tm = o_ref.shape[0]
    br_a = jnp.dot(_pair_value(i, np_tiles, oap_ref, oas_ref), wpa_ref[...], preferred_element_type=F32)
    br_b = jnp.dot(_pair_value(i, np_tiles, obp_ref, obs_ref), wpb_ref[...], preferred_element_type=F32)
    merged = (jax.nn.sigmoid(ga_ref[...].astype(F32)) * br_a
              + jax.nn.sigmoid(gb_ref[...].astype(F32)) * br_b)
    mix = jnp.dot(merged.astype(BF16), wout_ref[...], preferred_element_type=F32)
    x = _pair_value(i, np_tiles, xp_ref, xs_ref)
    x1 = _layer_norm(DEEPNORM_ALPHA * x + mix, g_ref[...], b_ref[...])
    o_ref[...] = x1
    for s in range(PACKED_ROWS):
        xw_ref[pl.ds(s, tm, stride=PACKED_ROWS), :] = _pack_bf16_pairs(x1, s)


def _merge(oa, ob, h, x, wpa, wpb, wout, g, b, *, tm, name):
    t = h.shape[0]
    np_tiles = x[0].shape[0] // tm
    p_map, s_map = _pair_maps(np_tiles)
    row = lambda i: (i, 0)
    full = lambda i: (0, 0)
    pair = [pl.BlockSpec((tm, D_MODEL), p_map), pl.BlockSpec((tm, D_MODEL), s_map)]
    return pl.pallas_call(
        functools.partial(_merge_kernel, np_tiles=np_tiles),
        grid=(t // tm,),
        in_specs=pair + pair + [
                  pl.BlockSpec((tm, D_MODEL), lambda i: (i, H_GATE_A // D_MODEL)),
                  pl.BlockSpec((tm, D_MODEL), lambda i: (i, H_GATE_B // D_MODEL))] + pair + [
                  pl.BlockSpec((D_MODEL, D_MODEL), full),
                  pl.BlockSpec((D_MODEL, D_MODEL), full),
                  pl.BlockSpec((D_MODEL, D_MODEL), full),
                  pl.BlockSpec((1, D_MODEL), full),
                  pl.BlockSpec((1, D_MODEL), full)],
        out_specs=[pl.BlockSpec((tm, D_MODEL), row),
                   pl.BlockSpec((tm * PACKED_ROWS, LANES), row)],
        out_shape=[jax.ShapeDtypeStruct((t, D_MODEL), F32),
                   jax.ShapeDtypeStruct((t * PACKED_ROWS, LANES), I32)],
        compiler_params=_cparams(("parallel",)),
        name=name,
    )(*oa, *ob, h, h, *x, wpa, wpb, wout, g, b)


def _router_kernel(x_ref, whi_ref, wlo_ref, bias_ref, idx_ref, wl_ref, rank_ref, cntc_ref, cntr_ref,
                   cntc_scr, cntr_scr):
    i = pl.program_id(0)
    tm = x_ref.shape[0]

    @pl.when(i == 0)
    def _():
        cntc_scr[...] = jnp.zeros_like(cntc_scr)
        cntr_scr[...] = jnp.zeros_like(cntr_scr)

    x_hi, x_lo = _split_bf16(x_ref[...])
    nt = (((1,), (1,)), ((), ()))
    logits = (lax.dot_general(whi_ref[...], x_hi, nt, preferred_element_type=F32)
              + lax.dot_general(whi_ref[...], x_lo, nt, preferred_element_type=F32)
              + lax.dot_general(wlo_ref[...], x_hi, nt, preferred_element_type=F32))
    scores = jax.nn.sigmoid(logits)
    biased = scores + bias_ref[...]

    grouped = biased.reshape(N_GROUPS, GROUP_SIZE, tm)
    m1 = jnp.max(grouped, axis=1)
    n_top = jnp.sum(jnp.where(grouped == m1[:, None, :], 1.0, 0.0), axis=1)
    m2 = jnp.max(jnp.where(grouped < m1[:, None, :], grouped, -jnp.inf), axis=1)
    gscore = m1 + jnp.where(n_top >= 2.0, m1, m2)

    g_iota = lax.broadcasted_iota(I32, (N_GROUPS, tm), 0)
    beaten = jnp.zeros((N_GROUPS, tm), I32)
    for g in range(N_GROUPS):
        other = gscore[g:g + 1, :]
        ahead = (other > gscore) | ((other == gscore) & (g < g_iota))
        beaten = beaten + jnp.where(ahead, 1, 0)
    keep = jnp.where(beaten < TOPK_GROUPS, 1.0, 0.0)
    masked = jnp.where(keep[:, None, :] > 0.5, grouped, -jnp.inf).reshape(N_EXPERTS, tm)

    e_iota = lax.broadcasted_iota(I32, (N_EXPERTS, tm), 0)
    sel_f = jnp.zeros((N_EXPERTS, tm), F32)
    ids = []
    for _ in range(TOP_K):
        best = jnp.max(masked, axis=0, keepdims=True)
        idx = jnp.min(jnp.where(masked == best, e_iota, N_EXPERTS), axis=0, keepdims=True)
        hit = e_iota == idx
        sel_f = sel_f + jnp.where(hit, 1.0, 0.0)
        masked = jnp.where(hit, -jnp.inf, masked)
        ids.append(idx)

    top_sum = jnp.sum(sel_f * scores, axis=0, keepdims=True)

    t_r = lax.broadcasted_iota(I32, (tm, tm), 0)
    t_c = lax.broadcasted_iota(I32, (tm, tm), 1)
    before = jnp.where(t_r < t_c, 1.0, 0.0).astype(BF16)
    sel_b = sel_f.astype(BF16)
    rank = jnp.dot(sel_b, before, preferred_element_type=F32) + cntc_scr[...]
    cntc_scr[...] = cntc_scr[...] + jnp.sum(sel_f, axis=1, keepdims=True)
    cntr_scr[...] = cntr_scr[...] + lax.dot_general(jnp.ones((SUBLANES, tm), BF16), sel_b, nt,
                                                    preferred_element_type=F32)

    idx_rows, w_rows, rank_rows = [], [], []
    for idx in ids:
        hit = e_iota == idx
        w = jnp.sum(jnp.where(hit, scores, 0.0), axis=0, keepdims=True)
        w_rows.append(w / top_sum * ROUTED_SCALE)
        rank_rows.append(jnp.sum(jnp.where(hit, rank, 0.0), axis=0, keepdims=True))
        idx_rows.append(idx)
    idx_ref[...] = jnp.concatenate(idx_rows, axis=0)
    rank_ref[...] = jnp.concatenate(rank_rows, axis=0).astype(I32)

    w_all = jnp.concatenate(w_rows, axis=0)
    w_1 = w_all.astype(BF16)
    r_1 = w_all - w_1.astype(F32)
    w_2 = r_1.astype(BF16)
    w_3 = (r_1 - w_2.astype(F32)).astype(BF16)
    terms = jnp.concatenate([w_1, w_2, w_3, jnp.zeros_like(w_1)], axis=0)
    t_row = lax.broadcasted_iota(I32, (4 * TOP_K, TOP_K * LANES), 0)
    t_col = lax.broadcasted_iota(I32, (4 * TOP_K, TOP_K * LANES), 1)
    spread = jnp.where((t_row % TOP_K) == (t_col // LANES), 1.0, 0.0).astype(BF16)
    w_lanes = lax.dot_general(terms, spread, (((0,), (0,)), ((), ())), preferred_element_type=F32)
    for k in range(TOP_K):
        wl_ref[pl.ds(k, tm, stride=TOP_K), :] = w_lanes[:, k * LANES:(k + 1) * LANES]

    @pl.when(i == pl.num_programs(0) - 1)
    def _():
        cntc_ref[...] = cntc_scr[...].astype(I32)
        cntr_ref[...] = cntr_scr[...].astype(I32)


def _router(x1, w_hi, w_lo, bias, *, tm, name):
    t = x1.shape[0]
    full = lambda i: (0, 0)
    tok = lambda i: (0, i)
    return pl.pallas_call(
        _router_kernel,
        grid=(t // tm,),
        in_specs=[pl.BlockSpec((tm, D_MODEL), lambda i: (i, 0)),
                  pl.BlockSpec((N_EXPERTS, D_MODEL), full),
                  pl.BlockSpec((N_EXPERTS, D_MODEL), full),
                  pl.BlockSpec((N_EXPERTS, 1), full)],
        out_specs=[pl.BlockSpec((TOP_K, tm), tok),
                   pl.BlockSpec((TOP_K * tm, LANES), lambda i: (i, 0)),
                   pl.BlockSpec((TOP_K, tm), tok),
                   pl.BlockSpec((N_EXPERTS, 1), full),
                   pl.BlockSpec((SUBLANES, N_EXPERTS), full)],
        out_shape=[jax.ShapeDtypeStruct((TOP_K, t), I32),
                   jax.ShapeDtypeStruct((TOP_K * t, LANES), F32),
                   jax.ShapeDtypeStruct((TOP_K, t), I32),
                   jax.ShapeDtypeStruct((N_EXPERTS, 1), I32),
                   jax.ShapeDtypeStruct((SUBLANES, N_EXPERTS), I32)],
        scratch_shapes=[pltpu.VMEM((N_EXPERTS, 1), F32),
                        pltpu.VMEM((SUBLANES, N_EXPERTS), F32)],
        compiler_params=_cparams(("arbitrary",)),
        name=name,
    )(x1, w_hi, w_lo, bias)


ITEM_FIELDS = 8


def _byte_split(v):
    return lax.shift_right_logical(v, 8).astype(F32), (v & 255).astype(F32)


def _plan_kernel(cntc_ref, cntr_ref, idx_ref, rank_ref, meta_ref, items_ref, m_scr, *, n_tok):
    i = pl.program_id(0)
    tm = idx_ref.shape[1]
    nbm = m_scr.shape[0]
    nip = items_ref.shape[1]
    nbp = nip - N_EXPERTS
    n_rows = n_tok * TOP_K
    n_blk = n_rows // EXPERT_ROWS
    nt = (((1,), (1,)), ((), ()))

    r_i = lax.broadcasted_iota(I32, (N_EXPERTS, N_EXPERTS), 0)
    c_i = lax.broadcasted_iota(I32, (N_EXPERTS, N_EXPERTS), 1)
    below = jnp.where(c_i < r_i, 1.0, 0.0).astype(BF16)
    ones_c = jnp.ones((N_EXPERTS, LANES), F32)
    c_hi, c_lo = _byte_split(cntc_ref[...])
    start_col = (256.0 * jnp.dot(below, (c_hi * ones_c).astype(BF16), preferred_element_type=F32)
                 + jnp.dot(below, (c_lo * ones_c).astype(BF16), preferred_element_type=F32))[:, 0:1]

    @pl.when(i == 0)
    def _():
        m_scr[...] = jnp.zeros_like(m_scr)
        above = jnp.where(r_i < c_i, 1.0, 0.0).astype(BF16)
        r_hi, r_lo = _byte_split(cntr_ref[...])
        start_row = (256.0 * jnp.dot(r_hi.astype(BF16), above, preferred_element_type=F32)
                     + jnp.dot(r_lo.astype(BF16), above, preferred_element_type=F32))[0:1, :]
        b_col = lax.broadcasted_iota(I32, (nbp, 1), 0)
        b_row = lax.broadcasted_iota(I32, (1, nbp), 1)
        blk_col = jnp.where(b_col < n_blk, b_col * EXPERT_ROWS, n_rows).astype(F32)
        blk_row = jnp.where(b_row < n_blk, b_row * EXPERT_ROWS, n_rows).astype(F32)
        v_col = jnp.concatenate([blk_col, start_col], axis=0)
        v_row = jnp.concatenate([blk_row, start_row], axis=1)
        j_col = lax.broadcasted_iota(I32, (nip, 1), 0)
        k_row = lax.broadcasted_iota(I32, (1, nip), 1)
        ahead = (v_row < v_col) | ((v_row == v_col) & (k_row < j_col))
        order_col = jnp.sum(jnp.where(ahead, 1.0, 0.0), axis=1, keepdims=True)
        pos_row = k_row.astype(F32)
        lo_abs = jnp.sum(jnp.where(order_col == pos_row, v_col, 0.0), axis=0, keepdims=True)
        hi_abs = jnp.sum(jnp.where(order_col == pos_row + 1.0, v_col, 0.0), axis=0, keepdims=True)
        hi_abs = jnp.where(k_row == nip - 1, float(n_rows), hi_abs)
        blk = jnp.minimum(jnp.floor(lo_abs * (1.0 / EXPERT_ROWS)), n_blk - 1.0)
        expert = jnp.sum(jnp.where(start_col <= lo_abs, 1.0, 0.0), axis=0, keepdims=True) - 1.0
        base = blk * EXPERT_ROWS
        e_col = lax.broadcasted_iota(I32, (N_EXPERTS, 1), 0).astype(F32)
        end_col = start_col + cntc_ref[...].astype(F32)
        seg_end = jnp.sum(jnp.where(e_col == expert, end_col, 0.0), axis=0, keepdims=True)
        follower = jnp.sum(jnp.where(start_col <= seg_end, 1.0, 0.0), axis=0, keepdims=True) - 1.0
        follower = jnp.where(seg_end < float(n_rows), follower, -1.0)
        fields = [blk, expert, lo_abs - base, hi_abs - base, follower]
        fields.append(jnp.zeros((ITEM_FIELDS - len(fields), nip), F32))
        items_ref[...] = jnp.concatenate(fields, axis=0).astype(I32)

    e_iota = lax.broadcasted_iota(I32, (N_EXPERTS, tm), 0)
    b_iota = lax.broadcasted_iota(I32, (nbm, tm), 0)
    l_iota = lax.broadcasted_iota(I32, (EXPERT_ROWS, tm), 0)
    tok = i * tm + lax.broadcasted_iota(I32, (1, tm), 1)
    tok_hi, tok_lo = _byte_split(tok)
    acc = jnp.zeros(m_scr.shape, F32)
    for k in range(TOP_K):
        hit = e_iota == idx_ref[k:k + 1, :]
        dest = (jnp.sum(jnp.where(hit, start_col, 0.0), axis=0, keepdims=True).astype(I32)
                + rank_ref[k:k + 1, :])
        oh_blk = jnp.where(b_iota == lax.shift_right_logical(dest, 8), 1.0, 0.0).astype(BF16)
        in_blk = l_iota == (dest & (EXPERT_ROWS - 1))
        vals = jnp.concatenate([jnp.where(in_blk, tok_hi, 0.0), jnp.where(in_blk, tok_lo, 0.0),
                                jnp.where(in_blk, float(k), 0.0)], axis=0).astype(BF16)
        acc = acc + lax.dot_general(oh_blk, vals, nt, preferred_element_type=F32)
    m_scr[...] = m_scr[...] + acc

    @pl.when(i == pl.num_programs(0) - 1)
    def _():
        m = m_scr[...]
        row_tok = m[:, 0:EXPERT_ROWS] * 256.0 + m[:, EXPERT_ROWS:2 * EXPERT_ROWS]
        row_slot = m[:, 2 * EXPERT_ROWS:3 * EXPERT_ROWS] * float(n_tok) + row_tok
        meta_ref[...] = jnp.concatenate([row_tok * float(PACKED_ROWS), row_slot * float(ROW_CHUNKS)],
                                        axis=1).astype(I32)


def _plan(cnt_col, cnt_row, idx, rank, *, tm, name):
    t = idx.shape[1]
    assert EXPERT_ROWS == 256 and (t * TOP_K) % (2 * EXPERT_ROWS) == 0
    n_blk = t * TOP_K // EXPERT_ROWS
    nbm = -(-n_blk // 16) * 16
    nip = -(-n_blk // LANES) * LANES + N_EXPERTS
    tok = lambda i: (0, i)
    full = lambda i: (0, 0)
    return pl.pallas_call(
        functools.partial(_plan_kernel, n_tok=t),
        grid=(t // tm,),
        in_specs=[pl.BlockSpec((N_EXPERTS, 1), full),
                  pl.BlockSpec((SUBLANES, N_EXPERTS), full),
                  pl.BlockSpec((TOP_K, tm), tok),
                  pl.BlockSpec((TOP_K, tm), tok)],
        out_specs=[pl.BlockSpec((nbm, 2 * EXPERT_ROWS), full),
                   pl.BlockSpec((ITEM_FIELDS, nip), full)],
        out_shape=[jax.ShapeDtypeStruct((nbm, 2 * EXPERT_ROWS), I32),
                   jax.ShapeDtypeStruct((ITEM_FIELDS, nip), I32)],
        scratch_shapes=[pltpu.VMEM((nbm, 3 * EXPERT_ROWS), F32)],
        compiler_params=_cparams(("arbitrary",)),
        name=name,
    )(cnt_col, cnt_row, idx, rank)


META_CHUNK = 4 * EXPERT_ROWS
GATHER_STRIDE = EXPERT_ROWS + 4
META_RING = 3
FFN_PIECES = 2 + D_MODEL // EXPERT_FF


def _experts_kernel(items_ref, xw_hbm, meta_hbm, wg_hbm, wu_hbm, wd_hbm, ys_hbm,
                    xw, meta, tile, lhs, wg_f, wu_f, wd_f, wg_b, wu_b, wd_b, yacc, ybuf_0, ybuf_1, state,
                    sem_x, sem_m, sem_y, sem_w, *, nip, n_blk):
    i = pl.program_id(0)
    ybuf = (ybuf_0, ybuf_1)
    blk = items_ref[i]
    expert = items_ref[nip + i]
    lo = items_ref[2 * nip + i]
    hi = items_ref[3 * nip + i]
    follower = items_ref[4 * nip + i]
    nonempty = hi > lo
    par = blk & 1
    chunk = lax.shift_right_logical(blk, 1)

    def meta_base(b):
        return (lax.shift_right_logical(b, 1) % META_RING) * META_CHUNK + (b & 1) * (2 * EXPERT_ROWS)

    mbase = meta_base(blk)
    prev_base = meta_base(jnp.maximum(blk - 1, 0))

    def meta_copy(c):
        slot = c % META_RING
        return pltpu.make_async_copy(meta_hbm.at[pl.ds(pl.multiple_of(c * META_CHUNK, META_CHUNK), META_CHUNK)],
                                     meta.at[pl.ds(pl.multiple_of(slot * META_CHUNK, META_CHUNK), META_CHUNK)],
                                     sem_m.at[slot])

    def ybuf_drain(slot):
        pltpu.make_async_copy(ys_hbm.at[pl.ds(0, EXPERT_ROWS * ROW_CHUNKS), :], ybuf[slot],
                              sem_y.at[slot]).wait()

    def ybuf_fill(slot):
        for j in range(ROW_CHUNKS):
            ybuf[slot][pl.ds(j, EXPERT_ROWS, stride=ROW_CHUNKS), :] = yacc[:, j * LANES:(j + 1) * LANES]

    def send_rows(slot, base, r0, r1):
        for r in range(r0, r1):
            src = ybuf[slot].at[pl.ds(r * ROW_CHUNKS, ROW_CHUNKS), :]
            dst_row = pl.multiple_of(meta[base + EXPERT_ROWS + r], ROW_CHUNKS)
            pltpu.make_async_copy(src, ys_hbm.at[pl.ds(dst_row, ROW_CHUNKS), :],
                                  sem_y.at[slot]).start(priority=r % 2)

    def weight_copies(e, slot):
        return [pltpu.make_async_copy(src.at[e], dst.at[slot], sem_w.at[slot])
                for src, dst in ((wg_hbm, wg_f), (wu_hbm, wu_f), (wd_hbm, wd_f))]

    @pl.when(i == 0)
    def _():
        resident = pltpu.make_async_copy(xw_hbm, xw, sem_x)
        resident.start()
        meta_copy(0).start()
        for cp in weight_copies(expert, 0):
            cp.start()
        state[0] = -1
        state[1] = 1
        resident.wait()

    @pl.when(nonempty & (expert != state[0]))
    def _():
        slot = 1 - state[1]
        for cp in weight_copies(expert, slot):
            cp.wait()
        state[0] = expert
        state[1] = slot

        @pl.when(follower >= 0)
        def _():
            for cp in weight_copies(follower, 1 - slot):
                cp.start()

        wg_b[...] = wg_f[slot].astype(BF16)
        wu_b[...] = wu_f[slot].astype(BF16)
        wd_b[...] = wd_f[slot].astype(BF16)

    @pl.when(nonempty & (lo == 0))
    def _():
        @pl.when(par == 0)
        def _():
            meta_copy(chunk).wait()

            @pl.when(2 * (chunk + 1) < n_blk)
            def _():
                meta_copy(chunk + 1).start()

        for r in range(EXPERT_ROWS):
            t4 = pl.multiple_of(meta[mbase + r], PACKED_ROWS)
            tile[pl.ds(r, PACKED_ROWS, stride=GATHER_STRIDE), :] = xw[pl.ds(t4, PACKED_ROWS), :]
        cols = []
        for s in range(PACKED_ROWS):
            cols.extend(_unpack_bf16_pairs(tile[pl.ds(s * GATHER_STRIDE, EXPERT_ROWS), :]))
        lhs[...] = jnp.concatenate(cols, axis=-1)

    def ffn(x, between=None, to_slot=None):
        done = [0]

        def piece_done():
            if between is not None:
                between(done[0])
            done[0] += 1

        gate = jnp.dot(x, wg_b[...], preferred_element_type=F32)
        piece_done()
        up = jnp.dot(x, wu_b[...], preferred_element_type=F32)
        piece_done()
        hid = (jax.nn.silu(gate) * up).astype(BF16)
        out = []
        for c in range(D_MODEL // EXPERT_FF):
            y_c = jnp.dot(hid, wd_b[:, c * EXPERT_FF:(c + 1) * EXPERT_FF], preferred_element_type=F32)
            if to_slot is None:
                out.append(y_c)
            else:
                for jj in range(EXPERT_FF // LANES):
                    j = c * (EXPERT_FF // LANES) + jj
                    ybuf[to_slot][pl.ds(j, EXPERT_ROWS, stride=ROW_CHUNKS), :] = y_c[:, jj * LANES:(jj + 1) * LANES]
            piece_done()
        assert done[0] == FFN_PIECES
        return jnp.concatenate(out, axis=-1) if to_slot is None else None

    def run_mask(y):
        row = lax.broadcasted_iota(I32, (EXPERT_ROWS, 1), 0)
        return jnp.where((row >= lo) & (row < hi), y, 0.0)

    whole = (lo == 0) & (hi == EXPERT_ROWS)
    first = nonempty & (lo == 0)
    has_prev = blk > 0

    for prev_slot in range(2):
        def send_piece(g, prev_slot=prev_slot):
            send_rows(prev_slot, prev_base, g * EXPERT_ROWS // FFN_PIECES, (g + 1) * EXPERT_ROWS // FFN_PIECES)

        sends = has_prev & (par == 1 - prev_slot)

        @pl.when(whole & sends)
        def _():
            @pl.when(blk >= 2)
            def _():
                ybuf_drain(1 - prev_slot)

            ffn(lhs[...], send_piece, to_slot=1 - prev_slot)

        @pl.when(first & jnp.logical_not(whole) & sends)
        def _():
            yacc[...] = run_mask(ffn(lhs[...], send_piece))

    @pl.when(whole & jnp.logical_not(has_prev))
    def _():
        ffn(lhs[...], to_slot=0)

    @pl.when(nonempty & jnp.logical_not(whole) & jnp.logical_not(first & has_prev))
    def _():
        y = run_mask(ffn(lhs[...]))

        @pl.when(lo == 0)
        def _():
            yacc[...] = y

        @pl.when(lo > 0)
        def _():
            yacc[...] = yacc[...] + y

    @pl.when(nonempty & (hi == EXPERT_ROWS))
    def _():
        @pl.when(jnp.logical_not(whole))
        def _():
            for slot in range(2):
                @pl.when(par == slot)
                def _():
                    @pl.when(blk >= 2)
                    def _():
                        ybuf_drain(slot)

                    ybuf_fill(slot)

        @pl.when(blk == n_blk - 1)
        def _():
            for slot in range(2):
                @pl.when(par == slot)
                def _():
                    send_rows(slot, mbase, 0, EXPERT_ROWS)

    @pl.when(i == pl.num_programs(0) - 1)
    def _():
        ybuf_drain(0)
        ybuf_drain(1)


def _experts(items, xw, meta, wg, wu, wd, *, n_tok, name):
    nip = items.shape[0] // ITEM_FIELDS
    n_rows = n_tok * TOP_K
    n_blk = n_rows // EXPERT_ROWS
    n_items = n_blk + N_EXPERTS
    assert n_blk >= 2 and n_blk % 2 == 0
    kern = functools.partial(_experts_kernel, nip=nip, n_blk=n_blk)
    any_spec = pl.BlockSpec(memory_space=pl.ANY)
    return pl.pallas_call(
        kern,
        grid_spec=pltpu.PrefetchScalarGridSpec(
            num_scalar_prefetch=1,
            grid=(n_items,),
            in_specs=[any_spec] * 5,
            out_specs=any_spec,
            scratch_shapes=[pltpu.VMEM(xw.shape, I32),
                            pltpu.SMEM((META_RING * META_CHUNK,), I32),
                            pltpu.VMEM((PACKED_ROWS * GATHER_STRIDE, LANES), I32),
                            pltpu.VMEM((EXPERT_ROWS, D_MODEL), BF16),
                            pltpu.VMEM((2, D_MODEL, EXPERT_FF), F32),
                            pltpu.VMEM((2, D_MODEL, EXPERT_FF), F32),
                            pltpu.VMEM((2, EXPERT_FF, D_MODEL), F32),
                            pltpu.VMEM((D_MODEL, EXPERT_FF), BF16),
                            pltpu.VMEM((D_MODEL, EXPERT_FF), BF16),
                            pltpu.VMEM((EXPERT_FF, D_MODEL), BF16),
                            pltpu.VMEM((EXPERT_ROWS, D_MODEL), F32),
                            pltpu.VMEM((EXPERT_ROWS * ROW_CHUNKS, LANES), F32),
                            pltpu.VMEM((EXPERT_ROWS * ROW_CHUNKS, LANES), F32),
                            pltpu.SMEM((2,), I32),
                            pltpu.SemaphoreType.DMA,
                            pltpu.SemaphoreType.DMA((META_RING,)),
                            pltpu.SemaphoreType.DMA((2,)),
                            pltpu.SemaphoreType.DMA((2,))],
        ),
        out_shape=jax.ShapeDtypeStruct((n_rows * ROW_CHUNKS, LANES), F32),
        compiler_params=_cparams(("arbitrary",)),
        name=name,
    )(items, xw, meta, wg, wu, wd)


def _combine_kernel(ys_ref, x_ref, wl_ref, wsg_ref, wsu_ref, wsd_ref, g_ref, b_ref, op_ref, os_ref,
                    routed, *, np_tiles):
    x = x_ref[...]
    xb = x.astype(BF16)
    hid = (jax.nn.silu(jnp.dot(xb, wsg_ref[...], preferred_element_type=F32))
           * jnp.dot(xb, wsu_ref[...], preferred_element_type=F32))
    shared = jnp.dot(hid.astype(BF16), wsd_ref[...], preferred_element_type=F32)
    acc = None
    for k in range(TOP_K):
        part = wl_ref[:, k:k + 1, :] * ys_ref[k]
        acc = part if acc is None else acc + part
    routed[...] = acc
    ffn = jnp.concatenate([routed[:, j, :] for j in range(ROW_CHUNKS)], axis=-1) + shared
    out = _layer_norm(DEEPNORM_ALPHA * x + ffn, g_ref[...], b_ref[...])
    i = pl.program_id(0)

    @pl.when(i < np_tiles)
    def _():
        op_ref[...] = out

    @pl.when(i >= np_tiles)
    def _():
        os_ref[...] = out


def _combine(ys, x1, w_tok, wsg, wsu, wsd, g, b, *, n_prompt, tm, name):
    t = x1.shape[0]
    np_tiles = n_prompt // tm
    p_map, s_map = _pair_maps(np_tiles)
    row = lambda i: (i, 0)
    full = lambda i: (0, 0)
    return pl.pallas_call(
        functools.partial(_combine_kernel, np_tiles=np_tiles),
        grid=(t // tm,),
        in_specs=[pl.BlockSpec((TOP_K, tm, ROW_CHUNKS, LANES), lambda i: (0, i, 0, 0)),
                  pl.BlockSpec((tm, D_MODEL), row),
                  pl.BlockSpec((tm, TOP_K, LANES), lambda i: (i, 0, 0)),
                  pl.BlockSpec((D_MODEL, SHARED_FF), full),
                  pl.BlockSpec((D_MODEL, SHARED_FF), full),
                  pl.BlockSpec((SHARED_FF, D_MODEL), full),
                  pl.BlockSpec((1, D_MODEL), full),
                  pl.BlockSpec((1, D_MODEL), full)],
        out_specs=[pl.BlockSpec((tm, D_MODEL), p_map), pl.BlockSpec((tm, D_MODEL), s_map)],
        out_shape=[jax.ShapeDtypeStruct((n_prompt, D_MODEL), F32),
                   jax.ShapeDtypeStruct((t - n_prompt, D_MODEL), F32)],
        scratch_shapes=[pltpu.VMEM((tm, ROW_CHUNKS, LANES), F32)],
        compiler_params=_cparams(("arbitrary",)),
        name=name,
    )(ys, x1, w_tok, wsg, wsu, wsd, g, b)


def _pack_w_in(w_in):
    sizes = (GLA_KEY_DIM, GLA_KEY_DIM, GLA_VAL_DIM, GLA_VAL_DIM, GLA_GATE_RANK,
             SWA_Q_DIM, SWA_KV_DIM, SWA_KV_DIM, D_MODEL, D_MODEL)
    offs = [0]
    for s in sizes:
        offs.append(offs[-1] + s)
    qa, ka, va, ga, gk, qb, kb, vb, gate_a, gate_b = (w_in[:, offs[i]:offs[i + 1]] for i in range(10))
    pad = lambda w, n: jnp.pad(w, ((0, 0), (0, n - w.shape[1])))
    packed = jnp.concatenate([qa, ka, va, ga, qb, gate_a, gate_b, kb, vb, pad(gk, 2 * LANES)], axis=1)
    assert packed.shape[1] == H_WIDTH
    return packed.astype(BF16), jnp.concatenate([kb, vb], axis=1).astype(BF16)


def kernel(x_prompt, x_sample, state_gla, cache_swa_k, cache_swa_v, w_in, w_gk_up, b_gk, gla_norm_g,
           attn_sinks, w_proj_a, w_proj_b, w_out, ln1_g, ln1_b, w_router, router_bias,
           w_expert_gate, w_expert_up, w_expert_down, w_shared_gate, w_shared_up, w_shared_down,
           ln2_g, ln2_b):
    assert w_in.shape[0] == 1, "single-layer trunk"
    bp, lp, d = x_prompt.shape
    bs, ls, _ = x_sample.shape
    assert d == D_MODEL and ls == SUBLANES and cache_swa_k.shape[2] == WINDOW
    tp, ts = bp * lp, bs * ls
    t = tp + ts

    xp = x_prompt.reshape(tp, d)
    xs = x_sample.reshape(ts, d)
    w_main, w_kv = _pack_w_in(w_in[0])
    h = _matmul(xp, xs, w_main, BF16, _pair_tile(tp, ts, 1024), H_TN, "proj_in")

    xp_tail = x_prompt[:, lp - WINDOW:].reshape(bp * WINDOW, d)
    kv_tail = _matmul(xp_tail, xs, w_kv, F32, _pair_tile(bp * WINDOW, ts, 512), 2 * SWA_KV_DIM,
                      "proj_kv_tail")

    wup = jnp.pad(w_gk_up[0], ((0, LANES - GLA_GATE_RANK), (0, 0))).astype(BF16)
    bgk = b_gk[0].reshape(1, GLA_KEY_DIM)
    gn = gla_norm_g[0].reshape(1, GLA_DV)
    oa_p, s_prompt = _gla(h, wup, bgk, gn, None, row0=0, n_seq=bp, seq_len=lp, par=math.gcd(bp, 4), nb=1,
                          c=GLA_CHUNK, sub=2, name="gla_prompt")
    oa_s, s_sample = _gla(h, wup, bgk, gn, state_gla[0], row0=tp, n_seq=bs, seq_len=ls, par=1, nb=8,
                          c=math.gcd(ls, GLA_CHUNK), sub=1, name="gla_sample")
    oa_p = oa_p.reshape(tp, GLA_VAL_DIM)
    oa_s = oa_s.reshape(ts, GLA_VAL_DIM)

    sinks = attn_sinks[0]
    k_past = cache_swa_k[0].reshape(bs, WINDOW, SWA_KV_DIM)
    v_past = cache_swa_v[0].reshape(bs, WINDOW, SWA_KV_DIM)
    ob_p = _swa_prompt(h, sinks, n_seq=bp, seq_len=lp, name="swa_prompt")
    ob_s = _swa_sample(h, sinks, k_past, v_past, row0=tp, n_seq=bs, lq=ls, nb=8, name="swa_sample")

    x1, xw = _merge((oa_p, oa_s), (ob_p, ob_s), h, (xp, xs), w_proj_a[0].astype(BF16),
                    w_proj_b[0].astype(BF16), w_out[0].astype(BF16), ln1_g[0].reshape(1, d),
                    ln1_b[0].reshape(1, d), tm=_pair_tile(tp, ts, 512), name="merge_ln1")

    wr_t = w_router[0].T
    wr_hi = wr_t.astype(BF16)
    wr_lo = (wr_t - wr_hi.astype(F32)).astype(BF16)
    tm_r = _tile(t, 512)
    idx, w_lanes, rank, cnt_col, cnt_row = _router(x1, wr_hi, wr_lo, router_bias[0].reshape(N_EXPERTS, 1),
                                                   tm=tm_r, name="router")
    meta, items = _plan(cnt_col, cnt_row, idx, rank, tm=tm_r, name="plan")
    ys = _experts(items.reshape(-1), xw, meta.reshape(-1), w_expert_gate[0], w_expert_up[0],
                  w_expert_down[0], n_tok=t, name="experts")
    y_p, y_s = _combine(ys.reshape(TOP_K, t, ROW_CHUNKS, LANES), x1, w_lanes.reshape(t, TOP_K, LANES),
                        w_shared_gate[0].astype(BF16), w_shared_up[0].astype(BF16),
                        w_shared_down[0].astype(BF16), ln2_g[0].reshape(1, d), ln2_b[0].reshape(1, d),
                        n_prompt=tp, tm=_pair_tile(tp, ts, 512), name="combine_ln2")

    y_prompt = y_p.reshape(bp, lp, d)
    y_sample = y_s.reshape(bs, ls, d)
    k_tail = kv_tail[:, :SWA_KV_DIM]
    v_tail = kv_tail[:, SWA_KV_DIM:]
    kv_shape = (SWA_KV_HEADS, SWA_HEAD_DIM)
    k_prompt = k_tail[:bp * WINDOW].reshape(1, bp, WINDOW, *kv_shape)
    v_prompt = v_tail[:bp * WINDOW].reshape(1, bp, WINDOW, *kv_shape)
    k_new = k_tail[bp * WINDOW:].reshape(bs, ls, *kv_shape)
    v_new = v_tail[bp * WINDOW:].reshape(bs, ls, *kv_shape)
    k_sample = jnp.concatenate([cache_swa_k[0][:, ls:], k_new], axis=1)[None]
    v_sample = jnp.concatenate([cache_swa_v[0][:, ls:], v_new], axis=1)[None]
    return (y_prompt, y_sample, s_prompt[None], s_sample[None], k_prompt, v_prompt, k_sample, v_sample)
```
